```python
import math, functools
import jax, jax.numpy as jnp
from jax import lax
import numpy as np

D_MODEL = 1024
BATCH = 2
SEQ = 8192
DEPTH = 1
DEC_BATCH = 32
DEC_SEQ = 32
PAST_LEN = 4096

CHUNK = 64
HEAD_DIM = 64
N_HEADS_A = 8
BAND_CHUNKS_A = 8
REACH_A = BAND_CHUNKS_A * CHUNK
REL_CLIP_A = 128
N_HEADS_B = 8
N_KV_B = 2
GROUP_B = N_HEADS_B // N_KV_B
WINDOW_B = 128
BAND_CHUNKS_B = WINDOW_B // CHUNK
REACH_B = BAND_CHUNKS_B * CHUNK
N_BUCKETS = 32
T5_MAX_DIST = 128
WIDTH_A = N_HEADS_A * HEAD_DIM
WIDTH_B = N_HEADS_B * HEAD_DIM
KV_WIDTH_B = N_KV_B * HEAD_DIM
MIX_WIDTH = WIDTH_A + WIDTH_B
IN_WIDTH = 3 * WIDTH_A + WIDTH_B + 2 * KV_WIDTH_B
SPLITS = (WIDTH_A, 2 * WIDTH_A, 3 * WIDTH_A, 3 * WIDTH_A + WIDTH_B, 3 * WIDTH_A + WIDTH_B + KV_WIDTH_B)
D_FF = 2816
N_MOD = 9
FFN_RES = 0.5
EPS = 1e-6
SCALE = HEAD_DIM ** -0.5
NEG_INF = -1e30

kernel_name = "hybrid_streaming_encoder_step"


def rmsnorm(x, g):
    xf = x.astype(jnp.float32)
    y = xf * lax.rsqrt(jnp.mean(xf * xf, axis=-1, keepdims=True) + EPS)
    return (y * g.astype(jnp.float32)).astype(x.dtype)


def adaln_in(x, g, shift, scale):
    return rmsnorm(x, g) * (1 + scale[:, None, :]) + shift[:, None, :]


def swiglu(h, w_gate, w_up, w_down):
    return (jax.nn.silu(h @ w_gate) * (h @ w_up)) @ w_down


def rel_k_minus_q(q_off, n_q, n_k):
    return jnp.arange(n_k)[None, :] - (jnp.arange(n_q) + q_off)[:, None]


def clipped_rel_bias(table, rel):
    return table[:, jnp.clip(rel, -REL_CLIP_A, REL_CLIP_A) + REL_CLIP_A]


def t5_bucket(rel):
    half = N_BUCKETS // 2
    max_exact = half // 2
    n = jnp.abs(rel)
    base = jnp.where(rel > 0, half, 0)
    nf = jnp.maximum(n, 1).astype(jnp.float32)
    large = max_exact + (jnp.log(nf / max_exact) / math.log(T5_MAX_DIST / max_exact)
                         * (half - max_exact)).astype(jnp.int32)
    large = jnp.minimum(large, half - 1)
    return base + jnp.where(n < max_exact, n, large)


def t5_rel_bias(table, rel):
    return table[:, t5_bucket(rel)]


def gather_band(t, n_prev):
    B, T = t.shape[:2]
    nc = T // CHUNK
    tc = t.reshape((B, nc, CHUNK) + t.shape[2:])
    tp = jnp.pad(tc, [(0, 0), (n_prev, 0)] + [(0, 0)] * (tc.ndim - 2))
    idx = jnp.arange(nc)[:, None] + jnp.arange(n_prev + 1)[None, :]
    band = tp[:, idx]
    return band.reshape((B, nc, (n_prev + 1) * CHUNK) + t.shape[2:])


def band_valid(nc, n_prev):
    k_pos = (jnp.arange(nc)[:, None] - n_prev) * CHUNK + jnp.arange((n_prev + 1) * CHUNK)[None, :]
    return (k_pos >= 0)[:, None, :]


def band_attn_a(q, k, v, bias, valid):
    s = jnp.einsum('bnqhd,bnkhd->bnhqk', q, k).astype(jnp.float32) * SCALE + bias.astype(jnp.float32)
    s = jnp.where(valid[None, :, None], s, NEG_INF)
    p = jax.nn.softmax(s, axis=-1).astype(v.dtype)
    return jnp.einsum('bnhqk,bnkhd->bnqhd', p, v)


def window_attn_b(q, k, v, bias, sinks, valid):
    n_q, n_k = bias.shape[1:]
    s = jnp.einsum('bnqcgd,bnkcd->bncgqk', q, k).astype(jnp.float32) * SCALE \
        + bias.reshape(N_KV_B, GROUP_B, n_q, n_k).astype(jnp.float32)
    s = jnp.where(valid[None, :, None, None], s, NEG_INF)
    sink = sinks.astype(jnp.float32).reshape(N_KV_B, GROUP_B, 1, 1)
    m = jnp.maximum(jnp.max(s, axis=-1, keepdims=True), sink)
    e = jnp.exp(s - m)
    p = e / (jnp.sum(e, axis=-1, keepdims=True) + jnp.exp(sink - m))
    return jnp.einsum('bncgqk,bnkcd->bnqcgd', p.astype(v.dtype), v)


def mixers_prompt(qa, ka, va, qb, kb, vb, rel_tab_a, t5_tab, sinks_b):
    B, T = qa.shape[:2]
    nc = T // CHUNK
    len_a = (BAND_CHUNKS_A + 1) * CHUNK
    bias_a = clipped_rel_bias(rel_tab_a, rel_k_minus_q(BAND_CHUNKS_A * CHUNK, CHUNK, len_a))
    oa = band_attn_a(qa.reshape(B, nc, CHUNK, N_HEADS_A, HEAD_DIM),
                     gather_band(ka, BAND_CHUNKS_A), gather_band(va, BAND_CHUNKS_A),
                     bias_a, band_valid(nc, BAND_CHUNKS_A))
    len_b = (BAND_CHUNKS_B + 1) * CHUNK
    bias_b = t5_rel_bias(t5_tab, rel_k_minus_q(BAND_CHUNKS_B * CHUNK, CHUNK, len_b))
    ob = window_attn_b(qb.reshape(B, nc, CHUNK, N_KV_B, GROUP_B, HEAD_DIM),
                       gather_band(kb, BAND_CHUNKS_B), gather_band(vb, BAND_CHUNKS_B),
                       bias_b, sinks_b, band_valid(nc, BAND_CHUNKS_B))
    la, lb = min(REACH_A, T), min(REACH_B, T)
    rows = (ka[:, T - la:], va[:, T - la:], kb[:, T - lb:], vb[:, T - lb:])
    return oa.reshape(B, T, WIDTH_A), ob.reshape(B, T, WIDTH_B), rows


def mixers_sample(qa, ka, va, qb, kb, vb, cache_ak, cache_av, cache_bk, cache_bv, rel_tab_a, t5_tab, sinks_b):
    B, S = qa.shape[:2]
    la, lb = cache_ak.shape[1], cache_bk.shape[1]
    ka_all = jnp.concatenate([cache_ak.astype(ka.dtype), ka], axis=1)[:, None]
    va_all = jnp.concatenate([cache_av.astype(va.dtype), va], axis=1)[:, None]
    bias_a = clipped_rel_bias(rel_tab_a, rel_k_minus_q(la, S, la + S))
    oa = band_attn_a(qa[:, None], ka_all, va_all, bias_a, jnp.ones((1, 1, la + S), bool))[:, 0]
    kb_all = jnp.concatenate([cache_bk.astype(kb.dtype), kb], axis=1)[:, None]
    vb_all = jnp.concatenate([cache_bv.astype(vb.dtype), vb], axis=1)[:, None]
    bias_b = t5_rel_bias(t5_tab, rel_k_minus_q(lb, S, lb + S))
    ob = window_attn_b(qb.reshape(B, 1, S, N_KV_B, GROUP_B, HEAD_DIM), kb_all, vb_all,
                       bias_b, sinks_b, jnp.ones((1, 1, lb + S), bool))[:, 0]
    return oa.reshape(B, S, WIDTH_A), ob.reshape(B, S, WIDTH_B), (ka, va, kb, vb)


def layer(x, c, w_mod, b_mod, gains, w1_gate, w1_up, w1_down, w_in, w_out, group_gains,
          w2_gate, w2_up, w2_down, mixers):
    mod = jax.nn.silu(c) @ w_mod + b_mod
    sh1, sc1, g1, sh2, sc2, g2, sh3, sc3, g3 = jnp.split(mod, N_MOD, axis=-1)
    h = adaln_in(x, gains[0], sh1, sc1)
    x = x + FFN_RES * g1[:, None] * rmsnorm(swiglu(h, w1_gate, w1_up, w1_down), gains[1])
    h = adaln_in(x, gains[2], sh2, sc2)
    B, T = h.shape[:2]
    qa, ka, va, qb, kb, vb = jnp.split(h @ w_in, SPLITS, axis=-1)
    qa = qa.reshape(B, T, N_HEADS_A, HEAD_DIM)
    ka = ka.reshape(B, T, N_HEADS_A, HEAD_DIM)
    va = va.reshape(B, T, N_HEADS_A, HEAD_DIM)
    qb = qb.reshape(B, T, N_HEADS_B, HEAD_DIM)
    kb = kb.reshape(B, T, N_KV_B, HEAD_DIM)
    vb = vb.reshape(B, T, N_KV_B, HEAD_DIM)
    oa, ob, rows = mixers(qa, ka, va, qb, kb, vb)
    o = jnp.concatenate([rmsnorm(oa, group_gains[:WIDTH_A]), rmsnorm(ob, group_gains[WIDTH_A:])], axis=-1) @ w_out
    x = x + g2[:, None] * rmsnorm(o, gains[3])
    h = adaln_in(x, gains[4], sh3, sc3)
    x = x + FFN_RES * g3[:, None] * rmsnorm(swiglu(h, w2_gate, w2_up, w2_down), gains[5])
    return x, rows


def setup_inputs(seed: int = 0) -> dict:
    key = jax.random.key(seed)
    ks = iter(jax.random.split(key, 32))

    def nrm(shape, s):
        return jax.random.normal(next(ks), shape, jnp.float32) * s

    la, lb = min(REACH_A, PAST_LEN), min(REACH_B, PAST_LEN)
    return {
        "x_prompt": nrm((BATCH, SEQ, D_MODEL), 1.0),
        "x_sample": nrm((DEC_BATCH, DEC_SEQ, D_MODEL), 1.0),
        "cache_a_k": nrm((DEPTH, DEC_BATCH, la, N_HEADS_A, HEAD_DIM), 1.0),
        "cache_a_v": nrm((DEPTH, DEC_BATCH, la, N_HEADS_A, HEAD_DIM), 1.0),
        "cache_b_k": nrm((DEPTH, DEC_BATCH, lb, N_KV_B, HEAD_DIM), 1.0),
        "cache_b_v": nrm((DEPTH, DEC_BATCH, lb, N_KV_B, HEAD_DIM), 1.0),
        "c_prompt": nrm((BATCH, D_MODEL), 1.0),
        "c_sample": nrm((DEC_BATCH, D_MODEL), 1.0),
        "w_mod": nrm((DEPTH, D_MODEL, N_MOD * D_MODEL), D_MODEL ** -0.5),
        "b_mod": nrm((DEPTH, N_MOD * D_MODEL), 0.02),
        "norm_gains": 1.0 + nrm((DEPTH, 6, D_MODEL), 0.05),
        "w1_gate": nrm((DEPTH, D_MODEL, D_FF), D_MODEL ** -0.5),
        "w1_up": nrm((DEPTH, D_MODEL, D_FF), D_MODEL ** -0.5),
        "w1_down": nrm((DEPTH, D_FF, D_MODEL), D_FF ** -0.5),
        "w_in": nrm((DEPTH, D_MODEL, IN_WIDTH), D_MODEL ** -0.5),
        "w_out": nrm((DEPTH, MIX_WIDTH, D_MODEL), MIX_WIDTH ** -0.5),
        "group_gains": 1.0 + nrm((DEPTH, MIX_WIDTH), 0.05),
        "rel_bias_a": nrm((DEPTH, N_HEADS_A, 2 * REL_CLIP_A + 1), 0.2),
        "t5_bias_table": nrm((N_HEADS_B, N_BUCKETS), 0.2),
        "sinks_b": nrm((DEPTH, N_HEADS_B), 1.0),
        "w2_gate": nrm((DEPTH, D_MODEL, D_FF), D_MODEL ** -0.5),
        "w2_up": nrm((DEPTH, D_MODEL, D_FF), D_MODEL ** -0.5),
        "w2_down": nrm((DEPTH, D_FF, D_MODEL), D_FF ** -0.5),
    }


def reference(x_prompt, x_sample, cache_a_k, cache_a_v, cache_b_k, cache_b_v, c_prompt, c_sample,
              w_mod, b_mod, norm_gains, w1_gate, w1_up, w1_down, w_in, w_out, group_gains,
              rel_bias_a, t5_bias_table, sinks_b, w2_gate, w2_up, w2_down):
    xp, xs = x_prompt, x_sample
    rows_p, rows_s = [], []
    for l in range(DEPTH):
        weights = (w_mod[l], b_mod[l], norm_gains[l], w1_gate[l], w1_up[l], w1_down[l], w_in[l], w_out[l],
                   group_gains[l], w2_gate[l], w2_up[l], w2_down[l])
        mix_p = functools.partial(mixers_prompt, rel_tab_a=rel_bias_a[l], t5_tab=t5_bias_table,
                                  sinks_b=sinks_b[l])
        mix_s = functools.partial(mixers_sample, cache_ak=cache_a_k[l], cache_av=cache_a_v[l],
                                  cache_bk=cache_b_k[l], cache_bv=cache_b_v[l], rel_tab_a=rel_bias_a[l],
                                  t5_tab=t5_bias_table, sinks_b=sinks_b[l])
        xp, rp = layer(xp, c_prompt, *weights, mix_p)
        xs, rs = layer(xs, c_sample, *weights, mix_s)
        rows_p.append(rp)
        rows_s.append(rs)
    new_a_k_prompt = jnp.stack([r[0] for r in rows_p])
    new_a_v_prompt = jnp.stack([r[1] for r in rows_p])
    new_b_k_prompt = jnp.stack([r[2] for r in rows_p])
    new_b_v_prompt = jnp.stack([r[3] for r in rows_p])
    new_a_k_sample = jnp.stack([r[0] for r in rows_s])
    new_a_v_sample = jnp.stack([r[1] for r in rows_s])
    new_b_k_sample = jnp.stack([r[2] for r in rows_s])
    new_b_v_sample = jnp.stack([r[3] for r in rows_s])
    return (xp, xs, new_a_k_prompt, new_a_v_prompt, new_b_k_prompt, new_b_v_prompt,
            new_a_k_sample, new_a_v_sample, new_b_k_sample, new_b_v_sample)
```

```python
import functools

import jax
import jax.numpy as jnp
from jax import lax
from jax.experimental import pallas as pl
from jax.experimental.pallas import tpu as pltpu

F32 = jnp.float32
BF16 = jnp.bfloat16

D_MODEL = 1024
D_FF = 2816
CHUNK = 64
HEAD_DIM = 64
N_HEADS = 8
N_PAIRS = N_HEADS // 2
WIDTH = N_HEADS * HEAD_DIM
KV_WIDTH_B = 2 * HEAD_DIM
REACH_A = 8 * CHUNK
REACH_B = 2 * CHUNK
REL_CLIP_A = 128
N_BUCKETS = 32
N_MOD = 9
FFN_RES = 0.5
EPS = 1e-6
SCALE = HEAD_DIM ** -0.5
NEG_INF = -1e30

V7X_LANES = 128
V7X_VMEM_BYTES = 64 * 1024 * 1024
V7X_VMEM_LIMIT_CAP = 56 * 1024 * 1024

GROUP_Q = 2 * CHUNK
BAND_A = REACH_A + GROUP_Q
BAND_B = REACH_B + GROUP_Q
ORIGIN_A = 1024
WIDTH_MASTER_A = ORIGIN_A + BAND_A
ORIGIN_B = 256
WIDTH_MASTER_B = ORIGIN_B + BAND_B
N_VAR_A = REACH_A // GROUP_Q + 1
N_VAR_B = REACH_B // GROUP_Q + 1


def _vmem_limit(estimate_bytes):
    return int(min(max(estimate_bytes, 32 * 1024 * 1024), V7X_VMEM_LIMIT_CAP))


def _rms(x, gain):
    ms = jnp.mean(x * x, axis=-1, keepdims=True)
    return x * lax.rsqrt(ms + EPS) * gain


def _adaln(x, gain, shift, scale):
    return _rms(x, gain) * (1.0 + scale) + shift


def _silu(x):
    return x * (1.0 / (1.0 + jnp.exp(-x)))


def _mod_kernel(c_ref, w_ref, b_ref, o_ref):
    s = _silu(c_ref[...]).astype(BF16)
    o_ref[0] = jnp.dot(s, w_ref[...].astype(BF16), preferred_element_type=F32) + b_ref[0]


def _modulation(c_rows, w_mod, b_mod):
    rows = c_rows.shape[0]
    return pl.pallas_call(
        _mod_kernel,
        grid=(N_MOD,),
        in_specs=[
            pl.BlockSpec((rows, D_MODEL), lambda j: (0, 0)),
            pl.BlockSpec((D_MODEL, D_MODEL), lambda j: (0, j)),
            pl.BlockSpec((1, 1, D_MODEL), lambda j: (j, 0, 0)),
        ],
        out_specs=pl.BlockSpec((1, rows, D_MODEL), lambda j: (j, 0, 0)),
        out_shape=jax.ShapeDtypeStruct((N_MOD, rows, D_MODEL), F32),
        name="modulation",
    )(c_rows, w_mod, b_mod.reshape(N_MOD, 1, D_MODEL))


def _skew(base_row, n_rows):
    width = base_row.shape[1]
    m = jnp.broadcast_to(base_row, (n_rows, width))
    row = lax.broadcasted_iota(jnp.int32, (n_rows, width), 0)
    shift = 1
    while shift < n_rows:
        m = jnp.where((row & shift) != 0, pltpu.roll(m, shift, axis=1), m)
        shift *= 2
    return m


def _band_mask(n_rows, width, origin, n_prev):
    row = lax.broadcasted_iota(jnp.int32, (n_rows, width), 0)
    col = lax.broadcasted_iota(jnp.int32, (n_rows, width), 1)
    q_chunk = row >> 6
    k_chunk = (col >> 6) - (origin // CHUNK)
    return (k_chunk >= q_chunk - n_prev) & (k_chunk <= q_chunk)


def _t5_bucket(rel):
    half = N_BUCKETS // 2
    max_exact = half // 2
    n = jnp.abs(rel)
    n2 = n * n
    large = jnp.full(rel.shape, max_exact, jnp.int32)
    for k in range(1, half - max_exact):
        large = large + jnp.where(n2 >= (max_exact * max_exact) * (2 ** k), 1, 0)
    return jnp.where(rel > 0, half, 0) + jnp.where(n < max_exact, n, large)


def _bias_kernel(rel_ref, t5_ref, ba_ref, bb_ref, bsa_ref, bsb_ref, *, s_len, la, lb):
    tab = rel_ref[0]
    n_tab = 2 * REL_CLIP_A + 1
    padded = jnp.concatenate([tab, jnp.zeros((1, WIDTH_MASTER_A - tab.shape[1]), F32)], axis=1)
    rolled = pltpu.roll(padded, ORIGIN_A - REL_CLIP_A, axis=1)
    col = lax.broadcasted_iota(jnp.int32, (1, WIDTH_MASTER_A), 1)
    u_a = jnp.where(col < ORIGIN_A - REL_CLIP_A, tab[:, 0:1],
                    jnp.where(col > ORIGIN_A + REL_CLIP_A, tab[:, n_tab - 1:n_tab], rolled))
    skew_a = _skew(u_a, GROUP_Q)
    master_a = jnp.where(_band_mask(GROUP_Q, WIDTH_MASTER_A, ORIGIN_A, REACH_A // CHUNK), skew_a, NEG_INF)
    for v in range(N_VAR_A):
        start = ORIGIN_A - min(v * GROUP_Q, REACH_A) if v < N_VAR_A - 1 else ORIGIN_A - REACH_A
        ba_ref[v, 0] = master_a[:, start:start + BAND_A]
    bsa_ref[0] = skew_a[:s_len, ORIGIN_A - la:ORIGIN_A + s_len]

    t5 = t5_ref[0]
    rel = lax.broadcasted_iota(jnp.int32, (1, WIDTH_MASTER_B), 1) - ORIGIN_B
    bucket = _t5_bucket(rel)
    u_b = jnp.zeros((1, WIDTH_MASTER_B), F32)
    for i in range(N_BUCKETS):
        u_b = jnp.where(bucket == i, t5[:, i:i + 1], u_b)
    skew_b = _skew(u_b, GROUP_Q)
    master_b = jnp.where(_band_mask(GROUP_Q, WIDTH_MASTER_B, ORIGIN_B, REACH_B // CHUNK), skew_b, NEG_INF)
    for v in range(N_VAR_B):
        start = ORIGIN_B - min(v * GROUP_Q, REACH_B) if v < N_VAR_B - 1 else ORIGIN_B - REACH_B
        bb_ref[v, 0] = master_b[:, start:start + BAND_B]
    bsb_ref[0] = skew_b[:s_len, ORIGIN_B - lb:ORIGIN_B + s_len]


def _bias_tables(rel_bias_a, t5_table, s_len, la, lb):
    n_tab = 2 * REL_CLIP_A + 1
    tab_w = 3 * V7X_LANES
    rel_p = jnp.pad(rel_bias_a, ((0, 0), (0, tab_w - n_tab))).reshape(N_HEADS, 1, tab_w)
    t5_p = t5_table.reshape(N_HEADS, 1, N_BUCKETS)
    kern = functools.partial(_bias_kernel, s_len=s_len, la=la, lb=lb)
    return pl.pallas_call(
        kern,
        grid=(N_HEADS,),
        in_specs=[
            pl.BlockSpec((1, 1, tab_w), lambda h: (h, 0, 0)),
            pl.BlockSpec((1, 1, N_BUCKETS), lambda h: (h, 0, 0)),
        ],
        out_specs=[
            pl.BlockSpec((N_VAR_A, 1, GROUP_Q, BAND_A), lambda h: (0, h, 0, 0)),
            pl.BlockSpec((N_VAR_B, 1, GROUP_Q, BAND_B), lambda h: (0, h, 0, 0)),
            pl.BlockSpec((1, s_len, la + s_len), lambda h: (h, 0, 0)),
            pl.BlockSpec((1, s_len, lb + s_len), lambda h: (h, 0, 0)),
        ],
        out_shape=[
            jax.ShapeDtypeStruct((N_VAR_A, N_HEADS, GROUP_Q, BAND_A), F32),
            jax.ShapeDtypeStruct((N_VAR_B, N_HEADS, GROUP_Q, BAND_B), F32),
            jax.ShapeDtypeStruct((N_HEADS, s_len, la + s_len), F32),
            jax.ShapeDtypeStruct((N_HEADS, s_len, lb + s_len), F32),
        ],
        name="bias_tables",
    )(rel_p, t5_p)


FF_CHUNKS = 2


def _swiglu(h, wg_ref, wu_ref, wd_ref):
    slab = D_FF // FF_CHUNKS
    acc = None
    for c in range(FF_CHUNKS):
        cols = slice(c * slab, (c + 1) * slab)
        g = jnp.dot(h, wg_ref[:, cols], preferred_element_type=F32)
        u = jnp.dot(h, wu_ref[:, cols], preferred_element_type=F32)
        a = (_silu(g) * u).astype(BF16)
        part = jnp.dot(a, wd_ref[cols, :], preferred_element_type=F32)
        acc = part if acc is None else acc + part
    return acc


def _ffn1_qkv_kernel(x_ref, mod_ref, gains_ref, wg_ref, wu_ref, wd_ref, win_ref,
                     x1_ref, qa_ref, ka_ref, va_ref, qb_ref, kb_ref, vb_ref,
                     ka32_ref, va32_ref, kb32_ref, vb32_ref):
    gt, rt, _ = x_ref.shape
    tm = gt * rt
    x = x_ref[...]
    gain = lambda k: gains_ref[k:k + 1, :]
    h = _adaln(x, gain(0), mod_ref[0], mod_ref[1]).reshape(tm, D_MODEL).astype(BF16)
    f = _swiglu(h, wg_ref, wu_ref, wd_ref).reshape(gt, rt, D_MODEL)
    x1 = x + (FFN_RES * mod_ref[2]) * _rms(f, gain(1))
    x1_ref[...] = x1
    h2 = _adaln(x1, gain(2), mod_ref[3], mod_ref[4]).reshape(tm, D_MODEL).astype(BF16)

    def proj(lo, width):
        return jnp.dot(h2, win_ref[:, lo:lo + width], preferred_element_type=F32)

    qa_ref[...] = proj(0, WIDTH).astype(BF16)
    ka = proj(WIDTH, WIDTH)
    va = proj(2 * WIDTH, WIDTH)
    qb_ref[...] = proj(3 * WIDTH, WIDTH).astype(BF16)
    kb = proj(4 * WIDTH, KV_WIDTH_B)
    vb = proj(4 * WIDTH + KV_WIDTH_B, KV_WIDTH_B)
    ka_ref[...] = ka.astype(BF16)
    va_ref[...] = va.astype(BF16)
    kb_ref[...] = kb.astype(BF16)
    vb_ref[...] = vb.astype(BF16)

    @pl.when(pl.program_id(1) == pl.num_programs(1) - 1)
    def _():
        ta = ka32_ref.shape[0]
        tb = kb32_ref.shape[0]
        ka32_ref[...] = ka[tm - ta:, :]
        va32_ref[...] = va[tm - ta:, :]
        kb32_ref[...] = kb[tm - tb:, :]
        vb32_ref[...] = vb[tm - tb:, :]


def _resident(shape):
    return pl.BlockSpec(shape, lambda *_: (0,) * len(shape), pipeline_mode=pl.Buffered(1))


def _ffn1_qkv(x, mod, gains, wg, wu, wd, win, gt, rt):
    n_groups, rows, _ = x.shape
    n_g, n_r = n_groups // gt, rows // rt
    tm = gt * rt
    tokens = n_groups * rows
    ta = gt * min(REACH_A, rows)
    tb = gt * min(REACH_B, rows)
    tile = lambda gi, ri: (gi * n_r + ri, 0)
    tail = lambda gi, ri: (gi, 0)
    tok = lambda w, dt: jax.ShapeDtypeStruct((tokens, w), dt)
    vmem = (2 * 3 * D_MODEL * D_FF + 2 * D_MODEL * win.shape[1]
            + 2 * 2 * 4 * tm * D_MODEL
            + 2 * 2 * tm * (3 * WIDTH + WIDTH + 2 * KV_WIDTH_B)
            + 2 * 4 * (2 * ta * WIDTH + 2 * tb * KV_WIDTH_B)
            + tm * (2 * 4 + 2) * (D_FF // FF_CHUNKS) + 3 * 4 * tm * D_MODEL
            + 4 * tm * win.shape[1])
    return pl.pallas_call(
        _ffn1_qkv_kernel,
        grid=(n_g, n_r),
        in_specs=[
            pl.BlockSpec((gt, rt, D_MODEL), lambda gi, ri: (gi, ri, 0)),
            pl.BlockSpec((N_MOD, gt, 1, D_MODEL), lambda gi, ri: (0, gi, 0, 0)),
            _resident(gains.shape),
            _resident(wg.shape), _resident(wu.shape), _resident(wd.shape), _resident(win.shape),
        ],
        out_specs=[
            pl.BlockSpec((gt, rt, D_MODEL), lambda gi, ri: (gi, ri, 0)),
            pl.BlockSpec((tm, WIDTH), tile), pl.BlockSpec((tm, WIDTH), tile), pl.BlockSpec((tm, WIDTH), tile),
            pl.BlockSpec((tm, WIDTH), tile), pl.BlockSpec((tm, KV_WIDTH_B), tile),
            pl.BlockSpec((tm, KV_WIDTH_B), tile),
            pl.BlockSpec((ta, WIDTH), tail), pl.BlockSpec((ta, WIDTH), tail),
            pl.BlockSpec((tb, KV_WIDTH_B), tail), pl.BlockSpec((tb, KV_WIDTH_B), tail),
        ],
        out_shape=[
            jax.ShapeDtypeStruct(x.shape, F32),
            tok(WIDTH, BF16), tok(WIDTH, BF16), tok(WIDTH, BF16),
            tok(WIDTH, BF16), tok(KV_WIDTH_B, BF16), tok(KV_WIDTH_B, BF16),
            jax.ShapeDtypeStruct((n_g * ta, WIDTH), F32), jax.ShapeDtypeStruct((n_g * ta, WIDTH), F32),
            jax.ShapeDtypeStruct((n_g * tb, KV_WIDTH_B), F32), jax.ShapeDtypeStruct((n_g * tb, KV_WIDTH_B), F32),
        ],
        compiler_params=pltpu.CompilerParams(
            dimension_semantics=("arbitrary", "arbitrary"), vmem_limit_bytes=_vmem_limit(vmem)),
        name="ffn1_qkv",
    )(x, mod, gains, wg, wu, wd, win)


def _out_ffn2_kernel(x_ref, o_ref, mod_ref, gains_ref, gg_ref, wout_ref, wg_ref, wu_ref, wd_ref, y_ref):
    gt, rt, _ = x_ref.shape
    tm = gt * rt
    gain = lambda k: gains_ref[k:k + 1, :]
    o = o_ref[...].astype(F32)
    gg = gg_ref[...]
    on = jnp.concatenate([_rms(o[:, :WIDTH], gg[:, :WIDTH]), _rms(o[:, WIDTH:], gg[:, WIDTH:])], axis=1)
    mixed = jnp.dot(on.astype(BF16), wout_ref[...], preferred_element_type=F32).reshape(gt, rt, D_MODEL)
    x2 = x_ref[...] + mod_ref[5] * _rms(mixed, gain(3))
    h = _adaln(x2, gain(4), mod_ref[6], mod_ref[7]).reshape(tm, D_MODEL).astype(BF16)
    f = _swiglu(h, wg_ref, wu_ref, wd_ref).reshape(gt, rt, D_MODEL)
    y_ref[...] = x2 + (FFN_RES * mod_ref[8]) * _rms(f, gain(5))


def _out_ffn2(x1, o, mod, gains, gg, wout, wg, wu, wd, gt, rt):
    n_groups, rows, _ = x1.shape
    n_g, n_r = n_groups // gt, rows // rt
    tm = gt * rt
    vmem = (2 * 3 * D_MODEL * D_FF + 2 * wout.size
            + 2 * 2 * 4 * tm * D_MODEL + 2 * 2 * tm * 2 * WIDTH
            + tm * (2 * 4 + 2) * (D_FF // FF_CHUNKS) + 5 * 4 * tm * D_MODEL)
    return pl.pallas_call(
        _out_ffn2_kernel,
        grid=(n_g, n_r),
        in_specs=[
            pl.BlockSpec((gt, rt, D_MODEL), lambda gi, ri: (gi, ri, 0)),
            pl.BlockSpec((tm, 2 * WIDTH), lambda gi, ri: (gi * n_r + ri, 0)),
            pl.BlockSpec((N_MOD, gt, 1, D_MODEL), lambda gi, ri: (0, gi, 0, 0)),
            _resident(gains.shape), _resident(gg.shape), _resident(wout.shape),
            _resident(wg.shape), _resident(wu.shape), _resident(wd.shape),
        ],
        out_specs=pl.BlockSpec((gt, rt, D_MODEL), lambda gi, ri: (gi, ri, 0)),
        out_shape=jax.ShapeDtypeStruct(x1.shape, F32),
        compiler_params=pltpu.CompilerParams(
            dimension_semantics=("arbitrary", "arbitrary"), vmem_limit_bytes=_vmem_limit(vmem)),
        name="out_ffn2",
    )(x1, o, mod, gains, gg, wout, wg, wu, wd)


def _pair_attention(q, k, v, bias_lo, bias_hi, sink=None):
    n_q = q.shape[0]
    qf = q.astype(F32)
    lane = lax.broadcasted_iota(jnp.int32, qf.shape, 1)
    is_lo = lane < HEAD_DIM
    q2 = jnp.concatenate([jnp.where(is_lo, qf, 0.0), jnp.where(is_lo, 0.0, qf)], axis=0).astype(BF16)
    s = lax.dot_general(q2, k, (((1,), (1,)), ((), ())), preferred_element_type=F32)
    s = s + jnp.concatenate([bias_lo, bias_hi], axis=0)
    m = jnp.max(s, axis=1, keepdims=True)
    if sink is not None:
        m = jnp.maximum(m, sink)
    e = jnp.exp(s - m)
    denom = jnp.sum(e, axis=1, keepdims=True)
    if sink is not None:
        denom = denom + jnp.exp(sink - m)
    z = jnp.dot(e.astype(BF16), v, preferred_element_type=F32) * (1.0 / denom)
    return jnp.where(is_lo, z[:n_q], z[n_q:])


def _sink_column(sinks_ref, pair, n_q):
    row = lax.broadcasted_iota(jnp.int32, (2 * n_q, 1), 0)
    return jnp.where(row < n_q, sinks_ref[pair], sinks_ref[N_PAIRS + pair])


def _attn_prompt_kernel(sinks_ref, qa_ref, qb_ref, ka_ref, va_ref, kb_ref, vb_ref, ba_ref, bb_ref, o_ref):
    g = pl.program_id(1)
    start_a = pl.multiple_of(jnp.maximum(g * GROUP_Q - REACH_A, 0), GROUP_Q)
    start_b = pl.multiple_of(jnp.maximum(g * GROUP_Q - REACH_B, 0), GROUP_Q)
    for p in range(N_PAIRS):
        lanes = slice(p * V7X_LANES, (p + 1) * V7X_LANES)
        o = _pair_attention(qa_ref[:, lanes],
                            ka_ref[pl.ds(start_a, BAND_A), lanes], va_ref[pl.ds(start_a, BAND_A), lanes],
                            ba_ref[0, 2 * p], ba_ref[0, 2 * p + 1])
        o_ref[:, lanes] = o.astype(BF16)
    kb = kb_ref[pl.ds(start_b, BAND_B), :]
    vb = vb_ref[pl.ds(start_b, BAND_B), :]
    for p in range(N_PAIRS):
        lanes = slice(p * V7X_LANES, (p + 1) * V7X_LANES)
        o = _pair_attention(qb_ref[:, lanes], kb, vb, bb_ref[0, p], bb_ref[0, N_PAIRS + p],
                            _sink_column(sinks_ref, p, GROUP_Q))
        o_ref[:, WIDTH + p * V7X_LANES:WIDTH + (p + 1) * V7X_LANES] = o.astype(BF16)


def _attn_prompt(sinks, qa, qb, ka, va, kb, vb, bias_a, bias_b, batch, seq):
    n_grp = seq // GROUP_Q
    q_spec = pl.BlockSpec((GROUP_Q, WIDTH), lambda b, g: (b * n_grp + g, 0))
    kv_spec = lambda w: pl.BlockSpec((seq, w), lambda b, g: (b, 0), pipeline_mode=pl.Buffered(1))
    vmem = (2 * seq * (2 * WIDTH + 2 * KV_WIDTH_B)
            + 2 * 4 * N_HEADS * GROUP_Q * (BAND_A + BAND_B)
            + 2 * 2 * GROUP_Q * 4 * WIDTH + 16 * 4 * 2 * GROUP_Q * BAND_A)
    return pl.pallas_call(
        _attn_prompt_kernel,
        grid=(batch, n_grp),
        in_specs=[
            pl.BlockSpec(memory_space=pltpu.SMEM),
            q_spec, q_spec,
            kv_spec(WIDTH), kv_spec(WIDTH), kv_spec(KV_WIDTH_B), kv_spec(KV_WIDTH_B),
            pl.BlockSpec((1, N_HEADS, GROUP_Q, BAND_A),
                         lambda b, g: (jnp.minimum(g, N_VAR_A - 1), 0, 0, 0)),
            pl.BlockSpec((1, N_HEADS, GROUP_Q, BAND_B),
                         lambda b, g: (jnp.minimum(g, N_VAR_B - 1), 0, 0, 0)),
        ],
        out_specs=pl.BlockSpec((GROUP_Q, 2 * WIDTH), lambda b, g: (b * n_grp + g, 0)),
        out_shape=jax.ShapeDtypeStruct((batch * seq, 2 * WIDTH), BF16),
        compiler_params=pltpu.CompilerParams(
            dimension_semantics=("arbitrary", "arbitrary"), vmem_limit_bytes=_vmem_limit(vmem)),
        name="attn_prompt",
    )(sinks, qa, qb, ka, va, kb, vb, bias_a, bias_b)


def _attn_sample_kernel(sinks_ref, qa_ref, qb_ref, ka_ref, va_ref, kb_ref, vb_ref,
                        cak_ref, cav_ref, cbk_ref, cbv_ref, ba_ref, bb_ref, o_ref):
    n_q = qa_ref.shape[0]
    for p in range(N_PAIRS):
        lanes = slice(p * V7X_LANES, (p + 1) * V7X_LANES)
        k = jnp.concatenate([cak_ref[0, :, lanes].astype(BF16), ka_ref[:, lanes]], axis=0)
        v = jnp.concatenate([cav_ref[0, :, lanes].astype(BF16), va_ref[:, lanes]], axis=0)
        o = _pair_attention(qa_ref[:, lanes], k, v, ba_ref[2 * p], ba_ref[2 * p + 1])
        o_ref[:, lanes] = o.astype(BF16)
    kb = jnp.concatenate([cbk_ref[0].astype(BF16), kb_ref[...]], axis=0)
    vb = jnp.concatenate([cbv_ref[0].astype(BF16), vb_ref[...]], axis=0)
    for p in range(N_PAIRS):
        lanes = slice(p * V7X_LANES, (p + 1) * V7X_LANES)
        o = _pair_attention(qb_ref[:, lanes], kb, vb, bb_ref[p], bb_ref[N_PAIRS + p],
                            _sink_column(sinks_ref, p, n_q))
        o_ref[:, WIDTH + p * V7X_LANES:WIDTH + (p + 1) * V7X_LANES] = o.astype(BF16)


def _attn_sample(sinks, qa, qb, ka, va, kb, vb, cak, cav, cbk, cbv, bias_a, bias_b, batch, s_len):
    la, lb = cak.shape[1], cbk.shape[1]
    row = lambda w: pl.BlockSpec((s_len, w), lambda b: (b, 0))
    cache = lambda length, w: pl.BlockSpec((1, length, w), lambda b: (b, 0, 0))
    whole = lambda a: pl.BlockSpec(a.shape, lambda b: (0,) * a.ndim)
    return pl.pallas_call(
        _attn_sample_kernel,
        grid=(batch,),
        in_specs=[
            pl.BlockSpec(memory_space=pltpu.SMEM),
            row(WIDTH), row(WIDTH), row(WIDTH), row(WIDTH), row(KV_WIDTH_B), row(KV_WIDTH_B),
            cache(la, WIDTH), cache(la, WIDTH), cache(lb, KV_WIDTH_B), cache(lb, KV_WIDTH_B),
            whole(bias_a), whole(bias_b),
        ],
        out_specs=pl.BlockSpec((s_len, 2 * WIDTH), lambda b: (b, 0)),
        out_shape=jax.ShapeDtypeStruct((batch * s_len, 2 * WIDTH), BF16),
        compiler_params=pltpu.CompilerParams(dimension_semantics=("arbitrary",)),
        name="attn_sample",
    )(sinks, qa, qb, ka, va, kb, vb, cak, cav, cbk, cbv, bias_a, bias_b)


def _pair_order():
    heads = [h for p in range(N_PAIRS) for h in (p, N_PAIRS + p)]
    return jnp.asarray([h * HEAD_DIM + d for h in heads for d in range(HEAD_DIM)], jnp.int32)


def _prep_weights(w_in, w_out, group_gains):
    order = _pair_order()
    qa, ka, va, qb, kvb = jnp.split(w_in, (WIDTH, 2 * WIDTH, 3 * WIDTH, 4 * WIDTH), axis=1)
    win = jnp.concatenate([qa * SCALE, ka, va, qb[:, order] * SCALE, kvb], axis=1).astype(BF16)
    rows = jnp.concatenate([jnp.arange(WIDTH, dtype=jnp.int32), WIDTH + order])
    return win, w_out[rows].astype(BF16), group_gains[rows].reshape(1, 2 * WIDTH)


def kernel(x_prompt, x_sample, cache_a_k, cache_a_v, cache_b_k, cache_b_v, c_prompt, c_sample, w_mod, b_mod,
           norm_gains, w1_gate, w1_up, w1_down, w_in, w_out, group_gains, rel_bias_a, t5_bias_table, sinks_b,
           w2_gate, w2_up, w2_down):
    depth = w_mod.shape[0]
    assert depth == 1
    batch, seq, _ = x_prompt.shape
    s_batch, s_len, _ = x_sample.shape
    la, lb = cache_a_k.shape[2], cache_b_k.shape[2]
    assert la == REACH_A and lb == REACH_B and seq % 512 == 0

    c_rows = jnp.concatenate([c_prompt, c_sample], axis=0)
    pad_rows = (-c_rows.shape[0]) % 8
    c_rows = jnp.pad(c_rows, ((0, pad_rows), (0, 0)))
    mod = _modulation(c_rows, w_mod[0], b_mod[0])
    mod_p = mod[:, :batch].reshape(N_MOD, batch, 1, D_MODEL)
    mod_s = mod[:, batch:batch + s_batch].reshape(N_MOD, s_batch, 1, D_MODEL)

    bias_a, bias_b, bias_sa, bias_sb = _bias_tables(rel_bias_a[0], t5_bias_table, s_len, la, lb)

    gains = norm_gains[0]
    w1 = (w1_gate[0].astype(BF16), w1_up[0].astype(BF16), w1_down[0].astype(BF16))
    w2 = (w2_gate[0].astype(BF16), w2_up[0].astype(BF16), w2_down[0].astype(BF16))
    win, wout, gg = _prep_weights(w_in[0], w_out[0], group_gains[0])
    sinks = sinks_b[0]

    rt_p = 512
    (x1p, qa, ka, va, qb, kb, vb, ka32, va32, kb32, vb32) = _ffn1_qkv(
        x_prompt, mod_p, gains, *w1, win, 1, rt_p)
    op = _attn_prompt(sinks, qa, qb, ka, va, kb, vb, bias_a, bias_b, batch, seq)
    y_prompt = _out_ffn2(x1p, op, mod_p, gains, gg, wout, *w2, 1, rt_p)
    new_p = (ka32.reshape(depth, batch, la, N_HEADS, HEAD_DIM), va32.reshape(depth, batch, la, N_HEADS, HEAD_DIM),
             kb32.reshape(depth, batch, lb, 2, HEAD_DIM), vb32.reshape(depth, batch, lb, 2, HEAD_DIM))

    gt_s = 16
    (x1s, qas, kas, vas, qbs, kbs, vbs, kas32, vas32, kbs32, vbs32) = _ffn1_qkv(
        x_sample, mod_s, gains, *w1, win, gt_s, s_len)
    os_ = _attn_sample(sinks, qas, qbs, kas, vas, kbs, vbs,
                       cache_a_k[0].reshape(s_batch, la, WIDTH), cache_a_v[0].reshape(s_batch, la, WIDTH),
                       cache_b_k[0].reshape(s_batch, lb, KV_WIDTH_B), cache_b_v[0].reshape(s_batch, lb, KV_WIDTH_B),
                       bias_sa, bias_sb, s_batch, s_len)
    y_sample = _out_ffn2(x1s, os_, mod_s, gains, gg, wout, *w2, gt_s, s_len)
    new_s = (kas32.reshape(depth, s_batch, s_len, N_HEADS, HEAD_DIM),
             vas32.reshape(depth, s_batch, s_len, N_HEADS, HEAD_DIM),
             kbs32.reshape(depth, s_batch, s_len, 2, HEAD_DIM), vbs32.reshape(depth, s_batch, s_len, 2, HEAD_DIM))

    return (y_prompt, y_sample) + new_p + new_s
```

```python
import functools

import jax
import jax.numpy as jnp
from jax import lax
from jax.experimental import pallas as pl
from jax.experimental.pallas import tpu as pltpu

F32 = jnp.float32
BF16 = jnp.bfloat16

D_MODEL = 1024
D_FF = 2816
CHUNK = 64
HEAD_DIM = 64
N_HEADS = 8
N_PAIRS = N_HEADS // 2
WIDTH = N_HEADS * HEAD_DIM
KV_WIDTH_B = 2 * HEAD_DIM
REACH_A = 8 * CHUNK
REACH_B = 2 * CHUNK
REL_CLIP_A = 128
N_BUCKETS = 32
N_MOD = 9
FFN_RES = 0.5
EPS = 1e-6
SCALE = HEAD_DIM ** -0.5
NEG_INF = -1e30

V7X_LANES = 128
V7X_VMEM_BYTES = 64 * 1024 * 1024
V7X_VMEM_LIMIT_CAP = 56 * 1024 * 1024

GROUP_Q = 2 * CHUNK
BAND_A = REACH_A + GROUP_Q
BAND_B = REACH_B + GROUP_Q
ORIGIN_A = 1024
WIDTH_MASTER_A = ORIGIN_A + BAND_A
ORIGIN_B = 256
WIDTH_MASTER_B = ORIGIN_B + BAND_B
N_VAR_A = REACH_A // GROUP_Q + 1
N_VAR_B = REACH_B // GROUP_Q + 1


def _vmem_limit(estimate_bytes):
    return int(min(max(estimate_bytes, 32 * 1024 * 1024), V7X_VMEM_LIMIT_CAP))


def _rms(x, gain):
    ms = jnp.mean(x * x, axis=-1, keepdims=True)
    return x * lax.rsqrt(ms + EPS) * gain


def _adaln(x, gain, shift, scale):
    return _rms(x, gain) * (1.0 + scale) + shift


def _silu(x):
    return x * (1.0 / (1.0 + jnp.exp(-x)))


def _mod_kernel(c_ref, w_ref, b_ref, o_ref):
    s = _silu(c_ref[...]).astype(BF16)
    o_ref[0] = jnp.dot(s, w_ref[...].astype(BF16), preferred_element_type=F32) + b_ref[0]


def _modulation(c_rows, w_mod, b_mod):
    rows = c_rows.shape[0]
    return pl.pallas_call(
        _mod_kernel,
        grid=(N_MOD,),
        in_specs=[
            pl.BlockSpec((rows, D_MODEL), lambda j: (0, 0)),
            pl.BlockSpec((D_MODEL, D_MODEL), lambda j: (0, j)),
            pl.BlockSpec((1, 1, D_MODEL), lambda j: (j, 0, 0)),
        ],
        out_specs=pl.BlockSpec((1, rows, D_MODEL), lambda j: (j, 0, 0)),
        out_shape=jax.ShapeDtypeStruct((N_MOD, rows, D_MODEL), F32),
        name="modulation",
    )(c_rows, w_mod, b_mod.reshape(N_MOD, 1, D_MODEL))


def _skew(base_row, n_rows):
    width = base_row.shape[1]
    m = jnp.broadcast_to(base_row, (n_rows, width))
    row = lax.broadcasted_iota(jnp.int32, (n_rows, width), 0)
    shift = 1
    while shift < n_rows:
        m = jnp.where((row & shift) != 0, pltpu.roll(m, shift, axis=1), m)
        shift *= 2
    return m


def _band_mask(n_rows, width, origin, n_prev):
    row = lax.broadcasted_iota(jnp.int32, (n_rows, width), 0)
    col = lax.broadcasted_iota(jnp.int32, (n_rows, width), 1)
    q_chunk = row >> 6
    k_chunk = (col >> 6) - (origin // CHUNK)
    return (k_chunk >= q_chunk - n_prev) & (k_chunk <= q_chunk)


def _t5_bucket(rel):
    half = N_BUCKETS // 2
    max_exact = half // 2
    n = jnp.abs(rel)
    n2 = n * n
    large = jnp.full(rel.shape, max_exact, jnp.int32)
    for k in range(1, half - max_exact):
        large = large + jnp.where(n2 >= (max_exact * max_exact) * (2 ** k), 1, 0)
    return jnp.where(rel > 0, half, 0) + jnp.where(n < max_exact, n, large)


def _bias_kernel(rel_ref, t5_ref, ba_ref, bb_ref, bsa_ref, bsb_ref, *, s_len, la, lb):
    tab = rel_ref[0]
    n_tab = 2 * REL_CLIP_A + 1
    padded = jnp.concatenate([tab, jnp.zeros((1, WIDTH_MASTER_A - tab.shape[1]), F32)], axis=1)
    rolled = pltpu.roll(padded, ORIGIN_A - REL_CLIP_A, axis=1)
    col = lax.broadcasted_iota(jnp.int32, (1, WIDTH_MASTER_A), 1)
    u_a = jnp.where(col < ORIGIN_A - REL_CLIP_A, tab[:, 0:1],
                    jnp.where(col > ORIGIN_A + REL_CLIP_A, tab[:, n_tab - 1:n_tab], rolled))
    skew_a = _skew(u_a, GROUP_Q)
    master_a = jnp.where(_band_mask(GROUP_Q, WIDTH_MASTER_A, ORIGIN_A, REACH_A // CHUNK), skew_a, NEG_INF)
    for v in range(N_VAR_A):
        start = ORIGIN_A - min(v * GROUP_Q, REACH_A) if v < N_VAR_A - 1 else ORIGIN_A - REACH_A
        ba_ref[v, 0] = master_a[:, start:start + BAND_A]
    bsa_ref[0] = skew_a[:s_len, ORIGIN_A - la:ORIGIN_A + s_len]

    t5 = t5_ref[0]
    rel = lax.broadcasted_iota(jnp.int32, (1, WIDTH_MASTER_B), 1) - ORIGIN_B
    bucket = _t5_bucket(rel)
    u_b = jnp.zeros((1, WIDTH_MASTER_B), F32)
    for i in range(N_BUCKETS):
        u_b = jnp.where(bucket == i, t5[:, i:i + 1], u_b)
    skew_b = _skew(u_b, GROUP_Q)
    master_b = jnp.where(_band_mask(GROUP_Q, WIDTH_MASTER_B, ORIGIN_B, REACH_B // CHUNK), skew_b, NEG_INF)
    for v in range(N_VAR_B):
        start = ORIGIN_B - min(v * GROUP_Q, REACH_B) if v < N_VAR_B - 1 else ORIGIN_B - REACH_B
        bb_ref[v, 0] = master_b[:, start:start + BAND_B]
    bsb_ref[0] = skew_b[:s_len, ORIGIN_B - lb:ORIGIN_B + s_len]


def _bias_tables(rel_bias_a, t5_table, s_len, la, lb):
    n_tab = 2 * REL_CLIP_A + 1
    tab_w = 3 * V7X_LANES
    rel_p = jnp.pad(rel_bias_a, ((0, 0), (0, tab_w - n_tab))).reshape(N_HEADS, 1, tab_w)
    t5_p = t5_table.reshape(N_HEADS, 1, N_BUCKETS)
    kern = functools.partial(_bias_kernel, s_len=s_len, la=la, lb=lb)
    return pl.pallas_call(
        kern,
        grid=(N_HEADS,),
        in_specs=[
            pl.BlockSpec((1, 1, tab_w), lambda h: (h, 0, 0)),
            pl.BlockSpec((1, 1, N_BUCKETS), lambda h: (h, 0, 0)),
        ],
        out_specs=[
            pl.BlockSpec((N_VAR_A, 1, GROUP_Q, BAND_A), lambda h: (0, h, 0, 0)),
            pl.BlockSpec((N_VAR_B, 1, GROUP_Q, BAND_B), lambda h: (0, h, 0, 0)),
            pl.BlockSpec((1, s_len, la + s_len), lambda h: (h, 0, 0)),
            pl.BlockSpec((1, s_len, lb + s_len), lambda h: (h, 0, 0)),
        ],
        out_shape=[
            jax.ShapeDtypeStruct((N_VAR_A, N_HEADS, GROUP_Q, BAND_A), F32),
            jax.ShapeDtypeStruct((N_VAR_B, N_HEADS, GROUP_Q, BAND_B), F32),
            jax.ShapeDtypeStruct((N_HEADS, s_len, la + s_len), F32),
            jax.ShapeDtypeStruct((N_HEADS, s_len, lb + s_len), F32),
        ],
        name="bias_tables",
    )(rel_p, t5_p)


V7X_MXU_DIM = 256
FF_SLABS = ((0, 6 * V7X_MXU_DIM), (6 * V7X_MXU_DIM, D_FF))
FF_SLAB_MAX = max(hi - lo for lo, hi in FF_SLABS)


def _swiglu(h, wg_ref, wu_ref, wd_ref):
    acc = None
    for lo, hi in FF_SLABS:
        cols = slice(lo, hi)
        g = jnp.dot(h, wg_ref[:, cols], preferred_element_type=F32)
        u = jnp.dot(h, wu_ref[:, cols], preferred_element_type=F32)
        a = (_silu(g) * u).astype(BF16)
        part = jnp.dot(a, wd_ref[cols, :], preferred_element_type=F32)
        acc = part if acc is None else acc + part
    return acc


N_SUB = 2


def _sub_tiles(gt, rt):
    if gt == 1:
        step = rt // N_SUB
        return [(slice(0, 1), slice(s * step, (s + 1) * step), s * step, step) for s in range(N_SUB)]
    step = gt // N_SUB
    return [(slice(s * step, (s + 1) * step), slice(0, rt), s * step * rt, step * rt) for s in range(N_SUB)]


def _ffn1_qkv_kernel(x_ref, mod_ref, gains_ref, wg_ref, wu_ref, wd_ref, win_ref,
                     x1_ref, qa_ref, ka_ref, va_ref, qb_ref, kb_ref, vb_ref,
                     ka32_ref, va32_ref, kb32_ref, vb32_ref):
    gt, rt, _ = x_ref.shape
    tm = gt * rt
    gain = lambda k: gains_ref[k:k + 1, :]

    def store_tail(ref, val, r0):
        first = tm - ref.shape[0]
        lo = max(r0, first)
        if lo < r0 + val.shape[0]:
            ref[lo - first:r0 + val.shape[0] - first, :] = val[lo - r0:, :]

    for gs, rs, r0, n in _sub_tiles(gt, rt):
        rows = slice(r0, r0 + n)
        mod = lambda k: mod_ref[k, gs]
        x = x_ref[gs, rs, :]
        h = _adaln(x, gain(0), mod(0), mod(1)).reshape(n, D_MODEL).astype(BF16)
        f = _swiglu(h, wg_ref, wu_ref, wd_ref).reshape(x.shape)
        x1 = x + (FFN_RES * mod(2)) * _rms(f, gain(1))
        x1_ref[gs, rs, :] = x1
        h2 = _adaln(x1, gain(2), mod(3), mod(4)).reshape(n, D_MODEL).astype(BF16)

        def proj(lo, width):
            return jnp.dot(h2, win_ref[:, lo:lo + width], preferred_element_type=F32)

        qa_ref[rows, :] = proj(0, WIDTH).astype(BF16)
        ka = proj(WIDTH, WIDTH)
        va = proj(2 * WIDTH, WIDTH)
        qb_ref[rows, :] = proj(3 * WIDTH, WIDTH).astype(BF16)
        kvb = proj(4 * WIDTH, 2 * KV_WIDTH_B)
        kb, vb = kvb[:, :KV_WIDTH_B], kvb[:, KV_WIDTH_B:]
        ka_ref[rows, :] = ka.astype(BF16)
        va_ref[rows, :] = va.astype(BF16)
        kb_ref[rows, :] = kb.astype(BF16)
        vb_ref[rows, :] = vb.astype(BF16)
        store_tail(ka32_ref, ka, r0)
        store_tail(va32_ref, va, r0)
        store_tail(kb32_ref, kb, r0)
        store_tail(vb32_ref, vb, r0)


def _resident(shape):
    return pl.BlockSpec(shape, lambda *_: (0,) * len(shape), pipeline_mode=pl.Buffered(1))


def _ffn1_qkv(x, mod, gains, wg, wu, wd, win, gt, rt):
    n_groups, rows, _ = x.shape
    n_g, n_r = n_groups // gt, rows // rt
    tm = gt * rt
    tokens = n_groups * rows
    ta = gt * min(REACH_A, rows)
    tb = gt * min(REACH_B, rows)
    tile = lambda gi, ri: (gi * n_r + ri, 0)
    tail = lambda gi, ri: (gi, 0)
    tok = lambda w, dt: jax.ShapeDtypeStruct((tokens, w), dt)
    vmem = (2 * 3 * D_MODEL * D_FF + 2 * D_MODEL * win.shape[1]
            + 2 * 2 * 4 * tm * D_MODEL
            + 2 * 2 * tm * (3 * WIDTH + WIDTH + 2 * KV_WIDTH_B)
            + 2 * 4 * (2 * ta * WIDTH + 2 * tb * KV_WIDTH_B)
            + tm * (2 * 4 + 2) * FF_SLAB_MAX + 3 * 4 * tm * D_MODEL
            + 4 * tm * win.shape[1])
    return pl.pallas_call(
        _ffn1_qkv_kernel,
        grid=(n_g, n_r),
        in_specs=[
            pl.BlockSpec((gt, rt, D_MODEL), lambda gi, ri: (gi, ri, 0)),
            pl.BlockSpec((N_MOD, gt, 1, D_MODEL), lambda gi, ri: (0, gi, 0, 0)),
            _resident(gains.shape),
            _resident(wg.shape), _resident(wu.shape), _resident(wd.shape), _resident(win.shape),
        ],
        out_specs=[
            pl.BlockSpec((gt, rt, D_MODEL), lambda gi, ri: (gi, ri, 0)),
            pl.BlockSpec((tm, WIDTH), tile), pl.BlockSpec((tm, WIDTH), tile), pl.BlockSpec((tm, WIDTH), tile),
            pl.BlockSpec((tm, WIDTH), tile), pl.BlockSpec((tm, KV_WIDTH_B), tile),
            pl.BlockSpec((tm, KV_WIDTH_B), tile),
            pl.BlockSpec((ta, WIDTH), tail), pl.BlockSpec((ta, WIDTH), tail),
            pl.BlockSpec((tb, KV_WIDTH_B), tail), pl.BlockSpec((tb, KV_WIDTH_B), tail),
        ],
        out_shape=[
            jax.ShapeDtypeStruct(x.shape, F32),
            tok(WIDTH, BF16), tok(WIDTH, BF16), tok(WIDTH, BF16),
            tok(WIDTH, BF16), tok(KV_WIDTH_B, BF16), tok(KV_WIDTH_B, BF16),
            jax.ShapeDtypeStruct((n_g * ta, WIDTH), F32), jax.ShapeDtypeStruct((n_g * ta, WIDTH), F32),
            jax.ShapeDtypeStruct((n_g * tb, KV_WIDTH_B), F32), jax.ShapeDtypeStruct((n_g * tb, KV_WIDTH_B), F32),
        ],
        compiler_params=pltpu.CompilerParams(
            dimension_semantics=("arbitrary", "arbitrary"), vmem_limit_bytes=_vmem_limit(vmem)),
        name="ffn1_qkv",
    )(x, mod, gains, wg, wu, wd, win)


def _out_ffn2_kernel(x_ref, o_ref, mod_ref, gains_ref, gg_ref, wout_ref, wg_ref, wu_ref, wd_ref, y_ref):
    gt, rt, _ = x_ref.shape
    gain = lambda k: gains_ref[k:k + 1, :]
    gg = gg_ref[...]
    for gs, rs, r0, n in _sub_tiles(gt, rt):
        mod = lambda k: mod_ref[k, gs]
        x = x_ref[gs, rs, :]
        o = o_ref[r0:r0 + n, :].astype(F32)
        on = jnp.concatenate([_rms(o[:, :WIDTH], gg[:, :WIDTH]), _rms(o[:, WIDTH:], gg[:, WIDTH:])], axis=1)
        mixed = jnp.dot(on.astype(BF16), wout_ref[...], preferred_element_type=F32).reshape(x.shape)
        x2 = x + mod(5) * _rms(mixed, gain(3))
        h = _adaln(x2, gain(4), mod(6), mod(7)).reshape(n, D_MODEL).astype(BF16)
        f = _swiglu(h, wg_ref, wu_ref, wd_ref).reshape(x.shape)
        y_ref[gs, rs, :] = x2 + (FFN_RES * mod(8)) * _rms(f, gain(5))


def _out_ffn2(x1, o, mod, gains, gg, wout, wg, wu, wd, gt, rt):
    n_groups, rows, _ = x1.shape
    n_g, n_r = n_groups // gt, rows // rt
    tm = gt * rt
    vmem = (2 * 3 * D_MODEL * D_FF + 2 * wout.size
            + 2 * 2 * 4 * tm * D_MODEL + 2 * 2 * tm * 2 * WIDTH
            + tm * (2 * 4 + 2) * FF_SLAB_MAX + 5 * 4 * tm * D_MODEL)
    return pl.pallas_call(
        _out_ffn2_kernel,
        grid=(n_g, n_r),
        in_specs=[
            pl.BlockSpec((gt, rt, D_MODEL), lambda gi, ri: (gi, ri, 0)),
            pl.BlockSpec((tm, 2 * WIDTH), lambda gi, ri: (gi * n_r + ri, 0)),
            pl.BlockSpec((N_MOD, gt, 1, D_MODEL), lambda gi, ri: (0, gi, 0, 0)),
            _resident(gains.shape), _resident(gg.shape), _resident(wout.shape),
            _resident(wg.shape), _resident(wu.shape), _resident(wd.shape),
        ],
        out_specs=pl.BlockSpec((gt, rt, D_MODEL), lambda gi, ri: (gi, ri, 0)),
        out_shape=jax.ShapeDtypeStruct(x1.shape, F32),
        compiler_params=pltpu.CompilerParams(
            dimension_semantics=("arbitrary", "arbitrary"), vmem_limit_bytes=_vmem_limit(vmem)),
        name="out_ffn2",
    )(x1, o, mod, gains, gg, wout, wg, wu, wd)


def _pair_attention(q, k, v, bias_lo, bias_hi, sink=None):
    n_q = q.shape[0]
    qf = q.astype(F32)
    lane = lax.broadcasted_iota(jnp.int32, qf.shape, 1)
    is_lo = lane < HEAD_DIM
    q2 = jnp.concatenate([jnp.where(is_lo, qf, 0.0), jnp.where(is_lo, 0.0, qf)], axis=0).astype(BF16)
    s = lax.dot_general(q2, k, (((1,), (1,)), ((), ())), preferred_element_type=F32)
    s = s + jnp.concatenate([bias_lo, bias_hi], axis=0)
    m = jnp.max(s, axis=1, keepdims=True)
    if sink is not None:
        m = jnp.maximum(m, sink)
    e = jnp.exp(s - m)
    denom = jnp.sum(e, axis=1, keepdims=True)
    if sink is not None:
        denom = denom + jnp.exp(sink - m)
    z = jnp.dot(e.astype(BF16), v, preferred_element_type=F32) * (1.0 / denom)
    return jnp.where(is_lo, z[:n_q], z[n_q:])


def _sink_column(sinks_ref, pair, n_q):
    row = lax.broadcasted_iota(jnp.int32, (2 * n_q, 1), 0)
    return jnp.where(row < n_q, sinks_ref[pair], sinks_ref[N_PAIRS + pair])


def _attn_prompt_kernel(sinks_ref, qa_ref, qb_ref, ka_ref, va_ref, kb_ref, vb_ref, ba_ref, bb_ref, o_ref):
    g = pl.program_id(1)
    start_a = pl.multiple_of(jnp.maximum(g * GROUP_Q - REACH_A, 0), GROUP_Q)
    start_b = pl.multiple_of(jnp.maximum(g * GROUP_Q - REACH_B, 0), GROUP_Q)
    for p in range(N_PAIRS):
        lanes = slice(p * V7X_LANES, (p + 1) * V7X_LANES)
        o = _pair_attention(qa_ref[:, lanes],
                            ka_ref[pl.ds(start_a, BAND_A), lanes], va_ref[pl.ds(start_a, BAND_A), lanes],
                            ba_ref[0, 2 * p], ba_ref[0, 2 * p + 1])
        o_ref[:, lanes] = o.astype(BF16)
    kb = kb_ref[pl.ds(start_b, BAND_B), :]
    vb = vb_ref[pl.ds(start_b, BAND_B), :]
    for p in range(N_PAIRS):
        lanes = slice(p * V7X_LANES, (p + 1) * V7X_LANES)
        o = _pair_attention(qb_ref[:, lanes], kb, vb, bb_ref[0, p], bb_ref[0, N_PAIRS + p],
                            _sink_column(sinks_ref, p, GROUP_Q))
        o_ref[:, WIDTH + p * V7X_LANES:WIDTH + (p + 1) * V7X_LANES] = o.astype(BF16)


def _attn_prompt(sinks, qa, qb, ka, va, kb, vb, bias_a, bias_b, batch, seq):
    n_grp = seq // GROUP_Q
    q_spec = pl.BlockSpec((GROUP_Q, WIDTH), lambda b, g: (b * n_grp + g, 0))
    kv_spec = lambda w: pl.BlockSpec((seq, w), lambda b, g: (b, 0), pipeline_mode=pl.Buffered(1))
    vmem = (2 * seq * (2 * WIDTH + 2 * KV_WIDTH_B)
            + 2 * 4 * N_HEADS * GROUP_Q * (BAND_A + BAND_B)
            + 2 * 2 * GROUP_Q * 4 * WIDTH + 16 * 4 * 2 * GROUP_Q * BAND_A)
    return pl.pallas_call(
        _attn_prompt_kernel,
        grid=(batch, n_grp),
        in_specs=[
            pl.BlockSpec(memory_space=pltpu.SMEM),
            q_spec, q_spec,
            kv_spec(WIDTH), kv_spec(WIDTH), kv_spec(KV_WIDTH_B), kv_spec(KV_WIDTH_B),
            pl.BlockSpec((1, N_HEADS, GROUP_Q, BAND_A),
                         lambda b, g: (jnp.minimum(g, N_VAR_A - 1), 0, 0, 0)),
            pl.BlockSpec((1, N_HEADS, GROUP_Q, BAND_B),
                         lambda b, g: (jnp.minimum(g, N_VAR_B - 1), 0, 0, 0)),
        ],
        out_specs=pl.BlockSpec((GROUP_Q, 2 * WIDTH), lambda b, g: (b * n_grp + g, 0)),
        out_shape=jax.ShapeDtypeStruct((batch * seq, 2 * WIDTH), BF16),
        compiler_params=pltpu.CompilerParams(
            dimension_semantics=("arbitrary", "arbitrary"), vmem_limit_bytes=_vmem_limit(vmem)),
        name="attn_prompt",
    )(sinks, qa, qb, ka, va, kb, vb, bias_a, bias_b)


def _attn_sample_kernel(sinks_ref, qa_ref, qb_ref, ka_ref, va_ref, kb_ref, vb_ref,
                        cak_ref, cav_ref, cbk_ref, cbv_ref, ba_ref, bb_ref, o_ref):
    n_q = qa_ref.shape[0]
    for p in range(N_PAIRS):
        lanes = slice(p * V7X_LANES, (p + 1) * V7X_LANES)
        k = jnp.concatenate([cak_ref[0, :, lanes].astype(BF16), ka_ref[:, lanes]], axis=0)
        v = jnp.concatenate([cav_ref[0, :, lanes].astype(BF16), va_ref[:, lanes]], axis=0)
        o = _pair_attention(qa_ref[:, lanes], k, v, ba_ref[2 * p], ba_ref[2 * p + 1])
        o_ref[:, lanes] = o.astype(BF16)
    kb = jnp.concatenate([cbk_ref[0].astype(BF16), kb_ref[...]], axis=0)
    vb = jnp.concatenate([cbv_ref[0].astype(BF16), vb_ref[...]], axis=0)
    for p in range(N_PAIRS):
        lanes = slice(p * V7X_LANES, (p + 1) * V7X_LANES)
        o = _pair_attention(qb_ref[:, lanes], kb, vb, bb_ref[p], bb_ref[N_PAIRS + p],
                            _sink_column(sinks_ref, p, n_q))
        o_ref[:, WIDTH + p * V7X_LANES:WIDTH + (p + 1) * V7X_LANES] = o.astype(BF16)


def _attn_sample(sinks, qa, qb, ka, va, kb, vb, cak, cav, cbk, cbv, bias_a, bias_b, batch, s_len):
    la, lb = cak.shape[1], cbk.shape[1]
    row = lambda w: pl.BlockSpec((s_len, w), lambda b: (b, 0))
    cache = lambda length, w: pl.BlockSpec((1, length, w), lambda b: (b, 0, 0))
    whole = lambda a: pl.BlockSpec(a.shape, lambda b: (0,) * a.ndim)
    return pl.pallas_call(
        _attn_sample_kernel,
        grid=(batch,),
        in_specs=[
            pl.BlockSpec(memory_space=pltpu.SMEM),
            row(WIDTH), row(WIDTH), row(WIDTH), row(WIDTH), row(KV_WIDTH_B), row(KV_WIDTH_B),
            cache(la, WIDTH), cache(la, WIDTH), cache(lb, KV_WIDTH_B), cache(lb, KV_WIDTH_B),
            whole(bias_a), whole(bias_b),
        ],
        out_specs=pl.BlockSpec((s_len, 2 * WIDTH), lambda b: (b, 0)),
        out_shape=jax.ShapeDtypeStruct((batch * s_len, 2 * WIDTH), BF16),
        compiler_params=pltpu.CompilerParams(dimension_semantics=("arbitrary",)),
        name="attn_sample",
    )(sinks, qa, qb, ka, va, kb, vb, cak, cav, cbk, cbv, bias_a, bias_b)


def _pair_order():
    heads = [h for p in range(N_PAIRS) for h in (p, N_PAIRS + p)]
    return jnp.asarray([h * HEAD_DIM + d for h in heads for d in range(HEAD_DIM)], jnp.int32)


def _prep_weights(w_in, w_out, group_gains):
    order = _pair_order()
    qa, ka, va, qb, kvb = jnp.split(w_in, (WIDTH, 2 * WIDTH, 3 * WIDTH, 4 * WIDTH), axis=1)
    win = jnp.concatenate([qa * SCALE, ka, va, qb[:, order] * SCALE, kvb], axis=1).astype(BF16)
    rows = jnp.concatenate([jnp.arange(WIDTH, dtype=jnp.int32), WIDTH + order])
    return win, w_out[rows].astype(BF16), group_gains[rows].reshape(1, 2 * WIDTH)


def kernel(x_prompt, x_sample, cache_a_k, cache_a_v, cache_b_k, cache_b_v, c_prompt, c_sample, w_mod, b_mod,
           norm_gains, w1_gate, w1_up, w1_down, w_in, w_out, group_gains, rel_bias_a, t5_bias_table, sinks_b,
           w2_gate, w2_up, w2_down):
    depth = w_mod.shape[0]
    assert depth == 1
    batch, seq, _ = x_prompt.shape
    s_batch, s_len, _ = x_sample.shape
    la, lb = cache_a_k.shape[2], cache_b_k.shape[2]
    assert la == REACH_A and lb == REACH_B and seq % 512 == 0

    c_rows = jnp.concatenate([c_prompt, c_sample], axis=0)
    pad_rows = (-c_rows.shape[0]) % 8
    c_rows = jnp.pad(c_rows, ((0, pad_rows), (0, 0)))
    mod = _modulation(c_rows, w_mod[0], b_mod[0])
    mod_p = mod[:, :batch].reshape(N_MOD, batch, 1, D_MODEL)
    mod_s = mod[:, batch:batch + s_batch].reshape(N_MOD, s_batch, 1, D_MODEL)

    bias_a, bias_b, bias_sa, bias_sb = _bias_tables(rel_bias_a[0], t5_bias_table, s_len, la, lb)

    gains = norm_gains[0]
    w1 = (w1_gate[0].astype(BF16), w1_up[0].astype(BF16), w1_down[0].astype(BF16))
    w2 = (w2_gate[0].astype(BF16), w2_up[0].astype(BF16), w2_down[0].astype(BF16))
    win, wout, gg = _prep_weights(w_in[0], w_out[0], group_gains[0])
    sinks = sinks_b[0]

    rt_p = 512
    (x1p, qa, ka, va, qb, kb, vb, ka32, va32, kb32, vb32) = _ffn1_qkv(
        x_prompt, mod_p, gains, *w1, win, 1, rt_p)
    op = _attn_prompt(sinks, qa, qb, ka, va, kb, vb, bias_a, bias_b, batch, seq)
    y_prompt = _out_ffn2(x1p, op, mod_p, gains, gg, wout, *w2, 1, rt_p)
    new_p = (ka32.reshape(depth, batch, la, N_HEADS, HEAD_DIM), va32.reshape(depth, batch, la, N_HEADS, HEAD_DIM),
             kb32.reshape(depth, batch, lb, 2, HEAD_DIM), vb32.reshape(depth, batch, lb, 2, HEAD_DIM))

    gt_s = 16
    (x1s, qas, kas, vas, qbs, kbs, vbs, kas32, vas32, kbs32, vbs32) = _ffn1_qkv(
        x_sample, mod_s, gains, *w1, win, gt_s, s_len)
    os_ = _attn_sample(sinks, qas, qbs, kas, vas, kbs, vbs,
                       cache_a_k[0].reshape(s_batch, la, WIDTH), cache_a_v[0].reshape(s_batch, la, WIDTH),
                       cache_b_k[0].reshape(s_batch, lb, KV_WIDTH_B), cache_b_v[0].reshape(s_batch, lb, KV_WIDTH_B),
                       bias_sa, bias_sb, s_batch, s_len)
    y_sample = _out_ffn2(x1s, os_, mod_s, gains, gg, wout, *w2, gt_s, s_len)
    new_s = (kas32.reshape(depth, s_batch, s_len, N_HEADS, HEAD_DIM),
             vas32.reshape(depth, s_batch, s_len, N_HEADS, HEAD_DIM),
             kbs32.reshape(depth, s_batch, s_len, 2, HEAD_DIM), vbs32.reshape(depth, s_batch, s_len, 2, HEAD_DIM))

    return (y_prompt, y_sample) + new_p + new_s
```

```python
import functools

import jax
import jax.numpy as jnp
from jax import lax
from jax.experimental import pallas as pl
from jax.experimental.pallas import tpu as pltpu

F32 = jnp.float32
BF16 = jnp.bfloat16

D_MODEL = 1024
D_FF = 2816
CHUNK = 64
HEAD_DIM = 64
N_HEADS = 8
N_PAIRS = N_HEADS // 2
WIDTH = N_HEADS * HEAD_DIM
KV_WIDTH_B = 2 * HEAD_DIM
REACH_A = 8 * CHUNK
REACH_B = 2 * CHUNK
REL_CLIP_A = 128
N_BUCKETS = 32
N_MOD = 9
FFN_RES = 0.5
EPS = 1e-6
SCALE = HEAD_DIM ** -0.5
NEG_INF = -1e30

V7X_LANES = 128
V7X_VMEM_BYTES = 64 * 1024 * 1024
V7X_VMEM_LIMIT_CAP = 56 * 1024 * 1024

GROUP_Q = 2 * CHUNK
BAND_A = REACH_A + GROUP_Q
BAND_B = REACH_B + GROUP_Q
ORIGIN_A = 1024
WIDTH_MASTER_A = ORIGIN_A + BAND_A
ORIGIN_B = 256
WIDTH_MASTER_B = ORIGIN_B + BAND_B
N_VAR_A = REACH_A // GROUP_Q + 1
N_VAR_B = REACH_B // GROUP_Q + 1


def _vmem_limit(estimate_bytes):
    return int(min(max(estimate_bytes, 32 * 1024 * 1024), V7X_VMEM_LIMIT_CAP))


def _rms(x, gain):
    ms = jnp.mean(x * x, axis=-1, keepdims=True)
    return x * lax.rsqrt(ms + EPS) * gain


def _adaln(x, gain, shift, scale):
    return _rms(x, gain) * (1.0 + scale) + shift


def _silu(x):
    return x * (1.0 / (1.0 + jnp.exp(-x)))


def _mod_kernel(c_ref, w_ref, b_ref, o_ref):
    s = _silu(c_ref[...]).astype(BF16)
    o_ref[0] = jnp.dot(s, w_ref[...].astype(BF16), preferred_element_type=F32) + b_ref[0]


def _modulation(c_rows, w_mod, b_mod):
    rows = c_rows.shape[0]
    return pl.pallas_call(
        _mod_kernel,
        grid=(N_MOD,),
        in_specs=[
            pl.BlockSpec((rows, D_MODEL), lambda j: (0, 0)),
            pl.BlockSpec((D_MODEL, D_MODEL), lambda j: (0, j)),
            pl.BlockSpec((1, 1, D_MODEL), lambda j: (j, 0, 0)),
        ],
        out_specs=pl.BlockSpec((1, rows, D_MODEL), lambda j: (j, 0, 0)),
        out_shape=jax.ShapeDtypeStruct((N_MOD, rows, D_MODEL), F32),
        name="modulation",
    )(c_rows, w_mod, b_mod.reshape(N_MOD, 1, D_MODEL))


def _skew(base_row, n_rows):
    width = base_row.shape[1]
    m = jnp.broadcast_to(base_row, (n_rows, width))
    row = lax.broadcasted_iota(jnp.int32, (n_rows, width), 0)
    shift = 1
    while shift < n_rows:
        m = jnp.where((row & shift) != 0, pltpu.roll(m, shift, axis=1), m)
        shift *= 2
    return m


def _band_mask(n_rows, width, origin, n_prev):
    row = lax.broadcasted_iota(jnp.int32, (n_rows, width), 0)
    col = lax.broadcasted_iota(jnp.int32, (n_rows, width), 1)
    q_chunk = row >> 6
    k_chunk = (col >> 6) - (origin // CHUNK)
    return (k_chunk >= q_chunk - n_prev) & (k_chunk <= q_chunk)


def _t5_bucket(rel):
    half = N_BUCKETS // 2
    max_exact = half // 2
    n = jnp.abs(rel)
    n2 = n * n
    large = jnp.full(rel.shape, max_exact, jnp.int32)
    for k in range(1, half - max_exact):
        large = large + jnp.where(n2 >= (max_exact * max_exact) * (2 ** k), 1, 0)
    return jnp.where(rel > 0, half, 0) + jnp.where(n < max_exact, n, large)


def _bias_kernel(rel_ref, t5_ref, ba_ref, bb_ref, bsa_ref, bsb_ref, *, s_len, la, lb):
    tab = rel_ref[0]
    n_tab = 2 * REL_CLIP_A + 1
    padded = jnp.concatenate([tab, jnp.zeros((1, WIDTH_MASTER_A - tab.shape[1]), F32)], axis=1)
    rolled = pltpu.roll(padded, ORIGIN_A - REL_CLIP_A, axis=1)
    col = lax.broadcasted_iota(jnp.int32, (1, WIDTH_MASTER_A), 1)
    u_a = jnp.where(col < ORIGIN_A - REL_CLIP_A, tab[:, 0:1],
                    jnp.where(col > ORIGIN_A + REL_CLIP_A, tab[:, n_tab - 1:n_tab], rolled))
    skew_a = _skew(u_a, GROUP_Q)
    master_a = jnp.where(_band_mask(GROUP_Q, WIDTH_MASTER_A, ORIGIN_A, REACH_A // CHUNK), skew_a, NEG_INF)
    for v in range(N_VAR_A):
        start = ORIGIN_A - min(v * GROUP_Q, REACH_A) if v < N_VAR_A - 1 else ORIGIN_A - REACH_A
        ba_ref[v, 0] = master_a[:, start:start + BAND_A]
    bsa_ref[0] = skew_a[:s_len, ORIGIN_A - la:ORIGIN_A + s_len]

    t5 = t5_ref[0]
    rel = lax.broadcasted_iota(jnp.int32, (1, WIDTH_MASTER_B), 1) - ORIGIN_B
    bucket = _t5_bucket(rel)
    u_b = jnp.zeros((1, WIDTH_MASTER_B), F32)
    for i in range(N_BUCKETS):
        u_b = jnp.where(bucket == i, t5[:, i:i + 1], u_b)
    skew_b = _skew(u_b, GROUP_Q)
    master_b = jnp.where(_band_mask(GROUP_Q, WIDTH_MASTER_B, ORIGIN_B, REACH_B // CHUNK), skew_b, NEG_INF)
    for v in range(N_VAR_B):
        start = ORIGIN_B - min(v * GROUP_Q, REACH_B) if v < N_VAR_B - 1 else ORIGIN_B - REACH_B
        bb_ref[v, 0] = master_b[:, start:start + BAND_B]
    bsb_ref[0] = skew_b[:s_len, ORIGIN_B - lb:ORIGIN_B + s_len]


def _bias_tables(rel_bias_a, t5_table, s_len, la, lb):
    n_tab = 2 * REL_CLIP_A + 1
    tab_w = 3 * V7X_LANES
    rel_p = jnp.pad(rel_bias_a, ((0, 0), (0, tab_w - n_tab))).reshape(N_HEADS, 1, tab_w)
    t5_p = t5_table.reshape(N_HEADS, 1, N_BUCKETS)
    kern = functools.partial(_bias_kernel, s_len=s_len, la=la, lb=lb)
    return pl.pallas_call(
        kern,
        grid=(N_HEADS,),
        in_specs=[
            pl.BlockSpec((1, 1, tab_w), lambda h: (h, 0, 0)),
            pl.BlockSpec((1, 1, N_BUCKETS), lambda h: (h, 0, 0)),
        ],
        out_specs=[
            pl.BlockSpec((N_VAR_A, 1, GROUP_Q, BAND_A), lambda h: (0, h, 0, 0)),
            pl.BlockSpec((N_VAR_B, 1, GROUP_Q, BAND_B), lambda h: (0, h, 0, 0)),
            pl.BlockSpec((1, s_len, la + s_len), lambda h: (h, 0, 0)),
            pl.BlockSpec((1, s_len, lb + s_len), lambda h: (h, 0, 0)),
        ],
        out_shape=[
            jax.ShapeDtypeStruct((N_VAR_A, N_HEADS, GROUP_Q, BAND_A), F32),
            jax.ShapeDtypeStruct((N_VAR_B, N_HEADS, GROUP_Q, BAND_B), F32),
            jax.ShapeDtypeStruct((N_HEADS, s_len, la + s_len), F32),
            jax.ShapeDtypeStruct((N_HEADS, s_len, lb + s_len), F32),
        ],
        name="bias_tables",
    )(rel_p, t5_p)


V7X_MXU_DIM = 256
FF_SLABS = ((0, 6 * V7X_MXU_DIM), (6 * V7X_MXU_DIM, D_FF))
FF_SLAB_MAX = max(hi - lo for lo, hi in FF_SLABS)


def _swiglu(h, wg_ref, wu_ref, wd_ref):
    acc = None
    for lo, hi in FF_SLABS:
        cols = slice(lo, hi)
        g = jnp.dot(h, wg_ref[:, cols], preferred_element_type=F32)
        u = jnp.dot(h, wu_ref[:, cols], preferred_element_type=F32)
        a = (_silu(g) * u).astype(BF16)
        part = jnp.dot(a, wd_ref[cols, :], preferred_element_type=F32)
        acc = part if acc is None else acc + part
    return acc


N_SUB = 2


def _sub_tiles(gt, rt):
    if gt == 1:
        step = rt // N_SUB
        return [(slice(0, 1), slice(s * step, (s + 1) * step), s * step, step) for s in range(N_SUB)]
    step = gt // N_SUB
    return [(slice(s * step, (s + 1) * step), slice(0, rt), s * step * rt, step * rt) for s in range(N_SUB)]


def _ffn1_qkv_kernel(x_ref, mod_ref, gains_ref, wg_ref, wu_ref, wd_ref, win_ref,
                     x1_ref, qa_ref, ka_ref, va_ref, qb_ref, kb_ref, vb_ref,
                     ka32_ref, va32_ref, kb32_ref, vb32_ref):
    gt, rt, _ = x_ref.shape
    tm = gt * rt
    gain = lambda k: gains_ref[k:k + 1, :]

    def store_tail(ref, val, r0):
        first = tm - ref.shape[0]
        lo = max(r0, first)
        if lo < r0 + val.shape[0]:
            ref[lo - first:r0 + val.shape[0] - first, :] = val[lo - r0:, :]

    for gs, rs, r0, n in _sub_tiles(gt, rt):
        rows = slice(r0, r0 + n)
        mod = lambda k: mod_ref[k, gs]
        x = x_ref[gs, rs, :]
        h = _adaln(x, gain(0), mod(0), mod(1)).reshape(n, D_MODEL).astype(BF16)
        f = _swiglu(h, wg_ref, wu_ref, wd_ref).reshape(x.shape)
        x1 = x + (FFN_RES * mod(2)) * _rms(f, gain(1))
        x1_ref[gs, rs, :] = x1
        h2 = _adaln(x1, gain(2), mod(3), mod(4)).reshape(n, D_MODEL).astype(BF16)

        def proj(lo, width):
            return jnp.dot(h2, win_ref[:, lo:lo + width], preferred_element_type=F32)

        qa_ref[rows, :] = proj(0, WIDTH).astype(BF16)
        ka = proj(WIDTH, WIDTH)
        va = proj(2 * WIDTH, WIDTH)
        qb_ref[rows, :] = proj(3 * WIDTH, WIDTH).astype(BF16)
        kvb = proj(4 * WIDTH, 2 * KV_WIDTH_B)
        kb, vb = kvb[:, :KV_WIDTH_B], kvb[:, KV_WIDTH_B:]
        ka_ref[rows, :] = ka.astype(BF16)
        va_ref[rows, :] = va.astype(BF16)
        kb_ref[rows, :] = kb.astype(BF16)
        vb_ref[rows, :] = vb.astype(BF16)
        store_tail(ka32_ref, ka, r0)
        store_tail(va32_ref, va, r0)
        store_tail(kb32_ref, kb, r0)
        store_tail(vb32_ref, vb, r0)


def _resident(shape):
    return pl.BlockSpec(shape, lambda *_: (0,) * len(shape), pipeline_mode=pl.Buffered(1))


def _ffn1_qkv(x, mod, gains, wg, wu, wd, win, gt, rt):
    n_groups, rows, _ = x.shape
    n_g, n_r = n_groups // gt, rows // rt
    tm = gt * rt
    tokens = n_groups * rows
    ta = gt * min(REACH_A, rows)
    tb = gt * min(REACH_B, rows)
    tile = lambda gi, ri: (gi * n_r + ri, 0)
    tail = lambda gi, ri: (gi, 0)
    tok = lambda w, dt: jax.ShapeDtypeStruct((tokens, w), dt)
    vmem = (2 * 3 * D_MODEL * D_FF + 2 * D_MODEL * win.shape[1]
            + 2 * 2 * 4 * tm * D_MODEL
            + 2 * 2 * tm * (3 * WIDTH + WIDTH + 2 * KV_WIDTH_B)
            + 2 * 4 * (2 * ta * WIDTH + 2 * tb * KV_WIDTH_B)
            + tm * (2 * 4 + 2) * FF_SLAB_MAX + 3 * 4 * tm * D_MODEL
            + 4 * tm * win.shape[1])
    return pl.pallas_call(
        _ffn1_qkv_kernel,
        grid=(n_g, n_r),
        in_specs=[
            pl.BlockSpec((gt, rt, D_MODEL), lambda gi, ri: (gi, ri, 0)),
            pl.BlockSpec((N_MOD, gt, 1, D_MODEL), lambda gi, ri: (0, gi, 0, 0)),
            _resident(gains.shape),
            _resident(wg.shape), _resident(wu.shape), _resident(wd.shape), _resident(win.shape),
        ],
        out_specs=[
            pl.BlockSpec((gt, rt, D_MODEL), lambda gi, ri: (gi, ri, 0)),
            pl.BlockSpec((tm, WIDTH), tile), pl.BlockSpec((tm, WIDTH), tile), pl.BlockSpec((tm, WIDTH), tile),
            pl.BlockSpec((tm, WIDTH), tile), pl.BlockSpec((tm, KV_WIDTH_B), tile),
            pl.BlockSpec((tm, KV_WIDTH_B), tile),
            pl.BlockSpec((ta, WIDTH), tail), pl.BlockSpec((ta, WIDTH), tail),
            pl.BlockSpec((tb, KV_WIDTH_B), tail), pl.BlockSpec((tb, KV_WIDTH_B), tail),
        ],
        out_shape=[
            jax.ShapeDtypeStruct(x.shape, F32),
            tok(WIDTH, BF16), tok(WIDTH, BF16), tok(WIDTH, BF16),
            tok(WIDTH, BF16), tok(KV_WIDTH_B, BF16), tok(KV_WIDTH_B, BF16),
            jax.ShapeDtypeStruct((n_g * ta, WIDTH), F32), jax.ShapeDtypeStruct((n_g * ta, WIDTH), F32),
            jax.ShapeDtypeStruct((n_g * tb, KV_WIDTH_B), F32), jax.ShapeDtypeStruct((n_g * tb, KV_WIDTH_B), F32),
        ],
        compiler_params=pltpu.CompilerParams(
            dimension_semantics=("arbitrary", "arbitrary"), vmem_limit_bytes=_vmem_limit(vmem)),
        name="ffn1_qkv",
    )(x, mod, gains, wg, wu, wd, win)


def _out_ffn2_kernel(x_ref, o_ref, mod_ref, gains_ref, gg_ref, wout_ref, wg_ref, wu_ref, wd_ref, y_ref):
    gt, rt, _ = x_ref.shape
    gain = lambda k: gains_ref[k:k + 1, :]
    gg = gg_ref[...]
    for gs, rs, r0, n in _sub_tiles(gt, rt):
        mod = lambda k: mod_ref[k, gs]
        x = x_ref[gs, rs, :]
        o = o_ref[r0:r0 + n, :].astype(F32)
        on = jnp.concatenate([_rms(o[:, :WIDTH], gg[:, :WIDTH]), _rms(o[:, WIDTH:], gg[:, WIDTH:])], axis=1)
        mixed = jnp.dot(on.astype(BF16), wout_ref[...], preferred_element_type=F32).reshape(x.shape)
        x2 = x + mod(5) * _rms(mixed, gain(3))
        h = _adaln(x2, gain(4), mod(6), mod(7)).reshape(n, D_MODEL).astype(BF16)
        f = _swiglu(h, wg_ref, wu_ref, wd_ref).reshape(x.shape)
        y_ref[gs, rs, :] = x2 + (FFN_RES * mod(8)) * _rms(f, gain(5))


def _out_ffn2(x1, o, mod, gains, gg, wout, wg, wu, wd, gt, rt):
    n_groups, rows, _ = x1.shape
    n_g, n_r = n_groups // gt, rows // rt
    tm = gt * rt
    vmem = (2 * 3 * D_MODEL * D_FF + 2 * wout.size
            + 2 * 2 * 4 * tm * D_MODEL + 2 * 2 * tm * 2 * WIDTH
            + tm * (2 * 4 + 2) * FF_SLAB_MAX + 5 * 4 * tm * D_MODEL)
    return pl.pallas_call(
        _out_ffn2_kernel,
        grid=(n_g, n_r),
        in_specs=[
            pl.BlockSpec((gt, rt, D_MODEL), lambda gi, ri: (gi, ri, 0)),
            pl.BlockSpec((tm, 2 * WIDTH), lambda gi, ri: (gi * n_r + ri, 0)),
            pl.BlockSpec((N_MOD, gt, 1, D_MODEL), lambda gi, ri: (0, gi, 0, 0)),
            _resident(gains.shape), _resident(gg.shape), _resident(wout.shape),
            _resident(wg.shape), _resident(wu.shape), _resident(wd.shape),
        ],
        out_specs=pl.BlockSpec((gt, rt, D_MODEL), lambda gi, ri: (gi, ri, 0)),
        out_shape=jax.ShapeDtypeStruct(x1.shape, F32),
        compiler_params=pltpu.CompilerParams(
            dimension_semantics=("arbitrary", "arbitrary"), vmem_limit_bytes=_vmem_limit(vmem)),
        name="out_ffn2",
    )(x1, o, mod, gains, gg, wout, wg, wu, wd)


def _softmax_terms(s, sink=None):
    m = jnp.max(s, axis=1, keepdims=True)
    if sink is not None:
        m = jnp.maximum(m, sink)
    e = jnp.exp(s - m)
    denom = jnp.sum(e, axis=1, keepdims=True)
    if sink is not None:
        denom = denom + jnp.exp(sink - m)
    return e.astype(BF16), 1.0 / denom


def _scores(q, k):
    return lax.dot_general(q, k, (((1,), (1,)), ((), ())), preferred_element_type=F32)


def _stack_pair(q):
    qf = q.astype(F32)
    is_lo = lax.broadcasted_iota(jnp.int32, qf.shape, 1) < HEAD_DIM
    return jnp.concatenate([jnp.where(is_lo, qf, 0.0), jnp.where(is_lo, 0.0, qf)], axis=0).astype(BF16)


def _unstack_pair(z):
    n_q = z.shape[0] // 2
    is_lo = lax.broadcasted_iota(jnp.int32, (n_q, z.shape[1]), 1) < HEAD_DIM
    return jnp.where(is_lo, z[:n_q], z[n_q:])


def _sink_column(sinks_ref, pair, n_q):
    row = lax.broadcasted_iota(jnp.int32, (2 * n_q, 1), 0)
    return jnp.where(row < n_q, sinks_ref[pair], sinks_ref[N_PAIRS + pair])


def _attn_prompt_kernel(sinks_ref, qa_ref, qb_ref, ka_ref, va_ref, kb_ref, vb_ref, ba_ref, bb_ref, o_ref):
    g = pl.program_id(1)
    start_a = pl.multiple_of(jnp.maximum(g * GROUP_Q - REACH_A, 0), GROUP_Q)
    start_b = pl.multiple_of(jnp.maximum(g * GROUP_Q - REACH_B, 0), GROUP_Q)
    slab = lambda p: slice(p * V7X_LANES, (p + 1) * V7X_LANES)
    band_a = pl.ds(start_a, BAND_A)
    band_b = pl.ds(start_b, BAND_B)

    s_a = [_scores(_stack_pair(qa_ref[:, slab(p)]), ka_ref[band_a, slab(p)])
           + jnp.concatenate([ba_ref[0, 2 * p], ba_ref[0, 2 * p + 1]], axis=0) for p in range(N_PAIRS)]
    kb = kb_ref[band_b, :]
    s_b = [_scores(_stack_pair(qb_ref[:, slab(p)]), kb)
           + jnp.concatenate([bb_ref[0, p], bb_ref[0, N_PAIRS + p]], axis=0) for p in range(N_PAIRS)]

    p_a = [_softmax_terms(s) for s in s_a]
    p_b = [_softmax_terms(s, _sink_column(sinks_ref, p, GROUP_Q)) for p, s in enumerate(s_b)]

    vb = vb_ref[band_b, :]
    for p, (e, r) in enumerate(p_a):
        z = jnp.dot(e, va_ref[band_a, slab(p)], preferred_element_type=F32) * r
        o_ref[:, slab(p)] = _unstack_pair(z).astype(BF16)
    for p, (e, r) in enumerate(p_b):
        z = jnp.dot(e, vb, preferred_element_type=F32) * r
        o_ref[:, slab(N_PAIRS + p)] = _unstack_pair(z).astype(BF16)


def _attn_prompt(sinks, qa, qb, ka, va, kb, vb, bias_a, bias_b, batch, seq):
    n_grp = seq // GROUP_Q
    q_spec = pl.BlockSpec((GROUP_Q, WIDTH), lambda b, g: (b * n_grp + g, 0))
    kv_spec = lambda w: pl.BlockSpec((seq, w), lambda b, g: (b, 0), pipeline_mode=pl.Buffered(1))
    vmem = (2 * seq * (2 * WIDTH + 2 * KV_WIDTH_B)
            + 2 * 4 * N_HEADS * GROUP_Q * (BAND_A + BAND_B)
            + 2 * 2 * GROUP_Q * 4 * WIDTH + 16 * 4 * 2 * GROUP_Q * BAND_A)
    return pl.pallas_call(
        _attn_prompt_kernel,
        grid=(batch, n_grp),
        in_specs=[
            pl.BlockSpec(memory_space=pltpu.SMEM),
            q_spec, q_spec,
            kv_spec(WIDTH), kv_spec(WIDTH), kv_spec(KV_WIDTH_B), kv_spec(KV_WIDTH_B),
            pl.BlockSpec((1, N_HEADS, GROUP_Q, BAND_A),
                         lambda b, g: (jnp.minimum(g, N_VAR_A - 1), 0, 0, 0)),
            pl.BlockSpec((1, N_HEADS, GROUP_Q, BAND_B),
                         lambda b, g: (jnp.minimum(g, N_VAR_B - 1), 0, 0, 0)),
        ],
        out_specs=pl.BlockSpec((GROUP_Q, 2 * WIDTH), lambda b, g: (b * n_grp + g, 0)),
        out_shape=jax.ShapeDtypeStruct((batch * seq, 2 * WIDTH), BF16),
        compiler_params=pltpu.CompilerParams(
            dimension_semantics=("arbitrary", "arbitrary"), vmem_limit_bytes=_vmem_limit(vmem)),
        name="attn_prompt",
    )(sinks, qa, qb, ka, va, kb, vb, bias_a, bias_b)


def _pair_ordered_cols(h):
    return (h % N_PAIRS) * V7X_LANES + (h // N_PAIRS) * HEAD_DIM


def _attn_sample_kernel(sinks_ref, qa_ref, qb_ref, ka_ref, va_ref, kb_ref, vb_ref,
                        cak_ref, cav_ref, cbk_ref, cbv_ref, ba_ref, bb_ref, o_ref):
    n_q = qa_ref.shape[0]
    n_kv = KV_WIDTH_B // HEAD_DIM
    per_kv = N_HEADS // n_kv
    la, lb = cak_ref.shape[0] // N_HEADS, cbk_ref.shape[0] // n_kv
    head = lambda ref, lo: ref[:, lo:lo + HEAD_DIM]

    def keys(cache_ref, new_ref, idx, n_rows, n_heads):
        rows = pl.ds(idx, n_rows, stride=n_heads)
        return jnp.concatenate([cache_ref[rows, :].astype(BF16), head(new_ref, idx * HEAD_DIM)], axis=0)

    s_a = [_scores(head(qa_ref, h * HEAD_DIM), keys(cak_ref, ka_ref, h, la, N_HEADS)) + ba_ref[h]
           for h in range(N_HEADS)]
    s_b = []
    for c in range(n_kv):
        heads = range(c * per_kv, (c + 1) * per_kv)
        q = jnp.concatenate([head(qb_ref, _pair_ordered_cols(h)) for h in heads], axis=0)
        bias = jnp.concatenate([bb_ref[h] for h in heads], axis=0)
        s_b.append(_scores(q, keys(cbk_ref, kb_ref, c, lb, n_kv)) + bias)

    row = lax.broadcasted_iota(jnp.int32, (per_kv * n_q, 1), 0)
    p_a = [_softmax_terms(s) for s in s_a]
    p_b = []
    for c in range(n_kv):
        sink = jnp.zeros((per_kv * n_q, 1), F32)
        for i in range(per_kv):
            sink = jnp.where((row >= i * n_q) & (row < (i + 1) * n_q), sinks_ref[c * per_kv + i], sink)
        p_b.append(_softmax_terms(s_b[c], sink))

    out_a = [jnp.dot(e, keys(cav_ref, va_ref, h, la, N_HEADS), preferred_element_type=F32) * r
             for h, (e, r) in enumerate(p_a)]
    out_b = [None] * N_HEADS
    for c, (e, r) in enumerate(p_b):
        o = jnp.dot(e, keys(cbv_ref, vb_ref, c, lb, n_kv), preferred_element_type=F32) * r
        for i in range(per_kv):
            out_b[c * per_kv + i] = o[i * n_q:(i + 1) * n_q]
    order_b = [h for p in range(N_PAIRS) for h in (p, N_PAIRS + p)]
    o_ref[...] = jnp.concatenate(out_a + [out_b[h] for h in order_b], axis=1).astype(BF16)


def _attn_sample(sinks, qa, qb, ka, va, kb, vb, cak, cav, cbk, cbv, bias_a, bias_b, batch, s_len):
    row = lambda w: pl.BlockSpec((s_len, w), lambda b: (b, 0))
    cache = lambda a: pl.BlockSpec((None,) + a.shape[1:], lambda b: (b, 0, 0))
    whole = lambda a: pl.BlockSpec(a.shape, lambda b: (0,) * a.ndim)
    return pl.pallas_call(
        _attn_sample_kernel,
        grid=(batch,),
        in_specs=[
            pl.BlockSpec(memory_space=pltpu.SMEM),
            row(WIDTH), row(WIDTH), row(WIDTH), row(WIDTH), row(KV_WIDTH_B), row(KV_WIDTH_B),
            cache(cak), cache(cav), cache(cbk), cache(cbv),
            whole(bias_a), whole(bias_b),
        ],
        out_specs=pl.BlockSpec((s_len, 2 * WIDTH), lambda b: (b, 0)),
        out_shape=jax.ShapeDtypeStruct((batch * s_len, 2 * WIDTH), BF16),
        compiler_params=pltpu.CompilerParams(dimension_semantics=("arbitrary",)),
        name="attn_sample",
    )(sinks, qa, qb, ka, va, kb, vb, cak, cav, cbk, cbv, bias_a, bias_b)


def _pair_order():
    heads = [h for p in range(N_PAIRS) for h in (p, N_PAIRS + p)]
    return jnp.asarray([h * HEAD_DIM + d for h in heads for d in range(HEAD_DIM)], jnp.int32)


def _prep_weights(w_in, w_out, group_gains):
    order = _pair_order()
    qa, ka, va, qb, kvb = jnp.split(w_in, (WIDTH, 2 * WIDTH, 3 * WIDTH, 4 * WIDTH), axis=1)
    win = jnp.concatenate([qa * SCALE, ka, va, qb[:, order] * SCALE, kvb], axis=1).astype(BF16)
    rows = jnp.concatenate([jnp.arange(WIDTH, dtype=jnp.int32), WIDTH + order])
    return win, w_out[rows].astype(BF16), group_gains[rows].reshape(1, 2 * WIDTH)


def kernel(x_prompt, x_sample, cache_a_k, cache_a_v, cache_b_k, cache_b_v, c_prompt, c_sample, w_mod, b_mod,
           norm_gains, w1_gate, w1_up, w1_down, w_in, w_out, group_gains, rel_bias_a, t5_bias_table, sinks_b,
           w2_gate, w2_up, w2_down):
    depth = w_mod.shape[0]
    assert depth == 1
    batch, seq, _ = x_prompt.shape
    s_batch, s_len, _ = x_sample.shape
    la, lb = cache_a_k.shape[2], cache_b_k.shape[2]
    assert la == REACH_A and lb == REACH_B and seq % 512 == 0

    c_rows = jnp.concatenate([c_prompt, c_sample], axis=0)
    pad_rows = (-c_rows.shape[0]) % 8
    c_rows = jnp.pad(c_rows, ((0, pad_rows), (0, 0)))
    mod = _modulation(c_rows, w_mod[0], b_mod[0])
    mod_p = mod[:, :batch].reshape(N_MOD, batch, 1, D_MODEL)
    mod_s = mod[:, batch:batch + s_batch].reshape(N_MOD, s_batch, 1, D_MODEL)

    bias_a, bias_b, bias_sa, bias_sb = _bias_tables(rel_bias_a[0], t5_bias_table, s_len, la, lb)

    gains = norm_gains[0]
    w1 = (w1_gate[0].astype(BF16), w1_up[0].astype(BF16), w1_down[0].astype(BF16))
    w2 = (w2_gate[0].astype(BF16), w2_up[0].astype(BF16), w2_down[0].astype(BF16))
    win, wout, gg = _prep_weights(w_in[0], w_out[0], group_gains[0])
    sinks = sinks_b[0]

    rt_p = 512
    (x1p, qa, ka, va, qb, kb, vb, ka32, va32, kb32, vb32) = _ffn1_qkv(
        x_prompt, mod_p, gains, *w1, win, 1, rt_p)
    op = _attn_prompt(sinks, qa, qb, ka, va, kb, vb, bias_a, bias_b, batch, seq)
    y_prompt = _out_ffn2(x1p, op, mod_p, gains, gg, wout, *w2, 1, rt_p)
    new_p = (ka32.reshape(depth, batch, la, N_HEADS, HEAD_DIM), va32.reshape(depth, batch, la, N_HEADS, HEAD_DIM),
             kb32.reshape(depth, batch, lb, 2, HEAD_DIM), vb32.reshape(depth, batch, lb, 2, HEAD_DIM))

    gt_s = 16
    (x1s, qas, kas, vas, qbs, kbs, vbs, kas32, vas32, kbs32, vbs32) = _ffn1_qkv(
        x_sample, mod_s, gains, *w1, win, gt_s, s_len)
    os_ = _attn_sample(sinks, qas, qbs, kas, vas, kbs, vbs,
                       cache_a_k[0].reshape(s_batch, la * N_HEADS, HEAD_DIM),
                       cache_a_v[0].reshape(s_batch, la * N_HEADS, HEAD_DIM),
                       cache_b_k[0].reshape(s_batch, lb * 2, HEAD_DIM),
                       cache_b_v[0].reshape(s_batch, lb * 2, HEAD_DIM),
                       bias_sa, bias_sb, s_batch, s_len)
    y_sample = _out_ffn2(x1s, os_, mod_s, gains, gg, wout, *w2, gt_s, s_len)
    new_s = (kas32.reshape(depth, s_batch, s_len, N_HEADS, HEAD_DIM),
             vas32.reshape(depth, s_batch, s_len, N_HEADS, HEAD_DIM),
             kbs32.reshape(depth, s_batch, s_len, 2, HEAD_DIM), vbs32.reshape(depth, s_batch, s_len, 2, HEAD_DIM))

    return (y_prompt, y_sample) + new_p + new_s
```

```python
import functools

import jax
import jax.numpy as jnp
from jax import lax
from jax.experimental import pallas as pl
from jax.experimental.pallas import tpu as pltpu

F32 = jnp.float32
BF16 = jnp.bfloat16

D_MODEL = 1024
D_FF = 2816
CHUNK = 64
HEAD_DIM = 64
N_HEADS = 8
N_PAIRS = N_HEADS // 2
WIDTH = N_HEADS * HEAD_DIM
KV_WIDTH_B = 2 * HEAD_DIM
REACH_A = 8 * CHUNK
REACH_B = 2 * CHUNK
REL_CLIP_A = 128
N_BUCKETS = 32
N_MOD = 9
FFN_RES = 0.5
EPS = 1e-6
SCALE = HEAD_DIM ** -0.5
NEG_INF = -1e30

V7X_LANES = 128
V7X_VMEM_BYTES = 64 * 1024 * 1024
V7X_VMEM_LIMIT_CAP = 56 * 1024 * 1024

GROUP_Q = 2 * CHUNK
BAND_A = REACH_A + GROUP_Q
BAND_B = REACH_B + GROUP_Q
ORIGIN_A = 1024
WIDTH_MASTER_A = ORIGIN_A + BAND_A
ORIGIN_B = 256
WIDTH_MASTER_B = ORIGIN_B + BAND_B
N_VAR_A = REACH_A // GROUP_Q + 1
N_VAR_B = REACH_B // GROUP_Q + 1


def _vmem_limit(estimate_bytes):
    return int(min(max(estimate_bytes, 32 * 1024 * 1024), V7X_VMEM_LIMIT_CAP))


def _rms(x, gain):
    ms = jnp.mean(x * x, axis=-1, keepdims=True)
    return x * lax.rsqrt(ms + EPS) * gain


def _adaln(x, gain, shift, scale):
    return _rms(x, gain) * (1.0 + scale) + shift


def _silu(x):
    return x * (1.0 / (1.0 + jnp.exp(-x)))


def _mod_kernel(c_ref, w_ref, b_ref, o_ref):
    s = _silu(c_ref[...]).astype(BF16)
    o_ref[0] = jnp.dot(s, w_ref[...].astype(BF16), preferred_element_type=F32) + b_ref[0]


def _modulation(c_rows, w_mod, b_mod):
    rows = c_rows.shape[0]
    return pl.pallas_call(
        _mod_kernel,
        grid=(N_MOD,),
        in_specs=[
            pl.BlockSpec((rows, D_MODEL), lambda j: (0, 0)),
            pl.BlockSpec((D_MODEL, D_MODEL), lambda j: (0, j)),
            pl.BlockSpec((1, 1, D_MODEL), lambda j: (j, 0, 0)),
        ],
        out_specs=pl.BlockSpec((1, rows, D_MODEL), lambda j: (j, 0, 0)),
        out_shape=jax.ShapeDtypeStruct((N_MOD, rows, D_MODEL), F32),
        name="modulation",
    )(c_rows, w_mod, b_mod.reshape(N_MOD, 1, D_MODEL))


def _skew(base_row, n_rows):
    width = base_row.shape[1]
    m = jnp.broadcast_to(base_row, (n_rows, width))
    row = lax.broadcasted_iota(jnp.int32, (n_rows, width), 0)
    shift = 1
    while shift < n_rows:
        m = jnp.where((row & shift) != 0, pltpu.roll(m, shift, axis=1), m)
        shift *= 2
    return m


def _band_mask(n_rows, width, origin, n_prev):
    row = lax.broadcasted_iota(jnp.int32, (n_rows, width), 0)
    col = lax.broadcasted_iota(jnp.int32, (n_rows, width), 1)
    q_chunk = row >> 6
    k_chunk = (col >> 6) - (origin // CHUNK)
    return (k_chunk >= q_chunk - n_prev) & (k_chunk <= q_chunk)


def _t5_bucket(rel):
    half = N_BUCKETS // 2
    max_exact = half // 2
    n = jnp.abs(rel)
    n2 = n * n
    large = jnp.full(rel.shape, max_exact, jnp.int32)
    for k in range(1, half - max_exact):
        large = large + jnp.where(n2 >= (max_exact * max_exact) * (2 ** k), 1, 0)
    return jnp.where(rel > 0, half, 0) + jnp.where(n < max_exact, n, large)


def _bias_kernel(rel_ref, t5_ref, ba_ref, bb_ref, bsa_ref, bsb_ref, *, s_len, la, lb):
    tab = rel_ref[0]
    n_tab = 2 * REL_CLIP_A + 1
    padded = jnp.concatenate([tab, jnp.zeros((1, WIDTH_MASTER_A - tab.shape[1]), F32)], axis=1)
    rolled = pltpu.roll(padded, ORIGIN_A - REL_CLIP_A, axis=1)
    col = lax.broadcasted_iota(jnp.int32, (1, WIDTH_MASTER_A), 1)
    u_a = jnp.where(col < ORIGIN_A - REL_CLIP_A, tab[:, 0:1],
                    jnp.where(col > ORIGIN_A + REL_CLIP_A, tab[:, n_tab - 1:n_tab], rolled))
    skew_a = _skew(u_a, GROUP_Q)
    master_a = jnp.where(_band_mask(GROUP_Q, WIDTH_MASTER_A, ORIGIN_A, REACH_A // CHUNK), skew_a, NEG_INF)
    for v in range(N_VAR_A):
        start = ORIGIN_A - min(v * GROUP_Q, REACH_A) if v < N_VAR_A - 1 else ORIGIN_A - REACH_A
        ba_ref[v, 0] = master_a[:, start:start + BAND_A]
    bsa_ref[0] = skew_a[:s_len, ORIGIN_A - la:ORIGIN_A + s_len]

    t5 = t5_ref[0]
    rel = lax.broadcasted_iota(jnp.int32, (1, WIDTH_MASTER_B), 1) - ORIGIN_B
    bucket = _t5_bucket(rel)
    u_b = jnp.zeros((1, WIDTH_MASTER_B), F32)
    for i in range(N_BUCKETS):
        u_b = jnp.where(bucket == i, t5[:, i:i + 1], u_b)
    skew_b = _skew(u_b, GROUP_Q)
    master_b = jnp.where(_band_mask(GROUP_Q, WIDTH_MASTER_B, ORIGIN_B, REACH_B // CHUNK), skew_b, NEG_INF)
    for v in range(N_VAR_B):
        start = ORIGIN_B - min(v * GROUP_Q, REACH_B) if v < N_VAR_B - 1 else ORIGIN_B - REACH_B
        bb_ref[v, 0] = master_b[:, start:start + BAND_B]
    bsb_ref[0] = skew_b[:s_len, ORIGIN_B - lb:ORIGIN_B + s_len]


def _bias_tables(rel_bias_a, t5_table, s_len, la, lb):
    n_tab = 2 * REL_CLIP_A + 1
    tab_w = 3 * V7X_LANES
    rel_p = jnp.pad(rel_bias_a, ((0, 0), (0, tab_w - n_tab))).reshape(N_HEADS, 1, tab_w)
    t5_p = t5_table.reshape(N_HEADS, 1, N_BUCKETS)
    kern = functools.partial(_bias_kernel, s_len=s_len, la=la, lb=lb)
    return pl.pallas_call(
        kern,
        grid=(N_HEADS,),
        in_specs=[
            pl.BlockSpec((1, 1, tab_w), lambda h: (h, 0, 0)),
            pl.BlockSpec((1, 1, N_BUCKETS), lambda h: (h, 0, 0)),
        ],
        out_specs=[
            pl.BlockSpec((N_VAR_A, 1, GROUP_Q, BAND_A), lambda h: (0, h, 0, 0)),
            pl.BlockSpec((N_VAR_B, 1, GROUP_Q, BAND_B), lambda h: (0, h, 0, 0)),
            pl.BlockSpec((1, s_len, la + s_len), lambda h: (h, 0, 0)),
            pl.BlockSpec((1, s_len, lb + s_len), lambda h: (h, 0, 0)),
        ],
        out_shape=[
            jax.ShapeDtypeStruct((N_VAR_A, N_HEADS, GROUP_Q, BAND_A), F32),
            jax.ShapeDtypeStruct((N_VAR_B, N_HEADS, GROUP_Q, BAND_B), F32),
            jax.ShapeDtypeStruct((N_HEADS, s_len, la + s_len), F32),
            jax.ShapeDtypeStruct((N_HEADS, s_len, lb + s_len), F32),
        ],
        name="bias_tables",
    )(rel_p, t5_p)


V7X_MXU_DIM = 256
FF_SLABS = ((0, 6 * V7X_MXU_DIM), (6 * V7X_MXU_DIM, D_FF))
FF_SLAB_MAX = max(hi - lo for lo, hi in FF_SLABS)


def _swiglu(h, wg_ref, wu_ref, wd_ref):
    acc = None
    for lo, hi in FF_SLABS:
        cols = slice(lo, hi)
        g = jnp.dot(h, wg_ref[:, cols], preferred_element_type=F32)
        u = jnp.dot(h, wu_ref[:, cols], preferred_element_type=F32)
        a = (_silu(g) * u).astype(BF16)
        part = jnp.dot(a, wd_ref[cols, :], preferred_element_type=F32)
        acc = part if acc is None else acc + part
    return acc


N_SUB = 2


def _sub_tiles(gt, rt):
    if gt == 1:
        step = rt // N_SUB
        return [(slice(0, 1), slice(s * step, (s + 1) * step), s * step, step) for s in range(N_SUB)]
    step = gt // N_SUB
    return [(slice(s * step, (s + 1) * step), slice(0, rt), s * step * rt, step * rt) for s in range(N_SUB)]


def _ffn1_qkv_kernel(x_ref, mod_ref, gains_ref, wg_ref, wu_ref, wd_ref, win_ref,
                     x1_ref, qa_ref, ka_ref, va_ref, qb_ref, kb_ref, vb_ref,
                     ka32_ref, va32_ref, kb32_ref, vb32_ref):
    gt, rt, _ = x_ref.shape
    tm = gt * rt
    gain = lambda k: gains_ref[k:k + 1, :]

    def store_tail(ref, val, r0):
        first = tm - ref.shape[0]
        lo = max(r0, first)
        if lo < r0 + val.shape[0]:
            ref[lo - first:r0 + val.shape[0] - first, :] = val[lo - r0:, :]

    for gs, rs, r0, n in _sub_tiles(gt, rt):
        rows = slice(r0, r0 + n)
        mod = lambda k: mod_ref[k, gs]
        x = x_ref[gs, rs, :]
        h = _adaln(x, gain(0), mod(0), mod(1)).reshape(n, D_MODEL).astype(BF16)
        f = _swiglu(h, wg_ref, wu_ref, wd_ref).reshape(x.shape)
        x1 = x + (FFN_RES * mod(2)) * _rms(f, gain(1))
        x1_ref[gs, rs, :] = x1
        h2 = _adaln(x1, gain(2), mod(3), mod(4)).reshape(n, D_MODEL).astype(BF16)

        def proj(lo, width):
            return jnp.dot(h2, win_ref[:, lo:lo + width], preferred_element_type=F32)

        qa_ref[rows, :] = proj(0, WIDTH).astype(BF16)
        ka = proj(WIDTH, WIDTH)
        va = proj(2 * WIDTH, WIDTH)
        qb_ref[rows, :] = proj(3 * WIDTH, WIDTH).astype(BF16)
        kvb = proj(4 * WIDTH, 2 * KV_WIDTH_B)
        kb, vb = kvb[:, :KV_WIDTH_B], kvb[:, KV_WIDTH_B:]
        ka_ref[rows, :] = ka.astype(BF16)
        va_ref[rows, :] = va.astype(BF16)
        kb_ref[rows, :] = kb.astype(BF16)
        vb_ref[rows, :] = vb.astype(BF16)
        store_tail(ka32_ref, ka, r0)
        store_tail(va32_ref, va, r0)
        store_tail(kb32_ref, kb, r0)
        store_tail(vb32_ref, vb, r0)


def _resident(shape):
    return pl.BlockSpec(shape, lambda *_: (0,) * len(shape), pipeline_mode=pl.Buffered(1))


def _ffn1_qkv(x, mod, gains, wg, wu, wd, win, gt, rt):
    n_groups, rows, _ = x.shape
    n_g, n_r = n_groups // gt, rows // rt
    tm = gt * rt
    tokens = n_groups * rows
    ta = gt * min(REACH_A, rows)
    tb = gt * min(REACH_B, rows)
    tile = lambda gi, ri: (gi * n_r + ri, 0)
    tail = lambda gi, ri: (gi, 0)
    tok = lambda w, dt: jax.ShapeDtypeStruct((tokens, w), dt)
    vmem = (2 * 3 * D_MODEL * D_FF + 2 * D_MODEL * win.shape[1]
            + 2 * 2 * 4 * tm * D_MODEL
            + 2 * 2 * tm * (3 * WIDTH + WIDTH + 2 * KV_WIDTH_B)
            + 2 * 4 * (2 * ta * WIDTH + 2 * tb * KV_WIDTH_B)
            + tm * (2 * 4 + 2) * FF_SLAB_MAX + 3 * 4 * tm * D_MODEL
            + 4 * tm * win.shape[1])
    return pl.pallas_call(
        _ffn1_qkv_kernel,
        grid=(n_g, n_r),
        in_specs=[
            pl.BlockSpec((gt, rt, D_MODEL), lambda gi, ri: (gi, ri, 0)),
            pl.BlockSpec((N_MOD, gt, 1, D_MODEL), lambda gi, ri: (0, gi, 0, 0)),
            _resident(gains.shape),
            _resident(wg.shape), _resident(wu.shape), _resident(wd.shape), _resident(win.shape),
        ],
        out_specs=[
            pl.BlockSpec((gt, rt, D_MODEL), lambda gi, ri: (gi, ri, 0)),
            pl.BlockSpec((tm, WIDTH), tile), pl.BlockSpec((tm, WIDTH), tile), pl.BlockSpec((tm, WIDTH), tile),
            pl.BlockSpec((tm, WIDTH), tile), pl.BlockSpec((tm, KV_WIDTH_B), tile),
            pl.BlockSpec((tm, KV_WIDTH_B), tile),
            pl.BlockSpec((ta, WIDTH), tail), pl.BlockSpec((ta, WIDTH), tail),
            pl.BlockSpec((tb, KV_WIDTH_B), tail), pl.BlockSpec((tb, KV_WIDTH_B), tail),
        ],
        out_shape=[
            jax.ShapeDtypeStruct(x.shape, F32),
            tok(WIDTH, BF16), tok(WIDTH, BF16), tok(WIDTH, BF16),
            tok(WIDTH, BF16), tok(KV_WIDTH_B, BF16), tok(KV_WIDTH_B, BF16),
            jax.ShapeDtypeStruct((n_g * ta, WIDTH), F32), jax.ShapeDtypeStruct((n_g * ta, WIDTH), F32),
            jax.ShapeDtypeStruct((n_g * tb, KV_WIDTH_B), F32), jax.ShapeDtypeStruct((n_g * tb, KV_WIDTH_B), F32),
        ],
        compiler_params=pltpu.CompilerParams(
            dimension_semantics=("arbitrary", "arbitrary"), vmem_limit_bytes=_vmem_limit(vmem)),
        name="ffn1_qkv",
    )(x, mod, gains, wg, wu, wd, win)


def _out_ffn2_kernel(x_ref, o_ref, mod_ref, gains_ref, gg_ref, wout_ref, wg_ref, wu_ref, wd_ref, y_ref):
    gt, rt, _ = x_ref.shape
    gain = lambda k: gains_ref[k:k + 1, :]
    gg = gg_ref[...]
    for gs, rs, r0, n in _sub_tiles(gt, rt):
        mod = lambda k: mod_ref[k, gs]
        x = x_ref[gs, rs, :]
        o = o_ref[r0:r0 + n, :].astype(F32)
        on = jnp.concatenate([_rms(o[:, :WIDTH], gg[:, :WIDTH]), _rms(o[:, WIDTH:], gg[:, WIDTH:])], axis=1)
        mixed = jnp.dot(on.astype(BF16), wout_ref[...], preferred_element_type=F32).reshape(x.shape)
        x2 = x + mod(5) * _rms(mixed, gain(3))
        h = _adaln(x2, gain(4), mod(6), mod(7)).reshape(n, D_MODEL).astype(BF16)
        f = _swiglu(h, wg_ref, wu_ref, wd_ref).reshape(x.shape)
        y_ref[gs, rs, :] = x2 + (FFN_RES * mod(8)) * _rms(f, gain(5))


def _out_ffn2(x1, o, mod, gains, gg, wout, wg, wu, wd, gt, rt):
    n_groups, rows, _ = x1.shape
    n_g, n_r = n_groups // gt, rows // rt
    tm = gt * rt
    vmem = (2 * 3 * D_MODEL * D_FF + 2 * wout.size
            + 2 * 2 * 4 * tm * D_MODEL + 2 * 2 * tm * 2 * WIDTH
            + tm * (2 * 4 + 2) * FF_SLAB_MAX + 5 * 4 * tm * D_MODEL)
    return pl.pallas_call(
        _out_ffn2_kernel,
        grid=(n_g, n_r),
        in_specs=[
            pl.BlockSpec((gt, rt, D_MODEL), lambda gi, ri: (gi, ri, 0)),
            pl.BlockSpec((tm, 2 * WIDTH), lambda gi, ri: (gi * n_r + ri, 0)),
            pl.BlockSpec((N_MOD, gt, 1, D_MODEL), lambda gi, ri: (0, gi, 0, 0)),
            _resident(gains.shape), _resident(gg.shape), _resident(wout.shape),
            _resident(wg.shape), _resident(wu.shape), _resident(wd.shape),
        ],
        out_specs=pl.BlockSpec((gt, rt, D_MODEL), lambda gi, ri: (gi, ri, 0)),
        out_shape=jax.ShapeDtypeStruct(x1.shape, F32),
        compiler_params=pltpu.CompilerParams(
            dimension_semantics=("arbitrary", "arbitrary"), vmem_limit_bytes=_vmem_limit(vmem)),
        name="out_ffn2",
    )(x1, o, mod, gains, gg, wout, wg, wu, wd)


def _softmax_terms(s, sink=None):
    m = jnp.max(s, axis=1, keepdims=True)
    if sink is not None:
        m = jnp.maximum(m, sink)
    e = jnp.exp(s - m)
    denom = jnp.sum(e, axis=1, keepdims=True)
    if sink is not None:
        denom = denom + jnp.exp(sink - m)
    return e.astype(BF16), 1.0 / denom


def _scores(q, k):
    return lax.dot_general(q, k, (((1,), (1,)), ((), ())), preferred_element_type=F32)


def _stack_pair(q):
    qf = q.astype(F32)
    is_lo = lax.broadcasted_iota(jnp.int32, qf.shape, 1) < HEAD_DIM
    return jnp.concatenate([jnp.where(is_lo, qf, 0.0), jnp.where(is_lo, 0.0, qf)], axis=0).astype(BF16)


def _unstack_pair(z):
    n_q = z.shape[0] // 2
    is_lo = lax.broadcasted_iota(jnp.int32, (n_q, z.shape[1]), 1) < HEAD_DIM
    return jnp.where(is_lo, z[:n_q], z[n_q:])


def _sink_column(sinks_ref, pair, n_q):
    row = lax.broadcasted_iota(jnp.int32, (2 * n_q, 1), 0)
    return jnp.where(row < n_q, sinks_ref[pair], sinks_ref[N_PAIRS + pair])


def _attn_prompt_kernel(sinks_ref, qa_ref, qb_ref, ka_ref, va_ref, kb_ref, vb_ref, ba_ref, bb_ref, o_ref):
    g = pl.program_id(1)
    start_a = pl.multiple_of(jnp.maximum(g * GROUP_Q - REACH_A, 0), GROUP_Q)
    start_b = pl.multiple_of(jnp.maximum(g * GROUP_Q - REACH_B, 0), GROUP_Q)
    slab = lambda p: slice(p * V7X_LANES, (p + 1) * V7X_LANES)
    band_a = pl.ds(start_a, BAND_A)
    band_b = pl.ds(start_b, BAND_B)

    s_a = [_scores(_stack_pair(qa_ref[:, slab(p)]), ka_ref[band_a, slab(p)])
           + jnp.concatenate([ba_ref[0, 2 * p], ba_ref[0, 2 * p + 1]], axis=0) for p in range(N_PAIRS)]
    kb = kb_ref[band_b, :]
    s_b = [_scores(_stack_pair(qb_ref[:, slab(p)]), kb)
           + jnp.concatenate([bb_ref[0, p], bb_ref[0, N_PAIRS + p]], axis=0) for p in range(N_PAIRS)]

    p_a = [_softmax_terms(s) for s in s_a]
    p_b = [_softmax_terms(s, _sink_column(sinks_ref, p, GROUP_Q)) for p, s in enumerate(s_b)]

    vb = vb_ref[band_b, :]
    for p, (e, r) in enumerate(p_a):
        z = jnp.dot(e, va_ref[band_a, slab(p)], preferred_element_type=F32) * r
        o_ref[:, slab(p)] = _unstack_pair(z).astype(BF16)
    for p, (e, r) in enumerate(p_b):
        z = jnp.dot(e, vb, preferred_element_type=F32) * r
        o_ref[:, slab(N_PAIRS + p)] = _unstack_pair(z).astype(BF16)


def _attn_prompt(sinks, qa, qb, ka, va, kb, vb, bias_a, bias_b, batch, seq):
    n_grp = seq // GROUP_Q
    q_spec = pl.BlockSpec((GROUP_Q, WIDTH), lambda b, g: (b * n_grp + g, 0))
    kv_spec = lambda w: pl.BlockSpec((seq, w), lambda b, g: (b, 0), pipeline_mode=pl.Buffered(1))
    vmem = (2 * seq * (2 * WIDTH + 2 * KV_WIDTH_B)
            + 2 * 4 * N_HEADS * GROUP_Q * (BAND_A + BAND_B)
            + 2 * 2 * GROUP_Q * 4 * WIDTH + 16 * 4 * 2 * GROUP_Q * BAND_A)
    return pl.pallas_call(
        _attn_prompt_kernel,
        grid=(batch, n_grp),
        in_specs=[
            pl.BlockSpec(memory_space=pltpu.SMEM),
            q_spec, q_spec,
            kv_spec(WIDTH), kv_spec(WIDTH), kv_spec(KV_WIDTH_B), kv_spec(KV_WIDTH_B),
            pl.BlockSpec((1, N_HEADS, GROUP_Q, BAND_A),
                         lambda b, g: (jnp.minimum(g, N_VAR_A - 1), 0, 0, 0)),
            pl.BlockSpec((1, N_HEADS, GROUP_Q, BAND_B),
                         lambda b, g: (jnp.minimum(g, N_VAR_B - 1), 0, 0, 0)),
        ],
        out_specs=pl.BlockSpec((GROUP_Q, 2 * WIDTH), lambda b, g: (b * n_grp + g, 0)),
        out_shape=jax.ShapeDtypeStruct((batch * seq, 2 * WIDTH), BF16),
        compiler_params=pltpu.CompilerParams(
            dimension_semantics=("arbitrary", "arbitrary"), vmem_limit_bytes=_vmem_limit(vmem)),
        name="attn_prompt",
    )(sinks, qa, qb, ka, va, kb, vb, bias_a, bias_b)


def _pair_ordered_cols(h):
    return (h % N_PAIRS) * V7X_LANES + (h // N_PAIRS) * HEAD_DIM


def _attn_sample_kernel(sinks_ref, qa_ref, qb_ref, ka_ref, va_ref, kb_ref, vb_ref,
                        cak_ref, cav_ref, cbk_ref, cbv_ref, ba_ref, bb_ref, o_ref):
    n_q = qa_ref.shape[0]
    n_kv = cbk_ref.shape[0]
    per_kv = N_HEADS // n_kv
    la = cak_ref.shape[2]
    head = lambda ref, idx: ref[:, idx * HEAD_DIM:(idx + 1) * HEAD_DIM]

    def scores(q, cache_t_ref, new_ref, idx):
        old = jnp.dot(q, cache_t_ref[idx].astype(BF16), preferred_element_type=F32)
        return jnp.concatenate([old, _scores(q, head(new_ref, idx))], axis=1)

    def values(e, cache_t_ref, new_ref, idx, n_old):
        old = _scores(e[:, :n_old], cache_t_ref[idx].astype(BF16))
        return old + jnp.dot(e[:, n_old:], head(new_ref, idx), preferred_element_type=F32)

    s_a = [scores(head(qa_ref, h), cak_ref, ka_ref, h) + ba_ref[h] for h in range(N_HEADS)]
    s_b = []
    for c in range(n_kv):
        heads = range(c * per_kv, (c + 1) * per_kv)
        q = jnp.concatenate([qb_ref[:, _pair_ordered_cols(h):_pair_ordered_cols(h) + HEAD_DIM] for h in heads],
                            axis=0)
        bias = jnp.concatenate([bb_ref[h] for h in heads], axis=0)
        s_b.append(scores(q, cbk_ref, kb_ref, c) + bias)

    row = lax.broadcasted_iota(jnp.int32, (per_kv * n_q, 1), 0)
    p_a = [_softmax_terms(s) for s in s_a]
    p_b = []
    for c in range(n_kv):
        sink = jnp.zeros((per_kv * n_q, 1), F32)
        for i in range(per_kv):
            sink = jnp.where((row >= i * n_q) & (row < (i + 1) * n_q), sinks_ref[c * per_kv + i], sink)
        p_b.append(_softmax_terms(s_b[c], sink))

    out_a = [values(e, cav_ref, va_ref, h, la) * r for h, (e, r) in enumerate(p_a)]
    out_b = [None] * N_HEADS
    for c, (e, r) in enumerate(p_b):
        o = values(e, cbv_ref, vb_ref, c, cbv_ref.shape[2]) * r
        for i in range(per_kv):
            out_b[c * per_kv + i] = o[i * n_q:(i + 1) * n_q]
    order_b = [h for p in range(N_PAIRS) for h in (p, N_PAIRS + p)]
    o_ref[...] = jnp.concatenate(out_a + [out_b[h] for h in order_b], axis=1).astype(BF16)


def _attn_sample(sinks, qa, qb, ka, va, kb, vb, cak, cav, cbk, cbv, bias_a, bias_b, batch, s_len):
    row = lambda w: pl.BlockSpec((s_len, w), lambda b: (b, 0))
    cache = lambda a: pl.BlockSpec((None,) + a.shape[1:], lambda b: (b, 0, 0, 0))
    whole = lambda a: pl.BlockSpec(a.shape, lambda b: (0,) * a.ndim)
    return pl.pallas_call(
        _attn_sample_kernel,
        grid=(batch,),
        in_specs=[
            pl.BlockSpec(memory_space=pltpu.SMEM),
            row(WIDTH), row(WIDTH), row(WIDTH), row(WIDTH), row(KV_WIDTH_B), row(KV_WIDTH_B),
            cache(cak), cache(cav), cache(cbk), cache(cbv),
            whole(bias_a), whole(bias_b),
        ],
        out_specs=pl.BlockSpec((s_len, 2 * WIDTH), lambda b: (b, 0)),
        out_shape=jax.ShapeDtypeStruct((batch * s_len, 2 * WIDTH), BF16),
        compiler_params=pltpu.CompilerParams(dimension_semantics=("arbitrary",)),
        name="attn_sample",
    )(sinks, qa, qb, ka, va, kb, vb, cak, cav, cbk, cbv, bias_a, bias_b)


def _pair_order():
    heads = [h for p in range(N_PAIRS) for h in (p, N_PAIRS + p)]
    return jnp.asarray([h * HEAD_DIM + d for h in heads for d in range(HEAD_DIM)], jnp.int32)


def _prep_weights(w_in, w_out, group_gains):
    order = _pair_order()
    qa, ka, va, qb, kvb = jnp.split(w_in, (WIDTH, 2 * WIDTH, 3 * WIDTH, 4 * WIDTH), axis=1)
    win = jnp.concatenate([qa * SCALE, ka, va, qb[:, order] * SCALE, kvb], axis=1).astype(BF16)
    rows = jnp.concatenate([jnp.arange(WIDTH, dtype=jnp.int32), WIDTH + order])
    return win, w_out[rows].astype(BF16), group_gains[rows].reshape(1, 2 * WIDTH)


def kernel(x_prompt, x_sample, cache_a_k, cache_a_v, cache_b_k, cache_b_v, c_prompt, c_sample, w_mod, b_mod,
           norm_gains, w1_gate, w1_up, w1_down, w_in, w_out, group_gains, rel_bias_a, t5_bias_table, sinks_b,
           w2_gate, w2_up, w2_down):
    depth = w_mod.shape[0]
    assert depth == 1
    batch, seq, _ = x_prompt.shape
    s_batch, s_len, _ = x_sample.shape
    la, lb = cache_a_k.shape[2], cache_b_k.shape[2]
    assert la == REACH_A and lb == REACH_B and seq % 512 == 0

    c_rows = jnp.concatenate([c_prompt, c_sample], axis=0)
    pad_rows = (-c_rows.shape[0]) % 8
    c_rows = jnp.pad(c_rows, ((0, pad_rows), (0, 0)))
    mod = _modulation(c_rows, w_mod[0], b_mod[0])
    mod_p = mod[:, :batch].reshape(N_MOD, batch, 1, D_MODEL)
    mod_s = mod[:, batch:batch + s_batch].reshape(N_MOD, s_batch, 1, D_MODEL)

    bias_a, bias_b, bias_sa, bias_sb = _bias_tables(rel_bias_a[0], t5_bias_table, s_len, la, lb)

    gains = norm_gains[0]
    w1 = (w1_gate[0].astype(BF16), w1_up[0].astype(BF16), w1_down[0].astype(BF16))
    w2 = (w2_gate[0].astype(BF16), w2_up[0].astype(BF16), w2_down[0].astype(BF16))
    win, wout, gg = _prep_weights(w_in[0], w_out[0], group_gains[0])
    sinks = sinks_b[0]

    rt_p = 512
    (x1p, qa, ka, va, qb, kb, vb, ka32, va32, kb32, vb32) = _ffn1_qkv(
        x_prompt, mod_p, gains, *w1, win, 1, rt_p)
    op = _attn_prompt(sinks, qa, qb, ka, va, kb, vb, bias_a, bias_b, batch, seq)
    y_prompt = _out_ffn2(x1p, op, mod_p, gains, gg, wout, *w2, 1, rt_p)
    new_p = (ka32.reshape(depth, batch, la, N_HEADS, HEAD_DIM), va32.reshape(depth, batch, la, N_HEADS, HEAD_DIM),
             kb32.reshape(depth, batch, lb, 2, HEAD_DIM), vb32.reshape(depth, batch, lb, 2, HEAD_DIM))

    gt_s = 16
    (x1s, qas, kas, vas, qbs, kbs, vbs, kas32, vas32, kbs32, vbs32) = _ffn1_qkv(
        x_sample, mod_s, gains, *w1, win, gt_s, s_len)
    os_ = _attn_sample(sinks, qas, qbs, kas, vas, kbs, vbs,
                       *(jnp.transpose(c[0], (0, 2, 3, 1)) for c in (cache_a_k, cache_a_v, cache_b_k, cache_b_v)),
                       bias_sa, bias_sb, s_batch, s_len)
    y_sample = _out_ffn2(x1s, os_, mod_s, gains, gg, wout, *w2, gt_s, s_len)
    new_s = (kas32.reshape(depth, s_batch, s_len, N_HEADS, HEAD_DIM),
             vas32.reshape(depth, s_batch, s_len, N_HEADS, HEAD_DIM),
             kbs32.reshape(depth, s_batch, s_len, 2, HEAD_DIM), vbs32.reshape(depth, s_batch, s_len, 2, HEAD_DIM))

    return (y_prompt, y_sample) + new_p + new_s
```

```python
import functools

import jax
import jax.numpy as jnp
from jax import lax
from jax.experimental import pallas as pl
from jax.experimental.pallas import tpu as pltpu

F32 = jnp.float32
BF16 = jnp.bfloat16

D_MODEL = 1024
D_FF = 2816
CHUNK = 64
HEAD_DIM = 64
N_HEADS = 8
N_PAIRS = N_HEADS // 2
WIDTH = N_HEADS * HEAD_DIM
KV_WIDTH_B = 2 * HEAD_DIM
REACH_A = 8 * CHUNK
REACH_B = 2 * CHUNK
REL_CLIP_A = 128
N_BUCKETS = 32
N_MOD = 9
FFN_RES = 0.5
EPS = 1e-6
SCALE = HEAD_DIM ** -0.5
NEG_INF = -1e30

V7X_LANES = 128
V7X_VMEM_BYTES = 64 * 1024 * 1024
V7X_VMEM_LIMIT_CAP = 56 * 1024 * 1024

GROUP_Q = 2 * CHUNK
BAND_A = REACH_A + GROUP_Q
BAND_B = REACH_B + GROUP_Q
ORIGIN_A = 1024
WIDTH_MASTER_A = ORIGIN_A + BAND_A
ORIGIN_B = 256
WIDTH_MASTER_B = ORIGIN_B + BAND_B
N_VAR_A = REACH_A // GROUP_Q + 1
N_VAR_B = REACH_B // GROUP_Q + 1


def _vmem_limit(estimate_bytes):
    return int(min(max(estimate_bytes, 32 * 1024 * 1024), V7X_VMEM_LIMIT_CAP))


def _rms(x, gain):
    ms = jnp.mean(x * x, axis=-1, keepdims=True)
    return x * lax.rsqrt(ms + EPS) * gain


def _adaln(x, gain, shift, scale):
    return _rms(x, gain) * (1.0 + scale) + shift


def _silu(x):
    return x * (1.0 / (1.0 + jnp.exp(-x)))


def _mod_kernel(c_ref, w_ref, b_ref, o_ref):
    s = _silu(c_ref[...]).astype(BF16)
    o_ref[0] = jnp.dot(s, w_ref[...].astype(BF16), preferred_element_type=F32) + b_ref[0]


def _modulation(c_rows, w_mod, b_mod):
    rows = c_rows.shape[0]
    return pl.pallas_call(
        _mod_kernel,
        grid=(N_MOD,),
        in_specs=[
            pl.BlockSpec((rows, D_MODEL), lambda j: (0, 0)),
            pl.BlockSpec((D_MODEL, D_MODEL), lambda j: (0, j)),
            pl.BlockSpec((1, 1, D_MODEL), lambda j: (j, 0, 0)),
        ],
        out_specs=pl.BlockSpec((1, rows, D_MODEL), lambda j: (j, 0, 0)),
        out_shape=jax.ShapeDtypeStruct((N_MOD, rows, D_MODEL), F32),
        name="modulation",
    )(c_rows, w_mod, b_mod.reshape(N_MOD, 1, D_MODEL))


def _skew(base_row, n_rows):
    width = base_row.shape[1]
    m = jnp.broadcast_to(base_row, (n_rows, width))
    row = lax.broadcasted_iota(jnp.int32, (n_rows, width), 0)
    shift = 1
    while shift < n_rows:
        m = jnp.where((row & shift) != 0, pltpu.roll(m, shift, axis=1), m)
        shift *= 2
    return m


def _band_mask(n_rows, width, origin, n_prev):
    row = lax.broadcasted_iota(jnp.int32, (n_rows, width), 0)
    col = lax.broadcasted_iota(jnp.int32, (n_rows, width), 1)
    q_chunk = row >> 6
    k_chunk = (col >> 6) - (origin // CHUNK)
    return (k_chunk >= q_chunk - n_prev) & (k_chunk <= q_chunk)


def _t5_bucket(rel):
    half = N_BUCKETS // 2
    max_exact = half // 2
    n = jnp.abs(rel)
    n2 = n * n
    large = jnp.full(rel.shape, max_exact, jnp.int32)
    for k in range(1, half - max_exact):
        large = large + jnp.where(n2 >= (max_exact * max_exact) * (2 ** k), 1, 0)
    return jnp.where(rel > 0, half, 0) + jnp.where(n < max_exact, n, large)


def _bias_kernel(rel_ref, t5_ref, ba_ref, bb_ref, bsa_ref, bsb_ref, *, s_len, la, lb):
    tab = rel_ref[0]
    n_tab = 2 * REL_CLIP_A + 1
    first, last = tab[:, 0:1], tab[:, n_tab - 1:n_tab]
    win_lo = ORIGIN_A - 2 * REL_CLIP_A
    col = lax.broadcasted_iota(jnp.int32, (1, tab.shape[1] + V7X_LANES), 1)
    shifted = jnp.concatenate([jnp.zeros((1, V7X_LANES), F32), tab], axis=1)
    u_win = jnp.where(col < V7X_LANES, first, jnp.where(col >= V7X_LANES + n_tab, last, shifted))
    window = _skew(u_win, GROUP_Q)[:, V7X_LANES:]
    n_left = win_lo + V7X_LANES
    n_right = WIDTH_MASTER_A - n_left - window.shape[1]
    skew_a = jnp.concatenate([jnp.broadcast_to(first, (GROUP_Q, n_left)), window,
                              jnp.broadcast_to(last, (GROUP_Q, n_right))], axis=1)
    master_a = jnp.where(_band_mask(GROUP_Q, WIDTH_MASTER_A, ORIGIN_A, REACH_A // CHUNK), skew_a, NEG_INF)
    for v in range(N_VAR_A):
        start = ORIGIN_A - min(v * GROUP_Q, REACH_A) if v < N_VAR_A - 1 else ORIGIN_A - REACH_A
        ba_ref[v, 0] = master_a[:, start:start + BAND_A]
    bsa_ref[0] = skew_a[:s_len, ORIGIN_A - la:ORIGIN_A + s_len]

    t5 = t5_ref[0]
    rel = lax.broadcasted_iota(jnp.int32, (1, WIDTH_MASTER_B), 1) - ORIGIN_B
    bucket = _t5_bucket(rel)
    u_b = jnp.zeros((1, WIDTH_MASTER_B), F32)
    for i in range(N_BUCKETS):
        u_b = jnp.where(bucket == i, t5[:, i:i + 1], u_b)
    skew_b = _skew(u_b, GROUP_Q)
    master_b = jnp.where(_band_mask(GROUP_Q, WIDTH_MASTER_B, ORIGIN_B, REACH_B // CHUNK), skew_b, NEG_INF)
    for v in range(N_VAR_B):
        start = ORIGIN_B - min(v * GROUP_Q, REACH_B) if v < N_VAR_B - 1 else ORIGIN_B - REACH_B
        bb_ref[v, 0] = master_b[:, start:start + BAND_B]
    bsb_ref[0] = skew_b[:s_len, ORIGIN_B - lb:ORIGIN_B + s_len]


def _bias_tables(rel_bias_a, t5_table, s_len, la, lb):
    n_tab = 2 * REL_CLIP_A + 1
    tab_w = 3 * V7X_LANES
    rel_p = jnp.pad(rel_bias_a, ((0, 0), (0, tab_w - n_tab))).reshape(N_HEADS, 1, tab_w)
    t5_p = t5_table.reshape(N_HEADS, 1, N_BUCKETS)
    kern = functools.partial(_bias_kernel, s_len=s_len, la=la, lb=lb)
    return pl.pallas_call(
        kern,
        grid=(N_HEADS,),
        in_specs=[
            pl.BlockSpec((1, 1, tab_w), lambda h: (h, 0, 0)),
            pl.BlockSpec((1, 1, N_BUCKETS), lambda h: (h, 0, 0)),
        ],
        out_specs=[
            pl.BlockSpec((N_VAR_A, 1, GROUP_Q, BAND_A), lambda h: (0, h, 0, 0)),
            pl.BlockSpec((N_VAR_B, 1, GROUP_Q, BAND_B), lambda h: (0, h, 0, 0)),
            pl.BlockSpec((1, s_len, la + s_len), lambda h: (h, 0, 0)),
            pl.BlockSpec((1, s_len, lb + s_len), lambda h: (h, 0, 0)),
        ],
        out_shape=[
            jax.ShapeDtypeStruct((N_VAR_A, N_HEADS, GROUP_Q, BAND_A), F32),
            jax.ShapeDtypeStruct((N_VAR_B, N_HEADS, GROUP_Q, BAND_B), F32),
            jax.ShapeDtypeStruct((N_HEADS, s_len, la + s_len), F32),
            jax.ShapeDtypeStruct((N_HEADS, s_len, lb + s_len), F32),
        ],
        name="bias_tables",
    )(rel_p, t5_p)


V7X_MXU_DIM = 256
FF_SLABS = ((0, 6 * V7X_MXU_DIM), (6 * V7X_MXU_DIM, D_FF))
FF_SLAB_MAX = max(hi - lo for lo, hi in FF_SLABS)


def _swiglu(h, wg_ref, wu_ref, wd_ref):
    acc = None
    for lo, hi in FF_SLABS:
        cols = slice(lo, hi)
        g = jnp.dot(h, wg_ref[:, cols], preferred_element_type=F32)
        u = jnp.dot(h, wu_ref[:, cols], preferred_element_type=F32)
        a = (_silu(g) * u).astype(BF16)
        part = jnp.dot(a, wd_ref[cols, :], preferred_element_type=F32)
        acc = part if acc is None else acc + part
    return acc


N_SUB = 2


def _sub_tiles(gt, rt):
    if gt == 1:
        step = rt // N_SUB
        return [(slice(0, 1), slice(s * step, (s + 1) * step), s * step, step) for s in range(N_SUB)]
    step = gt // N_SUB
    return [(slice(s * step, (s + 1) * step), slice(0, rt), s * step * rt, step * rt) for s in range(N_SUB)]


def _ffn1_qkv_kernel(x_ref, mod_ref, gains_ref, wg_ref, wu_ref, wd_ref, win_ref,
                     x1_ref, qa_ref, ka_ref, va_ref, qb_ref, kb_ref, vb_ref,
                     ka32_ref, va32_ref, kb32_ref, vb32_ref):
    gt, rt, _ = x_ref.shape
    tm = gt * rt
    gain = lambda k: gains_ref[k:k + 1, :]

    def store_tail(ref, val, r0):
        first = tm - ref.shape[0]
        lo = max(r0, first)
        if lo < r0 + val.shape[0]:
            ref[lo - first:r0 + val.shape[0] - first, :] = val[lo - r0:, :]

    def store_split(ref, rows, q):
        is_lo = (lax.broadcasted_iota(jnp.int32, q.shape, 1) & (V7X_LANES - 1)) < HEAD_DIM
        ref[rows, :WIDTH] = jnp.where(is_lo, q, 0.0).astype(BF16)
        ref[rows, WIDTH:] = jnp.where(is_lo, 0.0, q).astype(BF16)

    for gs, rs, r0, n in _sub_tiles(gt, rt):
        rows = slice(r0, r0 + n)
        mod = lambda k: mod_ref[k, gs]
        x = x_ref[gs, rs, :]
        h = _adaln(x, gain(0), mod(0), mod(1)).reshape(n, D_MODEL).astype(BF16)
        f = _swiglu(h, wg_ref, wu_ref, wd_ref).reshape(x.shape)
        x1 = x + (FFN_RES * mod(2)) * _rms(f, gain(1))
        x1_ref[gs, rs, :] = x1
        h2 = _adaln(x1, gain(2), mod(3), mod(4)).reshape(n, D_MODEL).astype(BF16)

        def proj(lo, width):
            return jnp.dot(h2, win_ref[:, lo:lo + width], preferred_element_type=F32)

        store_split(qa_ref, rows, proj(0, WIDTH))
        ka = proj(WIDTH, WIDTH)
        va = proj(2 * WIDTH, WIDTH)
        store_split(qb_ref, rows, proj(3 * WIDTH, WIDTH))
        kvb = proj(4 * WIDTH, 2 * KV_WIDTH_B)
        kb, vb = kvb[:, :KV_WIDTH_B], kvb[:, KV_WIDTH_B:]
        ka_ref[rows, :] = ka.astype(BF16)
        va_ref[rows, :] = va.astype(BF16)
        kb_ref[rows, :] = kb.astype(BF16)
        vb_ref[rows, :] = vb.astype(BF16)
        store_tail(ka32_ref, ka, r0)
        store_tail(va32_ref, va, r0)
        store_tail(kb32_ref, kb, r0)
        store_tail(vb32_ref, vb, r0)


def _resident(shape):
    return pl.BlockSpec(shape, lambda *_: (0,) * len(shape), pipeline_mode=pl.Buffered(1))


def _ffn1_qkv(x, mod, gains, wg, wu, wd, win, gt, rt):
    n_groups, rows, _ = x.shape
    n_g, n_r = n_groups // gt, rows // rt
    tm = gt * rt
    tokens = n_groups * rows
    ta = gt * min(REACH_A, rows)
    tb = gt * min(REACH_B, rows)
    tile = lambda gi, ri: (gi * n_r + ri, 0)
    tail = lambda gi, ri: (gi, 0)
    tok = lambda w, dt: jax.ShapeDtypeStruct((tokens, w), dt)
    vmem = (2 * 3 * D_MODEL * D_FF + 2 * D_MODEL * win.shape[1]
            + 2 * 2 * 4 * tm * D_MODEL
            + 2 * 2 * tm * (6 * WIDTH + 2 * KV_WIDTH_B)
            + 2 * 4 * (2 * ta * WIDTH + 2 * tb * KV_WIDTH_B)
            + tm * (2 * 4 + 2) * FF_SLAB_MAX + 3 * 4 * tm * D_MODEL
            + 4 * tm * win.shape[1])
    return pl.pallas_call(
        _ffn1_qkv_kernel,
        grid=(n_g, n_r),
        in_specs=[
            pl.BlockSpec((gt, rt, D_MODEL), lambda gi, ri: (gi, ri, 0)),
            pl.BlockSpec((N_MOD, gt, 1, D_MODEL), lambda gi, ri: (0, gi, 0, 0)),
            _resident(gains.shape),
            _resident(wg.shape), _resident(wu.shape), _resident(wd.shape), _resident(win.shape),
        ],
        out_specs=[
            pl.BlockSpec((gt, rt, D_MODEL), lambda gi, ri: (gi, ri, 0)),
            pl.BlockSpec((tm, 2 * WIDTH), tile), pl.BlockSpec((tm, WIDTH), tile), pl.BlockSpec((tm, WIDTH), tile),
            pl.BlockSpec((tm, 2 * WIDTH), tile), pl.BlockSpec((tm, KV_WIDTH_B), tile),
            pl.BlockSpec((tm, KV_WIDTH_B), tile),
            pl.BlockSpec((ta, WIDTH), tail), pl.BlockSpec((ta, WIDTH), tail),
            pl.BlockSpec((tb, KV_WIDTH_B), tail), pl.BlockSpec((tb, KV_WIDTH_B), tail),
        ],
        out_shape=[
            jax.ShapeDtypeStruct(x.shape, F32),
            tok(2 * WIDTH, BF16), tok(WIDTH, BF16), tok(WIDTH, BF16),
            tok(2 * WIDTH, BF16), tok(KV_WIDTH_B, BF16), tok(KV_WIDTH_B, BF16),
            jax.ShapeDtypeStruct((n_g * ta, WIDTH), F32), jax.ShapeDtypeStruct((n_g * ta, WIDTH), F32),
            jax.ShapeDtypeStruct((n_g * tb, KV_WIDTH_B), F32), jax.ShapeDtypeStruct((n_g * tb, KV_WIDTH_B), F32),
        ],
        compiler_params=pltpu.CompilerParams(
            dimension_semantics=("arbitrary", "arbitrary"), vmem_limit_bytes=_vmem_limit(vmem)),
        name="ffn1_qkv",
    )(x, mod, gains, wg, wu, wd, win)


def _out_ffn2_kernel(x_ref, o_ref, mod_ref, gains_ref, gg_ref, wout_ref, wg_ref, wu_ref, wd_ref, y_ref):
    gt, rt, _ = x_ref.shape
    gain = lambda k: gains_ref[k:k + 1, :]
    gg = gg_ref[...]
    for gs, rs, r0, n in _sub_tiles(gt, rt):
        mod = lambda k: mod_ref[k, gs]
        x = x_ref[gs, rs, :]
        o = o_ref[r0:r0 + n, :].astype(F32)
        on = jnp.concatenate([_rms(o[:, :WIDTH], gg[:, :WIDTH]), _rms(o[:, WIDTH:], gg[:, WIDTH:])], axis=1)
        mixed = jnp.dot(on.astype(BF16), wout_ref[...], preferred_element_type=F32).reshape(x.shape)
        x2 = x + mod(5) * _rms(mixed, gain(3))
        h = _adaln(x2, gain(4), mod(6), mod(7)).reshape(n, D_MODEL).astype(BF16)
        f = _swiglu(h, wg_ref, wu_ref, wd_ref).reshape(x.shape)
        y_ref[gs, rs, :] = x2 + (FFN_RES * mod(8)) * _rms(f, gain(5))


def _out_ffn2(x1, o, mod, gains, gg, wout, wg, wu, wd, gt, rt):
    n_groups, rows, _ = x1.shape
    n_g, n_r = n_groups // gt, rows // rt
    tm = gt * rt
    vmem = (2 * 3 * D_MODEL * D_FF + 2 * wout.size
            + 2 * 2 * 4 * tm * D_MODEL + 2 * 2 * tm * 2 * WIDTH
            + tm * (2 * 4 + 2) * FF_SLAB_MAX + 5 * 4 * tm * D_MODEL)
    return pl.pallas_call(
        _out_ffn2_kernel,
        grid=(n_g, n_r),
        in_specs=[
            pl.BlockSpec((gt, rt, D_MODEL), lambda gi, ri: (gi, ri, 0)),
            pl.BlockSpec((tm, 2 * WIDTH), lambda gi, ri: (gi * n_r + ri, 0)),
            pl.BlockSpec((N_MOD, gt, 1, D_MODEL), lambda gi, ri: (0, gi, 0, 0)),
            _resident(gains.shape), _resident(gg.shape), _resident(wout.shape),
            _resident(wg.shape), _resident(wu.shape), _resident(wd.shape),
        ],
        out_specs=pl.BlockSpec((gt, rt, D_MODEL), lambda gi, ri: (gi, ri, 0)),
        out_shape=jax.ShapeDtypeStruct(x1.shape, F32),
        compiler_params=pltpu.CompilerParams(
            dimension_semantics=("arbitrary", "arbitrary"), vmem_limit_bytes=_vmem_limit(vmem)),
        name="out_ffn2",
    )(x1, o, mod, gains, gg, wout, wg, wu, wd)


def _softmax_terms(s, sink=None):
    m = jnp.max(s, axis=1, keepdims=True)
    if sink is not None:
        m = jnp.maximum(m, sink)
    e = jnp.exp(s - m)
    denom = jnp.sum(e, axis=1, keepdims=True)
    if sink is not None:
        denom = denom + jnp.exp(sink - m)
    return e.astype(BF16), 1.0 / denom


def _scores(q, k):
    return lax.dot_general(q, k, (((1,), (1,)), ((), ())), preferred_element_type=F32)


def _unstack_pair(z):
    n_q = z.shape[0] // 2
    is_lo = lax.broadcasted_iota(jnp.int32, (n_q, z.shape[1]), 1) < HEAD_DIM
    return jnp.where(is_lo, z[:n_q], z[n_q:])


def _sink_column(sinks_ref, pair, n_q):
    row = lax.broadcasted_iota(jnp.int32, (2 * n_q, 1), 0)
    return jnp.where(row < n_q, sinks_ref[pair], sinks_ref[N_PAIRS + pair])


def _attn_prompt_kernel(sinks_ref, qa_ref, qb_ref, ka_ref, va_ref, kb_ref, vb_ref, ba_ref, bb_ref, o_ref):
    g = pl.program_id(1)
    start_a = pl.multiple_of(jnp.maximum(g * GROUP_Q - REACH_A, 0), GROUP_Q)
    start_b = pl.multiple_of(jnp.maximum(g * GROUP_Q - REACH_B, 0), GROUP_Q)
    slab = lambda p: slice(p * V7X_LANES, (p + 1) * V7X_LANES)
    band_a = pl.ds(start_a, BAND_A)
    band_b = pl.ds(start_b, BAND_B)

    kb = kb_ref[band_b, :]
    vb = vb_ref[band_b, :]

    def stacked(q_ref, p):
        return jnp.concatenate([q_ref[:, slab(p)], q_ref[:, WIDTH + p * V7X_LANES:WIDTH + (p + 1) * V7X_LANES]],
                               axis=0)

    def score(mixer, p):
        if mixer == "a":
            bias = jnp.concatenate([ba_ref[0, 2 * p], ba_ref[0, 2 * p + 1]], axis=0)
            return _scores(stacked(qa_ref, p), ka_ref[band_a, slab(p)]) + bias
        bias = jnp.concatenate([bb_ref[0, p], bb_ref[0, N_PAIRS + p]], axis=0)
        return _scores(stacked(qb_ref, p), kb) + bias

    def weights(s, sink):
        m = jnp.max(s, axis=1, keepdims=True)
        if sink is None:
            return jnp.exp(s - m).astype(BF16), None
        m = jnp.maximum(m, sink)
        return jnp.exp(s - m).astype(BF16), jnp.exp(sink - m)

    def attend(e, extra, v):
        z = jnp.dot(e, jnp.concatenate([v, jnp.ones_like(v)], axis=1), preferred_element_type=F32)
        denom = z[:, V7X_LANES:] if extra is None else z[:, V7X_LANES:] + extra
        return _unstack_pair(z[:, :V7X_LANES] * (1.0 / denom))

    jobs = [("a", p) for p in range(N_PAIRS)] + [("b", p) for p in range(N_PAIRS)]
    s_all = [score(mixer, p) for mixer, p in jobs]
    w_all = [weights(s, _sink_column(sinks_ref, p, GROUP_Q) if mixer == "b" else None)
             for s, (mixer, p) in zip(s_all, jobs)]
    for (e, extra), (mixer, p) in zip(w_all, jobs):
        if mixer == "a":
            o_ref[:, slab(p)] = attend(e, extra, va_ref[band_a, slab(p)]).astype(BF16)
        else:
            o_ref[:, slab(N_PAIRS + p)] = attend(e, extra, vb).astype(BF16)


def _attn_prompt(sinks, qa, qb, ka, va, kb, vb, bias_a, bias_b, batch, seq):
    n_grp = seq // GROUP_Q
    q_spec = pl.BlockSpec((GROUP_Q, 2 * WIDTH), lambda b, g: (b * n_grp + g, 0))
    kv_spec = lambda w: pl.BlockSpec((seq, w), lambda b, g: (b, 0), pipeline_mode=pl.Buffered(1))
    vmem = (2 * seq * (2 * WIDTH + 2 * KV_WIDTH_B)
            + 2 * 4 * N_HEADS * GROUP_Q * (BAND_A + BAND_B)
            + 2 * 2 * GROUP_Q * 4 * WIDTH + 16 * 4 * 2 * GROUP_Q * BAND_A)
    return pl.pallas_call(
        _attn_prompt_kernel,
        grid=(batch, n_grp),
        in_specs=[
            pl.BlockSpec(memory_space=pltpu.SMEM),
            q_spec, q_spec,
            kv_spec(WIDTH), kv_spec(WIDTH), kv_spec(KV_WIDTH_B), kv_spec(KV_WIDTH_B),
            pl.BlockSpec((1, N_HEADS, GROUP_Q, BAND_A),
                         lambda b, g: (jnp.minimum(g, N_VAR_A - 1), 0, 0, 0)),
            pl.BlockSpec((1, N_HEADS, GROUP_Q, BAND_B),
                         lambda b, g: (jnp.minimum(g, N_VAR_B - 1), 0, 0, 0)),
        ],
        out_specs=pl.BlockSpec((GROUP_Q, 2 * WIDTH), lambda b, g: (b * n_grp + g, 0)),
        out_shape=jax.ShapeDtypeStruct((batch * seq, 2 * WIDTH), BF16),
        compiler_params=pltpu.CompilerParams(
            dimension_semantics=("arbitrary", "arbitrary"), vmem_limit_bytes=_vmem_limit(vmem)),
        name="attn_prompt",
    )(sinks, qa, qb, ka, va, kb, vb, bias_a, bias_b)


def _pair_ordered_cols(h):
    return (h % N_PAIRS) * V7X_LANES + (h // N_PAIRS) * HEAD_DIM


def _attn_sample_kernel(sinks_ref, qa_ref, qb_ref, ka_ref, va_ref, kb_ref, vb_ref,
                        cak_ref, cav_ref, cbk_ref, cbv_ref, ba_ref, bb_ref, o_ref):
    n_q = qa_ref.shape[0]
    n_kv = cbk_ref.shape[0]
    per_kv = N_HEADS // n_kv
    la = cak_ref.shape[2]
    head = lambda ref, idx: ref[:, idx * HEAD_DIM:(idx + 1) * HEAD_DIM]

    def query(ref, lo):
        lo = lo if lo % V7X_LANES < HEAD_DIM else WIDTH + lo
        return ref[:, lo:lo + HEAD_DIM]

    def scores(q, cache_t_ref, new_ref, idx):
        old = jnp.dot(q, cache_t_ref[idx].astype(BF16), preferred_element_type=F32)
        return jnp.concatenate([old, _scores(q, head(new_ref, idx))], axis=1)

    def values(e, cache_t_ref, new_ref, idx, n_old):
        old = _scores(e[:, :n_old], cache_t_ref[idx].astype(BF16))
        return old + jnp.dot(e[:, n_old:], head(new_ref, idx), preferred_element_type=F32)

    s_a = [scores(query(qa_ref, h * HEAD_DIM), cak_ref, ka_ref, h) + ba_ref[h] for h in range(N_HEADS)]
    s_b = []
    for c in range(n_kv):
        heads = range(c * per_kv, (c + 1) * per_kv)
        q = jnp.concatenate([query(qb_ref, _pair_ordered_cols(h)) for h in heads], axis=0)
        bias = jnp.concatenate([bb_ref[h] for h in heads], axis=0)
        s_b.append(scores(q, cbk_ref, kb_ref, c) + bias)

    row = lax.broadcasted_iota(jnp.int32, (per_kv * n_q, 1), 0)
    p_a = [_softmax_terms(s) for s in s_a]
    p_b = []
    for c in range(n_kv):
        sink = jnp.zeros((per_kv * n_q, 1), F32)
        for i in range(per_kv):
            sink = jnp.where((row >= i * n_q) & (row < (i + 1) * n_q), sinks_ref[c * per_kv + i], sink)
        p_b.append(_softmax_terms(s_b[c], sink))

    out_a = [values(e, cav_ref, va_ref, h, la) * r for h, (e, r) in enumerate(p_a)]
    out_b = [None] * N_HEADS
    for c, (e, r) in enumerate(p_b):
        o = values(e, cbv_ref, vb_ref, c, cbv_ref.shape[2]) * r
        for i in range(per_kv):
            out_b[c * per_kv + i] = o[i * n_q:(i + 1) * n_q]
    order_b = [h for p in range(N_PAIRS) for h in (p, N_PAIRS + p)]
    o_ref[...] = jnp.concatenate(out_a + [out_b[h] for h in order_b], axis=1).astype(BF16)


def _attn_sample(sinks, qa, qb, ka, va, kb, vb, cak, cav, cbk, cbv, bias_a, bias_b, batch, s_len):
    row = lambda w: pl.BlockSpec((s_len, w), lambda b: (b, 0))
    cache = lambda a: pl.BlockSpec((None,) + a.shape[1:], lambda b: (b, 0, 0, 0))
    whole = lambda a: pl.BlockSpec(a.shape, lambda b: (0,) * a.ndim)
    return pl.pallas_call(
        _attn_sample_kernel,
        grid=(batch,),
        in_specs=[
            pl.BlockSpec(memory_space=pltpu.SMEM),
            row(2 * WIDTH), row(2 * WIDTH), row(WIDTH), row(WIDTH), row(KV_WIDTH_B), row(KV_WIDTH_B),
            cache(cak), cache(cav), cache(cbk), cache(cbv),
            whole(bias_a), whole(bias_b),
        ],
        out_specs=pl.BlockSpec((s_len, 2 * WIDTH), lambda b: (b, 0)),
        out_shape=jax.ShapeDtypeStruct((batch * s_len, 2 * WIDTH), BF16),
        compiler_params=pltpu.CompilerParams(dimension_semantics=("arbitrary",)),
        name="attn_sample",
    )(sinks, qa, qb, ka, va, kb, vb, cak, cav, cbk, cbv, bias_a, bias_b)


def _pair_order(x, axis):
    shape = x.shape
    split = shape[:axis] + (N_HEADS // N_PAIRS, N_PAIRS, HEAD_DIM) + shape[axis + 1:]
    return jnp.swapaxes(x.reshape(split), axis, axis + 1).reshape(shape)


def _prep_weights(w_in, w_out, group_gains):
    qa, ka, va, qb, kvb = jnp.split(w_in, (WIDTH, 2 * WIDTH, 3 * WIDTH, 4 * WIDTH), axis=1)
    win = jnp.concatenate([qa * SCALE, ka, va, _pair_order(qb, 1) * SCALE, kvb], axis=1).astype(BF16)
    wout = jnp.concatenate([w_out[:WIDTH], _pair_order(w_out[WIDTH:], 0)], axis=0).astype(BF16)
    gg = jnp.concatenate([group_gains[:WIDTH], _pair_order(group_gains[WIDTH:], 0)])
    return win, wout, gg.reshape(1, 2 * WIDTH)


def kernel(x_prompt, x_sample, cache_a_k, cache_a_v, cache_b_k, cache_b_v, c_prompt, c_sample, w_mod, b_mod,
           norm_gains, w1_gate, w1_up, w1_down, w_in, w_out, group_gains, rel_bias_a, t5_bias_table, sinks_b,
           w2_gate, w2_up, w2_down):
    depth = w_mod.shape[0]
    assert depth == 1
    batch, seq, _ = x_prompt.shape
    s_batch, s_len, _ = x_sample.shape
    la, lb = cache_a_k.shape[2], cache_b_k.shape[2]
    assert la == REACH_A and lb == REACH_B and seq % 512 == 0

    c_rows = jnp.concatenate([c_prompt, c_sample], axis=0)
    pad_rows = (-c_rows.shape[0]) % 8
    c_rows = jnp.pad(c_rows, ((0, pad_rows), (0, 0)))
    mod = _modulation(c_rows, w_mod[0], b_mod[0])
    mod_p = mod[:, :batch].reshape(N_MOD, batch, 1, D_MODEL)
    mod_s = mod[:, batch:batch + s_batch].reshape(N_MOD, s_batch, 1, D_MODEL)

    bias_a, bias_b, bias_sa, bias_sb = _bias_tables(rel_bias_a[0], t5_bias_table, s_len, la, lb)

    gains = norm_gains[0]
    w1 = (w1_gate[0].astype(BF16), w1_up[0].astype(BF16), w1_down[0].astype(BF16))
    w2 = (w2_gate[0].astype(BF16), w2_up[0].astype(BF16), w2_down[0].astype(BF16))
    win, wout, gg = _prep_weights(w_in[0], w_out[0], group_gains[0])
    sinks = sinks_b[0]

    rt_p = 512
    (x1p, qa, ka, va, qb, kb, vb, ka32, va32, kb32, vb32) = _ffn1_qkv(
        x_prompt, mod_p, gains, *w1, win, 1, rt_p)
    op = _attn_prompt(sinks, qa, qb, ka, va, kb, vb, bias_a, bias_b, batch, seq)
    y_prompt = _out_ffn2(x1p, op, mod_p, gains, gg, wout, *w2, 1, rt_p)
    new_p = (ka32.reshape(depth, batch, la, N_HEADS, HEAD_DIM), va32.reshape(depth, batch, la, N_HEADS, HEAD_DIM),
             kb32.reshape(depth, batch, lb, 2, HEAD_DIM), vb32.reshape(depth, batch, lb, 2, HEAD_DIM))

    gt_s = 16
    (x1s, qas, kas, vas, qbs, kbs, vbs, kas32, vas32, kbs32, vbs32) = _ffn1_qkv(
        x_sample, mod_s, gains, *w1, win, gt_s, s_len)
    os_ = _attn_sample(sinks, qas, qbs, kas, vas, kbs, vbs,
                       *(jnp.transpose(c[0], (0, 2, 3, 1)) for c in (cache_a_k, cache_a_v, cache_b_k, cache_b_v)),
                       bias_sa, bias_sb, s_batch, s_len)
    y_sample = _out_ffn2(x1s, os_, mod_s, gains, gg, wout, *w2, gt_s, s_len)
    new_s = (kas32.reshape(depth, s_batch, s_len, N_HEADS, HEAD_DIM),
             vas32.reshape(depth, s_batch, s_len, N_HEADS, HEAD_DIM),
             kbs32.reshape(depth, s_batch, s_len, 2, HEAD_DIM), vbs32.reshape(depth, s_batch, s_len, 2, HEAD_DIM))

    return (y_prompt, y_sample) + new_p + new_s
```

```python
import functools

import jax
import jax.numpy as jnp
from jax import lax
from jax.experimental import pallas as pl
from jax.experimental.pallas import tpu as pltpu

F32 = jnp.float32
BF16 = jnp.bfloat16

D_MODEL = 1024
D_FF = 2816
CHUNK = 64
HEAD_DIM = 64
N_HEADS = 8
N_PAIRS = N_HEADS // 2
WIDTH = N_HEADS * HEAD_DIM
KV_WIDTH_B = 2 * HEAD_DIM
REACH_A = 8 * CHUNK
REACH_B = 2 * CHUNK
REL_CLIP_A = 128
N_BUCKETS = 32
N_MOD = 9
FFN_RES = 0.5
EPS = 1e-6
SCALE = HEAD_DIM ** -0.5
NEG_INF = -1e30

V7X_LANES = 128
V7X_VMEM_BYTES = 64 * 1024 * 1024
V7X_VMEM_LIMIT_CAP = 56 * 1024 * 1024

GROUP_Q = 2 * CHUNK
BAND_A = REACH_A + GROUP_Q
BAND_B = REACH_B + GROUP_Q
ORIGIN_A = 1024
WIDTH_MASTER_A = ORIGIN_A + BAND_A
ORIGIN_B = 256
WIDTH_MASTER_B = ORIGIN_B + BAND_B
N_VAR_A = REACH_A // GROUP_Q + 1
N_VAR_B = REACH_B // GROUP_Q + 1


def _vmem_limit(estimate_bytes):
    return int(min(max(estimate_bytes, 32 * 1024 * 1024), V7X_VMEM_LIMIT_CAP))


def _rms(x, gain):
    ms = jnp.mean(x * x, axis=-1, keepdims=True)
    return x * lax.rsqrt(ms + EPS) * gain


def _adaln(x, gain, shift, scale):
    return _rms(x, gain) * (1.0 + scale) + shift


def _silu(x):
    return x * (1.0 / (1.0 + jnp.exp(-x)))


def _mod_kernel(c_ref, w_ref, b_ref, o_ref):
    s = _silu(c_ref[...]).astype(BF16)
    o_ref[0] = jnp.dot(s, w_ref[...].astype(BF16), preferred_element_type=F32) + b_ref[0]


def _modulation(c_rows, w_mod, b_mod):
    rows = c_rows.shape[0]
    return pl.pallas_call(
        _mod_kernel,
        grid=(N_MOD,),
        in_specs=[
            pl.BlockSpec((rows, D_MODEL), lambda j: (0, 0)),
            pl.BlockSpec((D_MODEL, D_MODEL), lambda j: (0, j)),
            pl.BlockSpec((1, 1, D_MODEL), lambda j: (j, 0, 0)),
        ],
        out_specs=pl.BlockSpec((1, rows, D_MODEL), lambda j: (j, 0, 0)),
        out_shape=jax.ShapeDtypeStruct((N_MOD, rows, D_MODEL), F32),
        name="modulation",
    )(c_rows, w_mod, b_mod.reshape(N_MOD, 1, D_MODEL))


def _skew(base_row, n_rows):
    width = base_row.shape[1]
    m = jnp.broadcast_to(base_row, (n_rows, width))
    row = lax.broadcasted_iota(jnp.int32, (n_rows, width), 0)
    shift = 1
    while shift < n_rows:
        m = jnp.where((row & shift) != 0, pltpu.roll(m, shift, axis=1), m)
        shift *= 2
    return m


def _band_mask(n_rows, width, origin, n_prev):
    row = lax.broadcasted_iota(jnp.int32, (n_rows, width), 0)
    col = lax.broadcasted_iota(jnp.int32, (n_rows, width), 1)
    q_chunk = row >> 6
    k_chunk = (col >> 6) - (origin // CHUNK)
    return (k_chunk >= q_chunk - n_prev) & (k_chunk <= q_chunk)


def _t5_bucket(rel):
    half = N_BUCKETS // 2
    max_exact = half // 2
    n = jnp.abs(rel)
    n2 = n * n
    large = jnp.full(rel.shape, max_exact, jnp.int32)
    for k in range(1, half - max_exact):
        large = large + jnp.where(n2 >= (max_exact * max_exact) * (2 ** k), 1, 0)
    return jnp.where(rel > 0, half, 0) + jnp.where(n < max_exact, n, large)


def _bias_kernel(rel_ref, t5_ref, ba_ref, bb_ref, bsa_ref, bsb_ref, *, s_len, la, lb):
    tab = rel_ref[0]
    n_tab = 2 * REL_CLIP_A + 1
    first, last = tab[:, 0:1], tab[:, n_tab - 1:n_tab]
    win_lo = ORIGIN_A - 2 * REL_CLIP_A
    col = lax.broadcasted_iota(jnp.int32, (1, tab.shape[1] + V7X_LANES), 1)
    shifted = jnp.concatenate([jnp.zeros((1, V7X_LANES), F32), tab], axis=1)
    u_win = jnp.where(col < V7X_LANES, first, jnp.where(col >= V7X_LANES + n_tab, last, shifted))
    window = _skew(u_win, GROUP_Q)[:, V7X_LANES:]
    n_left = win_lo + V7X_LANES
    n_right = WIDTH_MASTER_A - n_left - window.shape[1]
    skew_a = jnp.concatenate([jnp.broadcast_to(first, (GROUP_Q, n_left)), window,
                              jnp.broadcast_to(last, (GROUP_Q, n_right))], axis=1)
    master_a = jnp.where(_band_mask(GROUP_Q, WIDTH_MASTER_A, ORIGIN_A, REACH_A // CHUNK), skew_a, NEG_INF)
    for v in range(N_VAR_A):
        start = ORIGIN_A - min(v * GROUP_Q, REACH_A) if v < N_VAR_A - 1 else ORIGIN_A - REACH_A
        ba_ref[v, 0] = master_a[:, start:start + BAND_A]
    bsa_ref[0] = skew_a[:s_len, ORIGIN_A - la:ORIGIN_A + s_len]

    t5 = t5_ref[0]
    rel = lax.broadcasted_iota(jnp.int32, (1, WIDTH_MASTER_B), 1) - ORIGIN_B
    bucket = _t5_bucket(rel)
    u_b = jnp.zeros((1, WIDTH_MASTER_B), F32)
    for i in range(N_BUCKETS):
        u_b = jnp.where(bucket == i, t5[:, i:i + 1], u_b)
    skew_b = _skew(u_b, GROUP_Q)
    master_b = jnp.where(_band_mask(GROUP_Q, WIDTH_MASTER_B, ORIGIN_B, REACH_B // CHUNK), skew_b, NEG_INF)
    for v in range(N_VAR_B):
        start = ORIGIN_B - min(v * GROUP_Q, REACH_B) if v < N_VAR_B - 1 else ORIGIN_B - REACH_B
        bb_ref[v, 0] = master_b[:, start:start + BAND_B]
    bsb_ref[0] = skew_b[:s_len, ORIGIN_B - lb:ORIGIN_B + s_len]


def _bias_tables(rel_bias_a, t5_table, s_len, la, lb):
    n_tab = 2 * REL_CLIP_A + 1
    tab_w = 3 * V7X_LANES
    rel_p = jnp.pad(rel_bias_a, ((0, 0), (0, tab_w - n_tab))).reshape(N_HEADS, 1, tab_w)
    t5_p = t5_table.reshape(N_HEADS, 1, N_BUCKETS)
    kern = functools.partial(_bias_kernel, s_len=s_len, la=la, lb=lb)
    return pl.pallas_call(
        kern,
        grid=(N_HEADS,),
        in_specs=[
            pl.BlockSpec((1, 1, tab_w), lambda h: (h, 0, 0)),
            pl.BlockSpec((1, 1, N_BUCKETS), lambda h: (h, 0, 0)),
        ],
        out_specs=[
            pl.BlockSpec((N_VAR_A, 1, GROUP_Q, BAND_A), lambda h: (0, h, 0, 0)),
            pl.BlockSpec((N_VAR_B, 1, GROUP_Q, BAND_B), lambda h: (0, h, 0, 0)),
            pl.BlockSpec((1, s_len, la + s_len), lambda h: (h, 0, 0)),
            pl.BlockSpec((1, s_len, lb + s_len), lambda h: (h, 0, 0)),
        ],
        out_shape=[
            jax.ShapeDtypeStruct((N_VAR_A, N_HEADS, GROUP_Q, BAND_A), F32),
            jax.ShapeDtypeStruct((N_VAR_B, N_HEADS, GROUP_Q, BAND_B), F32),
            jax.ShapeDtypeStruct((N_HEADS, s_len, la + s_len), F32),
            jax.ShapeDtypeStruct((N_HEADS, s_len, lb + s_len), F32),
        ],
        name="bias_tables",
    )(rel_p, t5_p)


V7X_MXU_DIM = 256
FF_SLABS = ((0, 6 * V7X_MXU_DIM), (6 * V7X_MXU_DIM, D_FF))
FF_SLAB_MAX = max(hi - lo for lo, hi in FF_SLABS)


def _swiglu(h, wg_ref, wu_ref, wd_ref):
    acts = []
    for lo, hi in FF_SLABS:
        g = jnp.dot(h, wg_ref[:, lo:hi], preferred_element_type=F32)
        u = jnp.dot(h, wu_ref[:, lo:hi], preferred_element_type=F32)
        acts.append((_silu(g) * u).astype(BF16))
    acc = None
    for (lo, hi), a in zip(FF_SLABS, acts):
        part = jnp.dot(a, wd_ref[lo:hi, :], preferred_element_type=F32)
        acc = part if acc is None else acc + part
    return acc


N_SUB = 2


def _sub_tiles(gt, rt):
    if gt == 1:
        step = rt // N_SUB
        return [(slice(0, 1), slice(s * step, (s + 1) * step), s * step, step) for s in range(N_SUB)]
    step = gt // N_SUB
    return [(slice(s * step, (s + 1) * step), slice(0, rt), s * step * rt, step * rt) for s in range(N_SUB)]


def _ffn1_qkv_kernel(x_ref, mod_ref, gains_ref, wg_ref, wu_ref, wd_ref, win_ref,
                     x1_ref, qa_ref, ka_ref, va_ref, qb_ref, kb_ref, vb_ref,
                     ka32_ref, va32_ref, kb32_ref, vb32_ref):
    gt, rt, _ = x_ref.shape
    tm = gt * rt
    gain = lambda k: gains_ref[k:k + 1, :]

    def store_tail(ref, val, r0):
        first = tm - ref.shape[0]
        lo = max(r0, first)
        if lo < r0 + val.shape[0]:
            ref[lo - first:r0 + val.shape[0] - first, :] = val[lo - r0:, :]

    def store_split(ref, rows, q):
        is_lo = (lax.broadcasted_iota(jnp.int32, q.shape, 1) & (V7X_LANES - 1)) < HEAD_DIM
        ref[rows, :WIDTH] = jnp.where(is_lo, q, 0.0).astype(BF16)
        ref[rows, WIDTH:] = jnp.where(is_lo, 0.0, q).astype(BF16)

    subs = _sub_tiles(gt, rt)
    mods = [lambda k, gs=gs: mod_ref[k, gs] for gs, _, _, _ in subs]
    xs = [x_ref[gs, rs, :] for gs, rs, _, _ in subs]
    hs = [_adaln(x, gain(0), mod(0), mod(1)).reshape(n, D_MODEL).astype(BF16)
          for x, mod, (_, _, _, n) in zip(xs, mods, subs)]
    fs = [_swiglu(h, wg_ref, wu_ref, wd_ref).reshape(x.shape) for h, x in zip(hs, xs)]
    h2s = []
    for x, f, mod, (gs, rs, _, n) in zip(xs, fs, mods, subs):
        x1 = x + (FFN_RES * mod(2)) * _rms(f, gain(1))
        x1_ref[gs, rs, :] = x1
        h2s.append(_adaln(x1, gain(2), mod(3), mod(4)).reshape(n, D_MODEL).astype(BF16))

    for h2, (_, _, r0, n) in zip(h2s, subs):
        rows = slice(r0, r0 + n)

        def proj(lo, width):
            return jnp.dot(h2, win_ref[:, lo:lo + width], preferred_element_type=F32)

        store_split(qa_ref, rows, proj(0, WIDTH))
        ka = proj(WIDTH, WIDTH)
        va = proj(2 * WIDTH, WIDTH)
        store_split(qb_ref, rows, proj(3 * WIDTH, WIDTH))
        kvb = proj(4 * WIDTH, 2 * KV_WIDTH_B)
        kb, vb = kvb[:, :KV_WIDTH_B], kvb[:, KV_WIDTH_B:]
        ka_ref[rows, :] = ka.astype(BF16)
        va_ref[rows, :] = va.astype(BF16)
        kb_ref[rows, :] = kb.astype(BF16)
        vb_ref[rows, :] = vb.astype(BF16)
        store_tail(ka32_ref, ka, r0)
        store_tail(va32_ref, va, r0)
        store_tail(kb32_ref, kb, r0)
        store_tail(vb32_ref, vb, r0)


def _resident(shape):
    return pl.BlockSpec(shape, lambda *_: (0,) * len(shape), pipeline_mode=pl.Buffered(1))


def _ffn1_qkv(x, mod, gains, wg, wu, wd, win, gt, rt):
    n_groups, rows, _ = x.shape
    n_g, n_r = n_groups // gt, rows // rt
    tm = gt * rt
    tokens = n_groups * rows
    ta = gt * min(REACH_A, rows)
    tb = gt * min(REACH_B, rows)
    tile = lambda gi, ri: (gi * n_r + ri, 0)
    tail = lambda gi, ri: (gi, 0)
    tok = lambda w, dt: jax.ShapeDtypeStruct((tokens, w), dt)
    vmem = (2 * 3 * D_MODEL * D_FF + 2 * D_MODEL * win.shape[1]
            + 2 * 2 * 4 * tm * D_MODEL
            + 2 * 2 * tm * (6 * WIDTH + 2 * KV_WIDTH_B)
            + 2 * 4 * (2 * ta * WIDTH + 2 * tb * KV_WIDTH_B)
            + tm * (2 * 4 + 2) * FF_SLAB_MAX + 3 * 4 * tm * D_MODEL
            + 4 * tm * win.shape[1])
    return pl.pallas_call(
        _ffn1_qkv_kernel,
        grid=(n_g, n_r),
        in_specs=[
            pl.BlockSpec((gt, rt, D_MODEL), lambda gi, ri: (gi, ri, 0)),
            pl.BlockSpec((N_MOD, gt, 1, D_MODEL), lambda gi, ri: (0, gi, 0, 0)),
            _resident(gains.shape),
            _resident(wg.shape), _resident(wu.shape), _resident(wd.shape), _resident(win.shape),
        ],
        out_specs=[
            pl.BlockSpec((gt, rt, D_MODEL), lambda gi, ri: (gi, ri, 0)),
            pl.BlockSpec((tm, 2 * WIDTH), tile), pl.BlockSpec((tm, WIDTH), tile), pl.BlockSpec((tm, WIDTH), tile),
            pl.BlockSpec((tm, 2 * WIDTH), tile), pl.BlockSpec((tm, KV_WIDTH_B), tile),
            pl.BlockSpec((tm, KV_WIDTH_B), tile),
            pl.BlockSpec((ta, WIDTH), tail), pl.BlockSpec((ta, WIDTH), tail),
            pl.BlockSpec((tb, KV_WIDTH_B), tail), pl.BlockSpec((tb, KV_WIDTH_B), tail),
        ],
        out_shape=[
            jax.ShapeDtypeStruct(x.shape, F32),
            tok(2 * WIDTH, BF16), tok(WIDTH, BF16), tok(WIDTH, BF16),
            tok(2 * WIDTH, BF16), tok(KV_WIDTH_B, BF16), tok(KV_WIDTH_B, BF16),
            jax.ShapeDtypeStruct((n_g * ta, WIDTH), F32), jax.ShapeDtypeStruct((n_g * ta, WIDTH), F32),
            jax.ShapeDtypeStruct((n_g * tb, KV_WIDTH_B), F32), jax.ShapeDtypeStruct((n_g * tb, KV_WIDTH_B), F32),
        ],
        compiler_params=pltpu.CompilerParams(
            dimension_semantics=("arbitrary", "arbitrary"), vmem_limit_bytes=_vmem_limit(vmem)),
        name="ffn1_qkv",
    )(x, mod, gains, wg, wu, wd, win)


def _out_ffn2_kernel(x_ref, o_ref, mod_ref, gains_ref, gg_ref, wout_ref, wg_ref, wu_ref, wd_ref, y_ref):
    gt, rt, _ = x_ref.shape
    gain = lambda k: gains_ref[k:k + 1, :]
    gg = gg_ref[...]
    subs = _sub_tiles(gt, rt)
    mods = [lambda k, gs=gs: mod_ref[k, gs] for gs, _, _, _ in subs]
    xs = [x_ref[gs, rs, :] for gs, rs, _, _ in subs]
    ons = []
    for _, _, r0, n in subs:
        o = o_ref[r0:r0 + n, :].astype(F32)
        on = jnp.concatenate([_rms(o[:, :WIDTH], gg[:, :WIDTH]), _rms(o[:, WIDTH:], gg[:, WIDTH:])], axis=1)
        ons.append(on.astype(BF16))
    mixed = [jnp.dot(on, wout_ref[...], preferred_element_type=F32).reshape(x.shape) for on, x in zip(ons, xs)]
    x2s = [x + mod(5) * _rms(m, gain(3)) for x, m, mod in zip(xs, mixed, mods)]
    hs = [_adaln(x2, gain(4), mod(6), mod(7)).reshape(n, D_MODEL).astype(BF16)
          for x2, mod, (_, _, _, n) in zip(x2s, mods, subs)]
    fs = [_swiglu(h, wg_ref, wu_ref, wd_ref).reshape(x.shape) for h, x in zip(hs, xs)]
    for x2, f, mod, (gs, rs, _, _) in zip(x2s, fs, mods, subs):
        y_ref[gs, rs, :] = x2 + (FFN_RES * mod(8)) * _rms(f, gain(5))


def _out_ffn2(x1, o, mod, gains, gg, wout, wg, wu, wd, gt, rt):
    n_groups, rows, _ = x1.shape
    n_g, n_r = n_groups // gt, rows // rt
    tm = gt * rt
    vmem = (2 * 3 * D_MODEL * D_FF + 2 * wout.size
            + 2 * 2 * 4 * tm * D_MODEL + 2 * 2 * tm * 2 * WIDTH
            + tm * (2 * 4 + 2) * FF_SLAB_MAX + 5 * 4 * tm * D_MODEL)
    return pl.pallas_call(
        _out_ffn2_kernel,
        grid=(n_g, n_r),
        in_specs=[
            pl.BlockSpec((gt, rt, D_MODEL), lambda gi, ri: (gi, ri, 0)),
            pl.BlockSpec((tm, 2 * WIDTH), lambda gi, ri: (gi * n_r + ri, 0)),
            pl.BlockSpec((N_MOD, gt, 1, D_MODEL), lambda gi, ri: (0, gi, 0, 0)),
            _resident(gains.shape), _resident(gg.shape), _resident(wout.shape),
            _resident(wg.shape), _resident(wu.shape), _resident(wd.shape),
        ],
        out_specs=pl.BlockSpec((gt, rt, D_MODEL), lambda gi, ri: (gi, ri, 0)),
        out_shape=jax.ShapeDtypeStruct(x1.shape, F32),
        compiler_params=pltpu.CompilerParams(
            dimension_semantics=("arbitrary", "arbitrary"), vmem_limit_bytes=_vmem_limit(vmem)),
        name="out_ffn2",
    )(x1, o, mod, gains, gg, wout, wg, wu, wd)


def _softmax_terms(s, sink=None):
    m = jnp.max(s, axis=1, keepdims=True)
    if sink is not None:
        m = jnp.maximum(m, sink)
    e = jnp.exp(s - m)
    denom = jnp.sum(e, axis=1, keepdims=True)
    if sink is not None:
        denom = denom + jnp.exp(sink - m)
    return e.astype(BF16), 1.0 / denom


def _scores(q, k):
    return lax.dot_general(q, k, (((1,), (1,)), ((), ())), preferred_element_type=F32)


def _unstack_pair(z):
    n_q = z.shape[0] // 2
    is_lo = lax.broadcasted_iota(jnp.int32, (n_q, z.shape[1]), 1) < HEAD_DIM
    return jnp.where(is_lo, z[:n_q], z[n_q:])


def _sink_column(sinks_ref, pair, n_q):
    row = lax.broadcasted_iota(jnp.int32, (2 * n_q, 1), 0)
    return jnp.where(row < n_q, sinks_ref[pair], sinks_ref[N_PAIRS + pair])


def _attn_prompt_kernel(sinks_ref, qa_ref, qb_ref, ka_ref, va_ref, kb_ref, vb_ref, ba_ref, bb_ref, o_ref):
    g = pl.program_id(1)
    start_a = pl.multiple_of(jnp.maximum(g * GROUP_Q - REACH_A, 0), GROUP_Q)
    start_b = pl.multiple_of(jnp.maximum(g * GROUP_Q - REACH_B, 0), GROUP_Q)
    slab = lambda p: slice(p * V7X_LANES, (p + 1) * V7X_LANES)
    band_a = pl.ds(start_a, BAND_A)
    band_b = pl.ds(start_b, BAND_B)

    kb = kb_ref[band_b, :]
    vb = vb_ref[band_b, :]

    def stacked(q_ref, p):
        return jnp.concatenate([q_ref[:, slab(p)], q_ref[:, WIDTH + p * V7X_LANES:WIDTH + (p + 1) * V7X_LANES]],
                               axis=0)

    def score(mixer, p):
        if mixer == "a":
            bias = jnp.concatenate([ba_ref[0, 2 * p], ba_ref[0, 2 * p + 1]], axis=0)
            return _scores(stacked(qa_ref, p), ka_ref[band_a, slab(p)]) + bias
        bias = jnp.concatenate([bb_ref[0, p], bb_ref[0, N_PAIRS + p]], axis=0)
        return _scores(stacked(qb_ref, p), kb) + bias

    def weights(s, sink):
        m = jnp.max(s, axis=1, keepdims=True)
        if sink is None:
            return jnp.exp(s - m).astype(BF16), None
        m = jnp.maximum(m, sink)
        return jnp.exp(s - m).astype(BF16), jnp.exp(sink - m)

    def attend(e, extra, v):
        z = jnp.dot(e, jnp.concatenate([v, jnp.ones_like(v)], axis=1), preferred_element_type=F32)
        denom = z[:, V7X_LANES:] if extra is None else z[:, V7X_LANES:] + extra
        return _unstack_pair(z[:, :V7X_LANES] * (1.0 / denom))

    jobs = [("a", p) for p in range(N_PAIRS)] + [("b", p) for p in range(N_PAIRS)]
    s_all = [score(mixer, p) for mixer, p in jobs]
    w_all = [weights(s, _sink_column(sinks_ref, p, GROUP_Q) if mixer == "b" else None)
             for s, (mixer, p) in zip(s_all, jobs)]
    for (e, extra), (mixer, p) in zip(w_all, jobs):
        if mixer == "a":
            o_ref[:, slab(p)] = attend(e, extra, va_ref[band_a, slab(p)]).astype(BF16)
        else:
            o_ref[:, slab(N_PAIRS + p)] = attend(e, extra, vb).astype(BF16)


def _attn_prompt(sinks, qa, qb, ka, va, kb, vb, bias_a, bias_b, batch, seq):
    n_grp = seq // GROUP_Q
    q_spec = pl.BlockSpec((GROUP_Q, 2 * WIDTH), lambda b, g: (b * n_grp + g, 0))
    kv_spec = lambda w: pl.BlockSpec((seq, w), lambda b, g: (b, 0), pipeline_mode=pl.Buffered(1))
    vmem = (2 * seq * (2 * WIDTH + 2 * KV_WIDTH_B)
            + 2 * 4 * N_HEADS * GROUP_Q * (BAND_A + BAND_B)
            + 2 * 2 * GROUP_Q * 4 * WIDTH + 16 * 4 * 2 * GROUP_Q * BAND_A)
    return pl.pallas_call(
        _attn_prompt_kernel,
        grid=(batch, n_grp),
        in_specs=[
            pl.BlockSpec(memory_space=pltpu.SMEM),
            q_spec, q_spec,
            kv_spec(WIDTH), kv_spec(WIDTH), kv_spec(KV_WIDTH_B), kv_spec(KV_WIDTH_B),
            pl.BlockSpec((1, N_HEADS, GROUP_Q, BAND_A),
                         lambda b, g: (jnp.minimum(g, N_VAR_A - 1), 0, 0, 0)),
            pl.BlockSpec((1, N_HEADS, GROUP_Q, BAND_B),
                         lambda b, g: (jnp.minimum(g, N_VAR_B - 1), 0, 0, 0)),
        ],
        out_specs=pl.BlockSpec((GROUP_Q, 2 * WIDTH), lambda b, g: (b * n_grp + g, 0)),
        out_shape=jax.ShapeDtypeStruct((batch * seq, 2 * WIDTH), BF16),
        compiler_params=pltpu.CompilerParams(
            dimension_semantics=("arbitrary", "arbitrary"), vmem_limit_bytes=_vmem_limit(vmem)),
        name="attn_prompt",
    )(sinks, qa, qb, ka, va, kb, vb, bias_a, bias_b)


def _pair_ordered_cols(h):
    return (h % N_PAIRS) * V7X_LANES + (h // N_PAIRS) * HEAD_DIM


def _attn_sample_kernel(sinks_ref, qa_ref, qb_ref, ka_ref, va_ref, kb_ref, vb_ref,
                        cak_ref, cav_ref, cbk_ref, cbv_ref, ba_ref, bb_ref, o_ref):
    n_q = qa_ref.shape[0]
    n_kv = cbk_ref.shape[0]
    per_kv = N_HEADS // n_kv
    la = cak_ref.shape[2]
    head = lambda ref, idx: ref[:, idx * HEAD_DIM:(idx + 1) * HEAD_DIM]

    def query(ref, lo):
        lo = lo if lo % V7X_LANES < HEAD_DIM else WIDTH + lo
        return ref[:, lo:lo + HEAD_DIM]

    def scores(q, cache_t_ref, new_ref, idx):
        old = jnp.dot(q, cache_t_ref[idx].astype(BF16), preferred_element_type=F32)
        return jnp.concatenate([old, _scores(q, head(new_ref, idx))], axis=1)

    def values(e, cache_t_ref, new_ref, idx, n_old):
        old = _scores(e[:, :n_old], cache_t_ref[idx].astype(BF16))
        return old + jnp.dot(e[:, n_old:], head(new_ref, idx), preferred_element_type=F32)

    s_a = [scores(query(qa_ref, h * HEAD_DIM), cak_ref, ka_ref, h) + ba_ref[h] for h in range(N_HEADS)]
    s_b = []
    for c in range(n_kv):
        heads = range(c * per_kv, (c + 1) * per_kv)
        q = jnp.concatenate([query(qb_ref, _pair_ordered_cols(h)) for h in heads], axis=0)
        bias = jnp.concatenate([bb_ref[h] for h in heads], axis=0)
        s_b.append(scores(q, cbk_ref, kb_ref, c) + bias)

    row = lax.broadcasted_iota(jnp.int32, (per_kv * n_q, 1), 0)
    p_a = [_softmax_terms(s) for s in s_a]
    p_b = []
    for c in range(n_kv):
        sink = jnp.zeros((per_kv * n_q, 1), F32)
        for i in range(per_kv):
            sink = jnp.where((row >= i * n_q) & (row < (i + 1) * n_q), sinks_ref[c * per_kv + i], sink)
        p_b.append(_softmax_terms(s_b[c], sink))

    out_a = [values(e, cav_ref, va_ref, h, la) * r for h, (e, r) in enumerate(p_a)]
    out_b = [None] * N_HEADS
    for c, (e, r) in enumerate(p_b):
        o = values(e, cbv_ref, vb_ref, c, cbv_ref.shape[2]) * r
        for i in range(per_kv):
            out_b[c * per_kv + i] = o[i * n_q:(i + 1) * n_q]
    order_b = [h for p in range(N_PAIRS) for h in (p, N_PAIRS + p)]
    o_ref[...] = jnp.concatenate(out_a + [out_b[h] for h in order_b], axis=1).astype(BF16)


def _attn_sample(sinks, qa, qb, ka, va, kb, vb, cak, cav, cbk, cbv, bias_a, bias_b, batch, s_len):
    row = lambda w: pl.BlockSpec((s_len, w), lambda b: (b, 0))
    cache = lambda a: pl.BlockSpec((None,) + a.shape[1:], lambda b: (b, 0, 0, 0))
    whole = lambda a: pl.BlockSpec(a.shape, lambda b: (0,) * a.ndim)
    return pl.pallas_call(
        _attn_sample_kernel,
        grid=(batch,),
        in_specs=[
            pl.BlockSpec(memory_space=pltpu.SMEM),
            row(2 * WIDTH), row(2 * WIDTH), row(WIDTH), row(WIDTH), row(KV_WIDTH_B), row(KV_WIDTH_B),
            cache(cak), cache(cav), cache(cbk), cache(cbv),
            whole(bias_a), whole(bias_b),
        ],
        out_specs=pl.BlockSpec((s_len, 2 * WIDTH), lambda b: (b, 0)),
        out_shape=jax.ShapeDtypeStruct((batch * s_len, 2 * WIDTH), BF16),
        compiler_params=pltpu.CompilerParams(dimension_semantics=("arbitrary",)),
        name="attn_sample",
    )(sinks, qa, qb, ka, va, kb, vb, cak, cav, cbk, cbv, bias_a, bias_b)


def _pair_order(x, axis):
    shape = x.shape
    split = shape[:axis] + (N_HEADS // N_PAIRS, N_PAIRS, HEAD_DIM) + shape[axis + 1:]
    return jnp.swapaxes(x.reshape(split), axis, axis + 1).reshape(shape)


def _prep_weights(w_in, w_out, group_gains):
    qa, ka, va, qb, kvb = jnp.split(w_in, (WIDTH, 2 * WIDTH, 3 * WIDTH, 4 * WIDTH), axis=1)
    win = jnp.concatenate([qa * SCALE, ka, va, _pair_order(qb, 1) * SCALE, kvb], axis=1).astype(BF16)
    wout = jnp.concatenate([w_out[:WIDTH], _pair_order(w_out[WIDTH:], 0)], axis=0).astype(BF16)
    gg = jnp.concatenate([group_gains[:WIDTH], _pair_order(group_gains[WIDTH:], 0)])
    return win, wout, gg.reshape(1, 2 * WIDTH)


def kernel(x_prompt, x_sample, cache_a_k, cache_a_v, cache_b_k, cache_b_v, c_prompt, c_sample, w_mod, b_mod,
           norm_gains, w1_gate, w1_up, w1_down, w_in, w_out, group_gains, rel_bias_a, t5_bias_table, sinks_b,
           w2_gate, w2_up, w2_down):
    depth = w_mod.shape[0]
    assert depth == 1
    batch, seq, _ = x_prompt.shape
    s_batch, s_len, _ = x_sample.shape
    la, lb = cache_a_k.shape[2], cache_b_k.shape[2]
    assert la == REACH_A and lb == REACH_B and seq % 512 == 0

    c_rows = jnp.concatenate([c_prompt, c_sample], axis=0)
    pad_rows = (-c_rows.shape[0]) % 8
    c_rows = jnp.pad(c_rows, ((0, pad_rows), (0, 0)))
    mod = _modulation(c_rows, w_mod[0], b_mod[0])
    mod_p = mod[:, :batch].reshape(N_MOD, batch, 1, D_MODEL)
    mod_s = mod[:, batch:batch + s_batch].reshape(N_MOD, s_batch, 1, D_MODEL)

    bias_a, bias_b, bias_sa, bias_sb = _bias_tables(rel_bias_a[0], t5_bias_table, s_len, la, lb)

    gains = norm_gains[0]
    w1 = (w1_gate[0].astype(BF16), w1_up[0].astype(BF16), w1_down[0].astype(BF16))
    w2 = (w2_gate[0].astype(BF16), w2_up[0].astype(BF16), w2_down[0].astype(BF16))
    win, wout, gg = _prep_weights(w_in[0], w_out[0], group_gains[0])
    sinks = sinks_b[0]

    rt_p = 512
    (x1p, qa, ka, va, qb, kb, vb, ka32, va32, kb32, vb32) = _ffn1_qkv(
        x_prompt, mod_p, gains, *w1, win, 1, rt_p)
    op = _attn_prompt(sinks, qa, qb, ka, va, kb, vb, bias_a, bias_b, batch, seq)
    y_prompt = _out_ffn2(x1p, op, mod_p, gains, gg, wout, *w2, 1, rt_p)
    new_p = (ka32.reshape(depth, batch, la, N_HEADS, HEAD_DIM), va32.reshape(depth, batch, la, N_HEADS, HEAD_DIM),
             kb32.reshape(depth, batch, lb, 2, HEAD_DIM), vb32.reshape(depth, batch, lb, 2, HEAD_DIM))

    gt_s = 16
    (x1s, qas, kas, vas, qbs, kbs, vbs, kas32, vas32, kbs32, vbs32) = _ffn1_qkv(
        x_sample, mod_s, gains, *w1, win, gt_s, s_len)
    os_ = _attn_sample(sinks, qas, qbs, kas, vas, kbs, vbs,
                       *(jnp.transpose(c[0], (0, 2, 3, 1)) for c in (cache_a_k, cache_a_v, cache_b_k, cache_b_v)),
                       bias_sa, bias_sb, s_batch, s_len)
    y_sample = _out_ffn2(x1s, os_, mod_s, gains, gg, wout, *w2, gt_s, s_len)
    new_s = (kas32.reshape(depth, s_batch, s_len, N_HEADS, HEAD_DIM),
             vas32.reshape(depth, s_batch, s_len, N_HEADS, HEAD_DIM),
             kbs32.reshape(depth, s_batch, s_len, 2, HEAD_DIM), vbs32.reshape(depth, s_batch, s_len, 2, HEAD_DIM))

    return (y_prompt, y_sample) + new_p + new_s
```

```python
import functools

import jax
import jax.numpy as jnp
from jax import lax
from jax.experimental import pallas as pl
from jax.experimental.pallas import tpu as pltpu

F32 = jnp.float32
BF16 = jnp.bfloat16

D_MODEL = 1024
D_FF = 2816
CHUNK = 64
HEAD_DIM = 64
N_HEADS = 8
N_PAIRS = N_HEADS // 2
WIDTH = N_HEADS * HEAD_DIM
KV_WIDTH_B = 2 * HEAD_DIM
REACH_A = 8 * CHUNK
REACH_B = 2 * CHUNK
REL_CLIP_A = 128
N_BUCKETS = 32
N_MOD = 9
FFN_RES = 0.5
EPS = 1e-6
SCALE = HEAD_DIM ** -0.5
NEG_INF = -1e30

V7X_LANES = 128
V7X_VMEM_BYTES = 64 * 1024 * 1024
V7X_VMEM_LIMIT_CAP = 56 * 1024 * 1024

GROUP_Q = 2 * CHUNK
BAND_A = REACH_A + GROUP_Q
BAND_B = REACH_B + GROUP_Q
ORIGIN_A = 1024
WIDTH_MASTER_A = ORIGIN_A + BAND_A
ORIGIN_B = 256
WIDTH_MASTER_B = ORIGIN_B + BAND_B
N_VAR_A = REACH_A // GROUP_Q + 1
N_VAR_B = REACH_B // GROUP_Q + 1


def _vmem_limit(estimate_bytes):
    return int(min(max(estimate_bytes, 32 * 1024 * 1024), V7X_VMEM_LIMIT_CAP))


def _rms(x, gain):
    ms = jnp.mean(x * x, axis=-1, keepdims=True)
    return x * lax.rsqrt(ms + EPS) * gain


def _adaln(x, gain, shift, scale):
    return _rms(x, gain) * (1.0 + scale) + shift


def _silu(x):
    return x * (1.0 / (1.0 + jnp.exp(-x)))


def _mod_kernel(c_ref, w_ref, b_ref, o_ref):
    s = _silu(c_ref[...]).astype(BF16)
    o_ref[0] = jnp.dot(s, w_ref[...].astype(BF16), preferred_element_type=F32) + b_ref[0]


def _modulation(c_rows, w_mod, b_mod):
    rows = c_rows.shape[0]
    return pl.pallas_call(
        _mod_kernel,
        grid=(N_MOD,),
        in_specs=[
            pl.BlockSpec((rows, D_MODEL), lambda j: (0, 0)),
            pl.BlockSpec((D_MODEL, D_MODEL), lambda j: (0, j)),
            pl.BlockSpec((1, 1, D_MODEL), lambda j: (j, 0, 0)),
        ],
        out_specs=pl.BlockSpec((1, rows, D_MODEL), lambda j: (j, 0, 0)),
        out_shape=jax.ShapeDtypeStruct((N_MOD, rows, D_MODEL), F32),
        name="modulation",
    )(c_rows, w_mod, b_mod.reshape(N_MOD, 1, D_MODEL))


def _skew(base_row, n_rows):
    width = base_row.shape[1]
    m = jnp.broadcast_to(base_row, (n_rows, width))
    row = lax.broadcasted_iota(jnp.int32, (n_rows, width), 0)
    shift = 1
    while shift < n_rows:
        m = jnp.where((row & shift) != 0, pltpu.roll(m, shift, axis=1), m)
        shift *= 2
    return m


def _band_mask(n_rows, width, origin, n_prev):
    row = lax.broadcasted_iota(jnp.int32, (n_rows, width), 0)
    col = lax.broadcasted_iota(jnp.int32, (n_rows, width), 1)
    q_chunk = row >> 6
    k_chunk = (col >> 6) - (origin // CHUNK)
    return (k_chunk >= q_chunk - n_prev) & (k_chunk <= q_chunk)


def _t5_bucket(rel):
    half = N_BUCKETS // 2
    max_exact = half // 2
    n = jnp.abs(rel)
    n2 = n * n
    large = jnp.full(rel.shape, max_exact, jnp.int32)
    for k in range(1, half - max_exact):
        large = large + jnp.where(n2 >= (max_exact * max_exact) * (2 ** k), 1, 0)
    return jnp.where(rel > 0, half, 0) + jnp.where(n < max_exact, n, large)


def _bias_kernel(rel_ref, t5_ref, ba_ref, bb_ref, bsa_ref, bsb_ref, *, s_len, la, lb):
    tab = rel_ref[0]
    n_tab = 2 * REL_CLIP_A + 1
    first, last = tab[:, 0:1], tab[:, n_tab - 1:n_tab]
    win_lo = ORIGIN_A - 2 * REL_CLIP_A
    col = lax.broadcasted_iota(jnp.int32, (1, tab.shape[1] + V7X_LANES), 1)
    shifted = jnp.concatenate([jnp.zeros((1, V7X_LANES), F32), tab], axis=1)
    u_win = jnp.where(col < V7X_LANES, first, jnp.where(col >= V7X_LANES + n_tab, last, shifted))
    window = _skew(u_win, GROUP_Q)[:, V7X_LANES:]
    n_left = win_lo + V7X_LANES
    n_right = WIDTH_MASTER_A - n_left - window.shape[1]
    skew_a = jnp.concatenate([jnp.broadcast_to(first, (GROUP_Q, n_left)), window,
                              jnp.broadcast_to(last, (GROUP_Q, n_right))], axis=1)
    master_a = jnp.where(_band_mask(GROUP_Q, WIDTH_MASTER_A, ORIGIN_A, REACH_A // CHUNK), skew_a, NEG_INF)
    for v in range(N_VAR_A):
        start = ORIGIN_A - min(v * GROUP_Q, REACH_A) if v < N_VAR_A - 1 else ORIGIN_A - REACH_A
        ba_ref[v, 0] = master_a[:, start:start + BAND_A]
    bsa_ref[0] = skew_a[:s_len, ORIGIN_A - la:ORIGIN_A + s_len]

    t5 = t5_ref[0]
    rel = lax.broadcasted_iota(jnp.int32, (1, WIDTH_MASTER_B), 1) - ORIGIN_B
    bucket = _t5_bucket(rel)
    u_b = jnp.zeros((1, WIDTH_MASTER_B), F32)
    for i in range(N_BUCKETS):
        u_b = jnp.where(bucket == i, t5[:, i:i + 1], u_b)
    skew_b = _skew(u_b, GROUP_Q)
    master_b = jnp.where(_band_mask(GROUP_Q, WIDTH_MASTER_B, ORIGIN_B, REACH_B // CHUNK), skew_b, NEG_INF)
    for v in range(N_VAR_B):
        start = ORIGIN_B - min(v * GROUP_Q, REACH_B) if v < N_VAR_B - 1 else ORIGIN_B - REACH_B
        bb_ref[v, 0] = master_b[:, start:start + BAND_B]
    bsb_ref[0] = skew_b[:s_len, ORIGIN_B - lb:ORIGIN_B + s_len]


def _bias_tables(rel_bias_a, t5_table, s_len, la, lb):
    n_tab = 2 * REL_CLIP_A + 1
    tab_w = 3 * V7X_LANES
    rel_p = jnp.pad(rel_bias_a, ((0, 0), (0, tab_w - n_tab))).reshape(N_HEADS, 1, tab_w)
    t5_p = t5_table.reshape(N_HEADS, 1, N_BUCKETS)
    kern = functools.partial(_bias_kernel, s_len=s_len, la=la, lb=lb)
    return pl.pallas_call(
        kern,
        grid=(N_HEADS,),
        in_specs=[
            pl.BlockSpec((1, 1, tab_w), lambda h: (h, 0, 0)),
            pl.BlockSpec((1, 1, N_BUCKETS), lambda h: (h, 0, 0)),
        ],
        out_specs=[
            pl.BlockSpec((N_VAR_A, 1, GROUP_Q, BAND_A), lambda h: (0, h, 0, 0)),
            pl.BlockSpec((N_VAR_B, 1, GROUP_Q, BAND_B), lambda h: (0, h, 0, 0)),
            pl.BlockSpec((1, s_len, la + s_len), lambda h: (h, 0, 0)),
            pl.BlockSpec((1, s_len, lb + s_len), lambda h: (h, 0, 0)),
        ],
        out_shape=[
            jax.ShapeDtypeStruct((N_VAR_A, N_HEADS, GROUP_Q, BAND_A), F32),
            jax.ShapeDtypeStruct((N_VAR_B, N_HEADS, GROUP_Q, BAND_B), F32),
            jax.ShapeDtypeStruct((N_HEADS, s_len, la + s_len), F32),
            jax.ShapeDtypeStruct((N_HEADS, s_len, lb + s_len), F32),
        ],
        name="bias_tables",
    )(rel_p, t5_p)


V7X_MXU_DIM = 256
FF_SLABS = ((0, 6 * V7X_MXU_DIM), (6 * V7X_MXU_DIM, D_FF))
FF_SLAB_MAX = max(hi - lo for lo, hi in FF_SLABS)


def _swiglu(h, wg_ref, wu_ref, wd_ref):
    acts = []
    for lo, hi in FF_SLABS:
        g = jnp.dot(h, wg_ref[:, lo:hi], preferred_element_type=F32)
        u = jnp.dot(h, wu_ref[:, lo:hi], preferred_element_type=F32)
        acts.append((_silu(g) * u).astype(BF16))
    acc = None
    for (lo, hi), a in zip(FF_SLABS, acts):
        part = jnp.dot(a, wd_ref[lo:hi, :], preferred_element_type=F32)
        acc = part if acc is None else acc + part
    return acc


N_SUB = 2


def _sub_tiles(gt, rt):
    if gt == 1:
        step = rt // N_SUB
        return [(slice(0, 1), slice(s * step, (s + 1) * step), s * step, step) for s in range(N_SUB)]
    step = gt // N_SUB
    return [(slice(s * step, (s + 1) * step), slice(0, rt), s * step * rt, step * rt) for s in range(N_SUB)]


def _ffn1_qkv_kernel(x_ref, mod_ref, gains_ref, wg_ref, wu_ref, wd_ref, win_ref,
                     x1_ref, qa_ref, ka_ref, va_ref, qb_ref, kb_ref, vb_ref,
                     ka32_ref, va32_ref, kb32_ref, vb32_ref):
    gt, rt, _ = x_ref.shape
    tm = gt * rt
    gain = lambda k: gains_ref[k:k + 1, :]

    def store_tail(ref, val, r0):
        first = tm - ref.shape[0]
        lo = max(r0, first)
        if lo < r0 + val.shape[0]:
            ref[lo - first:r0 + val.shape[0] - first, :] = val[lo - r0:, :]

    def store_split(ref, rows, q):
        is_lo = (lax.broadcasted_iota(jnp.int32, q.shape, 1) & (V7X_LANES - 1)) < HEAD_DIM
        ref[rows, :WIDTH] = jnp.where(is_lo, q, 0.0).astype(BF16)
        ref[rows, WIDTH:] = jnp.where(is_lo, 0.0, q).astype(BF16)

    subs = _sub_tiles(gt, rt)
    mods = [lambda k, gs=gs: mod_ref[k, gs] for gs, _, _, _ in subs]
    xs = [x_ref[gs, rs, :] for gs, rs, _, _ in subs]
    hs = [_adaln(x, gain(0), mod(0), mod(1)).reshape(n, D_MODEL).astype(BF16)
          for x, mod, (_, _, _, n) in zip(xs, mods, subs)]
    fs = [_swiglu(h, wg_ref, wu_ref, wd_ref).reshape(x.shape) for h, x in zip(hs, xs)]
    h2s = []
    for x, f, mod, (gs, rs, _, n) in zip(xs, fs, mods, subs):
        x1 = x + (FFN_RES * mod(2)) * _rms(f, gain(1))
        x1_ref[gs, rs, :] = x1
        h2s.append(_adaln(x1, gain(2), mod(3), mod(4)).reshape(n, D_MODEL).astype(BF16))

    for h2, (_, _, r0, n) in zip(h2s, subs):
        rows = slice(r0, r0 + n)

        def proj(lo, width):
            return jnp.dot(h2, win_ref[:, lo:lo + width], preferred_element_type=F32)

        store_split(qa_ref, rows, proj(0, WIDTH))
        ka = proj(WIDTH, WIDTH)
        va = proj(2 * WIDTH, WIDTH)
        store_split(qb_ref, rows, proj(3 * WIDTH, WIDTH))
        kvb = proj(4 * WIDTH, 2 * KV_WIDTH_B)
        kb, vb = kvb[:, :KV_WIDTH_B], kvb[:, KV_WIDTH_B:]
        ka_ref[rows, :] = ka.astype(BF16)
        va_ref[rows, :] = va.astype(BF16)
        kb_ref[rows, :] = kb.astype(BF16)
        vb_ref[rows, :] = vb.astype(BF16)
        store_tail(ka32_ref, ka, r0)
        store_tail(va32_ref, va, r0)
        store_tail(kb32_ref, kb, r0)
        store_tail(vb32_ref, vb, r0)


def _resident(shape):
    return pl.BlockSpec(shape, lambda *_: (0,) * len(shape), pipeline_mode=pl.Buffered(1))


def _ffn1_qkv(x, mod, gains, wg, wu, wd, win, gt, rt):
    n_groups, rows, _ = x.shape
    n_g, n_r = n_groups // gt, rows // rt
    tm = gt * rt
    tokens = n_groups * rows
    ta = gt * min(REACH_A, rows)
    tb = gt * min(REACH_B, rows)
    tile = lambda gi, ri: (gi * n_r + ri, 0)
    tail = lambda gi, ri: (gi, 0)
    tok = lambda w, dt: jax.ShapeDtypeStruct((tokens, w), dt)
    vmem = (2 * 3 * D_MODEL * D_FF + 2 * D_MODEL * win.shape[1]
            + 2 * 2 * 4 * tm * D_MODEL
            + 2 * 2 * tm * (6 * WIDTH + 2 * KV_WIDTH_B)
            + 2 * 4 * (2 * ta * WIDTH + 2 * tb * KV_WIDTH_B)
            + tm * (2 * 4 + 2) * FF_SLAB_MAX + 3 * 4 * tm * D_MODEL
            + 4 * tm * win.shape[1])
    return pl.pallas_call(
        _ffn1_qkv_kernel,
        grid=(n_g, n_r),
        in_specs=[
            pl.BlockSpec((gt, rt, D_MODEL), lambda gi, ri: (gi, ri, 0)),
            pl.BlockSpec((N_MOD, gt, 1, D_MODEL), lambda gi, ri: (0, gi, 0, 0)),
            _resident(gains.shape),
            _resident(wg.shape), _resident(wu.shape), _resident(wd.shape), _resident(win.shape),
        ],
        out_specs=[
            pl.BlockSpec((gt, rt, D_MODEL), lambda gi, ri: (gi, ri, 0)),
            pl.BlockSpec((tm, 2 * WIDTH), tile), pl.BlockSpec((tm, WIDTH), tile), pl.BlockSpec((tm, WIDTH), tile),
            pl.BlockSpec((tm, 2 * WIDTH), tile), pl.BlockSpec((tm, KV_WIDTH_B), tile),
            pl.BlockSpec((tm, KV_WIDTH_B), tile),
            pl.BlockSpec((ta, WIDTH), tail), pl.BlockSpec((ta, WIDTH), tail),
            pl.BlockSpec((tb, KV_WIDTH_B), tail), pl.BlockSpec((tb, KV_WIDTH_B), tail),
        ],
        out_shape=[
            jax.ShapeDtypeStruct(x.shape, F32),
            tok(2 * WIDTH, BF16), tok(WIDTH, BF16), tok(WIDTH, BF16),
            tok(2 * WIDTH, BF16), tok(KV_WIDTH_B, BF16), tok(KV_WIDTH_B, BF16),
            jax.ShapeDtypeStruct((n_g * ta, WIDTH), F32), jax.ShapeDtypeStruct((n_g * ta, WIDTH), F32),
            jax.ShapeDtypeStruct((n_g * tb, KV_WIDTH_B), F32), jax.ShapeDtypeStruct((n_g * tb, KV_WIDTH_B), F32),
        ],
        compiler_params=pltpu.CompilerParams(
            dimension_semantics=("arbitrary", "arbitrary"), vmem_limit_bytes=_vmem_limit(vmem)),
        name="ffn1_qkv",
    )(x, mod, gains, wg, wu, wd, win)


def _out_ffn2_kernel(x_ref, o_ref, mod_ref, gains_ref, gg_ref, wout_ref, wg_ref, wu_ref, wd_ref, y_ref):
    gt, rt, _ = x_ref.shape
    gain = lambda k: gains_ref[k:k + 1, :]
    gg = gg_ref[...]
    subs = _sub_tiles(gt, rt)
    mods = [lambda k, gs=gs: mod_ref[k, gs] for gs, _, _, _ in subs]
    xs = [x_ref[gs, rs, :] for gs, rs, _, _ in subs]
    ons = []
    for _, _, r0, n in subs:
        o = o_ref[r0:r0 + n, :].astype(F32)
        on = jnp.concatenate([_rms(o[:, :WIDTH], gg[:, :WIDTH]), _rms(o[:, WIDTH:], gg[:, WIDTH:])], axis=1)
        ons.append(on.astype(BF16))
    mixed = [jnp.dot(on, wout_ref[...], preferred_element_type=F32).reshape(x.shape) for on, x in zip(ons, xs)]
    x2s = [x + mod(5) * _rms(m, gain(3)) for x, m, mod in zip(xs, mixed, mods)]
    hs = [_adaln(x2, gain(4), mod(6), mod(7)).reshape(n, D_MODEL).astype(BF16)
          for x2, mod, (_, _, _, n) in zip(x2s, mods, subs)]
    fs = [_swiglu(h, wg_ref, wu_ref, wd_ref).reshape(x.shape) for h, x in zip(hs, xs)]
    for x2, f, mod, (gs, rs, _, _) in zip(x2s, fs, mods, subs):
        y_ref[gs, rs, :] = x2 + (FFN_RES * mod(8)) * _rms(f, gain(5))


def _out_ffn2(x1, o, mod, gains, gg, wout, wg, wu, wd, gt, rt):
    n_groups, rows, _ = x1.shape
    n_g, n_r = n_groups // gt, rows // rt
    tm = gt * rt
    vmem = (2 * 3 * D_MODEL * D_FF + 2 * wout.size
            + 2 * 2 * 4 * tm * D_MODEL + 2 * 2 * tm * 2 * WIDTH
            + tm * (2 * 4 + 2) * FF_SLAB_MAX + 5 * 4 * tm * D_MODEL)
    return pl.pallas_call(
        _out_ffn2_kernel,
        grid=(n_g, n_r),
        in_specs=[
            pl.BlockSpec((gt, rt, D_MODEL), lambda gi, ri: (gi, ri, 0)),
            pl.BlockSpec((tm, 2 * WIDTH), lambda gi, ri: (gi * n_r + ri, 0)),
            pl.BlockSpec((N_MOD, gt, 1, D_MODEL), lambda gi, ri: (0, gi, 0, 0)),
            _resident(gains.shape), _resident(gg.shape), _resident(wout.shape),
            _resident(wg.shape), _resident(wu.shape), _resident(wd.shape),
        ],
        out_specs=pl.BlockSpec((gt, rt, D_MODEL), lambda gi, ri: (gi, ri, 0)),
        out_shape=jax.ShapeDtypeStruct(x1.shape, F32),
        compiler_params=pltpu.CompilerParams(
            dimension_semantics=("arbitrary", "arbitrary"), vmem_limit_bytes=_vmem_limit(vmem)),
        name="out_ffn2",
    )(x1, o, mod, gains, gg, wout, wg, wu, wd)


def _softmax_terms(s, sink=None):
    m = jnp.max(s, axis=1, keepdims=True)
    if sink is not None:
        m = jnp.maximum(m, sink)
    e = jnp.exp(s - m)
    denom = jnp.sum(e, axis=1, keepdims=True)
    if sink is not None:
        denom = denom + jnp.exp(sink - m)
    return e.astype(BF16), 1.0 / denom


def _scores(q, k):
    return lax.dot_general(q, k, (((1,), (1,)), ((), ())), preferred_element_type=F32)


def _unstack_pair(z):
    n_q = z.shape[0] // 2
    is_lo = lax.broadcasted_iota(jnp.int32, (n_q, z.shape[1]), 1) < HEAD_DIM
    return jnp.where(is_lo, z[:n_q], z[n_q:])


def _sink_column(sinks_ref, pair, n_q):
    row = lax.broadcasted_iota(jnp.int32, (2 * n_q, 1), 0)
    return jnp.where(row < n_q, sinks_ref[pair], sinks_ref[N_PAIRS + pair])


GROUPS_PER_STEP = 2


def _attn_prompt_kernel(sinks_ref, qa_ref, qb_ref, ka_ref, va_ref, kb_ref, vb_ref, *rest):
    ba_refs, bb_refs, o_ref = rest[:GROUPS_PER_STEP], rest[GROUPS_PER_STEP:2 * GROUPS_PER_STEP], rest[-1]
    slab = lambda p: slice(p * V7X_LANES, (p + 1) * V7X_LANES)
    bands = []
    for j in range(GROUPS_PER_STEP):
        g = pl.program_id(1) * GROUPS_PER_STEP + j
        start_a = pl.multiple_of(jnp.maximum(g * GROUP_Q - REACH_A, 0), GROUP_Q)
        start_b = pl.multiple_of(jnp.maximum(g * GROUP_Q - REACH_B, 0), GROUP_Q)
        bands.append((pl.ds(start_a, BAND_A), pl.ds(start_b, BAND_B)))

    def stacked(q_ref, j, p):
        rows = slice(j * GROUP_Q, (j + 1) * GROUP_Q)
        return jnp.concatenate([q_ref[rows, slab(p)], q_ref[rows, slab(N_PAIRS + p)]], axis=0)

    def score(mixer, j, p):
        if mixer == "a":
            bias = jnp.concatenate([ba_refs[j][0, 2 * p], ba_refs[j][0, 2 * p + 1]], axis=0)
            return _scores(stacked(qa_ref, j, p), ka_ref[bands[j][0], slab(p)]) + bias
        bias = jnp.concatenate([bb_refs[j][0, p], bb_refs[j][0, N_PAIRS + p]], axis=0)
        return _scores(stacked(qb_ref, j, p), kb_ref[bands[j][1], :]) + bias

    def weights(s, sink):
        m = jnp.max(s, axis=1, keepdims=True)
        if sink is None:
            return jnp.exp(s - m).astype(BF16), None
        m = jnp.maximum(m, sink)
        return jnp.exp(s - m).astype(BF16), jnp.exp(sink - m)

    def attend(e, extra, v):
        z = jnp.dot(e, jnp.concatenate([v, jnp.ones_like(v)], axis=1), preferred_element_type=F32)
        denom = z[:, V7X_LANES:] if extra is None else z[:, V7X_LANES:] + extra
        return _unstack_pair(z[:, :V7X_LANES] * (1.0 / denom))

    jobs = [(mixer, j, p) for mixer in ("a", "b") for j in range(GROUPS_PER_STEP) for p in range(N_PAIRS)]
    s_all = [score(*job) for job in jobs]
    w_all = [weights(s, _sink_column(sinks_ref, p, GROUP_Q) if mixer == "b" else None)
             for s, (mixer, j, p) in zip(s_all, jobs)]
    for (e, extra), (mixer, j, p) in zip(w_all, jobs):
        rows = slice(j * GROUP_Q, (j + 1) * GROUP_Q)
        if mixer == "a":
            o_ref[rows, slab(p)] = attend(e, extra, va_ref[bands[j][0], slab(p)]).astype(BF16)
        else:
            o_ref[rows, slab(N_PAIRS + p)] = attend(e, extra, vb_ref[bands[j][1], :]).astype(BF16)


def _attn_prompt(sinks, qa, qb, ka, va, kb, vb, bias_a, bias_b, batch, seq):
    rows = GROUPS_PER_STEP * GROUP_Q
    n_steps = seq // rows
    q_spec = pl.BlockSpec((rows, 2 * WIDTH), lambda b, s: (b * n_steps + s, 0))
    kv_spec = lambda w: pl.BlockSpec((seq, w), lambda b, s: (b, 0), pipeline_mode=pl.Buffered(1))
    bias_spec = lambda band, n_var, j: pl.BlockSpec(
        (1, N_HEADS, GROUP_Q, band),
        lambda b, s: (jnp.minimum(s * GROUPS_PER_STEP + j, n_var - 1), 0, 0, 0))
    vmem = (2 * seq * (2 * WIDTH + 2 * KV_WIDTH_B)
            + GROUPS_PER_STEP * 2 * 4 * N_HEADS * GROUP_Q * (BAND_A + BAND_B)
            + 2 * 2 * rows * 6 * WIDTH + GROUPS_PER_STEP * 6 * 4 * 2 * GROUP_Q * BAND_A)
    return pl.pallas_call(
        _attn_prompt_kernel,
        grid=(batch, n_steps),
        in_specs=[
            pl.BlockSpec(memory_space=pltpu.SMEM),
            q_spec, q_spec,
            kv_spec(WIDTH), kv_spec(WIDTH), kv_spec(KV_WIDTH_B), kv_spec(KV_WIDTH_B),
            *[bias_spec(BAND_A, N_VAR_A, j) for j in range(GROUPS_PER_STEP)],
            *[bias_spec(BAND_B, N_VAR_B, j) for j in range(GROUPS_PER_STEP)],
        ],
        out_specs=pl.BlockSpec((rows, 2 * WIDTH), lambda b, s: (b * n_steps + s, 0)),
        out_shape=jax.ShapeDtypeStruct((batch * seq, 2 * WIDTH), BF16),
        compiler_params=pltpu.CompilerParams(
            dimension_semantics=("arbitrary", "arbitrary"), vmem_limit_bytes=_vmem_limit(vmem)),
        name="attn_prompt",
    )(sinks, qa, qb, ka, va, kb, vb, *([bias_a] * GROUPS_PER_STEP), *([bias_b] * GROUPS_PER_STEP))


def _pair_ordered_cols(h):
    return (h % N_PAIRS) * V7X_LANES + (h // N_PAIRS) * HEAD_DIM


def _attn_sample_kernel(sinks_ref, qa_ref, qb_ref, ka_ref, va_ref, kb_ref, vb_ref,
                        cak_ref, cav_ref, cbk_ref, cbv_ref, ba_ref, bb_ref, o_ref):
    n_q = qa_ref.shape[0]
    n_kv = cbk_ref.shape[0]
    per_kv = N_HEADS // n_kv
    la = cak_ref.shape[2]
    head = lambda ref, idx: ref[:, idx * HEAD_DIM:(idx + 1) * HEAD_DIM]

    def query(ref, lo):
        lo = lo if lo % V7X_LANES < HEAD_DIM else WIDTH + lo
        return ref[:, lo:lo + HEAD_DIM]

    def scores(q, cache_t_ref, new_ref, idx):
        old = jnp.dot(q, cache_t_ref[idx].astype(BF16), preferred_element_type=F32)
        return jnp.concatenate([old, _scores(q, head(new_ref, idx))], axis=1)

    def values(e, cache_t_ref, new_ref, idx, n_old):
        old = _scores(e[:, :n_old], cache_t_ref[idx].astype(BF16))
        return old + jnp.dot(e[:, n_old:], head(new_ref, idx), preferred_element_type=F32)

    s_a = [scores(query(qa_ref, h * HEAD_DIM), cak_ref, ka_ref, h) + ba_ref[h] for h in range(N_HEADS)]
    s_b = []
    for c in range(n_kv):
        heads = range(c * per_kv, (c + 1) * per_kv)
        q = jnp.concatenate([query(qb_ref, _pair_ordered_cols(h)) for h in heads], axis=0)
        bias = jnp.concatenate([bb_ref[h] for h in heads], axis=0)
        s_b.append(scores(q, cbk_ref, kb_ref, c) + bias)

    row = lax.broadcasted_iota(jnp.int32, (per_kv * n_q, 1), 0)
    p_a = [_softmax_terms(s) for s in s_a]
    p_b = []
    for c in range(n_kv):
        sink = jnp.zeros((per_kv * n_q, 1), F32)
        for i in range(per_kv):
            sink = jnp.where((row >= i * n_q) & (row < (i + 1) * n_q), sinks_ref[c * per_kv + i], sink)
        p_b.append(_softmax_terms(s_b[c], sink))

    out_a = [values(e, cav_ref, va_ref, h, la) * r for h, (e, r) in enumerate(p_a)]
    out_b = [None] * N_HEADS
    for c, (e, r) in enumerate(p_b):
        o = values(e, cbv_ref, vb_ref, c, cbv_ref.shape[2]) * r
        for i in range(per_kv):
            out_b[c * per_kv + i] = o[i * n_q:(i + 1) * n_q]
    order_b = [h for p in range(N_PAIRS) for h in (p, N_PAIRS + p)]
    o_ref[...] = jnp.concatenate(out_a + [out_b[h] for h in order_b], axis=1).astype(BF16)


def _attn_sample(sinks, qa, qb, ka, va, kb, vb, cak, cav, cbk, cbv, bias_a, bias_b, batch, s_len):
    row = lambda w: pl.BlockSpec((s_len, w), lambda b: (b, 0))
    cache = lambda a: pl.BlockSpec((None,) + a.shape[1:], lambda b: (b, 0, 0, 0))
    whole = lambda a: pl.BlockSpec(a.shape, lambda b: (0,) * a.ndim)
    return pl.pallas_call(
        _attn_sample_kernel,
        grid=(batch,),
        in_specs=[
            pl.BlockSpec(memory_space=pltpu.SMEM),
            row(2 * WIDTH), row(2 * WIDTH), row(WIDTH), row(WIDTH), row(KV_WIDTH_B), row(KV_WIDTH_B),
            cache(cak), cache(cav), cache(cbk), cache(cbv),
            whole(bias_a), whole(bias_b),
        ],
        out_specs=pl.BlockSpec((s_len, 2 * WIDTH), lambda b: (b, 0)),
        out_shape=jax.ShapeDtypeStruct((batch * s_len, 2 * WIDTH), BF16),
        compiler_params=pltpu.CompilerParams(dimension_semantics=("arbitrary",)),
        name="attn_sample",
    )(sinks, qa, qb, ka, va, kb, vb, cak, cav, cbk, cbv, bias_a, bias_b)


def _pair_order(x, axis):
    shape = x.shape
    split = shape[:axis] + (N_HEADS // N_PAIRS, N_PAIRS, HEAD_DIM) + shape[axis + 1:]
    return jnp.swapaxes(x.reshape(split), axis, axis + 1).reshape(shape)


def _prep_weights(w_in, w_out, group_gains):
    qa, ka, va, qb, kvb = jnp.split(w_in, (WIDTH, 2 * WIDTH, 3 * WIDTH, 4 * WIDTH), axis=1)
    win = jnp.concatenate([qa * SCALE, ka, va, _pair_order(qb, 1) * SCALE, kvb], axis=1).astype(BF16)
    wout = jnp.concatenate([w_out[:WIDTH], _pair_order(w_out[WIDTH:], 0)], axis=0).astype(BF16)
    gg = jnp.concatenate([group_gains[:WIDTH], _pair_order(group_gains[WIDTH:], 0)])
    return win, wout, gg.reshape(1, 2 * WIDTH)


def kernel(x_prompt, x_sample, cache_a_k, cache_a_v, cache_b_k, cache_b_v, c_prompt, c_sample, w_mod, b_mod,
           norm_gains, w1_gate, w1_up, w1_down, w_in, w_out, group_gains, rel_bias_a, t5_bias_table, sinks_b,
           w2_gate, w2_up, w2_down):
    depth = w_mod.shape[0]
    assert depth == 1
    batch, seq, _ = x_prompt.shape
    s_batch, s_len, _ = x_sample.shape
    la, lb = cache_a_k.shape[2], cache_b_k.shape[2]
    assert la == REACH_A and lb == REACH_B and seq % 512 == 0

    c_rows = jnp.concatenate([c_prompt, c_sample], axis=0)
    pad_rows = (-c_rows.shape[0]) % 8
    c_rows = jnp.pad(c_rows, ((0, pad_rows), (0, 0)))
    mod = _modulation(c_rows, w_mod[0], b_mod[0])
    mod_p = mod[:, :batch].reshape(N_MOD, batch, 1, D_MODEL)
    mod_s = mod[:, batch:batch + s_batch].reshape(N_MOD, s_batch, 1, D_MODEL)

    bias_a, bias_b, bias_sa, bias_sb = _bias_tables(rel_bias_a[0], t5_bias_table, s_len, la, lb)

    gains = norm_gains[0]
    w1 = (w1_gate[0].astype(BF16), w1_up[0].astype(BF16), w1_down[0].astype(BF16))
    w2 = (w2_gate[0].astype(BF16), w2_up[0].astype(BF16), w2_down[0].astype(BF16))
    win, wout, gg = _prep_weights(w_in[0], w_out[0], group_gains[0])
    sinks = sinks_b[0]

    rt_p = 512
    (x1p, qa, ka, va, qb, kb, vb, ka32, va32, kb32, vb32) = _ffn1_qkv(
        x_prompt, mod_p, gains, *w1, win, 1, rt_p)
    op = _attn_prompt(sinks, qa, qb, ka, va, kb, vb, bias_a, bias_b, batch, seq)
    y_prompt = _out_ffn2(x1p, op, mod_p, gains, gg, wout, *w2, 1, rt_p)
    new_p = (ka32.reshape(depth, batch, la, N_HEADS, HEAD_DIM), va32.reshape(depth, batch, la, N_HEADS, HEAD_DIM),
             kb32.reshape(depth, batch, lb, 2, HEAD_DIM), vb32.reshape(depth, batch, lb, 2, HEAD_DIM))

    gt_s = 16
    (x1s, qas, kas, vas, qbs, kbs, vbs, kas32, vas32, kbs32, vbs32) = _ffn1_qkv(
        x_sample, mod_s, gains, *w1, win, gt_s, s_len)
    os_ = _attn_sample(sinks, qas, qbs, kas, vas, kbs, vbs,
                       *(jnp.transpose(c[0], (0, 2, 3, 1)) for c in (cache_a_k, cache_a_v, cache_b_k, cache_b_v)),
                       bias_sa, bias_sb, s_batch, s_len)
    y_sample = _out_ffn2(x1s, os_, mod_s, gains, gg, wout, *w2, gt_s, s_len)
    new_s = (kas32.reshape(depth, s_batch, s_len, N_HEADS, HEAD_DIM),
             vas32.reshape(depth, s_batch, s_len, N_HEADS, HEAD_DIM),
             kbs32.reshape(depth, s_batch, s_len, 2, HEAD_DIM), vbs32.reshape(depth, s_batch, s_len, 2, HEAD_DIM))

    return (y_prompt, y_sample) + new_p + new_s
```

```python
import functools

import jax
import jax.numpy as jnp
from jax import lax
from jax.experimental import pallas as pl
from jax.experimental.pallas import tpu as pltpu

F32 = jnp.float32
BF16 = jnp.bfloat16

D_MODEL = 1024
D_FF = 2816
CHUNK = 64
HEAD_DIM = 64
N_HEADS = 8
N_PAIRS = N_HEADS // 2
WIDTH = N_HEADS * HEAD_DIM
KV_WIDTH_B = 2 * HEAD_DIM
REACH_A = 8 * CHUNK
REACH_B = 2 * CHUNK
REL_CLIP_A = 128
N_BUCKETS = 32
N_MOD = 9
FFN_RES = 0.5
EPS = 1e-6
SCALE = HEAD_DIM ** -0.5
NEG_INF = -1e30

V7X_LANES = 128
V7X_VMEM_BYTES = 64 * 1024 * 1024
V7X_VMEM_LIMIT_CAP = 56 * 1024 * 1024

GROUP_Q = 2 * CHUNK
BAND_A = REACH_A + GROUP_Q
BAND_B = REACH_B + GROUP_Q
ORIGIN_A = 1024
WIDTH_MASTER_A = ORIGIN_A + BAND_A
ORIGIN_B = 256
WIDTH_MASTER_B = ORIGIN_B + BAND_B
N_VAR_A = REACH_A // GROUP_Q + 1
N_VAR_B = REACH_B // GROUP_Q + 1


def _vmem_limit(estimate_bytes):
    return int(min(max(estimate_bytes, 32 * 1024 * 1024), V7X_VMEM_LIMIT_CAP))


def _rms(x, gain):
    ms = jnp.mean(x * x, axis=-1, keepdims=True)
    return x * lax.rsqrt(ms + EPS) * gain


def _adaln(x, gain, shift, scale):
    return _rms(x, gain) * (1.0 + scale) + shift


def _silu(x):
    return x * (1.0 / (1.0 + jnp.exp(-x)))


def _mod_kernel(c_ref, w_ref, b_ref, o_ref):
    s = _silu(c_ref[...]).astype(BF16)
    o_ref[0] = jnp.dot(s, w_ref[...].astype(BF16), preferred_element_type=F32) + b_ref[0]


def _modulation(c_rows, w_mod, b_mod):
    rows = c_rows.shape[0]
    return pl.pallas_call(
        _mod_kernel,
        grid=(N_MOD,),
        in_specs=[
            pl.BlockSpec((rows, D_MODEL), lambda j: (0, 0)),
            pl.BlockSpec((D_MODEL, D_MODEL), lambda j: (0, j)),
            pl.BlockSpec((1, 1, D_MODEL), lambda j: (j, 0, 0)),
        ],
        out_specs=pl.BlockSpec((1, rows, D_MODEL), lambda j: (j, 0, 0)),
        out_shape=jax.ShapeDtypeStruct((N_MOD, rows, D_MODEL), F32),
        name="modulation",
    )(c_rows, w_mod, b_mod.reshape(N_MOD, 1, D_MODEL))


def _skew(base_row, n_rows):
    width = base_row.shape[1]
    m = jnp.broadcast_to(base_row, (n_rows, width))
    row = lax.broadcasted_iota(jnp.int32, (n_rows, width), 0)
    shift = 1
    while shift < n_rows:
        m = jnp.where((row & shift) != 0, pltpu.roll(m, shift, axis=1), m)
        shift *= 2
    return m


def _band_mask(n_rows, width, origin, n_prev):
    row = lax.broadcasted_iota(jnp.int32, (n_rows, width), 0)
    col = lax.broadcasted_iota(jnp.int32, (n_rows, width), 1)
    q_chunk = row >> 6
    k_chunk = (col >> 6) - (origin // CHUNK)
    return (k_chunk >= q_chunk - n_prev) & (k_chunk <= q_chunk)


def _t5_bucket(rel):
    half = N_BUCKETS // 2
    max_exact = half // 2
    n = jnp.abs(rel)
    n2 = n * n
    large = jnp.full(rel.shape, max_exact, jnp.int32)
    for k in range(1, half - max_exact):
        large = large + jnp.where(n2 >= (max_exact * max_exact) * (2 ** k), 1, 0)
    return jnp.where(rel > 0, half, 0) + jnp.where(n < max_exact, n, large)


def _bias_kernel(rel_ref, t5_ref, ba_ref, bb_ref, bsa_ref, bsb_ref, *, s_len, la, lb):
    tab = rel_ref[0]
    n_tab = 2 * REL_CLIP_A + 1
    first, last = tab[:, 0:1], tab[:, n_tab - 1:n_tab]
    win_lo = ORIGIN_A - 2 * REL_CLIP_A
    col = lax.broadcasted_iota(jnp.int32, (1, tab.shape[1] + V7X_LANES), 1)
    shifted = jnp.concatenate([jnp.zeros((1, V7X_LANES), F32), tab], axis=1)
    u_win = jnp.where(col < V7X_LANES, first, jnp.where(col >= V7X_LANES + n_tab, last, shifted))
    window = _skew(u_win, GROUP_Q)[:, V7X_LANES:]
    n_left = win_lo + V7X_LANES
    n_right = WIDTH_MASTER_A - n_left - window.shape[1]
    skew_a = jnp.concatenate([jnp.broadcast_to(first, (GROUP_Q, n_left)), window,
                              jnp.broadcast_to(last, (GROUP_Q, n_right))], axis=1)
    master_a = jnp.where(_band_mask(GROUP_Q, WIDTH_MASTER_A, ORIGIN_A, REACH_A // CHUNK), skew_a, NEG_INF)
    for v in range(N_VAR_A):
        start = ORIGIN_A - min(v * GROUP_Q, REACH_A) if v < N_VAR_A - 1 else ORIGIN_A - REACH_A
        ba_ref[v, 0] = master_a[:, start:start + BAND_A]
    bsa_ref[0] = skew_a[:s_len, ORIGIN_A - la:ORIGIN_A + s_len]

    t5 = t5_ref[0]
    rel = lax.broadcasted_iota(jnp.int32, (1, WIDTH_MASTER_B), 1) - ORIGIN_B
    bucket = _t5_bucket(rel)
    u_b = jnp.zeros((1, WIDTH_MASTER_B), F32)
    for i in range(N_BUCKETS):
        u_b = jnp.where(bucket == i, t5[:, i:i + 1], u_b)
    skew_b = _skew(u_b, GROUP_Q)
    master_b = jnp.where(_band_mask(GROUP_Q, WIDTH_MASTER_B, ORIGIN_B, REACH_B // CHUNK), skew_b, NEG_INF)
    for v in range(N_VAR_B):
        start = ORIGIN_B - min(v * GROUP_Q, REACH_B) if v < N_VAR_B - 1 else ORIGIN_B - REACH_B
        bb_ref[v, 0] = master_b[:, start:start + BAND_B]
    bsb_ref[0] = skew_b[:s_len, ORIGIN_B - lb:ORIGIN_B + s_len]


def _bias_tables(rel_bias_a, t5_table, s_len, la, lb):
    n_tab = 2 * REL_CLIP_A + 1
    tab_w = 3 * V7X_LANES
    rel_p = jnp.pad(rel_bias_a, ((0, 0), (0, tab_w - n_tab))).reshape(N_HEADS, 1, tab_w)
    t5_p = t5_table.reshape(N_HEADS, 1, N_BUCKETS)
    kern = functools.partial(_bias_kernel, s_len=s_len, la=la, lb=lb)
    return pl.pallas_call(
        kern,
        grid=(N_HEADS,),
        in_specs=[
            pl.BlockSpec((1, 1, tab_w), lambda h: (h, 0, 0)),
            pl.BlockSpec((1, 1, N_BUCKETS), lambda h: (h, 0, 0)),
        ],
        out_specs=[
            pl.BlockSpec((N_VAR_A, 1, GROUP_Q, BAND_A), lambda h: (0, h, 0, 0)),
            pl.BlockSpec((N_VAR_B, 1, GROUP_Q, BAND_B), lambda h: (0, h, 0, 0)),
            pl.BlockSpec((1, s_len, la + s_len), lambda h: (h, 0, 0)),
            pl.BlockSpec((1, s_len, lb + s_len), lambda h: (h, 0, 0)),
        ],
        out_shape=[
            jax.ShapeDtypeStruct((N_VAR_A, N_HEADS, GROUP_Q, BAND_A), F32),
            jax.ShapeDtypeStruct((N_VAR_B, N_HEADS, GROUP_Q, BAND_B), F32),
            jax.ShapeDtypeStruct((N_HEADS, s_len, la + s_len), F32),
            jax.ShapeDtypeStruct((N_HEADS, s_len, lb + s_len), F32),
        ],
        name="bias_tables",
    )(rel_p, t5_p)


V7X_MXU_DIM = 256
FF_SLABS = ((0, 6 * V7X_MXU_DIM), (6 * V7X_MXU_DIM, D_FF))
FF_SLAB_MAX = max(hi - lo for lo, hi in FF_SLABS)


def _swiglu(h, wg_ref, wu_ref, wd_ref):
    acts = []
    for lo, hi in FF_SLABS:
        g = jnp.dot(h, wg_ref[:, lo:hi], preferred_element_type=F32)
        u = jnp.dot(h, wu_ref[:, lo:hi], preferred_element_type=F32)
        acts.append((_silu(g) * u).astype(BF16))
    acc = None
    for (lo, hi), a in zip(FF_SLABS, acts):
        part = jnp.dot(a, wd_ref[lo:hi, :], preferred_element_type=F32)
        acc = part if acc is None else acc + part
    return acc


N_SUB = 2


def _sub_tiles(gt, rt):
    if gt == 1:
        step = rt // N_SUB
        return [(slice(0, 1), slice(s * step, (s + 1) * step), s * step, step) for s in range(N_SUB)]
    step = gt // N_SUB
    return [(slice(s * step, (s + 1) * step), slice(0, rt), s * step * rt, step * rt) for s in range(N_SUB)]


def _ffn1_qkv_kernel(x_ref, mod_ref, gains_ref, wg_ref, wu_ref, wd_ref, win_ref,
                     x1_ref, qa_ref, ka_ref, va_ref, qb_ref, kb_ref, vb_ref,
                     ka32_ref, va32_ref, kb32_ref, vb32_ref, *tail_scratch):
    gt, rt, _ = x_ref.shape
    tm = gt * rt
    gain = lambda k: gains_ref[k:k + 1, :]

    def store_tail(ref, val, r0):
        first = tm - ref.shape[0]
        lo = max(r0, first)
        if lo < r0 + val.shape[0]:
            ref[lo - first:r0 + val.shape[0] - first, :] = val[lo - r0:, :]

    def store_heads(ref, val, r0):
        n_heads = val.shape[1] // HEAD_DIM
        for h in range(n_heads):
            ref[pl.ds(r0 * n_heads + h, val.shape[0], stride=n_heads), :] = val[:, h * HEAD_DIM:(h + 1) * HEAD_DIM]

    def store_split(ref, rows, q):
        is_lo = (lax.broadcasted_iota(jnp.int32, q.shape, 1) & (V7X_LANES - 1)) < HEAD_DIM
        ref[rows, :WIDTH] = jnp.where(is_lo, q, 0.0).astype(BF16)
        ref[rows, WIDTH:] = jnp.where(is_lo, 0.0, q).astype(BF16)

    subs = _sub_tiles(gt, rt)
    mods = [lambda k, gs=gs: mod_ref[k, gs] for gs, _, _, _ in subs]
    xs = [x_ref[gs, rs, :] for gs, rs, _, _ in subs]
    hs = [_adaln(x, gain(0), mod(0), mod(1)).reshape(n, D_MODEL).astype(BF16)
          for x, mod, (_, _, _, n) in zip(xs, mods, subs)]
    fs = [_swiglu(h, wg_ref, wu_ref, wd_ref).reshape(x.shape) for h, x in zip(hs, xs)]
    h2s = []
    for x, f, mod, (gs, rs, _, n) in zip(xs, fs, mods, subs):
        x1 = x + (FFN_RES * mod(2)) * _rms(f, gain(1))
        x1_ref[gs, rs, :] = x1
        h2s.append(_adaln(x1, gain(2), mod(3), mod(4)).reshape(n, D_MODEL).astype(BF16))

    for h2, (_, _, r0, n) in zip(h2s, subs):
        rows = slice(r0, r0 + n)

        def proj(lo, width):
            return jnp.dot(h2, win_ref[:, lo:lo + width], preferred_element_type=F32)

        store_split(qa_ref, rows, proj(0, WIDTH))
        ka = proj(WIDTH, WIDTH)
        va = proj(2 * WIDTH, WIDTH)
        store_split(qb_ref, rows, proj(3 * WIDTH, WIDTH))
        kvb = proj(4 * WIDTH, 2 * KV_WIDTH_B)
        kb, vb = kvb[:, :KV_WIDTH_B], kvb[:, KV_WIDTH_B:]
        ka_ref[rows, :] = ka.astype(BF16)
        va_ref[rows, :] = va.astype(BF16)
        kb_ref[rows, :] = kb.astype(BF16)
        vb_ref[rows, :] = vb.astype(BF16)
        if tail_scratch:
            for ref, val in zip(tail_scratch, (ka, va, kb, vb)):
                store_tail(ref, val, r0)
        else:
            store_heads(ka32_ref, ka, r0)
            store_heads(va32_ref, va, r0)
            store_tail(kb32_ref, kb, r0)
            store_tail(vb32_ref, vb, r0)

    if tail_scratch:
        @pl.when(pl.program_id(1) == pl.num_programs(1) - 1)
        def _():
            for out_ref, ref in zip((ka32_ref, va32_ref, kb32_ref, vb32_ref), tail_scratch):
                out_ref[...] = ref[...].T


def _resident(shape):
    return pl.BlockSpec(shape, lambda *_: (0,) * len(shape), pipeline_mode=pl.Buffered(1))


def _ffn1_qkv(x, mod, gains, wg, wu, wd, win, gt, rt):
    n_groups, rows, _ = x.shape
    n_g, n_r = n_groups // gt, rows // rt
    tm = gt * rt
    tokens = n_groups * rows
    ta = gt * min(REACH_A, rows)
    tb = gt * min(REACH_B, rows)
    tile = lambda gi, ri: (gi * n_r + ri, 0)
    tail = lambda gi, ri: (gi, 0)
    tok = lambda w, dt: jax.ShapeDtypeStruct((tokens, w), dt)
    if gt == 1:
        assert ta <= tm and tb <= tm
        tail_specs = [pl.BlockSpec((None, w, t), lambda gi, ri: (gi, 0, 0))
                      for w, t in ((WIDTH, ta), (WIDTH, ta), (KV_WIDTH_B, tb), (KV_WIDTH_B, tb))]
        tail_shapes = [jax.ShapeDtypeStruct((n_g, w, t), F32)
                       for w, t in ((WIDTH, ta), (WIDTH, ta), (KV_WIDTH_B, tb), (KV_WIDTH_B, tb))]
        scratch = [pltpu.VMEM((ta, WIDTH), F32), pltpu.VMEM((ta, WIDTH), F32),
                   pltpu.VMEM((tb, KV_WIDTH_B), F32), pltpu.VMEM((tb, KV_WIDTH_B), F32)]
    else:
        assert n_r == 1
        tail_specs = [pl.BlockSpec((tm * N_HEADS, HEAD_DIM), tail), pl.BlockSpec((tm * N_HEADS, HEAD_DIM), tail),
                      pl.BlockSpec((tm, KV_WIDTH_B), tail), pl.BlockSpec((tm, KV_WIDTH_B), tail)]
        tail_shapes = [jax.ShapeDtypeStruct((tokens * N_HEADS, HEAD_DIM), F32),
                       jax.ShapeDtypeStruct((tokens * N_HEADS, HEAD_DIM), F32),
                       jax.ShapeDtypeStruct((tokens, KV_WIDTH_B), F32), jax.ShapeDtypeStruct((tokens, KV_WIDTH_B), F32)]
        scratch = []
    vmem = (2 * 3 * D_MODEL * D_FF + 2 * D_MODEL * win.shape[1]
            + 2 * 2 * 4 * tm * D_MODEL
            + 2 * 2 * tm * (6 * WIDTH + 2 * KV_WIDTH_B)
            + 2 * 4 * (2 * ta * WIDTH + 2 * tb * KV_WIDTH_B)
            + tm * (2 * 4 + 2) * FF_SLAB_MAX + 3 * 4 * tm * D_MODEL
            + 4 * tm * win.shape[1])
    return pl.pallas_call(
        _ffn1_qkv_kernel,
        grid=(n_g, n_r),
        in_specs=[
            pl.BlockSpec((gt, rt, D_MODEL), lambda gi, ri: (gi, ri, 0)),
            pl.BlockSpec((N_MOD, gt, 1, D_MODEL), lambda gi, ri: (0, gi, 0, 0)),
            _resident(gains.shape),
            _resident(wg.shape), _resident(wu.shape), _resident(wd.shape), _resident(win.shape),
        ],
        out_specs=[
            pl.BlockSpec((gt, rt, D_MODEL), lambda gi, ri: (gi, ri, 0)),
            pl.BlockSpec((tm, 2 * WIDTH), tile), pl.BlockSpec((tm, WIDTH), tile), pl.BlockSpec((tm, WIDTH), tile),
            pl.BlockSpec((tm, 2 * WIDTH), tile), pl.BlockSpec((tm, KV_WIDTH_B), tile),
            pl.BlockSpec((tm, KV_WIDTH_B), tile),
            *tail_specs,
        ],
        out_shape=[
            jax.ShapeDtypeStruct(x.shape, F32),
            tok(2 * WIDTH, BF16), tok(WIDTH, BF16), tok(WIDTH, BF16),
            tok(2 * WIDTH, BF16), tok(KV_WIDTH_B, BF16), tok(KV_WIDTH_B, BF16),
            *tail_shapes,
        ],
        scratch_shapes=scratch,
        compiler_params=pltpu.CompilerParams(
            dimension_semantics=("arbitrary", "arbitrary"), vmem_limit_bytes=_vmem_limit(vmem)),
        name="ffn1_qkv",
    )(x, mod, gains, wg, wu, wd, win)


def _out_ffn2_kernel(x_ref, o_ref, mod_ref, gains_ref, gg_ref, wout_ref, wg_ref, wu_ref, wd_ref, y_ref):
    gt, rt, _ = x_ref.shape
    gain = lambda k: gains_ref[k:k + 1, :]
    gg = gg_ref[...]
    subs = _sub_tiles(gt, rt)
    mods = [lambda k, gs=gs: mod_ref[k, gs] for gs, _, _, _ in subs]
    xs = [x_ref[gs, rs, :] for gs, rs, _, _ in subs]
    ons = []
    for _, _, r0, n in subs:
        o = o_ref[r0:r0 + n, :].astype(F32)
        on = jnp.concatenate([_rms(o[:, :WIDTH], gg[:, :WIDTH]), _rms(o[:, WIDTH:], gg[:, WIDTH:])], axis=1)
        ons.append(on.astype(BF16))
    mixed = [jnp.dot(on, wout_ref[...], preferred_element_type=F32).reshape(x.shape) for on, x in zip(ons, xs)]
    x2s = [x + mod(5) * _rms(m, gain(3)) for x, m, mod in zip(xs, mixed, mods)]
    hs = [_adaln(x2, gain(4), mod(6), mod(7)).reshape(n, D_MODEL).astype(BF16)
          for x2, mod, (_, _, _, n) in zip(x2s, mods, subs)]
    fs = [_swiglu(h, wg_ref, wu_ref, wd_ref).reshape(x.shape) for h, x in zip(hs, xs)]
    for x2, f, mod, (gs, rs, _, _) in zip(x2s, fs, mods, subs):
        y_ref[gs, rs, :] = x2 + (FFN_RES * mod(8)) * _rms(f, gain(5))


def _out_ffn2(x1, o, mod, gains, gg, wout, wg, wu, wd, gt, rt):
    n_groups, rows, _ = x1.shape
    n_g, n_r = n_groups // gt, rows // rt
    tm = gt * rt
    vmem = (2 * 3 * D_MODEL * D_FF + 2 * wout.size
            + 2 * 2 * 4 * tm * D_MODEL + 2 * 2 * tm * 2 * WIDTH
            + tm * (2 * 4 + 2) * FF_SLAB_MAX + 5 * 4 * tm * D_MODEL)
    return pl.pallas_call(
        _out_ffn2_kernel,
        grid=(n_g, n_r),
        in_specs=[
            pl.BlockSpec((gt, rt, D_MODEL), lambda gi, ri: (gi, ri, 0)),
            pl.BlockSpec((tm, 2 * WIDTH), lambda gi, ri: (gi * n_r + ri, 0)),
            pl.BlockSpec((N_MOD, gt, 1, D_MODEL), lambda gi, ri: (0, gi, 0, 0)),
            _resident(gains.shape), _resident(gg.shape), _resident(wout.shape),
            _resident(wg.shape), _resident(wu.shape), _resident(wd.shape),
        ],
        out_specs=pl.BlockSpec((gt, rt, D_MODEL), lambda gi, ri: (gi, ri, 0)),
        out_shape=jax.ShapeDtypeStruct(x1.shape, F32),
        compiler_params=pltpu.CompilerParams(
            dimension_semantics=("arbitrary", "arbitrary"), vmem_limit_bytes=_vmem_limit(vmem)),
        name="out_ffn2",
    )(x1, o, mod, gains, gg, wout, wg, wu, wd)


def _softmax_terms(s, sink=None):
    m = jnp.max(s, axis=1, keepdims=True)
    if sink is not None:
        m = jnp.maximum(m, sink)
    e = jnp.exp(s - m)
    denom = jnp.sum(e, axis=1, keepdims=True)
    if sink is not None:
        denom = denom + jnp.exp(sink - m)
    return e.astype(BF16), 1.0 / denom


def _scores(q, k):
    return lax.dot_general(q, k, (((1,), (1,)), ((), ())), preferred_element_type=F32)


def _unstack_pair(z):
    n_q = z.shape[0] // 2
    is_lo = lax.broadcasted_iota(jnp.int32, (n_q, z.shape[1]), 1) < HEAD_DIM
    return jnp.where(is_lo, z[:n_q], z[n_q:])


def _sink_column(sinks_ref, pair, n_q):
    row = lax.broadcasted_iota(jnp.int32, (2 * n_q, 1), 0)
    return jnp.where(row < n_q, sinks_ref[pair], sinks_ref[N_PAIRS + pair])


GROUPS_PER_STEP = 2


def _attn_prompt_kernel(sinks_ref, qa_ref, qb_ref, ka_ref, va_ref, kb_ref, vb_ref, *rest):
    ba_refs, bb_refs, o_ref = rest[:GROUPS_PER_STEP], rest[GROUPS_PER_STEP:2 * GROUPS_PER_STEP], rest[-1]
    slab = lambda p: slice(p * V7X_LANES, (p + 1) * V7X_LANES)
    bands = []
    for j in range(GROUPS_PER_STEP):
        g = pl.program_id(1) * GROUPS_PER_STEP + j
        start_a = pl.multiple_of(jnp.maximum(g * GROUP_Q - REACH_A, 0), GROUP_Q)
        start_b = pl.multiple_of(jnp.maximum(g * GROUP_Q - REACH_B, 0), GROUP_Q)
        bands.append((pl.ds(start_a, BAND_A), pl.ds(start_b, BAND_B)))

    def stacked(q_ref, j, p):
        rows = slice(j * GROUP_Q, (j + 1) * GROUP_Q)
        return jnp.concatenate([q_ref[rows, slab(p)], q_ref[rows, slab(N_PAIRS + p)]], axis=0)

    def score(mixer, j, p):
        if mixer == "a":
            bias = jnp.concatenate([ba_refs[j][0, 2 * p], ba_refs[j][0, 2 * p + 1]], axis=0)
            return _scores(stacked(qa_ref, j, p), ka_ref[bands[j][0], slab(p)]) + bias
        bias = jnp.concatenate([bb_refs[j][0, p], bb_refs[j][0, N_PAIRS + p]], axis=0)
        return _scores(stacked(qb_ref, j, p), kb_ref[bands[j][1], :]) + bias

    def weights(s, sink):
        m = jnp.max(s, axis=1, keepdims=True)
        if sink is None:
            return jnp.exp(s - m).astype(BF16), None
        m = jnp.maximum(m, sink)
        return jnp.exp(s - m).astype(BF16), jnp.exp(sink - m)

    def attend(e, extra, v):
        z = jnp.dot(e, jnp.concatenate([v, jnp.ones_like(v)], axis=1), preferred_element_type=F32)
        denom = z[:, V7X_LANES:] if extra is None else z[:, V7X_LANES:] + extra
        return _unstack_pair(z[:, :V7X_LANES] * (1.0 / denom))

    jobs = [(mixer, j, p) for mixer in ("a", "b") for j in range(GROUPS_PER_STEP) for p in range(N_PAIRS)]
    s_all = [score(*job) for job in jobs]
    w_all = [weights(s, _sink_column(sinks_ref, p, GROUP_Q) if mixer == "b" else None)
             for s, (mixer, j, p) in zip(s_all, jobs)]
    for (e, extra), (mixer, j, p) in zip(w_all, jobs):
        rows = slice(j * GROUP_Q, (j + 1) * GROUP_Q)
        if mixer == "a":
            o_ref[rows, slab(p)] = attend(e, extra, va_ref[bands[j][0], slab(p)]).astype(BF16)
        else:
            o_ref[rows, slab(N_PAIRS + p)] = attend(e, extra, vb_ref[bands[j][1], :]).astype(BF16)


def _attn_prompt(sinks, qa, qb, ka, va, kb, vb, bias_a, bias_b, batch, seq):
    rows = GROUPS_PER_STEP * GROUP_Q
    n_steps = seq // rows
    q_spec = pl.BlockSpec((rows, 2 * WIDTH), lambda b, s: (b * n_steps + s, 0))
    kv_spec = lambda w: pl.BlockSpec((seq, w), lambda b, s: (b, 0), pipeline_mode=pl.Buffered(1))
    bias_spec = lambda band, n_var, j: pl.BlockSpec(
        (1, N_HEADS, GROUP_Q, band),
        lambda b, s: (jnp.minimum(s * GROUPS_PER_STEP + j, n_var - 1), 0, 0, 0))
    vmem = (2 * seq * (2 * WIDTH + 2 * KV_WIDTH_B)
            + GROUPS_PER_STEP * 2 * 4 * N_HEADS * GROUP_Q * (BAND_A + BAND_B)
            + 2 * 2 * rows * 6 * WIDTH + GROUPS_PER_STEP * 6 * 4 * 2 * GROUP_Q * BAND_A)
    return pl.pallas_call(
        _attn_prompt_kernel,
        grid=(batch, n_steps),
        in_specs=[
            pl.BlockSpec(memory_space=pltpu.SMEM),
            q_spec, q_spec,
            kv_spec(WIDTH), kv_spec(WIDTH), kv_spec(KV_WIDTH_B), kv_spec(KV_WIDTH_B),
            *[bias_spec(BAND_A, N_VAR_A, j) for j in range(GROUPS_PER_STEP)],
            *[bias_spec(BAND_B, N_VAR_B, j) for j in range(GROUPS_PER_STEP)],
        ],
        out_specs=pl.BlockSpec((rows, 2 * WIDTH), lambda b, s: (b * n_steps + s, 0)),
        out_shape=jax.ShapeDtypeStruct((batch * seq, 2 * WIDTH), BF16),
        compiler_params=pltpu.CompilerParams(
            dimension_semantics=("arbitrary", "arbitrary"), vmem_limit_bytes=_vmem_limit(vmem)),
        name="attn_prompt",
    )(sinks, qa, qb, ka, va, kb, vb, *([bias_a] * GROUPS_PER_STEP), *([bias_b] * GROUPS_PER_STEP))


def _pair_ordered_cols(h):
    return (h % N_PAIRS) * V7X_LANES + (h // N_PAIRS) * HEAD_DIM


def _attn_sample_kernel(sinks_ref, qa_ref, qb_ref, ka_ref, va_ref, kb_ref, vb_ref,
                        cak_ref, cav_ref, cbk_ref, cbv_ref, ba_ref, bb_ref, o_ref):
    n_q = qa_ref.shape[0]
    n_kv = cbk_ref.shape[0]
    per_kv = N_HEADS // n_kv
    la = cak_ref.shape[2]
    head = lambda ref, idx: ref[:, idx * HEAD_DIM:(idx + 1) * HEAD_DIM]

    def query(ref, lo):
        lo = lo if lo % V7X_LANES < HEAD_DIM else WIDTH + lo
        return ref[:, lo:lo + HEAD_DIM]

    def scores(q, cache_t_ref, new_ref, idx):
        old = jnp.dot(q, cache_t_ref[idx].astype(BF16), preferred_element_type=F32)
        return jnp.concatenate([old, _scores(q, head(new_ref, idx))], axis=1)

    def values(e, cache_t_ref, new_ref, idx, n_old):
        old = _scores(e[:, :n_old], cache_t_ref[idx].astype(BF16))
        return old + jnp.dot(e[:, n_old:], head(new_ref, idx), preferred_element_type=F32)

    s_a = [scores(query(qa_ref, h * HEAD_DIM), cak_ref, ka_ref, h) + ba_ref[h] for h in range(N_HEADS)]
    s_b = []
    for c in range(n_kv):
        heads = range(c * per_kv, (c + 1) * per_kv)
        q = jnp.concatenate([query(qb_ref, _pair_ordered_cols(h)) for h in heads], axis=0)
        bias = jnp.concatenate([bb_ref[h] for h in heads], axis=0)
        s_b.append(scores(q, cbk_ref, kb_ref, c) + bias)

    row = lax.broadcasted_iota(jnp.int32, (per_kv * n_q, 1), 0)
    p_a = [_softmax_terms(s) for s in s_a]
    p_b = []
    for c in range(n_kv):
        sink = jnp.zeros((per_kv * n_q, 1), F32)
        for i in range(per_kv):
            sink = jnp.where((row >= i * n_q) & (row < (i + 1) * n_q), sinks_ref[c * per_kv + i], sink)
        p_b.append(_softmax_terms(s_b[c], sink))

    out_a = [values(e, cav_ref, va_ref, h, la) * r for h, (e, r) in enumerate(p_a)]
    out_b = [None] * N_HEADS
    for c, (e, r) in enumerate(p_b):
        o = values(e, cbv_ref, vb_ref, c, cbv_ref.shape[2]) * r
        for i in range(per_kv):
            out_b[c * per_kv + i] = o[i * n_q:(i + 1) * n_q]
    order_b = [h for p in range(N_PAIRS) for h in (p, N_PAIRS + p)]
    o_ref[...] = jnp.concatenate(out_a + [out_b[h] for h in order_b], axis=1).astype(BF16)


def _attn_sample(sinks, qa, qb, ka, va, kb, vb, cak, cav, cbk, cbv, bias_a, bias_b, batch, s_len):
    row = lambda w: pl.BlockSpec((s_len, w), lambda b: (b, 0))
    cache = lambda a: pl.BlockSpec((None,) + a.shape[1:], lambda b: (b, 0, 0, 0))
    whole = lambda a: pl.BlockSpec(a.shape, lambda b: (0,) * a.ndim)
    return pl.pallas_call(
        _attn_sample_kernel,
        grid=(batch,),
        in_specs=[
            pl.BlockSpec(memory_space=pltpu.SMEM),
            row(2 * WIDTH), row(2 * WIDTH), row(WIDTH), row(WIDTH), row(KV_WIDTH_B), row(KV_WIDTH_B),
            cache(cak), cache(cav), cache(cbk), cache(cbv),
            whole(bias_a), whole(bias_b),
        ],
        out_specs=pl.BlockSpec((s_len, 2 * WIDTH), lambda b: (b, 0)),
        out_shape=jax.ShapeDtypeStruct((batch * s_len, 2 * WIDTH), BF16),
        compiler_params=pltpu.CompilerParams(dimension_semantics=("arbitrary",)),
        name="attn_sample",
    )(sinks, qa, qb, ka, va, kb, vb, cak, cav, cbk, cbv, bias_a, bias_b)


def _pair_order(x, axis):
    shape = x.shape
    split = shape[:axis] + (N_HEADS // N_PAIRS, N_PAIRS, HEAD_DIM) + shape[axis + 1:]
    return jnp.swapaxes(x.reshape(split), axis, axis + 1).reshape(shape)


def _prep_weights(w_in, w_out, group_gains):
    qa, ka, va, qb, kvb = jnp.split(w_in, (WIDTH, 2 * WIDTH, 3 * WIDTH, 4 * WIDTH), axis=1)
    win = jnp.concatenate([qa * SCALE, ka, va, _pair_order(qb, 1) * SCALE, kvb], axis=1).astype(BF16)
    wout = jnp.concatenate([w_out[:WIDTH], _pair_order(w_out[WIDTH:], 0)], axis=0).astype(BF16)
    gg = jnp.concatenate([group_gains[:WIDTH], _pair_order(group_gains[WIDTH:], 0)])
    return win, wout, gg.reshape(1, 2 * WIDTH)


def kernel(x_prompt, x_sample, cache_a_k, cache_a_v, cache_b_k, cache_b_v, c_prompt, c_sample, w_mod, b_mod,
           norm_gains, w1_gate, w1_up, w1_down, w_in, w_out, group_gains, rel_bias_a, t5_bias_table, sinks_b,
           w2_gate, w2_up, w2_down):
    depth = w_mod.shape[0]
    assert depth == 1
    batch, seq, _ = x_prompt.shape
    s_batch, s_len, _ = x_sample.shape
    la, lb = cache_a_k.shape[2], cache_b_k.shape[2]
    assert la == REACH_A and lb == REACH_B and seq % 512 == 0

    c_rows = jnp.concatenate([c_prompt, c_sample], axis=0)
    pad_rows = (-c_rows.shape[0]) % 8
    c_rows = jnp.pad(c_rows, ((0, pad_rows), (0, 0)))
    mod = _modulation(c_rows, w_mod[0], b_mod[0])
    mod_p = mod[:, :batch].reshape(N_MOD, batch, 1, D_MODEL)
    mod_s = mod[:, batch:batch + s_batch].reshape(N_MOD, s_batch, 1, D_MODEL)

    bias_a, bias_b, bias_sa, bias_sb = _bias_tables(rel_bias_a[0], t5_bias_table, s_len, la, lb)

    gains = norm_gains[0]
    w1 = (w1_gate[0].astype(BF16), w1_up[0].astype(BF16), w1_down[0].astype(BF16))
    w2 = (w2_gate[0].astype(BF16), w2_up[0].astype(BF16), w2_down[0].astype(BF16))
    win, wout, gg = _prep_weights(w_in[0], w_out[0], group_gains[0])
    sinks = sinks_b[0]

    rt_p = 512
    (x1p, qa, ka, va, qb, kb, vb, ka32, va32, kb32, vb32) = _ffn1_qkv(
        x_prompt, mod_p, gains, *w1, win, 1, rt_p)
    op = _attn_prompt(sinks, qa, qb, ka, va, kb, vb, bias_a, bias_b, batch, seq)
    y_prompt = _out_ffn2(x1p, op, mod_p, gains, gg, wout, *w2, 1, rt_p)
    rows_last = lambda t: jnp.transpose(t.reshape(depth, batch, -1, HEAD_DIM, t.shape[-1]), (0, 1, 4, 2, 3))
    new_p = tuple(rows_last(t) for t in (ka32, va32, kb32, vb32))

    gt_s = 16
    (x1s, qas, kas, vas, qbs, kbs, vbs, kas32, vas32, kbs32, vbs32) = _ffn1_qkv(
        x_sample, mod_s, gains, *w1, win, gt_s, s_len)
    os_ = _attn_sample(sinks, qas, qbs, kas, vas, kbs, vbs,
                       *(jnp.transpose(c[0], (0, 2, 3, 1)) for c in (cache_a_k, cache_a_v, cache_b_k, cache_b_v)),
                       bias_sa, bias_sb, s_batch, s_len)
    y_sample = _out_ffn2(x1s, os_, mod_s, gains, gg, wout, *w2, gt_s, s_len)
    new_s = (kas32.reshape(depth, s_batch, s_len, N_HEADS, HEAD_DIM),
             vas32.reshape(depth, s_batch, s_len, N_HEADS, HEAD_DIM),
             kbs32.reshape(depth, s_batch, s_len, 2, HEAD_DIM), vbs32.reshape(depth, s_batch, s_len, 2, HEAD_DIM))

    return (y_prompt, y_sample) + new_p + new_s
```

```python
import functools
import math

import jax
import jax.numpy as jnp
from jax import lax
from jax.experimental import pallas as pl
from jax.experimental.pallas import tpu as pltpu

F32 = jnp.float32
BF16 = jnp.bfloat16

D_MODEL = 1024
D_FF = 2816
CHUNK = 64
HEAD_DIM = 64
N_HEADS = 8
N_PAIRS = N_HEADS // 2
WIDTH = N_HEADS * HEAD_DIM
KV_WIDTH_B = 2 * HEAD_DIM
REACH_A = 8 * CHUNK
REACH_B = 2 * CHUNK
REL_CLIP_A = 128
N_BUCKETS = 32
N_MOD = 9
FFN_RES = 0.5
EPS = 1e-6
SCALE = HEAD_DIM ** -0.5
NEG_INF = -1e30

V7X_LANES = 128
V7X_VMEM_BYTES = 64 * 1024 * 1024
V7X_VMEM_LIMIT_CAP = 56 * 1024 * 1024

GROUP_Q = 2 * CHUNK
BAND_A = REACH_A + GROUP_Q
BAND_B = REACH_B + GROUP_Q
ORIGIN_A = 1024
WIDTH_MASTER_A = ORIGIN_A + BAND_A
ORIGIN_B = 256
WIDTH_MASTER_B = ORIGIN_B + BAND_B
N_VAR_A = REACH_A // GROUP_Q + 1
N_VAR_B = REACH_B // GROUP_Q + 1


def _vmem_limit(estimate_bytes):
    return int(min(max(estimate_bytes, 32 * 1024 * 1024), V7X_VMEM_LIMIT_CAP))


def _rms(x, gain):
    ms = jnp.mean(x * x, axis=-1, keepdims=True)
    return x * lax.rsqrt(ms + EPS) * gain


def _adaln(x, gain, shift, scale):
    return _rms(x, gain) * (1.0 + scale) + shift


def _silu(x):
    return x * (1.0 / (1.0 + jnp.exp(-x)))


def _mod_kernel(c_ref, w_ref, b_ref, o_ref):
    s = _silu(c_ref[...]).astype(BF16)
    o_ref[0] = jnp.dot(s, w_ref[...].astype(BF16), preferred_element_type=F32) + b_ref[0]


def _modulation(c_rows, w_mod, b_mod):
    rows = c_rows.shape[0]
    return pl.pallas_call(
        _mod_kernel,
        grid=(N_MOD,),
        in_specs=[
            pl.BlockSpec((rows, D_MODEL), lambda j: (0, 0)),
            pl.BlockSpec((D_MODEL, D_MODEL), lambda j: (0, j)),
            pl.BlockSpec((1, 1, D_MODEL), lambda j: (j, 0, 0)),
        ],
        out_specs=pl.BlockSpec((1, rows, D_MODEL), lambda j: (j, 0, 0)),
        out_shape=jax.ShapeDtypeStruct((N_MOD, rows, D_MODEL), F32),
        name="modulation",
    )(c_rows, w_mod, b_mod.reshape(N_MOD, 1, D_MODEL))


def _skew(base_row, n_rows):
    width = base_row.shape[1]
    m = jnp.broadcast_to(base_row, (n_rows, width))
    row = lax.broadcasted_iota(jnp.int32, (n_rows, width), 0)
    shift = 1
    while shift < n_rows:
        m = jnp.where((row & shift) != 0, pltpu.roll(m, shift, axis=1), m)
        shift *= 2
    return m


def _band_mask(n_rows, width, origin, n_prev):
    row = lax.broadcasted_iota(jnp.int32, (n_rows, width), 0)
    col = lax.broadcasted_iota(jnp.int32, (n_rows, width), 1)
    q_chunk = row >> 6
    k_chunk = (col >> 6) - (origin // CHUNK)
    return (k_chunk >= q_chunk - n_prev) & (k_chunk <= q_chunk)


def _t5_bucket(rel):
    half = N_BUCKETS // 2
    max_exact = half // 2
    n = jnp.abs(rel)
    n2 = n * n
    large = jnp.full(rel.shape, max_exact, jnp.int32)
    for k in range(1, half - max_exact):
        large = large + jnp.where(n2 >= (max_exact * max_exact) * (2 ** k), 1, 0)
    return jnp.where(rel > 0, half, 0) + jnp.where(n < max_exact, n, large)


def _bias_kernel(rel_ref, t5_ref, *refs, s_len, la, lb):
    n_cast = (len(refs) - 4) // 2
    for src, dst in zip(refs[:n_cast], refs[n_cast + 4:]):
        dst[...] = src[...].astype(BF16)
    ba_ref, bb_ref, bsa_ref, bsb_ref = refs[n_cast:n_cast + 4]

    tab = rel_ref[0]
    n_tab = 2 * REL_CLIP_A + 1
    first, last = tab[:, 0:1], tab[:, n_tab - 1:n_tab]
    win_lo = ORIGIN_A - 2 * REL_CLIP_A
    col = lax.broadcasted_iota(jnp.int32, (1, tab.shape[1] + V7X_LANES), 1)
    shifted = jnp.concatenate([jnp.zeros((1, V7X_LANES), F32), tab], axis=1)
    u_win = jnp.where(col < V7X_LANES, first, jnp.where(col >= V7X_LANES + n_tab, last, shifted))
    window = _skew(u_win, GROUP_Q)[:, V7X_LANES:]
    n_left = win_lo + V7X_LANES
    n_right = WIDTH_MASTER_A - n_left - window.shape[1]
    skew_a = jnp.concatenate([jnp.broadcast_to(first, (GROUP_Q, n_left)), window,
                              jnp.broadcast_to(last, (GROUP_Q, n_right))], axis=1)
    master_a = jnp.where(_band_mask(GROUP_Q, WIDTH_MASTER_A, ORIGIN_A, REACH_A // CHUNK), skew_a, NEG_INF)
    for v in range(N_VAR_A):
        start = ORIGIN_A - min(v * GROUP_Q, REACH_A) if v < N_VAR_A - 1 else ORIGIN_A - REACH_A
        ba_ref[v, 0] = master_a[:, start:start + BAND_A]
    bsa_ref[0] = skew_a[:s_len, ORIGIN_A - la:ORIGIN_A + s_len]

    t5 = t5_ref[0]
    rel = lax.broadcasted_iota(jnp.int32, (1, WIDTH_MASTER_B), 1) - ORIGIN_B
    bucket = _t5_bucket(rel)
    u_b = jnp.zeros((1, WIDTH_MASTER_B), F32)
    for i in range(N_BUCKETS):
        u_b = jnp.where(bucket == i, t5[:, i:i + 1], u_b)
    skew_b = _skew(u_b, GROUP_Q)
    master_b = jnp.where(_band_mask(GROUP_Q, WIDTH_MASTER_B, ORIGIN_B, REACH_B // CHUNK), skew_b, NEG_INF)
    for v in range(N_VAR_B):
        start = ORIGIN_B - min(v * GROUP_Q, REACH_B) if v < N_VAR_B - 1 else ORIGIN_B - REACH_B
        bb_ref[v, 0] = master_b[:, start:start + BAND_B]
    bsb_ref[0] = skew_b[:s_len, ORIGIN_B - lb:ORIGIN_B + s_len]


V7X_BF16_SUBLANES = 16


def _cast_chunks(w, n_steps, step_of):
    rows, cols = w.shape
    n_chunks = n_steps
    while rows % (n_chunks * V7X_BF16_SUBLANES):
        n_chunks //= 2
    spec = pl.BlockSpec((rows // n_chunks, cols), lambda *g: (jnp.minimum(step_of(*g), n_chunks - 1), 0))
    return spec, jax.ShapeDtypeStruct(w.shape, BF16)


def _bias_tables(rel_bias_a, t5_table, s_len, la, lb, cast=()):
    n_tab = 2 * REL_CLIP_A + 1
    tab_w = 3 * V7X_LANES
    rel_p = jnp.pad(rel_bias_a, ((0, 0), (0, tab_w - n_tab))).reshape(N_HEADS, 1, tab_w)
    t5_p = t5_table.reshape(N_HEADS, 1, N_BUCKETS)
    kern = functools.partial(_bias_kernel, s_len=s_len, la=la, lb=lb)
    cast_specs, cast_shapes = zip(*[_cast_chunks(w, N_HEADS, lambda h: h) for w in cast]) if cast else ((), ())
    return pl.pallas_call(
        kern,
        grid=(N_HEADS,),
        in_specs=[
            pl.BlockSpec((1, 1, tab_w), lambda h: (h, 0, 0)),
            pl.BlockSpec((1, 1, N_BUCKETS), lambda h: (h, 0, 0)),
            *cast_specs,
        ],
        out_specs=[
            pl.BlockSpec((N_VAR_A, 1, GROUP_Q, BAND_A), lambda h: (0, h, 0, 0)),
            pl.BlockSpec((N_VAR_B, 1, GROUP_Q, BAND_B), lambda h: (0, h, 0, 0)),
            pl.BlockSpec((1, s_len, la + s_len), lambda h: (h, 0, 0)),
            pl.BlockSpec((1, s_len, lb + s_len), lambda h: (h, 0, 0)),
            *cast_specs,
        ],
        out_shape=[
            jax.ShapeDtypeStruct((N_VAR_A, N_HEADS, GROUP_Q, BAND_A), F32),
            jax.ShapeDtypeStruct((N_VAR_B, N_HEADS, GROUP_Q, BAND_B), F32),
            jax.ShapeDtypeStruct((N_HEADS, s_len, la + s_len), F32),
            jax.ShapeDtypeStruct((N_HEADS, s_len, lb + s_len), F32),
            *cast_shapes,
        ],
        compiler_params=pltpu.CompilerParams(dimension_semantics=("arbitrary",)),
        name="bias_tables",
    )(rel_p, t5_p, *cast)


V7X_MXU_DIM = 256
FF_SLABS = ((0, 6 * V7X_MXU_DIM), (6 * V7X_MXU_DIM, D_FF))
FF_SLAB_MAX = max(hi - lo for lo, hi in FF_SLABS)


def _swiglu(h, wg_ref, wu_ref, wd_ref):
    acts = []
    for lo, hi in FF_SLABS:
        g = jnp.dot(h, wg_ref[:, lo:hi], preferred_element_type=F32)
        u = jnp.dot(h, wu_ref[:, lo:hi], preferred_element_type=F32)
        acts.append((_silu(g) * u).astype(BF16))
    acc = None
    for (lo, hi), a in zip(FF_SLABS, acts):
        part = jnp.dot(a, wd_ref[lo:hi, :], preferred_element_type=F32)
        acc = part if acc is None else acc + part
    return acc


N_SUB = 2


def _sub_tiles(gt, rt):
    if gt == 1:
        step = rt // N_SUB
        return [(slice(0, 1), slice(s * step, (s + 1) * step), s * step, step) for s in range(N_SUB)]
    step = gt // N_SUB
    return [(slice(s * step, (s + 1) * step), slice(0, rt), s * step * rt, step * rt) for s in range(N_SUB)]


def _ffn1_qkv_kernel(x_ref, mod_ref, gains_ref, wg_ref, wu_ref, wd_ref, win_ref,
                     x1_ref, qa_ref, ka_ref, va_ref, qb_ref, kb_ref, vb_ref,
                     ka32_ref, va32_ref, kb32_ref, vb32_ref, *tail_scratch):
    gt, rt, _ = x_ref.shape
    tm = gt * rt
    gain = lambda k: gains_ref[k:k + 1, :]

    def store_tail(ref, val, r0):
        first = tm - ref.shape[0]
        lo = max(r0, first)
        if lo < r0 + val.shape[0]:
            ref[lo - first:r0 + val.shape[0] - first, :] = val[lo - r0:, :]

    def store_heads(ref, val, r0):
        n_heads = val.shape[1] // HEAD_DIM
        for h in range(n_heads):
            ref[pl.ds(r0 * n_heads + h, val.shape[0], stride=n_heads), :] = val[:, h * HEAD_DIM:(h + 1) * HEAD_DIM]

    def store_split(ref, rows, q):
        is_lo = (lax.broadcasted_iota(jnp.int32, q.shape, 1) & (V7X_LANES - 1)) < HEAD_DIM
        ref[rows, :WIDTH] = jnp.where(is_lo, q, 0.0).astype(BF16)
        ref[rows, WIDTH:] = jnp.where(is_lo, 0.0, q).astype(BF16)

    subs = _sub_tiles(gt, rt)
    mods = [lambda k, gs=gs: mod_ref[k, gs] for gs, _, _, _ in subs]
    xs = [x_ref[gs, rs, :] for gs, rs, _, _ in subs]
    hs = [_adaln(x, gain(0), mod(0), mod(1)).reshape(n, D_MODEL).astype(BF16)
          for x, mod, (_, _, _, n) in zip(xs, mods, subs)]
    fs = [_swiglu(h, wg_ref, wu_ref, wd_ref).reshape(x.shape) for h, x in zip(hs, xs)]
    h2s = []
    for x, f, mod, (gs, rs, _, n) in zip(xs, fs, mods, subs):
        x1 = x + (FFN_RES * mod(2)) * _rms(f, gain(1))
        x1_ref[gs, rs, :] = x1
        h2s.append(_adaln(x1, gain(2), mod(3), mod(4)).reshape(n, D_MODEL).astype(BF16))

    for h2, (_, _, r0, n) in zip(h2s, subs):
        rows = slice(r0, r0 + n)

        def proj(lo, width):
            return jnp.dot(h2, win_ref[:, lo:lo + width], preferred_element_type=F32)

        store_split(qa_ref, rows, proj(0, WIDTH))
        ka = proj(WIDTH, WIDTH)
        va = proj(2 * WIDTH, WIDTH)
        store_split(qb_ref, rows, proj(3 * WIDTH, WIDTH))
        kvb = proj(4 * WIDTH, 2 * KV_WIDTH_B)
        kb, vb = kvb[:, :KV_WIDTH_B], kvb[:, KV_WIDTH_B:]
        ka_ref[rows, :] = ka.astype(BF16)
        va_ref[rows, :] = va.astype(BF16)
        kb_ref[rows, :] = kb.astype(BF16)
        vb_ref[rows, :] = vb.astype(BF16)
        if tail_scratch:
            for ref, val in zip(tail_scratch, (ka, va, kb, vb)):
                store_tail(ref, val, r0)
        else:
            store_heads(ka32_ref, ka, r0)
            store_heads(va32_ref, va, r0)
            store_tail(kb32_ref, kb, r0)
            store_tail(vb32_ref, vb, r0)

    if tail_scratch:
        @pl.when(pl.program_id(1) == pl.num_programs(1) - 1)
        def _():
            for out_ref, ref in zip((ka32_ref, va32_ref, kb32_ref, vb32_ref), tail_scratch):
                out_ref[...] = ref[...].T


def _resident(shape):
    return pl.BlockSpec(shape, lambda *_: (0,) * len(shape), pipeline_mode=pl.Buffered(1))


def _ffn1_qkv(x, mod, gains, wg, wu, wd, win, gt, rt):
    n_groups, rows, _ = x.shape
    n_g, n_r = n_groups // gt, rows // rt
    tm = gt * rt
    tokens = n_groups * rows
    ta = gt * min(REACH_A, rows)
    tb = gt * min(REACH_B, rows)
    tile = lambda gi, ri: (gi * n_r + ri, 0)
    tail = lambda gi, ri: (gi, 0)
    tok = lambda w, dt: jax.ShapeDtypeStruct((tokens, w), dt)
    if gt == 1:
        assert ta <= tm and tb <= tm
        tail_specs = [pl.BlockSpec((None, w, t), lambda gi, ri: (gi, 0, 0))
                      for w, t in ((WIDTH, ta), (WIDTH, ta), (KV_WIDTH_B, tb), (KV_WIDTH_B, tb))]
        tail_shapes = [jax.ShapeDtypeStruct((n_g, w, t), F32)
                       for w, t in ((WIDTH, ta), (WIDTH, ta), (KV_WIDTH_B, tb), (KV_WIDTH_B, tb))]
        scratch = [pltpu.VMEM((ta, WIDTH), F32), pltpu.VMEM((ta, WIDTH), F32),
                   pltpu.VMEM((tb, KV_WIDTH_B), F32), pltpu.VMEM((tb, KV_WIDTH_B), F32)]
    else:
        assert n_r == 1
        tail_specs = [pl.BlockSpec((tm * N_HEADS, HEAD_DIM), tail), pl.BlockSpec((tm * N_HEADS, HEAD_DIM), tail),
                      pl.BlockSpec((tm, KV_WIDTH_B), tail), pl.BlockSpec((tm, KV_WIDTH_B), tail)]
        tail_shapes = [jax.ShapeDtypeStruct((tokens * N_HEADS, HEAD_DIM), F32),
                       jax.ShapeDtypeStruct((tokens * N_HEADS, HEAD_DIM), F32),
                       jax.ShapeDtypeStruct((tokens, KV_WIDTH_B), F32), jax.ShapeDtypeStruct((tokens, KV_WIDTH_B), F32)]
        scratch = []
    vmem = (2 * 3 * D_MODEL * D_FF + 2 * D_MODEL * win.shape[1]
            + 2 * 2 * 4 * tm * D_MODEL
            + 2 * 2 * tm * (6 * WIDTH + 2 * KV_WIDTH_B)
            + 2 * 4 * (2 * ta * WIDTH + 2 * tb * KV_WIDTH_B)
            + tm * (2 * 4 + 2) * FF_SLAB_MAX + 3 * 4 * tm * D_MODEL
            + 4 * tm * win.shape[1])
    return pl.pallas_call(
        _ffn1_qkv_kernel,
        grid=(n_g, n_r),
        in_specs=[
            pl.BlockSpec((gt, rt, D_MODEL), lambda gi, ri: (gi, ri, 0)),
            pl.BlockSpec((N_MOD, gt, 1, D_MODEL), lambda gi, ri: (0, gi, 0, 0)),
            _resident(gains.shape),
            _resident(wg.shape), _resident(wu.shape), _resident(wd.shape), _resident(win.shape),
        ],
        out_specs=[
            pl.BlockSpec((gt, rt, D_MODEL), lambda gi, ri: (gi, ri, 0)),
            pl.BlockSpec((tm, 2 * WIDTH), tile), pl.BlockSpec((tm, WIDTH), tile), pl.BlockSpec((tm, WIDTH), tile),
            pl.BlockSpec((tm, 2 * WIDTH), tile), pl.BlockSpec((tm, KV_WIDTH_B), tile),
            pl.BlockSpec((tm, KV_WIDTH_B), tile),
            *tail_specs,
        ],
        out_shape=[
            jax.ShapeDtypeStruct(x.shape, F32),
            tok(2 * WIDTH, BF16), tok(WIDTH, BF16), tok(WIDTH, BF16),
            tok(2 * WIDTH, BF16), tok(KV_WIDTH_B, BF16), tok(KV_WIDTH_B, BF16),
            *tail_shapes,
        ],
        scratch_shapes=scratch,
        compiler_params=pltpu.CompilerParams(
            dimension_semantics=("arbitrary", "arbitrary"), vmem_limit_bytes=_vmem_limit(vmem)),
        name="ffn1_qkv",
    )(x, mod, gains, wg, wu, wd, win)


def _out_ffn2_kernel(x_ref, o_ref, mod_ref, gains_ref, gg_ref, wout_ref, wg_ref, wu_ref, wd_ref, y_ref):
    gt, rt, _ = x_ref.shape
    gain = lambda k: gains_ref[k:k + 1, :]
    gg = gg_ref[...]
    subs = _sub_tiles(gt, rt)
    mods = [lambda k, gs=gs: mod_ref[k, gs] for gs, _, _, _ in subs]
    xs = [x_ref[gs, rs, :] for gs, rs, _, _ in subs]
    ons = []
    for _, _, r0, n in subs:
        o = o_ref[r0:r0 + n, :].astype(F32)
        on = jnp.concatenate([_rms(o[:, :WIDTH], gg[:, :WIDTH]), _rms(o[:, WIDTH:], gg[:, WIDTH:])], axis=1)
        ons.append(on.astype(BF16))
    mixed = [jnp.dot(on, wout_ref[...], preferred_element_type=F32).reshape(x.shape) for on, x in zip(ons, xs)]
    x2s = [x + mod(5) * _rms(m, gain(3)) for x, m, mod in zip(xs, mixed, mods)]
    hs = [_adaln(x2, gain(4), mod(6), mod(7)).reshape(n, D_MODEL).astype(BF16)
          for x2, mod, (_, _, _, n) in zip(x2s, mods, subs)]
    fs = [_swiglu(h, wg_ref, wu_ref, wd_ref).reshape(x.shape) for h, x in zip(hs, xs)]
    for x2, f, mod, (gs, rs, _, _) in zip(x2s, fs, mods, subs):
        y_ref[gs, rs, :] = x2 + (FFN_RES * mod(8)) * _rms(f, gain(5))


def _out_ffn2(x1, o, mod, gains, gg, wout, wg, wu, wd, gt, rt):
    n_groups, rows, _ = x1.shape
    n_g, n_r = n_groups // gt, rows // rt
    tm = gt * rt
    vmem = (2 * 3 * D_MODEL * D_FF + 2 * wout.size
            + 2 * 2 * 4 * tm * D_MODEL + 2 * 2 * tm * 2 * WIDTH
            + tm * (2 * 4 + 2) * FF_SLAB_MAX + 5 * 4 * tm * D_MODEL)
    return pl.pallas_call(
        _out_ffn2_kernel,
        grid=(n_g, n_r),
        in_specs=[
            pl.BlockSpec((gt, rt, D_MODEL), lambda gi, ri: (gi, ri, 0)),
            pl.BlockSpec((tm, 2 * WIDTH), lambda gi, ri: (gi * n_r + ri, 0)),
            pl.BlockSpec((N_MOD, gt, 1, D_MODEL), lambda gi, ri: (0, gi, 0, 0)),
            _resident(gains.shape), _resident(gg.shape), _resident(wout.shape),
            _resident(wg.shape), _resident(wu.shape), _resident(wd.shape),
        ],
        out_specs=pl.BlockSpec((gt, rt, D_MODEL), lambda gi, ri: (gi, ri, 0)),
        out_shape=jax.ShapeDtypeStruct(x1.shape, F32),
        compiler_params=pltpu.CompilerParams(
            dimension_semantics=("arbitrary", "arbitrary"), vmem_limit_bytes=_vmem_limit(vmem)),
        name="out_ffn2",
    )(x1, o, mod, gains, gg, wout, wg, wu, wd)


def _softmax_terms(s, sink=None):
    m = jnp.max(s, axis=1, keepdims=True)
    if sink is not None:
        m = jnp.maximum(m, sink)
    e = jnp.exp(s - m)
    denom = jnp.sum(e, axis=1, keepdims=True)
    if sink is not None:
        denom = denom + jnp.exp(sink - m)
    return e.astype(BF16), 1.0 / denom


def _scores(q, k):
    return lax.dot_general(q, k, (((1,), (1,)), ((), ())), preferred_element_type=F32)


def _unstack_pair(z):
    n_q = z.shape[0] // 2
    is_lo = lax.broadcasted_iota(jnp.int32, (n_q, z.shape[1]), 1) < HEAD_DIM
    return jnp.where(is_lo, z[:n_q], z[n_q:])


def _sink_column(sinks_ref, pair, n_q):
    row = lax.broadcasted_iota(jnp.int32, (2 * n_q, 1), 0)
    return jnp.where(row < n_q, sinks_ref[pair], sinks_ref[N_PAIRS + pair])


GROUPS_PER_STEP = 2


def _attn_prompt_kernel(sinks_ref, qa_ref, qb_ref, ka_ref, va_ref, kb_ref, vb_ref, *rest):
    ba_refs, bb_refs = rest[:GROUPS_PER_STEP], rest[GROUPS_PER_STEP:2 * GROUPS_PER_STEP]
    n_cast = (len(rest) - 2 * GROUPS_PER_STEP - 1) // 2
    o_ref = rest[2 * GROUPS_PER_STEP + n_cast]
    for src, dst in zip(rest[2 * GROUPS_PER_STEP:2 * GROUPS_PER_STEP + n_cast], rest[len(rest) - n_cast:]):
        dst[...] = src[...].astype(BF16)
    slab = lambda p: slice(p * V7X_LANES, (p + 1) * V7X_LANES)
    bands = []
    for j in range(GROUPS_PER_STEP):
        g = pl.program_id(1) * GROUPS_PER_STEP + j
        start_a = pl.multiple_of(jnp.maximum(g * GROUP_Q - REACH_A, 0), GROUP_Q)
        start_b = pl.multiple_of(jnp.maximum(g * GROUP_Q - REACH_B, 0), GROUP_Q)
        bands.append((pl.ds(start_a, BAND_A), pl.ds(start_b, BAND_B)))

    def stacked(q_ref, j, p):
        rows = slice(j * GROUP_Q, (j + 1) * GROUP_Q)
        return jnp.concatenate([q_ref[rows, slab(p)], q_ref[rows, slab(N_PAIRS + p)]], axis=0)

    def score(mixer, j, p):
        if mixer == "a":
            bias = jnp.concatenate([ba_refs[j][0, 2 * p], ba_refs[j][0, 2 * p + 1]], axis=0)
            return _scores(stacked(qa_ref, j, p), ka_ref[bands[j][0], slab(p)]) + bias
        bias = jnp.concatenate([bb_refs[j][0, p], bb_refs[j][0, N_PAIRS + p]], axis=0)
        return _scores(stacked(qb_ref, j, p), kb_ref[bands[j][1], :]) + bias

    def weights(s, sink):
        m = jnp.max(s, axis=1, keepdims=True)
        if sink is None:
            return jnp.exp(s - m).astype(BF16), None
        m = jnp.maximum(m, sink)
        return jnp.exp(s - m).astype(BF16), jnp.exp(sink - m)

    def attend(e, extra, v):
        z = jnp.dot(e, jnp.concatenate([v, jnp.ones_like(v)], axis=1), preferred_element_type=F32)
        denom = z[:, V7X_LANES:] if extra is None else z[:, V7X_LANES:] + extra
        return _unstack_pair(z[:, :V7X_LANES] * (1.0 / denom))

    jobs = [(mixer, j, p) for mixer in ("a", "b") for j in range(GROUPS_PER_STEP) for p in range(N_PAIRS)]
    s_all = [score(*job) for job in jobs]
    w_all = [weights(s, _sink_column(sinks_ref, p, GROUP_Q) if mixer == "b" else None)
             for s, (mixer, j, p) in zip(s_all, jobs)]
    for (e, extra), (mixer, j, p) in zip(w_all, jobs):
        rows = slice(j * GROUP_Q, (j + 1) * GROUP_Q)
        if mixer == "a":
            o_ref[rows, slab(p)] = attend(e, extra, va_ref[bands[j][0], slab(p)]).astype(BF16)
        else:
            o_ref[rows, slab(N_PAIRS + p)] = attend(e, extra, vb_ref[bands[j][1], :]).astype(BF16)


def _attn_prompt(sinks, qa, qb, ka, va, kb, vb, bias_a, bias_b, batch, seq, cast=()):
    rows = GROUPS_PER_STEP * GROUP_Q
    n_steps = seq // rows
    cast_specs, cast_shapes = (zip(*[_cast_chunks(w, batch * n_steps, lambda b, s: b * n_steps + s) for w in cast])
                               if cast else ((), ()))
    q_spec = pl.BlockSpec((rows, 2 * WIDTH), lambda b, s: (b * n_steps + s, 0))
    kv_spec = lambda w: pl.BlockSpec((seq, w), lambda b, s: (b, 0), pipeline_mode=pl.Buffered(1))
    bias_spec = lambda band, n_var, j: pl.BlockSpec(
        (1, N_HEADS, GROUP_Q, band),
        lambda b, s: (jnp.minimum(s * GROUPS_PER_STEP + j, n_var - 1), 0, 0, 0))
    vmem = (2 * seq * (2 * WIDTH + 2 * KV_WIDTH_B)
            + GROUPS_PER_STEP * 2 * 4 * N_HEADS * GROUP_Q * (BAND_A + BAND_B)
            + 2 * 2 * rows * 6 * WIDTH + GROUPS_PER_STEP * 6 * 4 * 2 * GROUP_Q * BAND_A
            + sum(2 * 6 * math.prod(spec.block_shape) for spec in cast_specs))
    out = pl.pallas_call(
        _attn_prompt_kernel,
        grid=(batch, n_steps),
        in_specs=[
            pl.BlockSpec(memory_space=pltpu.SMEM),
            q_spec, q_spec,
            kv_spec(WIDTH), kv_spec(WIDTH), kv_spec(KV_WIDTH_B), kv_spec(KV_WIDTH_B),
            *[bias_spec(BAND_A, N_VAR_A, j) for j in range(GROUPS_PER_STEP)],
            *[bias_spec(BAND_B, N_VAR_B, j) for j in range(GROUPS_PER_STEP)],
            *cast_specs,
        ],
        out_specs=[pl.BlockSpec((rows, 2 * WIDTH), lambda b, s: (b * n_steps + s, 0)), *cast_specs],
        out_shape=[jax.ShapeDtypeStruct((batch * seq, 2 * WIDTH), BF16), *cast_shapes],
        compiler_params=pltpu.CompilerParams(
            dimension_semantics=("arbitrary", "arbitrary"), vmem_limit_bytes=_vmem_limit(vmem)),
        name="attn_prompt",
    )(sinks, qa, qb, ka, va, kb, vb, *([bias_a] * GROUPS_PER_STEP), *([bias_b] * GROUPS_PER_STEP), *cast)
    return out[0], tuple(out[1:])


def _pair_ordered_cols(h):
    return (h % N_PAIRS) * V7X_LANES + (h // N_PAIRS) * HEAD_DIM


def _attn_sample_kernel(sinks_ref, qa_ref, qb_ref, ka_ref, va_ref, kb_ref, vb_ref,
                        cak_ref, cav_ref, cbk_ref, cbv_ref, ba_ref, bb_ref, o_ref):
    n_q = qa_ref.shape[0]
    n_kv = cbk_ref.shape[0]
    per_kv = N_HEADS // n_kv
    la = cak_ref.shape[2]
    head = lambda ref, idx: ref[:, idx * HEAD_DIM:(idx + 1) * HEAD_DIM]

    def query(ref, lo):
        lo = lo if lo % V7X_LANES < HEAD_DIM else WIDTH + lo
        return ref[:, lo:lo + HEAD_DIM]

    def scores(q, cache_t_ref, new_ref, idx):
        old = jnp.dot(q, cache_t_ref[idx].astype(BF16), preferred_element_type=F32)
        return jnp.concatenate([old, _scores(q, head(new_ref, idx))], axis=1)

    def values(e, cache_t_ref, new_ref, idx, n_old):
        old = _scores(e[:, :n_old], cache_t_ref[idx].astype(BF16))
        return old + jnp.dot(e[:, n_old:], head(new_ref, idx), preferred_element_type=F32)

    s_a = [scores(query(qa_ref, h * HEAD_DIM), cak_ref, ka_ref, h) + ba_ref[h] for h in range(N_HEADS)]
    s_b = []
    for c in range(n_kv):
        heads = range(c * per_kv, (c + 1) * per_kv)
        q = jnp.concatenate([query(qb_ref, _pair_ordered_cols(h)) for h in heads], axis=0)
        bias = jnp.concatenate([bb_ref[h] for h in heads], axis=0)
        s_b.append(scores(q, cbk_ref, kb_ref, c) + bias)

    row = lax.broadcasted_iota(jnp.int32, (per_kv * n_q, 1), 0)
    p_a = [_softmax_terms(s) for s in s_a]
    p_b = []
    for c in range(n_kv):
        sink = jnp.zeros((per_kv * n_q, 1), F32)
        for i in range(per_kv):
            sink = jnp.where((row >= i * n_q) & (row < (i + 1) * n_q), sinks_ref[c * per_kv + i], sink)
        p_b.append(_softmax_terms(s_b[c], sink))

    out_a = [values(e, cav_ref, va_ref, h, la) * r for h, (e, r) in enumerate(p_a)]
    out_b = [None] * N_HEADS
    for c, (e, r) in enumerate(p_b):
        o = values(e, cbv_ref, vb_ref, c, cbv_ref.shape[2]) * r
        for i in range(per_kv):
            out_b[c * per_kv + i] = o[i * n_q:(i + 1) * n_q]
    order_b = [h for p in range(N_PAIRS) for h in (p, N_PAIRS + p)]
    o_ref[...] = jnp.concatenate(out_a + [out_b[h] for h in order_b], axis=1).astype(BF16)


def _attn_sample(sinks, qa, qb, ka, va, kb, vb, cak, cav, cbk, cbv, bias_a, bias_b, batch, s_len):
    row = lambda w: pl.BlockSpec((s_len, w), lambda b: (b, 0))
    cache = lambda a: pl.BlockSpec((None,) + a.shape[1:], lambda b: (b, 0, 0, 0))
    whole = lambda a: pl.BlockSpec(a.shape, lambda b: (0,) * a.ndim)
    return pl.pallas_call(
        _attn_sample_kernel,
        grid=(batch,),
        in_specs=[
            pl.BlockSpec(memory_space=pltpu.SMEM),
            row(2 * WIDTH), row(2 * WIDTH), row(WIDTH), row(WIDTH), row(KV_WIDTH_B), row(KV_WIDTH_B),
            cache(cak), cache(cav), cache(cbk), cache(cbv),
            whole(bias_a), whole(bias_b),
        ],
        out_specs=pl.BlockSpec((s_len, 2 * WIDTH), lambda b: (b, 0)),
        out_shape=jax.ShapeDtypeStruct((batch * s_len, 2 * WIDTH), BF16),
        compiler_params=pltpu.CompilerParams(dimension_semantics=("arbitrary",)),
        name="attn_sample",
    )(sinks, qa, qb, ka, va, kb, vb, cak, cav, cbk, cbv, bias_a, bias_b)


def _pair_order(x, axis):
    shape = x.shape
    split = shape[:axis] + (N_HEADS // N_PAIRS, N_PAIRS, HEAD_DIM) + shape[axis + 1:]
    return jnp.swapaxes(x.reshape(split), axis, axis + 1).reshape(shape)


def _prep_weights(w_in, w_out, group_gains):
    qa, ka, va, qb, kvb = jnp.split(w_in, (WIDTH, 2 * WIDTH, 3 * WIDTH, 4 * WIDTH), axis=1)
    win = jnp.concatenate([qa * SCALE, ka, va, _pair_order(qb, 1) * SCALE, kvb], axis=1).astype(BF16)
    wout = jnp.concatenate([w_out[:WIDTH], _pair_order(w_out[WIDTH:], 0)], axis=0).astype(BF16)
    gg = jnp.concatenate([group_gains[:WIDTH], _pair_order(group_gains[WIDTH:], 0)])
    return win, wout, gg.reshape(1, 2 * WIDTH)


def kernel(x_prompt, x_sample, cache_a_k, cache_a_v, cache_b_k, cache_b_v, c_prompt, c_sample, w_mod, b_mod,
           norm_gains, w1_gate, w1_up, w1_down, w_in, w_out, group_gains, rel_bias_a, t5_bias_table, sinks_b,
           w2_gate, w2_up, w2_down):
    depth = w_mod.shape[0]
    assert depth == 1
    batch, seq, _ = x_prompt.shape
    s_batch, s_len, _ = x_sample.shape
    la, lb = cache_a_k.shape[2], cache_b_k.shape[2]
    assert la == REACH_A and lb == REACH_B and seq % 512 == 0

    c_rows = jnp.concatenate([c_prompt, c_sample], axis=0)
    pad_rows = (-c_rows.shape[0]) % 8
    c_rows = jnp.pad(c_rows, ((0, pad_rows), (0, 0)))
    mod = _modulation(c_rows, w_mod[0], b_mod[0])
    mod_p = mod[:, :batch].reshape(N_MOD, batch, 1, D_MODEL)
    mod_s = mod[:, batch:batch + s_batch].reshape(N_MOD, s_batch, 1, D_MODEL)

    bias_a, bias_b, bias_sa, bias_sb, *w1 = _bias_tables(
        rel_bias_a[0], t5_bias_table, s_len, la, lb, cast=(w1_gate[0], w1_up[0], w1_down[0]))

    gains = norm_gains[0]
    win, wout, gg = _prep_weights(w_in[0], w_out[0], group_gains[0])
    sinks = sinks_b[0]

    rt_p = 512
    (x1p, qa, ka, va, qb, kb, vb, ka32, va32, kb32, vb32) = _ffn1_qkv(
        x_prompt, mod_p, gains, *w1, win, 1, rt_p)
    op, w2 = _attn_prompt(sinks, qa, qb, ka, va, kb, vb, bias_a, bias_b, batch, seq,
                          cast=(w2_gate[0], w2_up[0], w2_down[0]))
    y_prompt = _out_ffn2(x1p, op, mod_p, gains, gg, wout, *w2, 1, rt_p)
    rows_last = lambda t: jnp.transpose(t.reshape(depth, batch, -1, HEAD_DIM, t.shape[-1]), (0, 1, 4, 2, 3))
    new_p = tuple(rows_last(t) for t in (ka32, va32, kb32, vb32))

    gt_s = 16
    (x1s, qas, kas, vas, qbs, kbs, vbs, kas32, vas32, kbs32, vbs32) = _ffn1_qkv(
        x_sample, mod_s, gains, *w1, win, gt_s, s_len)
    os_ = _attn_sample(sinks, qas, qbs, kas, vas, kbs, vbs,
                       *(jnp.transpose(c[0], (0, 2, 3, 1)) for c in (cache_a_k, cache_a_v, cache_b_k, cache_b_v)),
                       bias_sa, bias_sb, s_batch, s_len)
    y_sample = _out_ffn2(x1s, os_, mod_s, gains, gg, wout, *w2, gt_s, s_len)
    new_s = (kas32.reshape(depth, s_batch, s_len, N_HEADS, HEAD_DIM),
             vas32.reshape(depth, s_batch, s_len, N_HEADS, HEAD_DIM),
             kbs32.reshape(depth, s_batch, s_len, 2, HEAD_DIM), vbs32.reshape(depth, s_batch, s_len, 2, HEAD_DIM))

    return (y_prompt, y_sample) + new_p + new_s
```

```python
import functools
import math

import jax
import jax.numpy as jnp
from jax import lax
from jax.experimental import pallas as pl
from jax.experimental.pallas import tpu as pltpu

F32 = jnp.float32
BF16 = jnp.bfloat16

D_MODEL = 1024
D_FF = 2816
CHUNK = 64
HEAD_DIM = 64
N_HEADS = 8
N_PAIRS = N_HEADS // 2
WIDTH = N_HEADS * HEAD_DIM
KV_WIDTH_B = 2 * HEAD_DIM
REACH_A = 8 * CHUNK
REACH_B = 2 * CHUNK
REL_CLIP_A = 128
N_BUCKETS = 32
N_MOD = 9
FFN_RES = 0.5
EPS = 1e-6
SCALE = HEAD_DIM ** -0.5
NEG_INF = -1e30

V7X_LANES = 128
V7X_VMEM_BYTES = 64 * 1024 * 1024
V7X_VMEM_LIMIT_CAP = 56 * 1024 * 1024

GROUP_Q = 2 * CHUNK
BAND_A = REACH_A + GROUP_Q
BAND_B = REACH_B + GROUP_Q
ORIGIN_A = 1024
WIDTH_MASTER_A = ORIGIN_A + BAND_A
ORIGIN_B = 256
WIDTH_MASTER_B = ORIGIN_B + BAND_B
N_VAR_A = REACH_A // GROUP_Q + 1
N_VAR_B = REACH_B // GROUP_Q + 1


def _vmem_limit(estimate_bytes):
    return int(min(max(estimate_bytes, 32 * 1024 * 1024), V7X_VMEM_LIMIT_CAP))


def _rms(x, gain):
    ms = jnp.mean(x * x, axis=-1, keepdims=True)
    return x * lax.rsqrt(ms + EPS) * gain


def _adaln(x, gain, shift, scale):
    return _rms(x, gain) * (1.0 + scale) + shift


def _silu(x):
    return x * (1.0 / (1.0 + jnp.exp(-x)))


def _mod_kernel(c_ref, w_ref, b_ref, o_ref):
    s = _silu(c_ref[...]).astype(BF16)
    o_ref[0] = jnp.dot(s, w_ref[...].astype(BF16), preferred_element_type=F32) + b_ref[0]


def _modulation(c_rows, w_mod, b_mod):
    rows = c_rows.shape[0]
    return pl.pallas_call(
        _mod_kernel,
        grid=(N_MOD,),
        in_specs=[
            pl.BlockSpec((rows, D_MODEL), lambda j: (0, 0)),
            pl.BlockSpec((D_MODEL, D_MODEL), lambda j: (0, j)),
            pl.BlockSpec((1, 1, D_MODEL), lambda j: (j, 0, 0)),
        ],
        out_specs=pl.BlockSpec((1, rows, D_MODEL), lambda j: (j, 0, 0)),
        out_shape=jax.ShapeDtypeStruct((N_MOD, rows, D_MODEL), F32),
        name="modulation",
    )(c_rows, w_mod, b_mod.reshape(N_MOD, 1, D_MODEL))


def _skew(base_row, n_rows):
    width = base_row.shape[1]
    m = jnp.broadcast_to(base_row, (n_rows, width))
    row = lax.broadcasted_iota(jnp.int32, (n_rows, width), 0)
    shift = 1
    while shift < n_rows:
        m = jnp.where((row & shift) != 0, pltpu.roll(m, shift, axis=1), m)
        shift *= 2
    return m


def _band_mask(n_rows, width, origin, n_prev):
    row = lax.broadcasted_iota(jnp.int32, (n_rows, width), 0)
    col = lax.broadcasted_iota(jnp.int32, (n_rows, width), 1)
    q_chunk = row >> 6
    k_chunk = (col >> 6) - (origin // CHUNK)
    return (k_chunk >= q_chunk - n_prev) & (k_chunk <= q_chunk)


def _t5_bucket(rel):
    half = N_BUCKETS // 2
    max_exact = half // 2
    n = jnp.abs(rel)
    n2 = n * n
    large = jnp.full(rel.shape, max_exact, jnp.int32)
    for k in range(1, half - max_exact):
        large = large + jnp.where(n2 >= (max_exact * max_exact) * (2 ** k), 1, 0)
    return jnp.where(rel > 0, half, 0) + jnp.where(n < max_exact, n, large)


def _bias_kernel(rel_ref, t5_ref, *refs, s_len, la, lb):
    n_cast = (len(refs) - 4) // 2
    for src, dst in zip(refs[:n_cast], refs[n_cast + 4:]):
        dst[...] = src[...].astype(BF16)
    ba_ref, bb_ref, bsa_ref, bsb_ref = refs[n_cast:n_cast + 4]

    tab = rel_ref[0]
    n_tab = 2 * REL_CLIP_A + 1
    first, last = tab[:, 0:1], tab[:, n_tab - 1:n_tab]
    win_lo = ORIGIN_A - 2 * REL_CLIP_A
    col = lax.broadcasted_iota(jnp.int32, (1, tab.shape[1] + V7X_LANES), 1)
    shifted = jnp.concatenate([jnp.zeros((1, V7X_LANES), F32), tab], axis=1)
    u_win = jnp.where(col < V7X_LANES, first, jnp.where(col >= V7X_LANES + n_tab, last, shifted))
    window = _skew(u_win, GROUP_Q)[:, V7X_LANES:]
    n_left = win_lo + V7X_LANES
    n_right = WIDTH_MASTER_A - n_left - window.shape[1]
    skew_a = jnp.concatenate([jnp.broadcast_to(first, (GROUP_Q, n_left)), window,
                              jnp.broadcast_to(last, (GROUP_Q, n_right))], axis=1)
    master_a = jnp.where(_band_mask(GROUP_Q, WIDTH_MASTER_A, ORIGIN_A, REACH_A // CHUNK), skew_a, NEG_INF)
    for v in range(N_VAR_A):
        start = ORIGIN_A - min(v * GROUP_Q, REACH_A) if v < N_VAR_A - 1 else ORIGIN_A - REACH_A
        ba_ref[v, 0] = master_a[:, start:start + BAND_A]
    bsa_ref[0] = skew_a[:s_len, ORIGIN_A - la:ORIGIN_A + s_len]

    t5 = t5_ref[0]
    rel = lax.broadcasted_iota(jnp.int32, (1, WIDTH_MASTER_B), 1) - ORIGIN_B
    bucket = _t5_bucket(rel)
    u_b = jnp.zeros((1, WIDTH_MASTER_B), F32)
    for i in range(N_BUCKETS):
        u_b = jnp.where(bucket == i, t5[:, i:i + 1], u_b)
    skew_b = _skew(u_b, GROUP_Q)
    master_b = jnp.where(_band_mask(GROUP_Q, WIDTH_MASTER_B, ORIGIN_B, REACH_B // CHUNK), skew_b, NEG_INF)
    for v in range(N_VAR_B):
        start = ORIGIN_B - min(v * GROUP_Q, REACH_B) if v < N_VAR_B - 1 else ORIGIN_B - REACH_B
        bb_ref[v, 0] = master_b[:, start:start + BAND_B]
    bsb_ref[0] = skew_b[:s_len, ORIGIN_B - lb:ORIGIN_B + s_len]


V7X_BF16_SUBLANES = 16


def _cast_chunks(w, n_steps, step_of):
    rows, cols = w.shape
    n_chunks = n_steps
    while rows % (n_chunks * V7X_BF16_SUBLANES):
        n_chunks //= 2
    spec = pl.BlockSpec((rows // n_chunks, cols), lambda *g: (jnp.minimum(step_of(*g), n_chunks - 1), 0))
    return spec, jax.ShapeDtypeStruct(w.shape, BF16)


def _bias_tables(rel_bias_a, t5_table, s_len, la, lb, cast=()):
    n_tab = 2 * REL_CLIP_A + 1
    tab_w = 3 * V7X_LANES
    rel_p = jnp.pad(rel_bias_a, ((0, 0), (0, tab_w - n_tab))).reshape(N_HEADS, 1, tab_w)
    t5_p = t5_table.reshape(N_HEADS, 1, N_BUCKETS)
    kern = functools.partial(_bias_kernel, s_len=s_len, la=la, lb=lb)
    cast_specs, cast_shapes = zip(*[_cast_chunks(w, N_HEADS, lambda h: h) for w in cast]) if cast else ((), ())
    return pl.pallas_call(
        kern,
        grid=(N_HEADS,),
        in_specs=[
            pl.BlockSpec((1, 1, tab_w), lambda h: (h, 0, 0)),
            pl.BlockSpec((1, 1, N_BUCKETS), lambda h: (h, 0, 0)),
            *cast_specs,
        ],
        out_specs=[
            pl.BlockSpec((N_VAR_A, 1, GROUP_Q, BAND_A), lambda h: (0, h, 0, 0)),
            pl.BlockSpec((N_VAR_B, 1, GROUP_Q, BAND_B), lambda h: (0, h, 0, 0)),
            pl.BlockSpec((1, s_len, la + s_len), lambda h: (h, 0, 0)),
            pl.BlockSpec((1, s_len, lb + s_len), lambda h: (h, 0, 0)),
            *cast_specs,
        ],
        out_shape=[
            jax.ShapeDtypeStruct((N_VAR_A, N_HEADS, GROUP_Q, BAND_A), F32),
            jax.ShapeDtypeStruct((N_VAR_B, N_HEADS, GROUP_Q, BAND_B), F32),
            jax.ShapeDtypeStruct((N_HEADS, s_len, la + s_len), F32),
            jax.ShapeDtypeStruct((N_HEADS, s_len, lb + s_len), F32),
            *cast_shapes,
        ],
        compiler_params=pltpu.CompilerParams(dimension_semantics=("arbitrary",)),
        name="bias_tables",
    )(rel_p, t5_p, *cast)


V7X_MXU_DIM = 256
FF_SLABS = ((0, 6 * V7X_MXU_DIM), (6 * V7X_MXU_DIM, D_FF))
FF_SLAB_MAX = max(hi - lo for lo, hi in FF_SLABS)


def _swiglu(h, wg_ref, wu_ref, wd_ref):
    acts = []
    for lo, hi in FF_SLABS:
        g = jnp.dot(h, wg_ref[:, lo:hi], preferred_element_type=F32)
        u = jnp.dot(h, wu_ref[:, lo:hi], preferred_element_type=F32)
        acts.append((_silu(g) * u).astype(BF16))
    acc = None
    for (lo, hi), a in zip(FF_SLABS, acts):
        part = jnp.dot(a, wd_ref[lo:hi, :], preferred_element_type=F32)
        acc = part if acc is None else acc + part
    return acc


N_SUB = 2


def _sub_tiles(gt, rt):
    if gt == 1:
        step = rt // N_SUB
        return [(slice(0, 1), slice(s * step, (s + 1) * step), s * step, step) for s in range(N_SUB)]
    step = gt // N_SUB
    return [(slice(s * step, (s + 1) * step), slice(0, rt), s * step * rt, step * rt) for s in range(N_SUB)]


def _ffn1_qkv_kernel(x_ref, mod_ref, gains_ref, wg_ref, wu_ref, wd_ref, win_ref,
                     x1_ref, qa_ref, ka_ref, va_ref, qb_ref, kb_ref, vb_ref,
                     ka32_ref, va32_ref, kb32_ref, vb32_ref, *tail_scratch):
    gt, rt, _ = x_ref.shape
    tm = gt * rt
    gain = lambda k: gains_ref[k:k + 1, :]

    def store_tail(ref, val, r0):
        first = tm - ref.shape[0]
        lo = max(r0, first)
        if lo < r0 + val.shape[0]:
            ref[lo - first:r0 + val.shape[0] - first, :] = val[lo - r0:, :]

    def store_heads(ref, val, r0):
        n_heads = val.shape[1] // HEAD_DIM
        for h in range(n_heads):
            ref[pl.ds(r0 * n_heads + h, val.shape[0], stride=n_heads), :] = val[:, h * HEAD_DIM:(h + 1) * HEAD_DIM]

    def store_split(ref, rows, q, pairs):
        q = q * SCALE
        head = lambda h: q[:, h * HEAD_DIM:(h + 1) * HEAD_DIM]
        zeros = jnp.zeros((q.shape[0], HEAD_DIM), F32)
        ref[rows, :WIDTH] = jnp.concatenate([t for a, _ in pairs for t in (head(a), zeros)], axis=1).astype(BF16)
        ref[rows, WIDTH:] = jnp.concatenate([t for _, b in pairs for t in (zeros, head(b))], axis=1).astype(BF16)

    pairs_a = [(2 * p, 2 * p + 1) for p in range(N_PAIRS)]
    pairs_b = [(p, N_PAIRS + p) for p in range(N_PAIRS)]

    subs = _sub_tiles(gt, rt)
    mods = [lambda k, gs=gs: mod_ref[k, gs] for gs, _, _, _ in subs]
    xs = [x_ref[gs, rs, :] for gs, rs, _, _ in subs]
    hs = [_adaln(x, gain(0), mod(0), mod(1)).reshape(n, D_MODEL).astype(BF16)
          for x, mod, (_, _, _, n) in zip(xs, mods, subs)]
    fs = [_swiglu(h, wg_ref, wu_ref, wd_ref).reshape(x.shape) for h, x in zip(hs, xs)]
    h2s = []
    for x, f, mod, (gs, rs, _, n) in zip(xs, fs, mods, subs):
        x1 = x + (FFN_RES * mod(2)) * _rms(f, gain(1))
        x1_ref[gs, rs, :] = x1
        h2s.append(_adaln(x1, gain(2), mod(3), mod(4)).reshape(n, D_MODEL).astype(BF16))

    for h2, (_, _, r0, n) in zip(h2s, subs):
        rows = slice(r0, r0 + n)

        def proj(lo, width):
            return jnp.dot(h2, win_ref[:, lo:lo + width], preferred_element_type=F32)

        store_split(qa_ref, rows, proj(0, WIDTH), pairs_a)
        ka = proj(WIDTH, WIDTH)
        va = proj(2 * WIDTH, WIDTH)
        store_split(qb_ref, rows, proj(3 * WIDTH, WIDTH), pairs_b)
        kvb = proj(4 * WIDTH, 2 * KV_WIDTH_B)
        kb, vb = kvb[:, :KV_WIDTH_B], kvb[:, KV_WIDTH_B:]
        ka_ref[rows, :] = ka.astype(BF16)
        va_ref[rows, :] = va.astype(BF16)
        kb_ref[rows, :] = kb.astype(BF16)
        vb_ref[rows, :] = vb.astype(BF16)
        if tail_scratch:
            for ref, val in zip(tail_scratch, (ka, va, kb, vb)):
                store_tail(ref, val, r0)
        else:
            store_heads(ka32_ref, ka, r0)
            store_heads(va32_ref, va, r0)
            store_tail(kb32_ref, kb, r0)
            store_tail(vb32_ref, vb, r0)

    if tail_scratch:
        @pl.when(pl.program_id(1) == pl.num_programs(1) - 1)
        def _():
            for out_ref, ref in zip((ka32_ref, va32_ref, kb32_ref, vb32_ref), tail_scratch):
                out_ref[...] = ref[...].T


def _resident(shape):
    return pl.BlockSpec(shape, lambda *_: (0,) * len(shape), pipeline_mode=pl.Buffered(1))


def _ffn1_qkv(x, mod, gains, wg, wu, wd, win, gt, rt):
    n_groups, rows, _ = x.shape
    n_g, n_r = n_groups // gt, rows // rt
    tm = gt * rt
    tokens = n_groups * rows
    ta = gt * min(REACH_A, rows)
    tb = gt * min(REACH_B, rows)
    tile = lambda gi, ri: (gi * n_r + ri, 0)
    tail = lambda gi, ri: (gi, 0)
    tok = lambda w, dt: jax.ShapeDtypeStruct((tokens, w), dt)
    if gt == 1:
        assert ta <= tm and tb <= tm
        tail_specs = [pl.BlockSpec((None, w, t), lambda gi, ri: (gi, 0, 0))
                      for w, t in ((WIDTH, ta), (WIDTH, ta), (KV_WIDTH_B, tb), (KV_WIDTH_B, tb))]
        tail_shapes = [jax.ShapeDtypeStruct((n_g, w, t), F32)
                       for w, t in ((WIDTH, ta), (WIDTH, ta), (KV_WIDTH_B, tb), (KV_WIDTH_B, tb))]
        scratch = [pltpu.VMEM((ta, WIDTH), F32), pltpu.VMEM((ta, WIDTH), F32),
                   pltpu.VMEM((tb, KV_WIDTH_B), F32), pltpu.VMEM((tb, KV_WIDTH_B), F32)]
    else:
        assert n_r == 1
        tail_specs = [pl.BlockSpec((tm * N_HEADS, HEAD_DIM), tail), pl.BlockSpec((tm * N_HEADS, HEAD_DIM), tail),
                      pl.BlockSpec((tm, KV_WIDTH_B), tail), pl.BlockSpec((tm, KV_WIDTH_B), tail)]
        tail_shapes = [jax.ShapeDtypeStruct((tokens * N_HEADS, HEAD_DIM), F32),
                       jax.ShapeDtypeStruct((tokens * N_HEADS, HEAD_DIM), F32),
                       jax.ShapeDtypeStruct((tokens, KV_WIDTH_B), F32), jax.ShapeDtypeStruct((tokens, KV_WIDTH_B), F32)]
        scratch = []
    vmem = (2 * 3 * D_MODEL * D_FF + 2 * D_MODEL * win.shape[1]
            + 2 * 2 * 4 * tm * D_MODEL
            + 2 * 2 * tm * (6 * WIDTH + 2 * KV_WIDTH_B)
            + 2 * 4 * (2 * ta * WIDTH + 2 * tb * KV_WIDTH_B)
            + tm * (2 * 4 + 2) * FF_SLAB_MAX + 3 * 4 * tm * D_MODEL
            + 4 * tm * win.shape[1])
    return pl.pallas_call(
        _ffn1_qkv_kernel,
        grid=(n_g, n_r),
        in_specs=[
            pl.BlockSpec((gt, rt, D_MODEL), lambda gi, ri: (gi, ri, 0)),
            pl.BlockSpec((N_MOD, gt, 1, D_MODEL), lambda gi, ri: (0, gi, 0, 0)),
            _resident(gains.shape),
            _resident(wg.shape), _resident(wu.shape), _resident(wd.shape), _resident(win.shape),
        ],
        out_specs=[
            pl.BlockSpec((gt, rt, D_MODEL), lambda gi, ri: (gi, ri, 0)),
            pl.BlockSpec((tm, 2 * WIDTH), tile), pl.BlockSpec((tm, WIDTH), tile), pl.BlockSpec((tm, WIDTH), tile),
            pl.BlockSpec((tm, 2 * WIDTH), tile), pl.BlockSpec((tm, KV_WIDTH_B), tile),
            pl.BlockSpec((tm, KV_WIDTH_B), tile),
            *tail_specs,
        ],
        out_shape=[
            jax.ShapeDtypeStruct(x.shape, F32),
            tok(2 * WIDTH, BF16), tok(WIDTH, BF16), tok(WIDTH, BF16),
            tok(2 * WIDTH, BF16), tok(KV_WIDTH_B, BF16), tok(KV_WIDTH_B, BF16),
            *tail_shapes,
        ],
        scratch_shapes=scratch,
        compiler_params=pltpu.CompilerParams(
            dimension_semantics=("arbitrary", "arbitrary"), vmem_limit_bytes=_vmem_limit(vmem)),
        name="ffn1_qkv",
    )(x, mod, gains, wg, wu, wd, win)


def _out_ffn2_kernel(x_ref, o_ref, mod_ref, gains_ref, gg_ref, wout_ref, wg_ref, wu_ref, wd_ref, y_ref):
    gt, rt, _ = x_ref.shape
    gain = lambda k: gains_ref[k:k + 1, :]
    gg = gg_ref[...]
    subs = _sub_tiles(gt, rt)
    mods = [lambda k, gs=gs: mod_ref[k, gs] for gs, _, _, _ in subs]
    xs = [x_ref[gs, rs, :] for gs, rs, _, _ in subs]
    ons = []
    for _, _, r0, n in subs:
        o = o_ref[r0:r0 + n, :].astype(F32)
        on = jnp.concatenate([_rms(o[:, :WIDTH], gg[:, :WIDTH]), _rms(o[:, WIDTH:], gg[:, WIDTH:])], axis=1)
        ons.append(on.astype(BF16))
    mixed = [jnp.dot(on, wout_ref[...], preferred_element_type=F32).reshape(x.shape) for on, x in zip(ons, xs)]
    x2s = [x + mod(5) * _rms(m, gain(3)) for x, m, mod in zip(xs, mixed, mods)]
    hs = [_adaln(x2, gain(4), mod(6), mod(7)).reshape(n, D_MODEL).astype(BF16)
          for x2, mod, (_, _, _, n) in zip(x2s, mods, subs)]
    fs = [_swiglu(h, wg_ref, wu_ref, wd_ref).reshape(x.shape) for h, x in zip(hs, xs)]
    for x2, f, mod, (gs, rs, _, _) in zip(x2s, fs, mods, subs):
        y_ref[gs, rs, :] = x2 + (FFN_RES * mod(8)) * _rms(f, gain(5))


def _out_ffn2(x1, o, mod, gains, gg, wout, wg, wu, wd, gt, rt):
    n_groups, rows, _ = x1.shape
    n_g, n_r = n_groups // gt, rows // rt
    tm = gt * rt
    vmem = (2 * 3 * D_MODEL * D_FF + 2 * wout.size
            + 2 * 2 * 4 * tm * D_MODEL + 2 * 2 * tm * 2 * WIDTH
            + tm * (2 * 4 + 2) * FF_SLAB_MAX + 5 * 4 * tm * D_MODEL)
    return pl.pallas_call(
        _out_ffn2_kernel,
        grid=(n_g, n_r),
        in_specs=[
            pl.BlockSpec((gt, rt, D_MODEL), lambda gi, ri: (gi, ri, 0)),
            pl.BlockSpec((tm, 2 * WIDTH), lambda gi, ri: (gi * n_r + ri, 0)),
            pl.BlockSpec((N_MOD, gt, 1, D_MODEL), lambda gi, ri: (0, gi, 0, 0)),
            _resident(gains.shape), _resident(gg.shape), _resident(wout.shape),
            _resident(wg.shape), _resident(wu.shape), _resident(wd.shape),
        ],
        out_specs=pl.BlockSpec((gt, rt, D_MODEL), lambda gi, ri: (gi, ri, 0)),
        out_shape=jax.ShapeDtypeStruct(x1.shape, F32),
        compiler_params=pltpu.CompilerParams(
            dimension_semantics=("arbitrary", "arbitrary"), vmem_limit_bytes=_vmem_limit(vmem)),
        name="out_ffn2",
    )(x1, o, mod, gains, gg, wout, wg, wu, wd)


def _softmax_terms(s, sink=None):
    m = jnp.max(s, axis=1, keepdims=True)
    if sink is not None:
        m = jnp.maximum(m, sink)
    e = jnp.exp(s - m)
    denom = jnp.sum(e, axis=1, keepdims=True)
    if sink is not None:
        denom = denom + jnp.exp(sink - m)
    return e.astype(BF16), 1.0 / denom


def _scores(q, k):
    return lax.dot_general(q, k, (((1,), (1,)), ((), ())), preferred_element_type=F32)


def _unstack_pair(z):
    n_q = z.shape[0] // 2
    is_lo = lax.broadcasted_iota(jnp.int32, (n_q, z.shape[1]), 1) < HEAD_DIM
    return jnp.where(is_lo, z[:n_q], z[n_q:])


def _sink_column(sinks_ref, pair, n_q):
    row = lax.broadcasted_iota(jnp.int32, (2 * n_q, 1), 0)
    return jnp.where(row < n_q, sinks_ref[pair], sinks_ref[N_PAIRS + pair])


GROUPS_PER_STEP = 2


def _attn_prompt_kernel(sinks_ref, qa_ref, qb_ref, ka_ref, va_ref, kb_ref, vb_ref, *rest):
    ba_refs, bb_refs = rest[:GROUPS_PER_STEP], rest[GROUPS_PER_STEP:2 * GROUPS_PER_STEP]
    n_cast = (len(rest) - 2 * GROUPS_PER_STEP - 1) // 2
    o_ref = rest[2 * GROUPS_PER_STEP + n_cast]
    for src, dst in zip(rest[2 * GROUPS_PER_STEP:2 * GROUPS_PER_STEP + n_cast], rest[len(rest) - n_cast:]):
        dst[...] = src[...].astype(BF16)
    slab = lambda p: slice(p * V7X_LANES, (p + 1) * V7X_LANES)
    bands = []
    for j in range(GROUPS_PER_STEP):
        g = pl.program_id(1) * GROUPS_PER_STEP + j
        start_a = pl.multiple_of(jnp.maximum(g * GROUP_Q - REACH_A, 0), GROUP_Q)
        start_b = pl.multiple_of(jnp.maximum(g * GROUP_Q - REACH_B, 0), GROUP_Q)
        bands.append((pl.ds(start_a, BAND_A), pl.ds(start_b, BAND_B)))

    def stacked(q_ref, j, p):
        rows = slice(j * GROUP_Q, (j + 1) * GROUP_Q)
        return jnp.concatenate([q_ref[rows, slab(p)], q_ref[rows, slab(N_PAIRS + p)]], axis=0)

    def score(mixer, j, p):
        if mixer == "a":
            bias = jnp.concatenate([ba_refs[j][0, 2 * p], ba_refs[j][0, 2 * p + 1]], axis=0)
            return _scores(stacked(qa_ref, j, p), ka_ref[bands[j][0], slab(p)]) + bias
        bias = jnp.concatenate([bb_refs[j][0, p], bb_refs[j][0, N_PAIRS + p]], axis=0)
        return _scores(stacked(qb_ref, j, p), kb_ref[bands[j][1], :]) + bias

    def weights(s, sink):
        m = jnp.max(s, axis=1, keepdims=True)
        if sink is None:
            return jnp.exp(s - m).astype(BF16), None
        m = jnp.maximum(m, sink)
        return jnp.exp(s - m).astype(BF16), jnp.exp(sink - m)

    def attend(e, extra, v):
        z = jnp.dot(e, jnp.concatenate([v, jnp.ones_like(v)], axis=1), preferred_element_type=F32)
        denom = z[:, V7X_LANES:] if extra is None else z[:, V7X_LANES:] + extra
        return _unstack_pair(z[:, :V7X_LANES] * (1.0 / denom))

    jobs = [(mixer, j, p) for mixer in ("a", "b") for j in range(GROUPS_PER_STEP) for p in range(N_PAIRS)]
    s_all = [score(*job) for job in jobs]
    w_all = [weights(s, _sink_column(sinks_ref, p, GROUP_Q) if mixer == "b" else None)
             for s, (mixer, j, p) in zip(s_all, jobs)]
    for (e, extra), (mixer, j, p) in zip(w_all, jobs):
        rows = slice(j * GROUP_Q, (j + 1) * GROUP_Q)
        if mixer == "a":
            o_ref[rows, slab(p)] = attend(e, extra, va_ref[bands[j][0], slab(p)]).astype(BF16)
        else:
            o_ref[rows, slab(N_PAIRS + p)] = attend(e, extra, vb_ref[bands[j][1], :]).astype(BF16)


def _attn_prompt(sinks, qa, qb, ka, va, kb, vb, bias_a, bias_b, batch, seq, cast=()):
    rows = GROUPS_PER_STEP * GROUP_Q
    n_steps = seq // rows
    cast_specs, cast_shapes = (zip(*[_cast_chunks(w, batch * n_steps, lambda b, s: b * n_steps + s) for w in cast])
                               if cast else ((), ()))
    q_spec = pl.BlockSpec((rows, 2 * WIDTH), lambda b, s: (b * n_steps + s, 0))
    kv_spec = lambda w: pl.BlockSpec((seq, w), lambda b, s: (b, 0), pipeline_mode=pl.Buffered(1))
    bias_spec = lambda band, n_var, j: pl.BlockSpec(
        (1, N_HEADS, GROUP_Q, band),
        lambda b, s: (jnp.minimum(s * GROUPS_PER_STEP + j, n_var - 1), 0, 0, 0))
    vmem = (2 * seq * (2 * WIDTH + 2 * KV_WIDTH_B)
            + GROUPS_PER_STEP * 2 * 4 * N_HEADS * GROUP_Q * (BAND_A + BAND_B)
            + 2 * 2 * rows * 6 * WIDTH + GROUPS_PER_STEP * 6 * 4 * 2 * GROUP_Q * BAND_A
            + sum(2 * 6 * math.prod(spec.block_shape) for spec in cast_specs))
    out = pl.pallas_call(
        _attn_prompt_kernel,
        grid=(batch, n_steps),
        in_specs=[
            pl.BlockSpec(memory_space=pltpu.SMEM),
            q_spec, q_spec,
            kv_spec(WIDTH), kv_spec(WIDTH), kv_spec(KV_WIDTH_B), kv_spec(KV_WIDTH_B),
            *[bias_spec(BAND_A, N_VAR_A, j) for j in range(GROUPS_PER_STEP)],
            *[bias_spec(BAND_B, N_VAR_B, j) for j in range(GROUPS_PER_STEP)],
            *cast_specs,
        ],
        out_specs=[pl.BlockSpec((rows, 2 * WIDTH), lambda b, s: (b * n_steps + s, 0)), *cast_specs],
        out_shape=[jax.ShapeDtypeStruct((batch * seq, 2 * WIDTH), BF16), *cast_shapes],
        compiler_params=pltpu.CompilerParams(
            dimension_semantics=("arbitrary", "arbitrary"), vmem_limit_bytes=_vmem_limit(vmem)),
        name="attn_prompt",
    )(sinks, qa, qb, ka, va, kb, vb, *([bias_a] * GROUPS_PER_STEP), *([bias_b] * GROUPS_PER_STEP), *cast)
    return out[0], tuple(out[1:])


def _pair_ordered_cols(h):
    return (h % N_PAIRS) * V7X_LANES + (h // N_PAIRS) * HEAD_DIM


def _attn_sample_kernel(sinks_ref, qa_ref, qb_ref, ka_ref, va_ref, kb_ref, vb_ref,
                        cak_ref, cav_ref, cbk_ref, cbv_ref, ba_ref, bb_ref, o_ref):
    n_q = qa_ref.shape[0]
    n_kv = cbk_ref.shape[0]
    per_kv = N_HEADS // n_kv
    la = cak_ref.shape[2]
    head = lambda ref, idx: ref[:, idx * HEAD_DIM:(idx + 1) * HEAD_DIM]

    def query(ref, lo):
        lo = lo if lo % V7X_LANES < HEAD_DIM else WIDTH + lo
        return ref[:, lo:lo + HEAD_DIM]

    def scores(q, cache_t_ref, new_ref, idx):
        old = jnp.dot(q, cache_t_ref[idx].astype(BF16), preferred_element_type=F32)
        return jnp.concatenate([old, _scores(q, head(new_ref, idx))], axis=1)

    def values(e, cache_t_ref, new_ref, idx, n_old):
        old = _scores(e[:, :n_old], cache_t_ref[idx].astype(BF16))
        return old + jnp.dot(e[:, n_old:], head(new_ref, idx), preferred_element_type=F32)

    s_a = [scores(query(qa_ref, h * HEAD_DIM), cak_ref, ka_ref, h) + ba_ref[h] for h in range(N_HEADS)]
    s_b = []
    for c in range(n_kv):
        heads = range(c * per_kv, (c + 1) * per_kv)
        q = jnp.concatenate([query(qb_ref, _pair_ordered_cols(h)) for h in heads], axis=0)
        bias = jnp.concatenate([bb_ref[h] for h in heads], axis=0)
        s_b.append(scores(q, cbk_ref, kb_ref, c) + bias)

    row = lax.broadcasted_iota(jnp.int32, (per_kv * n_q, 1), 0)
    p_a = [_softmax_terms(s) for s in s_a]
    p_b = []
    for c in range(n_kv):
        sink = jnp.zeros((per_kv * n_q, 1), F32)
        for i in range(per_kv):
            sink = jnp.where((row >= i * n_q) & (row < (i + 1) * n_q), sinks_ref[c * per_kv + i], sink)
        p_b.append(_softmax_terms(s_b[c], sink))

    out_a = [values(e, cav_ref, va_ref, h, la) * r for h, (e, r) in enumerate(p_a)]
    out_b = [None] * N_HEADS
    for c, (e, r) in enumerate(p_b):
        o = values(e, cbv_ref, vb_ref, c, cbv_ref.shape[2]) * r
        for i in range(per_kv):
            out_b[c * per_kv + i] = o[i * n_q:(i + 1) * n_q]
    order_b = [h for p in range(N_PAIRS) for h in (p, N_PAIRS + p)]
    o_ref[...] = jnp.concatenate(out_a + [out_b[h] for h in order_b], axis=1).astype(BF16)


def _attn_sample(sinks, qa, qb, ka, va, kb, vb, cak, cav, cbk, cbv, bias_a, bias_b, batch, s_len):
    row = lambda w: pl.BlockSpec((s_len, w), lambda b: (b, 0))
    cache = lambda a: pl.BlockSpec((None,) + a.shape[1:], lambda b: (b, 0, 0, 0))
    whole = lambda a: pl.BlockSpec(a.shape, lambda b: (0,) * a.ndim)
    return pl.pallas_call(
        _attn_sample_kernel,
        grid=(batch,),
        in_specs=[
            pl.BlockSpec(memory_space=pltpu.SMEM),
            row(2 * WIDTH), row(2 * WIDTH), row(WIDTH), row(WIDTH), row(KV_WIDTH_B), row(KV_WIDTH_B),
            cache(cak), cache(cav), cache(cbk), cache(cbv),
            whole(bias_a), whole(bias_b),
        ],
        out_specs=pl.BlockSpec((s_len, 2 * WIDTH), lambda b: (b, 0)),
        out_shape=jax.ShapeDtypeStruct((batch * s_len, 2 * WIDTH), BF16),
        compiler_params=pltpu.CompilerParams(dimension_semantics=("arbitrary",)),
        name="attn_sample",
    )(sinks, qa, qb, ka, va, kb, vb, cak, cav, cbk, cbv, bias_a, bias_b)


def _pair_order(x, axis):
    shape = x.shape
    split = shape[:axis] + (N_HEADS // N_PAIRS, N_PAIRS, HEAD_DIM) + shape[axis + 1:]
    return jnp.swapaxes(x.reshape(split), axis, axis + 1).reshape(shape)


def _prep_out_weights(w_out, group_gains):
    wout = jnp.concatenate([w_out[:WIDTH], _pair_order(w_out[WIDTH:], 0)], axis=0).astype(BF16)
    gg = jnp.concatenate([group_gains[:WIDTH], _pair_order(group_gains[WIDTH:], 0)])
    return wout, gg.reshape(1, 2 * WIDTH)


def kernel(x_prompt, x_sample, cache_a_k, cache_a_v, cache_b_k, cache_b_v, c_prompt, c_sample, w_mod, b_mod,
           norm_gains, w1_gate, w1_up, w1_down, w_in, w_out, group_gains, rel_bias_a, t5_bias_table, sinks_b,
           w2_gate, w2_up, w2_down):
    depth = w_mod.shape[0]
    assert depth == 1
    batch, seq, _ = x_prompt.shape
    s_batch, s_len, _ = x_sample.shape
    la, lb = cache_a_k.shape[2], cache_b_k.shape[2]
    assert la == REACH_A and lb == REACH_B and seq % 512 == 0

    c_rows = jnp.concatenate([c_prompt, c_sample], axis=0)
    pad_rows = (-c_rows.shape[0]) % 8
    c_rows = jnp.pad(c_rows, ((0, pad_rows), (0, 0)))
    mod = _modulation(c_rows, w_mod[0], b_mod[0])
    mod_p = mod[:, :batch].reshape(N_MOD, batch, 1, D_MODEL)
    mod_s = mod[:, batch:batch + s_batch].reshape(N_MOD, s_batch, 1, D_MODEL)

    bias_a, bias_b, bias_sa, bias_sb, *w1, win = _bias_tables(
        rel_bias_a[0], t5_bias_table, s_len, la, lb, cast=(w1_gate[0], w1_up[0], w1_down[0], w_in[0]))

    gains = norm_gains[0]
    wout, gg = _prep_out_weights(w_out[0], group_gains[0])
    sinks = sinks_b[0]

    rt_p = 512
    (x1p, qa, ka, va, qb, kb, vb, ka32, va32, kb32, vb32) = _ffn1_qkv(
        x_prompt, mod_p, gains, *w1, win, 1, rt_p)
    op, w2 = _attn_prompt(sinks, qa, qb, ka, va, kb, vb, bias_a, bias_b, batch, seq,
                          cast=(w2_gate[0], w2_up[0], w2_down[0]))
    y_prompt = _out_ffn2(x1p, op, mod_p, gains, gg, wout, *w2, 1, rt_p)
    rows_last = lambda t: jnp.transpose(t.reshape(depth, batch, -1, HEAD_DIM, t.shape[-1]), (0, 1, 4, 2, 3))
    new_p = tuple(rows_last(t) for t in (ka32, va32, kb32, vb32))

    gt_s = 16
    (x1s, qas, kas, vas, qbs, kbs, vbs, kas32, vas32, kbs32, vbs32) = _ffn1_qkv(
        x_sample, mod_s, gains, *w1, win, gt_s, s_len)
    os_ = _attn_sample(sinks, qas, qbs, kas, vas, kbs, vbs,
                       *(jnp.transpose(c[0], (0, 2, 3, 1)) for c in (cache_a_k, cache_a_v, cache_b_k, cache_b_v)),
                       bias_sa, bias_sb, s_batch, s_len)
    y_sample = _out_ffn2(x1s, os_, mod_s, gains, gg, wout, *w2, gt_s, s_len)
    new_s = (kas32.reshape(depth, s_batch, s_len, N_HEADS, HEAD_DIM),
             vas32.reshape(depth, s_batch, s_len, N_HEADS, HEAD_DIM),
             kbs32.reshape(depth, s_batch, s_len, 2, HEAD_DIM), vbs32.reshape(depth, s_batch, s_len, 2, HEAD_DIM))

    return (y_prompt, y_sample) + new_p + new_s
```

```python
import functools
import math

import jax
import jax.numpy as jnp
from jax import lax
from jax.experimental import pallas as pl
from jax.experimental.pallas import tpu as pltpu

F32 = jnp.float32
BF16 = jnp.bfloat16

D_MODEL = 1024
D_FF = 2816
CHUNK = 64
HEAD_DIM = 64
N_HEADS = 8
N_PAIRS = N_HEADS // 2
WIDTH = N_HEADS * HEAD_DIM
KV_WIDTH_B = 2 * HEAD_DIM
REACH_A = 8 * CHUNK
REACH_B = 2 * CHUNK
REL_CLIP_A = 128
N_BUCKETS = 32
N_MOD = 9
FFN_RES = 0.5
EPS = 1e-6
SCALE = HEAD_DIM ** -0.5
NEG_INF = -1e30

V7X_LANES = 128
V7X_VMEM_BYTES = 64 * 1024 * 1024
V7X_VMEM_LIMIT_CAP = 56 * 1024 * 1024

GROUP_Q = 2 * CHUNK
BAND_A = REACH_A + GROUP_Q
BAND_B = REACH_B + GROUP_Q
ORIGIN_A = 1024
WIDTH_MASTER_A = ORIGIN_A + BAND_A
ORIGIN_B = 256
WIDTH_MASTER_B = ORIGIN_B + BAND_B
N_VAR_A = REACH_A // GROUP_Q + 1
N_VAR_B = REACH_B // GROUP_Q + 1


def _vmem_limit(estimate_bytes):
    return int(min(max(estimate_bytes, 32 * 1024 * 1024), V7X_VMEM_LIMIT_CAP))


def _rms(x, gain):
    ms = jnp.mean(x * x, axis=-1, keepdims=True)
    return x * lax.rsqrt(ms + EPS) * gain


def _adaln(x, gain, shift, scale):
    return _rms(x, gain) * (1.0 + scale) + shift


def _silu(x):
    return x * (1.0 / (1.0 + jnp.exp(-x)))


def _mod_kernel(c_ref, w_ref, b_ref, o_ref):
    s = _silu(c_ref[...]).astype(BF16)
    o_ref[0] = jnp.dot(s, w_ref[...].astype(BF16), preferred_element_type=F32) + b_ref[0]


def _modulation(c_rows, w_mod, b_mod):
    rows = c_rows.shape[0]
    return pl.pallas_call(
        _mod_kernel,
        grid=(N_MOD,),
        in_specs=[
            pl.BlockSpec((rows, D_MODEL), lambda j: (0, 0)),
            pl.BlockSpec((D_MODEL, D_MODEL), lambda j: (0, j)),
            pl.BlockSpec((1, 1, D_MODEL), lambda j: (j, 0, 0)),
        ],
        out_specs=pl.BlockSpec((1, rows, D_MODEL), lambda j: (j, 0, 0)),
        out_shape=jax.ShapeDtypeStruct((N_MOD, rows, D_MODEL), F32),
        name="modulation",
    )(c_rows, w_mod, b_mod.reshape(N_MOD, 1, D_MODEL))


def _skew(base_row, n_rows):
    width = base_row.shape[1]
    m = jnp.broadcast_to(base_row, (n_rows, width))
    row = lax.broadcasted_iota(jnp.int32, (n_rows, width), 0)
    shift = 1
    while shift < n_rows:
        m = jnp.where((row & shift) != 0, pltpu.roll(m, shift, axis=1), m)
        shift *= 2
    return m


def _band_mask(n_rows, width, origin, n_prev):
    row = lax.broadcasted_iota(jnp.int32, (n_rows, width), 0)
    col = lax.broadcasted_iota(jnp.int32, (n_rows, width), 1)
    q_chunk = row >> 6
    k_chunk = (col >> 6) - (origin // CHUNK)
    return (k_chunk >= q_chunk - n_prev) & (k_chunk <= q_chunk)


def _t5_bucket(rel):
    half = N_BUCKETS // 2
    max_exact = half // 2
    n = jnp.abs(rel)
    n2 = n * n
    large = jnp.full(rel.shape, max_exact, jnp.int32)
    for k in range(1, half - max_exact):
        large = large + jnp.where(n2 >= (max_exact * max_exact) * (2 ** k), 1, 0)
    return jnp.where(rel > 0, half, 0) + jnp.where(n < max_exact, n, large)


def _bias_kernel(rel_ref, t5_ref, *refs, s_len, la, lb):
    n_cast = (len(refs) - 4) // 2
    for src, dst in zip(refs[:n_cast], refs[n_cast + 4:]):
        dst[...] = src[...].astype(BF16)
    ba_ref, bb_ref, bsa_ref, bsb_ref = refs[n_cast:n_cast + 4]

    tab = rel_ref[0]
    n_tab = 2 * REL_CLIP_A + 1
    first, last = tab[:, 0:1], tab[:, n_tab - 1:n_tab]
    win_lo = ORIGIN_A - 2 * REL_CLIP_A
    col = lax.broadcasted_iota(jnp.int32, (1, tab.shape[1] + V7X_LANES), 1)
    shifted = jnp.concatenate([jnp.zeros((1, V7X_LANES), F32), tab], axis=1)
    u_win = jnp.where(col < V7X_LANES, first, jnp.where(col >= V7X_LANES + n_tab, last, shifted))
    window = _skew(u_win, GROUP_Q)[:, V7X_LANES:]
    n_left = win_lo + V7X_LANES
    n_right = WIDTH_MASTER_A - n_left - window.shape[1]
    skew_a = jnp.concatenate([jnp.broadcast_to(first, (GROUP_Q, n_left)), window,
                              jnp.broadcast_to(last, (GROUP_Q, n_right))], axis=1)
    master_a = jnp.where(_band_mask(GROUP_Q, WIDTH_MASTER_A, ORIGIN_A, REACH_A // CHUNK), skew_a, NEG_INF)
    for v in range(N_VAR_A):
        start = ORIGIN_A - min(v * GROUP_Q, REACH_A) if v < N_VAR_A - 1 else ORIGIN_A - REACH_A
        ba_ref[v, 0] = master_a[:, start:start + BAND_A]
    bsa_ref[0] = skew_a[:s_len, ORIGIN_A - la:ORIGIN_A + s_len]

    t5 = t5_ref[0]
    rel = lax.broadcasted_iota(jnp.int32, (1, WIDTH_MASTER_B), 1) - ORIGIN_B
    bucket = _t5_bucket(rel)
    u_b = jnp.zeros((1, WIDTH_MASTER_B), F32)
    for i in range(N_BUCKETS):
        u_b = jnp.where(bucket == i, t5[:, i:i + 1], u_b)
    skew_b = _skew(u_b, GROUP_Q)
    master_b = jnp.where(_band_mask(GROUP_Q, WIDTH_MASTER_B, ORIGIN_B, REACH_B // CHUNK), skew_b, NEG_INF)
    for v in range(N_VAR_B):
        start = ORIGIN_B - min(v * GROUP_Q, REACH_B) if v < N_VAR_B - 1 else ORIGIN_B - REACH_B
        bb_ref[v, 0] = master_b[:, start:start + BAND_B]
    bsb_ref[0] = skew_b[:s_len, ORIGIN_B - lb:ORIGIN_B + s_len]


V7X_BF16_SUBLANES = 16


def _cast_chunks(w, n_steps, step_of):
    rows, cols = w.shape
    n_chunks = n_steps
    while rows % (n_chunks * V7X_BF16_SUBLANES):
        n_chunks //= 2
    spec = pl.BlockSpec((rows // n_chunks, cols), lambda *g: (jnp.minimum(step_of(*g), n_chunks - 1), 0))
    return spec, jax.ShapeDtypeStruct(w.shape, BF16)


def _bias_tables(rel_bias_a, t5_table, s_len, la, lb, cast=()):
    n_tab = 2 * REL_CLIP_A + 1
    tab_w = 3 * V7X_LANES
    rel_p = jnp.pad(rel_bias_a, ((0, 0), (0, tab_w - n_tab))).reshape(N_HEADS, 1, tab_w)
    t5_p = t5_table.reshape(N_HEADS, 1, N_BUCKETS)
    kern = functools.partial(_bias_kernel, s_len=s_len, la=la, lb=lb)
    cast_specs, cast_shapes = zip(*[_cast_chunks(w, N_HEADS, lambda h: h) for w in cast]) if cast else ((), ())
    return pl.pallas_call(
        kern,
        grid=(N_HEADS,),
        in_specs=[
            pl.BlockSpec((1, 1, tab_w), lambda h: (h, 0, 0)),
            pl.BlockSpec((1, 1, N_BUCKETS), lambda h: (h, 0, 0)),
            *cast_specs,
        ],
        out_specs=[
            pl.BlockSpec((N_VAR_A, 1, GROUP_Q, BAND_A), lambda h: (0, h, 0, 0)),
            pl.BlockSpec((N_VAR_B, 1, GROUP_Q, BAND_B), lambda h: (0, h, 0, 0)),
            pl.BlockSpec((1, s_len, la + s_len), lambda h: (h, 0, 0)),
            pl.BlockSpec((1, s_len, lb + s_len), lambda h: (h, 0, 0)),
            *cast_specs,
        ],
        out_shape=[
            jax.ShapeDtypeStruct((N_VAR_A, N_HEADS, GROUP_Q, BAND_A), F32),
            jax.ShapeDtypeStruct((N_VAR_B, N_HEADS, GROUP_Q, BAND_B), F32),
            jax.ShapeDtypeStruct((N_HEADS, s_len, la + s_len), F32),
            jax.ShapeDtypeStruct((N_HEADS, s_len, lb + s_len), F32),
            *cast_shapes,
        ],
        compiler_params=pltpu.CompilerParams(dimension_semantics=("arbitrary",)),
        name="bias_tables",
    )(rel_p, t5_p, *cast)


V7X_MXU_DIM = 256
FF_SLABS = ((0, 6 * V7X_MXU_DIM), (6 * V7X_MXU_DIM, D_FF))
FF_SLAB_MAX = max(hi - lo for lo, hi in FF_SLABS)


def _swiglu(h, wg_ref, wu_ref, wd_ref):
    acts = []
    for lo, hi in FF_SLABS:
        g = jnp.dot(h, wg_ref[:, lo:hi], preferred_element_type=F32)
        u = jnp.dot(h, wu_ref[:, lo:hi], preferred_element_type=F32)
        acts.append((_silu(g) * u).astype(BF16))
    acc = None
    for (lo, hi), a in zip(FF_SLABS, acts):
        part = jnp.dot(a, wd_ref[lo:hi, :], preferred_element_type=F32)
        acc = part if acc is None else acc + part
    return acc


N_SUB = 2


def _sub_tiles(gt, rt):
    if gt == 1:
        step = rt // N_SUB
        return [(slice(0, 1), slice(s * step, (s + 1) * step), s * step, step) for s in range(N_SUB)]
    step = gt // N_SUB
    return [(slice(s * step, (s + 1) * step), slice(0, rt), s * step * rt, step * rt) for s in range(N_SUB)]


def _ffn1_qkv_kernel(x_ref, mod_ref, gains_ref, wg_ref, wu_ref, wd_ref, win_ref,
                     x1_ref, qa_ref, ka_ref, va_ref, qb_ref, kb_ref, vb_ref,
                     ka32_ref, va32_ref, kb32_ref, vb32_ref, *tail_scratch):
    gt, rt, _ = x_ref.shape
    tm = gt * rt
    gain = lambda k: gains_ref[k:k + 1, :]

    def store_tail(ref, val, r0):
        first = tm - ref.shape[0]
        lo = max(r0, first)
        if lo < r0 + val.shape[0]:
            ref[lo - first:r0 + val.shape[0] - first, :] = val[lo - r0:, :]

    def store_heads(ref, val, r0):
        n_heads = val.shape[1] // HEAD_DIM
        for h in range(n_heads):
            ref[pl.ds(r0 * n_heads + h, val.shape[0], stride=n_heads), :] = val[:, h * HEAD_DIM:(h + 1) * HEAD_DIM]

    def store_split(ref, rows, q, pairs):
        q = q * SCALE
        head = lambda h: q[:, h * HEAD_DIM:(h + 1) * HEAD_DIM]
        zeros = jnp.zeros((q.shape[0], HEAD_DIM), F32)
        ref[rows, :WIDTH] = jnp.concatenate([t for a, _ in pairs for t in (head(a), zeros)], axis=1).astype(BF16)
        ref[rows, WIDTH:] = jnp.concatenate([t for _, b in pairs for t in (zeros, head(b))], axis=1).astype(BF16)

    pairs_a = [(2 * p, 2 * p + 1) for p in range(N_PAIRS)]
    pairs_b = [(p, N_PAIRS + p) for p in range(N_PAIRS)]

    subs = _sub_tiles(gt, rt)
    mods = [lambda k, gs=gs: mod_ref[k, gs] for gs, _, _, _ in subs]
    xs = [x_ref[gs, rs, :] for gs, rs, _, _ in subs]
    hs = [_adaln(x, gain(0), mod(0), mod(1)).reshape(n, D_MODEL).astype(BF16)
          for x, mod, (_, _, _, n) in zip(xs, mods, subs)]
    fs = [_swiglu(h, wg_ref, wu_ref, wd_ref).reshape(x.shape) for h, x in zip(hs, xs)]
    h2s = []
    for x, f, mod, (gs, rs, _, n) in zip(xs, fs, mods, subs):
        x1 = x + (FFN_RES * mod(2)) * _rms(f, gain(1))
        x1_ref[gs, rs, :] = x1
        h2s.append(_adaln(x1, gain(2), mod(3), mod(4)).reshape(n, D_MODEL).astype(BF16))

    for h2, (_, _, r0, n) in zip(h2s, subs):
        rows = slice(r0, r0 + n)

        def proj(lo, width):
            return jnp.dot(h2, win_ref[:, lo:lo + width], preferred_element_type=F32)

        store_split(qa_ref, rows, proj(0, WIDTH), pairs_a)
        ka = proj(WIDTH, WIDTH)
        va = proj(2 * WIDTH, WIDTH)
        store_split(qb_ref, rows, proj(3 * WIDTH, WIDTH), pairs_b)
        kvb = proj(4 * WIDTH, 2 * KV_WIDTH_B)
        kb, vb = kvb[:, :KV_WIDTH_B], kvb[:, KV_WIDTH_B:]
        ka_ref[rows, :] = ka.astype(BF16)
        va_ref[rows, :] = va.astype(BF16)
        kb_ref[rows, :] = kb.astype(BF16)
        vb_ref[rows, :] = vb.astype(BF16)
        if tail_scratch:
            for ref, val in zip(tail_scratch, (ka, va, kb, vb)):
                store_tail(ref, val, r0)
        else:
            store_heads(ka32_ref, ka, r0)
            store_heads(va32_ref, va, r0)
            store_tail(kb32_ref, kb, r0)
            store_tail(vb32_ref, vb, r0)

    if tail_scratch:
        @pl.when(pl.program_id(1) == pl.num_programs(1) - 1)
        def _():
            for out_ref, ref in zip((ka32_ref, va32_ref, kb32_ref, vb32_ref), tail_scratch):
                out_ref[...] = ref[...].T


def _resident(shape):
    return pl.BlockSpec(shape, lambda *_: (0,) * len(shape), pipeline_mode=pl.Buffered(1))


def _ffn1_qkv(x, mod, gains, wg, wu, wd, win, gt, rt):
    n_groups, rows, _ = x.shape
    n_g, n_r = n_groups // gt, rows // rt
    tm = gt * rt
    tokens = n_groups * rows
    ta = gt * min(REACH_A, rows)
    tb = gt * min(REACH_B, rows)
    tile = lambda gi, ri: (gi * n_r + ri, 0)
    tail = lambda gi, ri: (gi, 0)
    tok = lambda w, dt: jax.ShapeDtypeStruct((tokens, w), dt)
    if gt == 1:
        assert ta <= tm and tb <= tm
        tail_specs = [pl.BlockSpec((None, w, t), lambda gi, ri: (gi, 0, 0))
                      for w, t in ((WIDTH, ta), (WIDTH, ta), (KV_WIDTH_B, tb), (KV_WIDTH_B, tb))]
        tail_shapes = [jax.ShapeDtypeStruct((n_g, w, t), F32)
                       for w, t in ((WIDTH, ta), (WIDTH, ta), (KV_WIDTH_B, tb), (KV_WIDTH_B, tb))]
        scratch = [pltpu.VMEM((ta, WIDTH), F32), pltpu.VMEM((ta, WIDTH), F32),
                   pltpu.VMEM((tb, KV_WIDTH_B), F32), pltpu.VMEM((tb, KV_WIDTH_B), F32)]
    else:
        assert n_r == 1
        tail_specs = [pl.BlockSpec((tm * N_HEADS, HEAD_DIM), tail), pl.BlockSpec((tm * N_HEADS, HEAD_DIM), tail),
                      pl.BlockSpec((tm, KV_WIDTH_B), tail), pl.BlockSpec((tm, KV_WIDTH_B), tail)]
        tail_shapes = [jax.ShapeDtypeStruct((tokens * N_HEADS, HEAD_DIM), F32),
                       jax.ShapeDtypeStruct((tokens * N_HEADS, HEAD_DIM), F32),
                       jax.ShapeDtypeStruct((tokens, KV_WIDTH_B), F32), jax.ShapeDtypeStruct((tokens, KV_WIDTH_B), F32)]
        scratch = []
    vmem = (2 * 3 * D_MODEL * D_FF + 2 * D_MODEL * win.shape[1]
            + 2 * 2 * 4 * tm * D_MODEL
            + 2 * 2 * tm * (6 * WIDTH + 2 * KV_WIDTH_B)
            + 2 * 4 * (2 * ta * WIDTH + 2 * tb * KV_WIDTH_B)
            + tm * (2 * 4 + 2) * FF_SLAB_MAX + 3 * 4 * tm * D_MODEL
            + 4 * tm * win.shape[1])
    return pl.pallas_call(
        _ffn1_qkv_kernel,
        grid=(n_g, n_r),
        in_specs=[
            pl.BlockSpec((gt, rt, D_MODEL), lambda gi, ri: (gi, ri, 0)),
            pl.BlockSpec((N_MOD, gt, 1, D_MODEL), lambda gi, ri: (0, gi, 0, 0)),
            _resident(gains.shape),
            _resident(wg.shape), _resident(wu.shape), _resident(wd.shape), _resident(win.shape),
        ],
        out_specs=[
            pl.BlockSpec((gt, rt, D_MODEL), lambda gi, ri: (gi, ri, 0)),
            pl.BlockSpec((tm, 2 * WIDTH), tile), pl.BlockSpec((tm, WIDTH), tile), pl.BlockSpec((tm, WIDTH), tile),
            pl.BlockSpec((tm, 2 * WIDTH), tile), pl.BlockSpec((tm, KV_WIDTH_B), tile),
            pl.BlockSpec((tm, KV_WIDTH_B), tile),
            *tail_specs,
        ],
        out_shape=[
            jax.ShapeDtypeStruct(x.shape, F32),
            tok(2 * WIDTH, BF16), tok(WIDTH, BF16), tok(WIDTH, BF16),
            tok(2 * WIDTH, BF16), tok(KV_WIDTH_B, BF16), tok(KV_WIDTH_B, BF16),
            *tail_shapes,
        ],
        scratch_shapes=scratch,
        compiler_params=pltpu.CompilerParams(
            dimension_semantics=("arbitrary", "arbitrary"), vmem_limit_bytes=_vmem_limit(vmem)),
        name="ffn1_qkv",
    )(x, mod, gains, wg, wu, wd, win)


def _out_ffn2_kernel(x_ref, o_ref, mod_ref, gains_ref, gg_ref, wout_ref, wg_ref, wu_ref, wd_ref, y_ref):
    gt, rt, _ = x_ref.shape
    gain = lambda k: gains_ref[k:k + 1, :]
    gg = gg_ref[...]
    subs = _sub_tiles(gt, rt)
    mods = [lambda k, gs=gs: mod_ref[k, gs] for gs, _, _, _ in subs]
    xs = [x_ref[gs, rs, :] for gs, rs, _, _ in subs]
    ons = []
    for _, _, r0, n in subs:
        o = o_ref[r0:r0 + n, :].astype(F32)
        on = jnp.concatenate([_rms(o[:, :WIDTH], gg[:, :WIDTH]), _rms(o[:, WIDTH:], gg[:, WIDTH:])], axis=1)
        ons.append(on.astype(BF16))
    mixed = [jnp.dot(on, wout_ref[...], preferred_element_type=F32).reshape(x.shape) for on, x in zip(ons, xs)]
    x2s = [x + mod(5) * _rms(m, gain(3)) for x, m, mod in zip(xs, mixed, mods)]
    hs = [_adaln(x2, gain(4), mod(6), mod(7)).reshape(n, D_MODEL).astype(BF16)
          for x2, mod, (_, _, _, n) in zip(x2s, mods, subs)]
    fs = [_swiglu(h, wg_ref, wu_ref, wd_ref).reshape(x.shape) for h, x in zip(hs, xs)]
    for x2, f, mod, (gs, rs, _, _) in zip(x2s, fs, mods, subs):
        y_ref[gs, rs, :] = x2 + (FFN_RES * mod(8)) * _rms(f, gain(5))


def _out_ffn2(x1, o, mod, gains, gg, wout, wg, wu, wd, gt, rt):
    n_groups, rows, _ = x1.shape
    n_g, n_r = n_groups // gt, rows // rt
    tm = gt * rt
    vmem = (2 * 3 * D_MODEL * D_FF + 2 * wout.size
            + 2 * 2 * 4 * tm * D_MODEL + 2 * 2 * tm * 2 * WIDTH
            + tm * (2 * 4 + 2) * FF_SLAB_MAX + 5 * 4 * tm * D_MODEL)
    return pl.pallas_call(
        _out_ffn2_kernel,
        grid=(n_g, n_r),
        in_specs=[
            pl.BlockSpec((gt, rt, D_MODEL), lambda gi, ri: (gi, ri, 0)),
            pl.BlockSpec((tm, 2 * WIDTH), lambda gi, ri: (gi * n_r + ri, 0)),
            pl.BlockSpec((N_MOD, gt, 1, D_MODEL), lambda gi, ri: (0, gi, 0, 0)),
            _resident(gains.shape), _resident(gg.shape), _resident(wout.shape),
            _resident(wg.shape), _resident(wu.shape), _resident(wd.shape),
        ],
        out_specs=pl.BlockSpec((gt, rt, D_MODEL), lambda gi, ri: (gi, ri, 0)),
        out_shape=jax.ShapeDtypeStruct(x1.shape, F32),
        compiler_params=pltpu.CompilerParams(
            dimension_semantics=("arbitrary", "arbitrary"), vmem_limit_bytes=_vmem_limit(vmem)),
        name="out_ffn2",
    )(x1, o, mod, gains, gg, wout, wg, wu, wd)


def _softmax_terms(s, sink=None):
    m = jnp.max(s, axis=1, keepdims=True)
    if sink is not None:
        m = jnp.maximum(m, sink)
    e = jnp.exp(s - m)
    denom = jnp.sum(e, axis=1, keepdims=True)
    if sink is not None:
        denom = denom + jnp.exp(sink - m)
    return e.astype(BF16), 1.0 / denom


def _scores(q, k):
    return lax.dot_general(q, k, (((1,), (1,)), ((), ())), preferred_element_type=F32)


def _unstack_pair(z):
    n_q = z.shape[0] // 2
    is_lo = lax.broadcasted_iota(jnp.int32, (n_q, z.shape[1]), 1) < HEAD_DIM
    return jnp.where(is_lo, z[:n_q], z[n_q:])


def _sink_column(sinks_ref, pair, n_q):
    row = lax.broadcasted_iota(jnp.int32, (2 * n_q, 1), 0)
    return jnp.where(row < n_q, sinks_ref[pair], sinks_ref[N_PAIRS + pair])


GROUPS_PER_STEP = 2


def _attn_prompt_kernel(sinks_ref, qa_ref, qb_ref, ka_ref, va_ref, kb_ref, vb_ref, *rest):
    ba_refs, bb_refs = rest[:GROUPS_PER_STEP], rest[GROUPS_PER_STEP:2 * GROUPS_PER_STEP]
    n_cast = (len(rest) - 2 * GROUPS_PER_STEP - 1) // 2
    o_ref = rest[2 * GROUPS_PER_STEP + n_cast]
    for src, dst in zip(rest[2 * GROUPS_PER_STEP:2 * GROUPS_PER_STEP + n_cast], rest[len(rest) - n_cast:]):
        dst[...] = src[...].astype(BF16)
    slab = lambda p: slice(p * V7X_LANES, (p + 1) * V7X_LANES)
    bands = []
    for j in range(GROUPS_PER_STEP):
        g = pl.program_id(1) * GROUPS_PER_STEP + j
        start_a = pl.multiple_of(jnp.maximum(g * GROUP_Q - REACH_A, 0), GROUP_Q)
        start_b = pl.multiple_of(jnp.maximum(g * GROUP_Q - REACH_B, 0), GROUP_Q)
        bands.append((pl.ds(start_a, BAND_A), pl.ds(start_b, BAND_B)))

    def stacked(q_ref, j, p):
        rows = slice(j * GROUP_Q, (j + 1) * GROUP_Q)
        return jnp.concatenate([q_ref[rows, slab(p)], q_ref[rows, slab(N_PAIRS + p)]], axis=0)

    def score(mixer, j, p):
        if mixer == "a":
            bias = jnp.concatenate([ba_refs[j][0, 2 * p], ba_refs[j][0, 2 * p + 1]], axis=0)
            return _scores(stacked(qa_ref, j, p), ka_ref[bands[j][0], slab(p)]) + bias
        bias = jnp.concatenate([bb_refs[j][0, p], bb_refs[j][0, N_PAIRS + p]], axis=0)
        return _scores(stacked(qb_ref, j, p), kb_ref[bands[j][1], :]) + bias

    def weights(s, sink):
        m = jnp.max(s, axis=1, keepdims=True)
        if sink is None:
            return jnp.exp(s - m).astype(BF16), None
        m = jnp.maximum(m, sink)
        return jnp.exp(s - m).astype(BF16), jnp.exp(sink - m)

    def attend(e, extra, v):
        z = jnp.dot(e, jnp.concatenate([v, jnp.ones_like(v)], axis=1), preferred_element_type=F32)
        denom = z[:, V7X_LANES:] if extra is None else z[:, V7X_LANES:] + extra
        return _unstack_pair(z[:, :V7X_LANES] * (1.0 / denom))

    jobs = [(mixer, j, p) for mixer in ("a", "b") for j in range(GROUPS_PER_STEP) for p in range(N_PAIRS)]
    s_all = [score(*job) for job in jobs]
    w_all = [weights(s, _sink_column(sinks_ref, p, GROUP_Q) if mixer == "b" else None)
             for s, (mixer, j, p) in zip(s_all, jobs)]
    for (e, extra), (mixer, j, p) in zip(w_all, jobs):
        rows = slice(j * GROUP_Q, (j + 1) * GROUP_Q)
        if mixer == "a":
            o_ref[rows, slab(p)] = attend(e, extra, va_ref[bands[j][0], slab(p)]).astype(BF16)
        else:
            o_ref[rows, slab(N_PAIRS + p)] = attend(e, extra, vb_ref[bands[j][1], :]).astype(BF16)


def _attn_prompt(sinks, qa, qb, ka, va, kb, vb, bias_a, bias_b, batch, seq, cast=()):
    rows = GROUPS_PER_STEP * GROUP_Q
    n_steps = seq // rows
    cast_specs, cast_shapes = (zip(*[_cast_chunks(w, batch * n_steps, lambda b, s: b * n_steps + s) for w in cast])
                               if cast else ((), ()))
    q_spec = pl.BlockSpec((rows, 2 * WIDTH), lambda b, s: (b * n_steps + s, 0))
    kv_spec = lambda w: pl.BlockSpec((seq, w), lambda b, s: (b, 0), pipeline_mode=pl.Buffered(1))
    bias_spec = lambda band, n_var, j: pl.BlockSpec(
        (1, N_HEADS, GROUP_Q, band),
        lambda b, s: (jnp.minimum(s * GROUPS_PER_STEP + j, n_var - 1), 0, 0, 0))
    vmem = (2 * seq * (2 * WIDTH + 2 * KV_WIDTH_B)
            + GROUPS_PER_STEP * 2 * 4 * N_HEADS * GROUP_Q * (BAND_A + BAND_B)
            + 2 * 2 * rows * 6 * WIDTH + GROUPS_PER_STEP * 6 * 4 * 2 * GROUP_Q * BAND_A
            + sum(2 * 6 * math.prod(spec.block_shape) for spec in cast_specs))
    out = pl.pallas_call(
        _attn_prompt_kernel,
        grid=(batch, n_steps),
        in_specs=[
            pl.BlockSpec(memory_space=pltpu.SMEM),
            q_spec, q_spec,
            kv_spec(WIDTH), kv_spec(WIDTH), kv_spec(KV_WIDTH_B), kv_spec(KV_WIDTH_B),
            *[bias_spec(BAND_A, N_VAR_A, j) for j in range(GROUPS_PER_STEP)],
            *[bias_spec(BAND_B, N_VAR_B, j) for j in range(GROUPS_PER_STEP)],
            *cast_specs,
        ],
        out_specs=[pl.BlockSpec((rows, 2 * WIDTH), lambda b, s: (b * n_steps + s, 0)), *cast_specs],
        out_shape=[jax.ShapeDtypeStruct((batch * seq, 2 * WIDTH), BF16), *cast_shapes],
        compiler_params=pltpu.CompilerParams(
            dimension_semantics=("arbitrary", "arbitrary"), vmem_limit_bytes=_vmem_limit(vmem)),
        name="attn_prompt",
    )(sinks, qa, qb, ka, va, kb, vb, *([bias_a] * GROUPS_PER_STEP), *([bias_b] * GROUPS_PER_STEP), *cast)
    return out[0], tuple(out[1:])


def _pair_ordered_cols(h):
    return (h % N_PAIRS) * V7X_LANES + (h // N_PAIRS) * HEAD_DIM


SAMPLE_ROWS_PER_STEP = 4


def _attn_sample_kernel(sinks_ref, qa_ref, qb_ref, ka_ref, va_ref, kb_ref, vb_ref,
                        cak_ref, cav_ref, cbk_ref, cbv_ref, ba_ref, bb_ref, o_ref):
    n_rows = cak_ref.shape[0]
    n_q = qa_ref.shape[0] // n_rows
    n_kv = cbk_ref.shape[1]
    per_kv = N_HEADS // n_kv
    la, lb = cak_ref.shape[3], cbk_ref.shape[3]
    head = lambda ref, r, idx: ref[r * n_q:(r + 1) * n_q, idx * HEAD_DIM:(idx + 1) * HEAD_DIM]

    def query(ref, r, lo):
        lo = lo if lo % V7X_LANES < HEAD_DIM else WIDTH + lo
        return ref[r * n_q:(r + 1) * n_q, lo:lo + HEAD_DIM]

    def scores(q, cache_t_ref, new_ref, r, idx):
        old = jnp.dot(q, cache_t_ref[r, idx].astype(BF16), preferred_element_type=F32)
        return jnp.concatenate([old, _scores(q, head(new_ref, r, idx))], axis=1)

    def values(e, cache_t_ref, new_ref, r, idx, n_old):
        old = _scores(e[:, :n_old], cache_t_ref[r, idx].astype(BF16))
        return old + jnp.dot(e[:, n_old:], head(new_ref, r, idx), preferred_element_type=F32)

    jobs_a = [(r, h) for r in range(n_rows) for h in range(N_HEADS)]
    jobs_b = [(r, c) for r in range(n_rows) for c in range(n_kv)]
    s_a = [scores(query(qa_ref, r, h * HEAD_DIM), cak_ref, ka_ref, r, h) + ba_ref[h] for r, h in jobs_a]
    s_b = []
    for r, c in jobs_b:
        heads = range(c * per_kv, (c + 1) * per_kv)
        q = jnp.concatenate([query(qb_ref, r, _pair_ordered_cols(h)) for h in heads], axis=0)
        bias = jnp.concatenate([bb_ref[h] for h in heads], axis=0)
        s_b.append(scores(q, cbk_ref, kb_ref, r, c) + bias)

    row = lax.broadcasted_iota(jnp.int32, (per_kv * n_q, 1), 0)
    sinks = []
    for c in range(n_kv):
        sink = jnp.zeros((per_kv * n_q, 1), F32)
        for i in range(per_kv):
            sink = jnp.where((row >= i * n_q) & (row < (i + 1) * n_q), sinks_ref[c * per_kv + i], sink)
        sinks.append(sink)
    p_a = [_softmax_terms(s) for s in s_a]
    p_b = [_softmax_terms(s, sinks[c]) for s, (_, c) in zip(s_b, jobs_b)]

    out_a = {job: values(e, cav_ref, va_ref, *job, la) * inv for job, (e, inv) in zip(jobs_a, p_a)}
    out_b = {}
    for (r, c), (e, inv) in zip(jobs_b, p_b):
        o = values(e, cbv_ref, vb_ref, r, c, lb) * inv
        for i in range(per_kv):
            out_b[(r, c * per_kv + i)] = o[i * n_q:(i + 1) * n_q]
    order_b = [h for p in range(N_PAIRS) for h in (p, N_PAIRS + p)]
    for r in range(n_rows):
        pieces = [out_a[(r, h)] for h in range(N_HEADS)] + [out_b[(r, h)] for h in order_b]
        o_ref[r * n_q:(r + 1) * n_q, :] = jnp.concatenate(pieces, axis=1).astype(BF16)


def _attn_sample(sinks, qa, qb, ka, va, kb, vb, cak, cav, cbk, cbv, bias_a, bias_b, batch, s_len):
    n_rows = SAMPLE_ROWS_PER_STEP
    row = lambda w: pl.BlockSpec((n_rows * s_len, w), lambda b: (b, 0))
    cache = lambda a: pl.BlockSpec((n_rows,) + a.shape[1:], lambda b: (b, 0, 0, 0))
    whole = lambda a: pl.BlockSpec(a.shape, lambda b: (0,) * a.ndim)
    return pl.pallas_call(
        _attn_sample_kernel,
        grid=(batch // n_rows,),
        in_specs=[
            pl.BlockSpec(memory_space=pltpu.SMEM),
            row(2 * WIDTH), row(2 * WIDTH), row(WIDTH), row(WIDTH), row(KV_WIDTH_B), row(KV_WIDTH_B),
            cache(cak), cache(cav), cache(cbk), cache(cbv),
            whole(bias_a), whole(bias_b),
        ],
        out_specs=pl.BlockSpec((n_rows * s_len, 2 * WIDTH), lambda b: (b, 0)),
        out_shape=jax.ShapeDtypeStruct((batch * s_len, 2 * WIDTH), BF16),
        compiler_params=pltpu.CompilerParams(dimension_semantics=("arbitrary",)),
        name="attn_sample",
    )(sinks, qa, qb, ka, va, kb, vb, cak, cav, cbk, cbv, bias_a, bias_b)


def _pair_order(x, axis):
    shape = x.shape
    split = shape[:axis] + (N_HEADS // N_PAIRS, N_PAIRS, HEAD_DIM) + shape[axis + 1:]
    return jnp.swapaxes(x.reshape(split), axis, axis + 1).reshape(shape)


def _prep_out_weights(w_out, group_gains):
    wout = jnp.concatenate([w_out[:WIDTH], _pair_order(w_out[WIDTH:], 0)], axis=0).astype(BF16)
    gg = jnp.concatenate([group_gains[:WIDTH], _pair_order(group_gains[WIDTH:], 0)])
    return wout, gg.reshape(1, 2 * WIDTH)


def kernel(x_prompt, x_sample, cache_a_k, cache_a_v, cache_b_k, cache_b_v, c_prompt, c_sample, w_mod, b_mod,
           norm_gains, w1_gate, w1_up, w1_down, w_in, w_out, group_gains, rel_bias_a, t5_bias_table, sinks_b,
           w2_gate, w2_up, w2_down):
    depth = w_mod.shape[0]
    assert depth == 1
    batch, seq, _ = x_prompt.shape
    s_batch, s_len, _ = x_sample.shape
    la, lb = cache_a_k.shape[2], cache_b_k.shape[2]
    assert la == REACH_A and lb == REACH_B and seq % 512 == 0

    c_rows = jnp.concatenate([c_prompt, c_sample], axis=0)
    pad_rows = (-c_rows.shape[0]) % 8
    c_rows = jnp.pad(c_rows, ((0, pad_rows), (0, 0)))
    mod = _modulation(c_rows, w_mod[0], b_mod[0])
    mod_p = mod[:, :batch].reshape(N_MOD, batch, 1, D_MODEL)
    mod_s = mod[:, batch:batch + s_batch].reshape(N_MOD, s_batch, 1, D_MODEL)

    bias_a, bias_b, bias_sa, bias_sb, *w1, win = _bias_tables(
        rel_bias_a[0], t5_bias_table, s_len, la, lb, cast=(w1_gate[0], w1_up[0], w1_down[0], w_in[0]))

    gains = norm_gains[0]
    wout, gg = _prep_out_weights(w_out[0], group_gains[0])
    sinks = sinks_b[0]

    rt_p = 512
    (x1p, qa, ka, va, qb, kb, vb, ka32, va32, kb32, vb32) = _ffn1_qkv(
        x_prompt, mod_p, gains, *w1, win, 1, rt_p)
    op, w2 = _attn_prompt(sinks, qa, qb, ka, va, kb, vb, bias_a, bias_b, batch, seq,
                          cast=(w2_gate[0], w2_up[0], w2_down[0]))
    y_prompt = _out_ffn2(x1p, op, mod_p, gains, gg, wout, *w2, 1, rt_p)
    rows_last = lambda t: jnp.transpose(t.reshape(depth, batch, -1, HEAD_DIM, t.shape[-1]), (0, 1, 4, 2, 3))
    new_p = tuple(rows_last(t) for t in (ka32, va32, kb32, vb32))

    gt_s = 16
    (x1s, qas, kas, vas, qbs, kbs, vbs, kas32, vas32, kbs32, vbs32) = _ffn1_qkv(
        x_sample, mod_s, gains, *w1, win, gt_s, s_len)
    os_ = _attn_sample(sinks, qas, qbs, kas, vas, kbs, vbs,
                       *(jnp.transpose(c[0], (0, 2, 3, 1)) for c in (cache_a_k, cache_a_v, cache_b_k, cache_b_v)),
                       bias_sa, bias_sb, s_batch, s_len)
    y_sample = _out_ffn2(x1s, os_, mod_s, gains, gg, wout, *w2, gt_s, s_len)
    new_s = (kas32.reshape(depth, s_batch, s_len, N_HEADS, HEAD_DIM),
             vas32.reshape(depth, s_batch, s_len, N_HEADS, HEAD_DIM),
             kbs32.reshape(depth, s_batch, s_len, 2, HEAD_DIM), vbs32.reshape(depth, s_batch, s_len, 2, HEAD_DIM))

    return (y_prompt, y_sample) + new_p + new_s
```

```python
import functools
import math

import jax
import jax.numpy as jnp
from jax import lax
from jax.experimental import pallas as pl
from jax.experimental.pallas import tpu as pltpu

F32 = jnp.float32
BF16 = jnp.bfloat16

D_MODEL = 1024
D_FF = 2816
CHUNK = 64
HEAD_DIM = 64
N_HEADS = 8
N_PAIRS = N_HEADS // 2
WIDTH = N_HEADS * HEAD_DIM
KV_WIDTH_B = 2 * HEAD_DIM
REACH_A = 8 * CHUNK
REACH_B = 2 * CHUNK
REL_CLIP_A = 128
N_BUCKETS = 32
N_MOD = 9
FFN_RES = 0.5
EPS = 1e-6
SCALE = HEAD_DIM ** -0.5
NEG_INF = -1e30

V7X_LANES = 128
V7X_VMEM_BYTES = 64 * 1024 * 1024
V7X_VMEM_LIMIT_CAP = 60 * 1024 * 1024

GROUP_Q = 2 * CHUNK
BAND_A = REACH_A + GROUP_Q
BAND_B = REACH_B + GROUP_Q
ORIGIN_A = 1024
WIDTH_MASTER_A = ORIGIN_A + BAND_A
ORIGIN_B = 256
WIDTH_MASTER_B = ORIGIN_B + BAND_B
N_VAR_A = REACH_A // GROUP_Q + 1
N_VAR_B = REACH_B // GROUP_Q + 1


def _vmem_limit(estimate_bytes):
    return int(min(max(estimate_bytes, 32 * 1024 * 1024), V7X_VMEM_LIMIT_CAP))


def _rms(x, gain):
    ms = jnp.mean(x * x, axis=-1, keepdims=True)
    return x * lax.rsqrt(ms + EPS) * gain


def _adaln(x, gain, shift, scale):
    return _rms(x, gain) * (1.0 + scale) + shift


def _silu(x):
    return x * (1.0 / (1.0 + jnp.exp(-x)))


def _mod_kernel(c_ref, w_ref, b_ref, o_ref):
    s = _silu(c_ref[...]).astype(BF16)
    o_ref[0] = jnp.dot(s, w_ref[...].astype(BF16), preferred_element_type=F32) + b_ref[0]


def _modulation(c_rows, w_mod, b_mod):
    rows = c_rows.shape[0]
    return pl.pallas_call(
        _mod_kernel,
        grid=(N_MOD,),
        in_specs=[
            pl.BlockSpec((rows, D_MODEL), lambda j: (0, 0)),
            pl.BlockSpec((D_MODEL, D_MODEL), lambda j: (0, j)),
            pl.BlockSpec((1, 1, D_MODEL), lambda j: (j, 0, 0)),
        ],
        out_specs=pl.BlockSpec((1, rows, D_MODEL), lambda j: (j, 0, 0)),
        out_shape=jax.ShapeDtypeStruct((N_MOD, rows, D_MODEL), F32),
        name="modulation",
    )(c_rows, w_mod, b_mod.reshape(N_MOD, 1, D_MODEL))


def _skew(base_row, n_rows):
    width = base_row.shape[1]
    m = jnp.broadcast_to(base_row, (n_rows, width))
    row = lax.broadcasted_iota(jnp.int32, (n_rows, width), 0)
    shift = 1
    while shift < n_rows:
        m = jnp.where((row & shift) != 0, pltpu.roll(m, shift, axis=1), m)
        shift *= 2
    return m


def _band_mask(n_rows, width, origin, n_prev):
    row = lax.broadcasted_iota(jnp.int32, (n_rows, width), 0)
    col = lax.broadcasted_iota(jnp.int32, (n_rows, width), 1)
    q_chunk = row >> 6
    k_chunk = (col >> 6) - (origin // CHUNK)
    return (k_chunk >= q_chunk - n_prev) & (k_chunk <= q_chunk)


def _t5_bucket(rel):
    half = N_BUCKETS // 2
    max_exact = half // 2
    n = jnp.abs(rel)
    n2 = n * n
    large = jnp.full(rel.shape, max_exact, jnp.int32)
    for k in range(1, half - max_exact):
        large = large + jnp.where(n2 >= (max_exact * max_exact) * (2 ** k), 1, 0)
    return jnp.where(rel > 0, half, 0) + jnp.where(n < max_exact, n, large)


def _bias_kernel(rel_ref, t5_ref, *refs, s_len, la, lb):
    n_cast = (len(refs) - 4) // 2
    for src, dst in zip(refs[:n_cast], refs[n_cast + 4:]):
        dst[...] = src[...].astype(BF16)
    ba_ref, bb_ref, bsa_ref, bsb_ref = refs[n_cast:n_cast + 4]

    tab = rel_ref[0]
    n_tab = 2 * REL_CLIP_A + 1
    first, last = tab[:, 0:1], tab[:, n_tab - 1:n_tab]
    win_lo = ORIGIN_A - 2 * REL_CLIP_A
    col = lax.broadcasted_iota(jnp.int32, (1, tab.shape[1] + V7X_LANES), 1)
    shifted = jnp.concatenate([jnp.zeros((1, V7X_LANES), F32), tab], axis=1)
    u_win = jnp.where(col < V7X_LANES, first, jnp.where(col >= V7X_LANES + n_tab, last, shifted))
    window = _skew(u_win, GROUP_Q)[:, V7X_LANES:]
    n_left = win_lo + V7X_LANES
    n_right = WIDTH_MASTER_A - n_left - window.shape[1]
    skew_a = jnp.concatenate([jnp.broadcast_to(first, (GROUP_Q, n_left)), window,
                              jnp.broadcast_to(last, (GROUP_Q, n_right))], axis=1)
    master_a = jnp.where(_band_mask(GROUP_Q, WIDTH_MASTER_A, ORIGIN_A, REACH_A // CHUNK), skew_a, NEG_INF)
    for v in range(N_VAR_A):
        start = ORIGIN_A - min(v * GROUP_Q, REACH_A) if v < N_VAR_A - 1 else ORIGIN_A - REACH_A
        ba_ref[v, 0] = master_a[:, start:start + BAND_A]
    bsa_ref[0] = skew_a[:s_len, ORIGIN_A - la:ORIGIN_A + s_len]

    t5 = t5_ref[0]
    rel = lax.broadcasted_iota(jnp.int32, (1, WIDTH_MASTER_B), 1) - ORIGIN_B
    bucket = _t5_bucket(rel)
    u_b = jnp.zeros((1, WIDTH_MASTER_B), F32)
    for i in range(N_BUCKETS):
        u_b = jnp.where(bucket == i, t5[:, i:i + 1], u_b)
    skew_b = _skew(u_b, GROUP_Q)
    master_b = jnp.where(_band_mask(GROUP_Q, WIDTH_MASTER_B, ORIGIN_B, REACH_B // CHUNK), skew_b, NEG_INF)
    for v in range(N_VAR_B):
        start = ORIGIN_B - min(v * GROUP_Q, REACH_B) if v < N_VAR_B - 1 else ORIGIN_B - REACH_B
        bb_ref[v, 0] = master_b[:, start:start + BAND_B]
    bsb_ref[0] = skew_b[:s_len, ORIGIN_B - lb:ORIGIN_B + s_len]


V7X_BF16_SUBLANES = 16


def _cast_chunks(w, n_steps, step_of):
    rows, cols = w.shape
    n_chunks = n_steps
    while rows % (n_chunks * V7X_BF16_SUBLANES):
        n_chunks //= 2
    spec = pl.BlockSpec((rows // n_chunks, cols), lambda *g: (jnp.minimum(step_of(*g), n_chunks - 1), 0))
    return spec, jax.ShapeDtypeStruct(w.shape, BF16)


def _bias_tables(rel_bias_a, t5_table, s_len, la, lb, cast=()):
    n_tab = 2 * REL_CLIP_A + 1
    tab_w = 3 * V7X_LANES
    rel_p = jnp.pad(rel_bias_a, ((0, 0), (0, tab_w - n_tab))).reshape(N_HEADS, 1, tab_w)
    t5_p = t5_table.reshape(N_HEADS, 1, N_BUCKETS)
    kern = functools.partial(_bias_kernel, s_len=s_len, la=la, lb=lb)
    cast_specs, cast_shapes = zip(*[_cast_chunks(w, N_HEADS, lambda h: h) for w in cast]) if cast else ((), ())
    return pl.pallas_call(
        kern,
        grid=(N_HEADS,),
        in_specs=[
            pl.BlockSpec((1, 1, tab_w), lambda h: (h, 0, 0)),
            pl.BlockSpec((1, 1, N_BUCKETS), lambda h: (h, 0, 0)),
            *cast_specs,
        ],
        out_specs=[
            pl.BlockSpec((N_VAR_A, 1, GROUP_Q, BAND_A), lambda h: (0, h, 0, 0)),
            pl.BlockSpec((N_VAR_B, 1, GROUP_Q, BAND_B), lambda h: (0, h, 0, 0)),
            pl.BlockSpec((1, s_len, la + s_len), lambda h: (h, 0, 0)),
            pl.BlockSpec((1, s_len, lb + s_len), lambda h: (h, 0, 0)),
            *cast_specs,
        ],
        out_shape=[
            jax.ShapeDtypeStruct((N_VAR_A, N_HEADS, GROUP_Q, BAND_A), F32),
            jax.ShapeDtypeStruct((N_VAR_B, N_HEADS, GROUP_Q, BAND_B), F32),
            jax.ShapeDtypeStruct((N_HEADS, s_len, la + s_len), F32),
            jax.ShapeDtypeStruct((N_HEADS, s_len, lb + s_len), F32),
            *cast_shapes,
        ],
        compiler_params=pltpu.CompilerParams(dimension_semantics=("arbitrary",)),
        name="bias_tables",
    )(rel_p, t5_p, *cast)


V7X_MXU_DIM = 256
FF_SLABS = ((0, 6 * V7X_MXU_DIM), (6 * V7X_MXU_DIM, D_FF))
FF_SLAB_MAX = max(hi - lo for lo, hi in FF_SLABS)


def _swiglu(h, wg_ref, wu_ref, wd_ref):
    acts = []
    for lo, hi in FF_SLABS:
        g = jnp.dot(h, wg_ref[:, lo:hi], preferred_element_type=F32)
        u = jnp.dot(h, wu_ref[:, lo:hi], preferred_element_type=F32)
        acts.append((_silu(g) * u).astype(BF16))
    acc = None
    for (lo, hi), a in zip(FF_SLABS, acts):
        part = jnp.dot(a, wd_ref[lo:hi, :], preferred_element_type=F32)
        acc = part if acc is None else acc + part
    return acc


N_SUB = 2
FFN_GROUPS = 16
FFN_ROWS = 32
FFN_TILE = FFN_GROUPS * FFN_ROWS


def _sub_tiles():
    step = FFN_GROUPS // N_SUB
    return [(slice(s * step, (s + 1) * step), s * step * FFN_ROWS, step * FFN_ROWS) for s in range(N_SUB)]


def _tile_inputs(is_sample, p_ref, s_ref, modp_ref, mods_ref):
    tile = jnp.where(is_sample, s_ref[...], p_ref[0].reshape(s_ref.shape))

    def mod(k, gs):
        rows = mods_ref[k, gs]
        return jnp.where(is_sample, rows, jnp.broadcast_to(modp_ref[k], rows.shape))

    return tile, mod


def _ffn1_qkv_kernel(xp_ref, xs_ref, modp_ref, mods_ref, gains_ref, wg_ref, wu_ref, wd_ref, win_ref,
                     x1_ref, qa_ref, ka_ref, va_ref, qb_ref, kb_ref, vb_ref,
                     pka_ref, pva_ref, pkb_ref, pvb_ref, ska_ref, sva_ref, skb_ref, svb_ref,
                     *tail_scratch, n_prompt_steps, tiles_per_seq):
    step = pl.program_id(0)
    is_sample = step >= n_prompt_steps
    gain = lambda k: gains_ref[k:k + 1, :]
    x_tile, mod = _tile_inputs(is_sample, xp_ref, xs_ref, modp_ref, mods_ref)

    def store_split(ref, rows, q, pairs):
        q = q * SCALE
        head = lambda h: q[:, h * HEAD_DIM:(h + 1) * HEAD_DIM]
        zeros = jnp.zeros((q.shape[0], HEAD_DIM), F32)
        ref[rows, :WIDTH] = jnp.concatenate([t for a, _ in pairs for t in (head(a), zeros)], axis=1).astype(BF16)
        ref[rows, WIDTH:] = jnp.concatenate([t for _, b in pairs for t in (zeros, head(b))], axis=1).astype(BF16)

    pairs_a = [(2 * p, 2 * p + 1) for p in range(N_PAIRS)]
    pairs_b = [(p, N_PAIRS + p) for p in range(N_PAIRS)]

    subs = _sub_tiles()
    xs = [x_tile[gs] for gs, _, _ in subs]
    hs = [_adaln(x, gain(0), mod(0, gs), mod(1, gs)).reshape(n, D_MODEL).astype(BF16)
          for x, (gs, _, n) in zip(xs, subs)]
    fs = [_swiglu(h, wg_ref, wu_ref, wd_ref).reshape(x.shape) for h, x in zip(hs, xs)]
    h2s = []
    for x, f, (gs, _, n) in zip(xs, fs, subs):
        x1 = x + (FFN_RES * mod(2, gs)) * _rms(f, gain(1))
        x1_ref[gs] = x1
        h2s.append(_adaln(x1, gain(2), mod(3, gs), mod(4, gs)).reshape(n, D_MODEL).astype(BF16))

    for h2, (_, r0, n) in zip(h2s, subs):
        rows = slice(r0, r0 + n)

        def proj(lo, width):
            return jnp.dot(h2, win_ref[:, lo:lo + width], preferred_element_type=F32)

        store_split(qa_ref, rows, proj(0, WIDTH), pairs_a)
        ka = proj(WIDTH, WIDTH)
        va = proj(2 * WIDTH, WIDTH)
        store_split(qb_ref, rows, proj(3 * WIDTH, WIDTH), pairs_b)
        kvb = proj(4 * WIDTH, 2 * KV_WIDTH_B)
        kb, vb = kvb[:, :KV_WIDTH_B], kvb[:, KV_WIDTH_B:]
        ka_ref[rows, :] = ka.astype(BF16)
        va_ref[rows, :] = va.astype(BF16)
        kb_ref[rows, :] = kb.astype(BF16)
        vb_ref[rows, :] = vb.astype(BF16)
        for ref, val in zip(tail_scratch, (ka, va, kb, vb)):
            ref[rows, :] = val

    @pl.when(jnp.logical_and(step < n_prompt_steps, step % tiles_per_seq == tiles_per_seq - 1))
    def _():
        for out_ref, ref in zip((pka_ref, pva_ref, pkb_ref, pvb_ref), tail_scratch):
            n_last = out_ref.shape[1]
            out_ref[...] = ref[FFN_TILE - n_last:, :].T

    @pl.when(is_sample)
    def _():
        for out_ref, ref in ((ska_ref, tail_scratch[0]), (sva_ref, tail_scratch[1])):
            for h in range(N_HEADS):
                out_ref[pl.ds(h, FFN_TILE, stride=N_HEADS), :] = ref[:, h * HEAD_DIM:(h + 1) * HEAD_DIM]
        skb_ref[...] = tail_scratch[2][...]
        svb_ref[...] = tail_scratch[3][...]


def _resident(shape):
    return pl.BlockSpec(shape, lambda *_: (0,) * len(shape), pipeline_mode=pl.Buffered(1))


def _tile_specs(seq, prompt_step, sample_step):
    tiles_per_seq = seq // FFN_TILE
    return [
        pl.BlockSpec((1, FFN_TILE, D_MODEL), lambda i: (prompt_step(i) // tiles_per_seq, prompt_step(i) % tiles_per_seq, 0)),
        pl.BlockSpec((FFN_GROUPS, FFN_ROWS, D_MODEL), lambda i: (sample_step(i), 0, 0), pipeline_mode=pl.Buffered(1)),
        pl.BlockSpec((N_MOD, 1, 1, D_MODEL), lambda i: (0, prompt_step(i) // tiles_per_seq, 0, 0)),
        pl.BlockSpec((N_MOD, FFN_GROUPS, 1, D_MODEL), lambda i: (0, sample_step(i), 0, 0)),
    ]


def _ffn1_qkv(x_p, x_s, mod_p, mod_s, gains, wg, wu, wd, win):
    batch, seq, _ = x_p.shape
    s_batch, s_len, _ = x_s.shape
    assert seq % FFN_TILE == 0 and s_len == FFN_ROWS and s_batch % FFN_GROUPS == 0
    tiles_per_seq = seq // FFN_TILE
    n_p, n_s = batch * tiles_per_seq, s_batch // FFN_GROUPS
    n_steps = n_p + n_s
    tokens = n_steps * FFN_TILE
    la, lb = min(REACH_A, seq), min(REACH_B, seq)
    prompt_step = lambda i: jnp.minimum(i, n_p - 1)
    sample_step = lambda i: jnp.clip(i - n_p, 0, n_s - 1)
    tile = lambda i: (i, 0)
    tok = lambda w, dt: jax.ShapeDtypeStruct((tokens, w), dt)
    seq_of = lambda i: (prompt_step(i) // tiles_per_seq, 0, 0)
    rare = lambda shape, index_map: pl.BlockSpec(shape, index_map, pipeline_mode=pl.Buffered(1))
    vmem = (2 * 3 * D_MODEL * D_FF + 2 * D_MODEL * win.shape[1]
            + 3 * 2 * 4 * FFN_TILE * D_MODEL
            + 2 * 2 * FFN_TILE * (6 * WIDTH + 2 * KV_WIDTH_B)
            + 5 * 4 * FFN_TILE * (2 * WIDTH + 2 * KV_WIDTH_B)
            + FFN_TILE * (2 * 4 + 2) * FF_SLAB_MAX + 3 * 4 * FFN_TILE * D_MODEL
            + 4 * FFN_TILE * win.shape[1])
    kern = functools.partial(_ffn1_qkv_kernel, n_prompt_steps=n_p, tiles_per_seq=tiles_per_seq)
    return pl.pallas_call(
        kern,
        grid=(n_steps,),
        in_specs=[
            *_tile_specs(seq, prompt_step, sample_step),
            _resident(gains.shape),
            _resident(wg.shape), _resident(wu.shape), _resident(wd.shape), _resident(win.shape),
        ],
        out_specs=[
            pl.BlockSpec((FFN_GROUPS, FFN_ROWS, D_MODEL), lambda i: (i, 0, 0)),
            pl.BlockSpec((FFN_TILE, 2 * WIDTH), tile), pl.BlockSpec((FFN_TILE, WIDTH), tile),
            pl.BlockSpec((FFN_TILE, WIDTH), tile), pl.BlockSpec((FFN_TILE, 2 * WIDTH), tile),
            pl.BlockSpec((FFN_TILE, KV_WIDTH_B), tile), pl.BlockSpec((FFN_TILE, KV_WIDTH_B), tile),
            rare((None, WIDTH, la), seq_of), rare((None, WIDTH, la), seq_of),
            rare((None, KV_WIDTH_B, lb), seq_of), rare((None, KV_WIDTH_B, lb), seq_of),
            rare((FFN_TILE * N_HEADS, HEAD_DIM), lambda i: (sample_step(i), 0)),
            rare((FFN_TILE * N_HEADS, HEAD_DIM), lambda i: (sample_step(i), 0)),
            rare((FFN_TILE, KV_WIDTH_B), lambda i: (sample_step(i), 0)),
            rare((FFN_TILE, KV_WIDTH_B), lambda i: (sample_step(i), 0)),
        ],
        out_shape=[
            jax.ShapeDtypeStruct((n_steps * FFN_GROUPS, FFN_ROWS, D_MODEL), F32),
            tok(2 * WIDTH, BF16), tok(WIDTH, BF16), tok(WIDTH, BF16),
            tok(2 * WIDTH, BF16), tok(KV_WIDTH_B, BF16), tok(KV_WIDTH_B, BF16),
            jax.ShapeDtypeStruct((batch, WIDTH, la), F32), jax.ShapeDtypeStruct((batch, WIDTH, la), F32),
            jax.ShapeDtypeStruct((batch, KV_WIDTH_B, lb), F32), jax.ShapeDtypeStruct((batch, KV_WIDTH_B, lb), F32),
            jax.ShapeDtypeStruct((n_s * FFN_TILE * N_HEADS, HEAD_DIM), F32),
            jax.ShapeDtypeStruct((n_s * FFN_TILE * N_HEADS, HEAD_DIM), F32),
            jax.ShapeDtypeStruct((n_s * FFN_TILE, KV_WIDTH_B), F32),
            jax.ShapeDtypeStruct((n_s * FFN_TILE, KV_WIDTH_B), F32),
        ],
        scratch_shapes=[pltpu.VMEM((FFN_TILE, WIDTH), F32), pltpu.VMEM((FFN_TILE, WIDTH), F32),
                        pltpu.VMEM((FFN_TILE, KV_WIDTH_B), F32), pltpu.VMEM((FFN_TILE, KV_WIDTH_B), F32)],
        compiler_params=pltpu.CompilerParams(
            dimension_semantics=("arbitrary",), vmem_limit_bytes=_vmem_limit(vmem)),
        name="ffn1_qkv",
    )(x_p, x_s, mod_p, mod_s, gains, wg, wu, wd, win)


def _out_ffn2_kernel(x_ref, op_ref, os_ref, modp_ref, mods_ref, gains_ref, gg_ref, wout_ref,
                     wg_ref, wu_ref, wd_ref, yp_ref, ys_ref, *, n_sample_steps):
    is_sample = pl.program_id(0) < n_sample_steps
    gain = lambda k: gains_ref[k:k + 1, :]
    gg = gg_ref[...]
    o_tile = jnp.where(is_sample, os_ref[...], op_ref[...])

    def mod(k, gs):
        rows = mods_ref[k, gs]
        return jnp.where(is_sample, rows, jnp.broadcast_to(modp_ref[k], rows.shape))

    subs = _sub_tiles()
    xs = [x_ref[gs] for gs, _, _ in subs]
    ons = []
    for _, r0, n in subs:
        o = o_tile[r0:r0 + n, :].astype(F32)
        on = jnp.concatenate([_rms(o[:, :WIDTH], gg[:, :WIDTH]), _rms(o[:, WIDTH:], gg[:, WIDTH:])], axis=1)
        ons.append(on.astype(BF16))
    mixed = [jnp.dot(on, wout_ref[...], preferred_element_type=F32).reshape(x.shape) for on, x in zip(ons, xs)]
    x2s = [x + mod(5, gs) * _rms(m, gain(3)) for x, m, (gs, _, _) in zip(xs, mixed, subs)]
    hs = [_adaln(x2, gain(4), mod(6, gs), mod(7, gs)).reshape(n, D_MODEL).astype(BF16)
          for x2, (gs, _, n) in zip(x2s, subs)]
    fs = [_swiglu(h, wg_ref, wu_ref, wd_ref).reshape(x.shape) for h, x in zip(hs, xs)]
    for x2, f, (gs, r0, n) in zip(x2s, fs, subs):
        y = x2 + (FFN_RES * mod(8, gs)) * _rms(f, gain(5))
        yp_ref[0, r0:r0 + n, :] = y.reshape(n, D_MODEL)

    @pl.when(is_sample)
    def _():
        ys_ref[...] = yp_ref[0].reshape(ys_ref.shape)


def _out_ffn2(x1, o_p, o_s, mod_p, mod_s, gains, gg, wout, wg, wu, wd, batch, seq, s_batch):
    tiles_per_seq = seq // FFN_TILE
    n_p, n_s = batch * tiles_per_seq, s_batch // FFN_GROUPS
    prompt_step = lambda i: jnp.clip(i - n_s, 0, n_p - 1)
    sample_step = lambda i: jnp.minimum(i, n_s - 1)
    x1_tile = lambda i: (jnp.where(i < n_s, n_p + i, i - n_s), 0, 0)
    _, _, modp_spec, mods_spec = _tile_specs(seq, prompt_step, sample_step)
    vmem = (2 * 3 * D_MODEL * D_FF + 2 * wout.size
            + 3 * 2 * 4 * FFN_TILE * D_MODEL + 2 * 2 * 2 * FFN_TILE * 2 * WIDTH
            + FFN_TILE * (2 * 4 + 2) * FF_SLAB_MAX + 5 * 4 * FFN_TILE * D_MODEL)
    kern = functools.partial(_out_ffn2_kernel, n_sample_steps=n_s)
    return pl.pallas_call(
        kern,
        grid=(n_s + n_p,),
        in_specs=[
            pl.BlockSpec((FFN_GROUPS, FFN_ROWS, D_MODEL), x1_tile),
            pl.BlockSpec((FFN_TILE, 2 * WIDTH), lambda i: (prompt_step(i), 0)),
            pl.BlockSpec((FFN_TILE, 2 * WIDTH), lambda i: (sample_step(i), 0)),
            modp_spec, mods_spec,
            _resident(gains.shape), _resident(gg.shape), _resident(wout.shape),
            _resident(wg.shape), _resident(wu.shape), _resident(wd.shape),
        ],
        out_specs=[
            pl.BlockSpec((1, FFN_TILE, D_MODEL),
                         lambda i: (prompt_step(i) // tiles_per_seq, prompt_step(i) % tiles_per_seq, 0)),
            pl.BlockSpec((FFN_GROUPS, FFN_ROWS, D_MODEL), lambda i: (sample_step(i), 0, 0)),
        ],
        out_shape=[jax.ShapeDtypeStruct((batch, seq, D_MODEL), F32),
                   jax.ShapeDtypeStruct((s_batch, FFN_ROWS, D_MODEL), F32)],
        compiler_params=pltpu.CompilerParams(
            dimension_semantics=("arbitrary",), vmem_limit_bytes=_vmem_limit(vmem)),
        name="out_ffn2",
    )(x1, o_p, o_s, mod_p, mod_s, gains, gg, wout, wg, wu, wd)


def _softmax_terms(s, sink=None):
    m = jnp.max(s, axis=1, keepdims=True)
    if sink is not None:
        m = jnp.maximum(m, sink)
    e = jnp.exp(s - m)
    denom = jnp.sum(e, axis=1, keepdims=True)
    if sink is not None:
        denom = denom + jnp.exp(sink - m)
    return e.astype(BF16), 1.0 / denom


def _scores(q, k):
    return lax.dot_general(q, k, (((1,), (1,)), ((), ())), preferred_element_type=F32)


def _unstack_pair(z):
    n_q = z.shape[0] // 2
    is_lo = lax.broadcasted_iota(jnp.int32, (n_q, z.shape[1]), 1) < HEAD_DIM
    return jnp.where(is_lo, z[:n_q], z[n_q:])


def _sink_column(sinks_ref, pair, n_q):
    row = lax.broadcasted_iota(jnp.int32, (2 * n_q, 1), 0)
    return jnp.where(row < n_q, sinks_ref[pair], sinks_ref[N_PAIRS + pair])


GROUPS_PER_STEP = 2


def _attn_prompt_kernel(sinks_ref, qa_ref, qb_ref, ka_ref, va_ref, kb_ref, vb_ref, *rest):
    ba_refs, bb_refs = rest[:GROUPS_PER_STEP], rest[GROUPS_PER_STEP:2 * GROUPS_PER_STEP]
    n_cast = (len(rest) - 2 * GROUPS_PER_STEP - 1) // 2
    o_ref = rest[2 * GROUPS_PER_STEP + n_cast]
    for src, dst in zip(rest[2 * GROUPS_PER_STEP:2 * GROUPS_PER_STEP + n_cast], rest[len(rest) - n_cast:]):
        dst[...] = src[...].astype(BF16)
    slab = lambda p: slice(p * V7X_LANES, (p + 1) * V7X_LANES)
    bands = []
    for j in range(GROUPS_PER_STEP):
        g = pl.program_id(1) * GROUPS_PER_STEP + j
        start_a = pl.multiple_of(jnp.maximum(g * GROUP_Q - REACH_A, 0), GROUP_Q)
        start_b = pl.multiple_of(jnp.maximum(g * GROUP_Q - REACH_B, 0), GROUP_Q)
        bands.append((pl.ds(start_a, BAND_A), pl.ds(start_b, BAND_B)))

    def stacked(q_ref, j, p):
        rows = slice(j * GROUP_Q, (j + 1) * GROUP_Q)
        return jnp.concatenate([q_ref[rows, slab(p)], q_ref[rows, slab(N_PAIRS + p)]], axis=0)

    def score(mixer, j, p):
        if mixer == "a":
            bias = jnp.concatenate([ba_refs[j][0, 2 * p], ba_refs[j][0, 2 * p + 1]], axis=0)
            return _scores(stacked(qa_ref, j, p), ka_ref[bands[j][0], slab(p)]) + bias
        bias = jnp.concatenate([bb_refs[j][0, p], bb_refs[j][0, N_PAIRS + p]], axis=0)
        return _scores(stacked(qb_ref, j, p), kb_ref[bands[j][1], :]) + bias

    def weights(s, sink):
        m = jnp.max(s, axis=1, keepdims=True)
        if sink is None:
            return jnp.exp(s - m).astype(BF16), None
        m = jnp.maximum(m, sink)
        return jnp.exp(s - m).astype(BF16), jnp.exp(sink - m)

    def attend(e, extra, v):
        z = jnp.dot(e, jnp.concatenate([v, jnp.ones_like(v)], axis=1), preferred_element_type=F32)
        denom = z[:, V7X_LANES:] if extra is None else z[:, V7X_LANES:] + extra
        return _unstack_pair(z[:, :V7X_LANES] * (1.0 / denom))

    jobs = [(mixer, j, p) for mixer in ("a", "b") for j in range(GROUPS_PER_STEP) for p in range(N_PAIRS)]
    s_all = [score(*job) for job in jobs]
    w_all = [weights(s, _sink_column(sinks_ref, p, GROUP_Q) if mixer == "b" else None)
             for s, (mixer, j, p) in zip(s_all, jobs)]
    for (e, extra), (mixer, j, p) in zip(w_all, jobs):
        rows = slice(j * GROUP_Q, (j + 1) * GROUP_Q)
        if mixer == "a":
            o_ref[rows, slab(p)] = attend(e, extra, va_ref[bands[j][0], slab(p)]).astype(BF16)
        else:
            o_ref[rows, slab(N_PAIRS + p)] = attend(e, extra, vb_ref[bands[j][1], :]).astype(BF16)


def _attn_prompt(sinks, qa, qb, ka, va, kb, vb, bias_a, bias_b, batch, seq, cast=()):
    rows = GROUPS_PER_STEP * GROUP_Q
    n_steps = seq // rows
    cast_specs, cast_shapes = (zip(*[_cast_chunks(w, batch * n_steps, lambda b, s: b * n_steps + s) for w in cast])
                               if cast else ((), ()))
    q_spec = pl.BlockSpec((rows, 2 * WIDTH), lambda b, s: (b * n_steps + s, 0))
    kv_spec = lambda w: pl.BlockSpec((seq, w), lambda b, s: (b, 0), pipeline_mode=pl.Buffered(1))
    bias_spec = lambda band, n_var, j: pl.BlockSpec(
        (1, N_HEADS, GROUP_Q, band),
        lambda b, s: (jnp.minimum(s * GROUPS_PER_STEP + j, n_var - 1), 0, 0, 0))
    vmem = (2 * seq * (2 * WIDTH + 2 * KV_WIDTH_B)
            + GROUPS_PER_STEP * 2 * 4 * N_HEADS * GROUP_Q * (BAND_A + BAND_B)
            + 2 * 2 * rows * 6 * WIDTH + GROUPS_PER_STEP * 6 * 4 * 2 * GROUP_Q * BAND_A
            + sum(2 * 6 * math.prod(spec.block_shape) for spec in cast_specs))
    out = pl.pallas_call(
        _attn_prompt_kernel,
        grid=(batch, n_steps),
        in_specs=[
            pl.BlockSpec(memory_space=pltpu.SMEM),
            q_spec, q_spec,
            kv_spec(WIDTH), kv_spec(WIDTH), kv_spec(KV_WIDTH_B), kv_spec(KV_WIDTH_B),
            *[bias_spec(BAND_A, N_VAR_A, j) for j in range(GROUPS_PER_STEP)],
            *[bias_spec(BAND_B, N_VAR_B, j) for j in range(GROUPS_PER_STEP)],
            *cast_specs,
        ],
        out_specs=[pl.BlockSpec((rows, 2 * WIDTH), lambda b, s: (b * n_steps + s, 0)), *cast_specs],
        out_shape=[jax.ShapeDtypeStruct((batch * seq, 2 * WIDTH), BF16), *cast_shapes],
        compiler_params=pltpu.CompilerParams(
            dimension_semantics=("arbitrary", "arbitrary"), vmem_limit_bytes=_vmem_limit(vmem)),
        name="attn_prompt",
    )(sinks, qa, qb, ka, va, kb, vb, *([bias_a] * GROUPS_PER_STEP), *([bias_b] * GROUPS_PER_STEP), *cast)
    return out[0], tuple(out[1:])


def _pair_ordered_cols(h):
    return (h % N_PAIRS) * V7X_LANES + (h // N_PAIRS) * HEAD_DIM


SAMPLE_ROWS_PER_STEP = 4


def _attn_sample_kernel(sinks_ref, qa_ref, qb_ref, ka_ref, va_ref, kb_ref, vb_ref,
                        cak_ref, cav_ref, cbk_ref, cbv_ref, ba_ref, bb_ref, o_ref):
    n_rows = cak_ref.shape[0]
    n_q = qa_ref.shape[0] // n_rows
    n_kv = cbk_ref.shape[1]
    per_kv = N_HEADS // n_kv
    la, lb = cak_ref.shape[3], cbk_ref.shape[3]
    head = lambda ref, r, idx: ref[r * n_q:(r + 1) * n_q, idx * HEAD_DIM:(idx + 1) * HEAD_DIM]

    def query(ref, r, lo):
        lo = lo if lo % V7X_LANES < HEAD_DIM else WIDTH + lo
        return ref[r * n_q:(r + 1) * n_q, lo:lo + HEAD_DIM]

    def scores(q, cache_t_ref, new_ref, r, idx):
        old = jnp.dot(q, cache_t_ref[r, idx].astype(BF16), preferred_element_type=F32)
        return jnp.concatenate([old, _scores(q, head(new_ref, r, idx))], axis=1)

    def values(e, cache_t_ref, new_ref, r, idx, n_old):
        old = _scores(e[:, :n_old], cache_t_ref[r, idx].astype(BF16))
        return old + jnp.dot(e[:, n_old:], head(new_ref, r, idx), preferred_element_type=F32)

    jobs_a = [(r, h) for r in range(n_rows) for h in range(N_HEADS)]
    jobs_b = [(r, c) for r in range(n_rows) for c in range(n_kv)]
    s_a = [scores(query(qa_ref, r, h * HEAD_DIM), cak_ref, ka_ref, r, h) + ba_ref[h] for r, h in jobs_a]
    s_b = []
    for r, c in jobs_b:
        heads = range(c * per_kv, (c + 1) * per_kv)
        q = jnp.concatenate([query(qb_ref, r, _pair_ordered_cols(h)) for h in heads], axis=0)
        bias = jnp.concatenate([bb_ref[h] for h in heads], axis=0)
        s_b.append(scores(q, cbk_ref, kb_ref, r, c) + bias)

    row = lax.broadcasted_iota(jnp.int32, (per_kv * n_q, 1), 0)
    sinks = []
    for c in range(n_kv):
        sink = jnp.zeros((per_kv * n_q, 1), F32)
        for i in range(per_kv):
            sink = jnp.where((row >= i * n_q) & (row < (i + 1) * n_q), sinks_ref[c * per_kv + i], sink)
        sinks.append(sink)
    p_a = [_softmax_terms(s) for s in s_a]
    p_b = [_softmax_terms(s, sinks[c]) for s, (_, c) in zip(s_b, jobs_b)]

    out_a = {job: values(e, cav_ref, va_ref, *job, la) * inv for job, (e, inv) in zip(jobs_a, p_a)}
    out_b = {}
    for (r, c), (e, inv) in zip(jobs_b, p_b):
        o = values(e, cbv_ref, vb_ref, r, c, lb) * inv
        for i in range(per_kv):
            out_b[(r, c * per_kv + i)] = o[i * n_q:(i + 1) * n_q]
    order_b = [h for p in range(N_PAIRS) for h in (p, N_PAIRS + p)]
    for r in range(n_rows):
        pieces = [out_a[(r, h)] for h in range(N_HEADS)] + [out_b[(r, h)] for h in order_b]
        o_ref[r * n_q:(r + 1) * n_q, :] = jnp.concatenate(pieces, axis=1).astype(BF16)


def _attn_sample(sinks, qa, qb, ka, va, kb, vb, cak, cav, cbk, cbv, bias_a, bias_b, batch, s_len, first_row):
    n_rows = SAMPLE_ROWS_PER_STEP
    first_block = first_row // (n_rows * s_len)
    row = lambda w: pl.BlockSpec((n_rows * s_len, w), lambda b: (first_block + b, 0))
    cache = lambda a: pl.BlockSpec((n_rows,) + a.shape[1:], lambda b: (b, 0, 0, 0))
    whole = lambda a: pl.BlockSpec(a.shape, lambda b: (0,) * a.ndim)
    return pl.pallas_call(
        _attn_sample_kernel,
        grid=(batch // n_rows,),
        in_specs=[
            pl.BlockSpec(memory_space=pltpu.SMEM),
            row(2 * WIDTH), row(2 * WIDTH), row(WIDTH), row(WIDTH), row(KV_WIDTH_B), row(KV_WIDTH_B),
            cache(cak), cache(cav), cache(cbk), cache(cbv),
            whole(bias_a), whole(bias_b),
        ],
        out_specs=pl.BlockSpec((n_rows * s_len, 2 * WIDTH), lambda b: (b, 0)),
        out_shape=jax.ShapeDtypeStruct((batch * s_len, 2 * WIDTH), BF16),
        compiler_params=pltpu.CompilerParams(dimension_semantics=("arbitrary",)),
        name="attn_sample",
    )(sinks, qa, qb, ka, va, kb, vb, cak, cav, cbk, cbv, bias_a, bias_b)


def _pair_order(x, axis):
    shape = x.shape
    split = shape[:axis] + (N_HEADS // N_PAIRS, N_PAIRS, HEAD_DIM) + shape[axis + 1:]
    return jnp.swapaxes(x.reshape(split), axis, axis + 1).reshape(shape)


def _prep_out_weights(w_out, group_gains):
    wout = jnp.concatenate([w_out[:WIDTH], _pair_order(w_out[WIDTH:], 0)], axis=0).astype(BF16)
    gg = jnp.concatenate([group_gains[:WIDTH], _pair_order(group_gains[WIDTH:], 0)])
    return wout, gg.reshape(1, 2 * WIDTH)


def kernel(x_prompt, x_sample, cache_a_k, cache_a_v, cache_b_k, cache_b_v, c_prompt, c_sample, w_mod, b_mod,
           norm_gains, w1_gate, w1_up, w1_down, w_in, w_out, group_gains, rel_bias_a, t5_bias_table, sinks_b,
           w2_gate, w2_up, w2_down):
    depth = w_mod.shape[0]
    assert depth == 1
    batch, seq, _ = x_prompt.shape
    s_batch, s_len, _ = x_sample.shape
    la, lb = cache_a_k.shape[2], cache_b_k.shape[2]
    assert la == REACH_A and lb == REACH_B and seq % 512 == 0

    c_rows = jnp.concatenate([c_prompt, c_sample], axis=0)
    pad_rows = (-c_rows.shape[0]) % 8
    c_rows = jnp.pad(c_rows, ((0, pad_rows), (0, 0)))
    mod = _modulation(c_rows, w_mod[0], b_mod[0])
    mod_p = mod[:, :batch].reshape(N_MOD, batch, 1, D_MODEL)
    mod_s = mod[:, batch:batch + s_batch].reshape(N_MOD, s_batch, 1, D_MODEL)

    bias_a, bias_b, bias_sa, bias_sb, *w1, win = _bias_tables(
        rel_bias_a[0], t5_bias_table, s_len, la, lb, cast=(w1_gate[0], w1_up[0], w1_down[0], w_in[0]))

    gains = norm_gains[0]
    wout, gg = _prep_out_weights(w_out[0], group_gains[0])
    sinks = sinks_b[0]

    (x1, qa, ka, va, qb, kb, vb, *new_kv) = _ffn1_qkv(x_prompt, x_sample, mod_p, mod_s, gains, *w1, win)
    op, w2 = _attn_prompt(sinks, qa, qb, ka, va, kb, vb, bias_a, bias_b, batch, seq,
                          cast=(w2_gate[0], w2_up[0], w2_down[0]))
    os_ = _attn_sample(sinks, qa, qb, ka, va, kb, vb,
                       *(jnp.transpose(c[0], (0, 2, 3, 1)) for c in (cache_a_k, cache_a_v, cache_b_k, cache_b_v)),
                       bias_sa, bias_sb, s_batch, s_len, batch * seq)
    y_prompt, y_sample = _out_ffn2(x1, op, os_, mod_p, mod_s, gains, gg, wout, *w2, batch, seq, s_batch)
    rows_last = lambda t: jnp.transpose(t.reshape(depth, batch, -1, HEAD_DIM, t.shape[-1]), (0, 1, 4, 2, 3))
    new_p = tuple(rows_last(t) for t in new_kv[:4])
    new_s = tuple(t.reshape(depth, s_batch, s_len, -1, HEAD_DIM) for t in new_kv[4:])
    return (y_prompt, y_sample) + new_p + new_s
```

```python
import functools
import math

import jax
import jax.numpy as jnp
from jax import lax
from jax.experimental import pallas as pl
from jax.experimental.pallas import tpu as pltpu

F32 = jnp.float32
BF16 = jnp.bfloat16

D_MODEL = 1024
D_FF = 2816
CHUNK = 64
HEAD_DIM = 64
N_HEADS = 8
N_PAIRS = N_HEADS // 2
WIDTH = N_HEADS * HEAD_DIM
KV_WIDTH_B = 2 * HEAD_DIM
REACH_A = 8 * CHUNK
REACH_B = 2 * CHUNK
REL_CLIP_A = 128
N_BUCKETS = 32
N_MOD = 9
FFN_RES = 0.5
EPS = 1e-6
SCALE = HEAD_DIM ** -0.5
NEG_INF = -1e30

V7X_LANES = 128
V7X_VMEM_BYTES = 64 * 1024 * 1024
V7X_VMEM_LIMIT_CAP = 56 * 1024 * 1024

GROUP_Q = 2 * CHUNK
BAND_A = REACH_A + GROUP_Q
BAND_B = REACH_B + GROUP_Q
ORIGIN_A = 1024
WIDTH_MASTER_A = ORIGIN_A + BAND_A
ORIGIN_B = 256
WIDTH_MASTER_B = ORIGIN_B + BAND_B
N_VAR_A = REACH_A // GROUP_Q + 1
N_VAR_B = REACH_B // GROUP_Q + 1


def _vmem_limit(estimate_bytes):
    return int(min(max(estimate_bytes, 32 * 1024 * 1024), V7X_VMEM_LIMIT_CAP))


def _rms(x, gain):
    ms = jnp.mean(x * x, axis=-1, keepdims=True)
    return x * lax.rsqrt(ms + EPS) * gain


def _adaln(x, gain, shift, scale):
    return _rms(x, gain) * (1.0 + scale) + shift


def _silu(x):
    return x * (1.0 / (1.0 + jnp.exp(-x)))


def _mod_kernel(c_ref, w_ref, b_ref, o_ref):
    s = _silu(c_ref[...]).astype(BF16)
    o_ref[0] = jnp.dot(s, w_ref[...].astype(BF16), preferred_element_type=F32) + b_ref[0]


def _modulation(c_rows, w_mod, b_mod):
    rows = c_rows.shape[0]
    return pl.pallas_call(
        _mod_kernel,
        grid=(N_MOD,),
        in_specs=[
            pl.BlockSpec((rows, D_MODEL), lambda j: (0, 0)),
            pl.BlockSpec((D_MODEL, D_MODEL), lambda j: (0, j)),
            pl.BlockSpec((1, 1, D_MODEL), lambda j: (j, 0, 0)),
        ],
        out_specs=pl.BlockSpec((1, rows, D_MODEL), lambda j: (j, 0, 0)),
        out_shape=jax.ShapeDtypeStruct((N_MOD, rows, D_MODEL), F32),
        name="modulation",
    )(c_rows, w_mod, b_mod.reshape(N_MOD, 1, D_MODEL))


def _skew(base_row, n_rows):
    width = base_row.shape[1]
    m = jnp.broadcast_to(base_row, (n_rows, width))
    row = lax.broadcasted_iota(jnp.int32, (n_rows, width), 0)
    shift = 1
    while shift < n_rows:
        m = jnp.where((row & shift) != 0, pltpu.roll(m, shift, axis=1), m)
        shift *= 2
    return m


def _band_mask(n_rows, width, origin, n_prev):
    row = lax.broadcasted_iota(jnp.int32, (n_rows, width), 0)
    col = lax.broadcasted_iota(jnp.int32, (n_rows, width), 1)
    q_chunk = row >> 6
    k_chunk = (col >> 6) - (origin // CHUNK)
    return (k_chunk >= q_chunk - n_prev) & (k_chunk <= q_chunk)


def _t5_bucket(rel):
    half = N_BUCKETS // 2
    max_exact = half // 2
    n = jnp.abs(rel)
    n2 = n * n
    large = jnp.full(rel.shape, max_exact, jnp.int32)
    for k in range(1, half - max_exact):
        large = large + jnp.where(n2 >= (max_exact * max_exact) * (2 ** k), 1, 0)
    return jnp.where(rel > 0, half, 0) + jnp.where(n < max_exact, n, large)


def _bias_kernel(rel_ref, t5_ref, *refs, s_len, la, lb):
    n_cast = (len(refs) - 4) // 2
    for src, dst in zip(refs[:n_cast], refs[n_cast + 4:]):
        dst[...] = src[...].astype(BF16)
    ba_ref, bb_ref, bsa_ref, bsb_ref = refs[n_cast:n_cast + 4]

    tab = rel_ref[0]
    n_tab = 2 * REL_CLIP_A + 1
    first, last = tab[:, 0:1], tab[:, n_tab - 1:n_tab]
    win_lo = ORIGIN_A - 2 * REL_CLIP_A
    col = lax.broadcasted_iota(jnp.int32, (1, tab.shape[1] + V7X_LANES), 1)
    shifted = jnp.concatenate([jnp.zeros((1, V7X_LANES), F32), tab], axis=1)
    u_win = jnp.where(col < V7X_LANES, first, jnp.where(col >= V7X_LANES + n_tab, last, shifted))
    window = _skew(u_win, GROUP_Q)[:, V7X_LANES:]
    n_left = win_lo + V7X_LANES
    n_right = WIDTH_MASTER_A - n_left - window.shape[1]
    skew_a = jnp.concatenate([jnp.broadcast_to(first, (GROUP_Q, n_left)), window,
                              jnp.broadcast_to(last, (GROUP_Q, n_right))], axis=1)
    master_a = jnp.where(_band_mask(GROUP_Q, WIDTH_MASTER_A, ORIGIN_A, REACH_A // CHUNK), skew_a, NEG_INF)
    for v in range(N_VAR_A):
        start = ORIGIN_A - min(v * GROUP_Q, REACH_A) if v < N_VAR_A - 1 else ORIGIN_A - REACH_A
        ba_ref[v, 0] = master_a[:, start:start + BAND_A]
    bsa_ref[0] = skew_a[:s_len, ORIGIN_A - la:ORIGIN_A + s_len]

    t5 = t5_ref[0]
    rel = lax.broadcasted_iota(jnp.int32, (1, WIDTH_MASTER_B), 1) - ORIGIN_B
    bucket = _t5_bucket(rel)
    u_b = jnp.zeros((1, WIDTH_MASTER_B), F32)
    for i in range(N_BUCKETS):
        u_b = jnp.where(bucket == i, t5[:, i:i + 1], u_b)
    skew_b = _skew(u_b, GROUP_Q)
    master_b = jnp.where(_band_mask(GROUP_Q, WIDTH_MASTER_B, ORIGIN_B, REACH_B // CHUNK), skew_b, NEG_INF)
    for v in range(N_VAR_B):
        start = ORIGIN_B - min(v * GROUP_Q, REACH_B) if v < N_VAR_B - 1 else ORIGIN_B - REACH_B
        bb_ref[v, 0] = master_b[:, start:start + BAND_B]
    bsb_ref[0] = skew_b[:s_len, ORIGIN_B - lb:ORIGIN_B + s_len]


V7X_BF16_SUBLANES = 16


def _cast_chunks(w, n_steps, step_of):
    rows, cols = w.shape
    n_chunks = n_steps
    while rows % (n_chunks * V7X_BF16_SUBLANES):
        n_chunks //= 2
    spec = pl.BlockSpec((rows // n_chunks, cols), lambda *g: (jnp.minimum(step_of(*g), n_chunks - 1), 0))
    return spec, jax.ShapeDtypeStruct(w.shape, BF16)


def _bias_tables(rel_bias_a, t5_table, s_len, la, lb, cast=()):
    n_tab = 2 * REL_CLIP_A + 1
    tab_w = 3 * V7X_LANES
    rel_p = jnp.pad(rel_bias_a, ((0, 0), (0, tab_w - n_tab))).reshape(N_HEADS, 1, tab_w)
    t5_p = t5_table.reshape(N_HEADS, 1, N_BUCKETS)
    kern = functools.partial(_bias_kernel, s_len=s_len, la=la, lb=lb)
    cast_specs, cast_shapes = zip(*[_cast_chunks(w, N_HEADS, lambda h: h) for w in cast]) if cast else ((), ())
    return pl.pallas_call(
        kern,
        grid=(N_HEADS,),
        in_specs=[
            pl.BlockSpec((1, 1, tab_w), lambda h: (h, 0, 0)),
            pl.BlockSpec((1, 1, N_BUCKETS), lambda h: (h, 0, 0)),
            *cast_specs,
        ],
        out_specs=[
            pl.BlockSpec((N_VAR_A, 1, GROUP_Q, BAND_A), lambda h: (0, h, 0, 0)),
            pl.BlockSpec((N_VAR_B, 1, GROUP_Q, BAND_B), lambda h: (0, h, 0, 0)),
            pl.BlockSpec((1, s_len, la + s_len), lambda h: (h, 0, 0)),
            pl.BlockSpec((1, s_len, lb + s_len), lambda h: (h, 0, 0)),
            *cast_specs,
        ],
        out_shape=[
            jax.ShapeDtypeStruct((N_VAR_A, N_HEADS, GROUP_Q, BAND_A), F32),
            jax.ShapeDtypeStruct((N_VAR_B, N_HEADS, GROUP_Q, BAND_B), F32),
            jax.ShapeDtypeStruct((N_HEADS, s_len, la + s_len), F32),
            jax.ShapeDtypeStruct((N_HEADS, s_len, lb + s_len), F32),
            *cast_shapes,
        ],
        compiler_params=pltpu.CompilerParams(dimension_semantics=("arbitrary",)),
        name="bias_tables",
    )(rel_p, t5_p, *cast)


V7X_MXU_DIM = 256
FF_SLABS = ((0, 6 * V7X_MXU_DIM), (6 * V7X_MXU_DIM, D_FF))
FF_SLAB_MAX = max(hi - lo for lo, hi in FF_SLABS)


def _swiglu(h, wg_ref, wu_ref, wd_ref):
    acts = []
    for lo, hi in FF_SLABS:
        g = jnp.dot(h, wg_ref[:, lo:hi], preferred_element_type=F32)
        u = jnp.dot(h, wu_ref[:, lo:hi], preferred_element_type=F32)
        acts.append((_silu(g) * u).astype(BF16))
    acc = None
    for (lo, hi), a in zip(FF_SLABS, acts):
        part = jnp.dot(a, wd_ref[lo:hi, :], preferred_element_type=F32)
        acc = part if acc is None else acc + part
    return acc


SUB_TOKENS = V7X_MXU_DIM


def _sub_tiles(gt, rt):
    n_sub = gt * rt // SUB_TOKENS
    if gt == 1:
        step = rt // n_sub
        return [(slice(0, 1), slice(s * step, (s + 1) * step), s * step, step) for s in range(n_sub)]
    step = gt // n_sub
    return [(slice(s * step, (s + 1) * step), slice(0, rt), s * step * rt, step * rt) for s in range(n_sub)]


def _ffn1_qkv_kernel(x_ref, mod_ref, gains_ref, wg_ref, wu_ref, wd_ref, win_ref,
                     x1_ref, qa_ref, ka_ref, va_ref, qb_ref, kb_ref, vb_ref,
                     ka32_ref, va32_ref, kb32_ref, vb32_ref, *tail_scratch):
    gt, rt, _ = x_ref.shape
    tm = gt * rt
    gain = lambda k: gains_ref[k:k + 1, :]

    def store_tail(ref, val, r0):
        first = tm - ref.shape[0]
        lo = max(r0, first)
        if lo < r0 + val.shape[0]:
            ref[lo - first:r0 + val.shape[0] - first, :] = val[lo - r0:, :]

    def store_heads(ref, val, r0):
        n_heads = val.shape[1] // HEAD_DIM
        for h in range(n_heads):
            ref[pl.ds(r0 * n_heads + h, val.shape[0], stride=n_heads), :] = val[:, h * HEAD_DIM:(h + 1) * HEAD_DIM]

    def store_split(ref, rows, q, pairs):
        q = q * SCALE
        head = lambda h: q[:, h * HEAD_DIM:(h + 1) * HEAD_DIM]
        zeros = jnp.zeros((q.shape[0], HEAD_DIM), F32)
        ref[rows, :WIDTH] = jnp.concatenate([t for a, _ in pairs for t in (head(a), zeros)], axis=1).astype(BF16)
        ref[rows, WIDTH:] = jnp.concatenate([t for _, b in pairs for t in (zeros, head(b))], axis=1).astype(BF16)

    pairs_a = [(2 * p, 2 * p + 1) for p in range(N_PAIRS)]
    pairs_b = [(p, N_PAIRS + p) for p in range(N_PAIRS)]

    subs = _sub_tiles(gt, rt)
    mods = [lambda k, gs=gs: mod_ref[k, gs] for gs, _, _, _ in subs]
    xs = [x_ref[gs, rs, :] for gs, rs, _, _ in subs]
    hs = [_adaln(x, gain(0), mod(0), mod(1)).reshape(n, D_MODEL).astype(BF16)
          for x, mod, (_, _, _, n) in zip(xs, mods, subs)]
    fs = [_swiglu(h, wg_ref, wu_ref, wd_ref).reshape(x.shape) for h, x in zip(hs, xs)]
    h2s = []
    for x, f, mod, (gs, rs, _, n) in zip(xs, fs, mods, subs):
        x1 = x + (FFN_RES * mod(2)) * _rms(f, gain(1))
        x1_ref[gs, rs, :] = x1
        h2s.append(_adaln(x1, gain(2), mod(3), mod(4)).reshape(n, D_MODEL).astype(BF16))

    for h2, (_, _, r0, n) in zip(h2s, subs):
        rows = slice(r0, r0 + n)

        def proj(lo, width):
            return jnp.dot(h2, win_ref[:, lo:lo + width], preferred_element_type=F32)

        store_split(qa_ref, rows, proj(0, WIDTH), pairs_a)
        ka = proj(WIDTH, WIDTH)
        va = proj(2 * WIDTH, WIDTH)
        store_split(qb_ref, rows, proj(3 * WIDTH, WIDTH), pairs_b)
        kvb = proj(4 * WIDTH, 2 * KV_WIDTH_B)
        kb, vb = kvb[:, :KV_WIDTH_B], kvb[:, KV_WIDTH_B:]
        ka_ref[rows, :] = ka.astype(BF16)
        va_ref[rows, :] = va.astype(BF16)
        kb_ref[rows, :] = kb.astype(BF16)
        vb_ref[rows, :] = vb.astype(BF16)
        if tail_scratch:
            for ref, val in zip(tail_scratch, (ka, va, kb, vb)):
                store_tail(ref, val, r0)
        else:
            store_heads(ka32_ref, ka, r0)
            store_heads(va32_ref, va, r0)
            store_tail(kb32_ref, kb, r0)
            store_tail(vb32_ref, vb, r0)

    if tail_scratch:
        @pl.when(pl.program_id(1) == pl.num_programs(1) - 1)
        def _():
            for out_ref, ref in zip((ka32_ref, va32_ref, kb32_ref, vb32_ref), tail_scratch):
                out_ref[...] = ref[...].T


def _resident(shape):
    return pl.BlockSpec(shape, lambda *_: (0,) * len(shape), pipeline_mode=pl.Buffered(1))


def _ffn1_qkv(x, mod, gains, wg, wu, wd, win, gt, rt):
    n_groups, rows, _ = x.shape
    n_g, n_r = n_groups // gt, rows // rt
    tm = gt * rt
    tokens = n_groups * rows
    ta = gt * min(REACH_A, rows)
    tb = gt * min(REACH_B, rows)
    tile = lambda gi, ri: (gi * n_r + ri, 0)
    tail = lambda gi, ri: (gi, 0)
    tok = lambda w, dt: jax.ShapeDtypeStruct((tokens, w), dt)
    if gt == 1:
        assert ta <= tm and tb <= tm
        tail_specs = [pl.BlockSpec((None, w, t), lambda gi, ri: (gi, 0, 0))
                      for w, t in ((WIDTH, ta), (WIDTH, ta), (KV_WIDTH_B, tb), (KV_WIDTH_B, tb))]
        tail_shapes = [jax.ShapeDtypeStruct((n_g, w, t), F32)
                       for w, t in ((WIDTH, ta), (WIDTH, ta), (KV_WIDTH_B, tb), (KV_WIDTH_B, tb))]
        scratch = [pltpu.VMEM((ta, WIDTH), F32), pltpu.VMEM((ta, WIDTH), F32),
                   pltpu.VMEM((tb, KV_WIDTH_B), F32), pltpu.VMEM((tb, KV_WIDTH_B), F32)]
    else:
        assert n_r == 1
        tail_specs = [pl.BlockSpec((tm * N_HEADS, HEAD_DIM), tail), pl.BlockSpec((tm * N_HEADS, HEAD_DIM), tail),
                      pl.BlockSpec((tm, KV_WIDTH_B), tail), pl.BlockSpec((tm, KV_WIDTH_B), tail)]
        tail_shapes = [jax.ShapeDtypeStruct((tokens * N_HEADS, HEAD_DIM), F32),
                       jax.ShapeDtypeStruct((tokens * N_HEADS, HEAD_DIM), F32),
                       jax.ShapeDtypeStruct((tokens, KV_WIDTH_B), F32), jax.ShapeDtypeStruct((tokens, KV_WIDTH_B), F32)]
        scratch = []
    vmem = (2 * 3 * D_MODEL * D_FF + 2 * D_MODEL * win.shape[1]
            + 2 * 2 * 4 * tm * D_MODEL
            + 2 * 2 * tm * (6 * WIDTH + 2 * KV_WIDTH_B)
            + 2 * 4 * (2 * ta * WIDTH + 2 * tb * KV_WIDTH_B)
            + tm * (2 * 4 + 2) * FF_SLAB_MAX + 3 * 4 * tm * D_MODEL
            + 4 * tm * win.shape[1])
    return pl.pallas_call(
        _ffn1_qkv_kernel,
        grid=(n_g, n_r),
        in_specs=[
            pl.BlockSpec((gt, rt, D_MODEL), lambda gi, ri: (gi, ri, 0)),
            pl.BlockSpec((N_MOD, gt, 1, D_MODEL), lambda gi, ri: (0, gi, 0, 0)),
            _resident(gains.shape),
            _resident(wg.shape), _resident(wu.shape), _resident(wd.shape), _resident(win.shape),
        ],
        out_specs=[
            pl.BlockSpec((gt, rt, D_MODEL), lambda gi, ri: (gi, ri, 0)),
            pl.BlockSpec((tm, 2 * WIDTH), tile), pl.BlockSpec((tm, WIDTH), tile), pl.BlockSpec((tm, WIDTH), tile),
            pl.BlockSpec((tm, 2 * WIDTH), tile), pl.BlockSpec((tm, KV_WIDTH_B), tile),
            pl.BlockSpec((tm, KV_WIDTH_B), tile),
            *tail_specs,
        ],
        out_shape=[
            jax.ShapeDtypeStruct(x.shape, F32),
            tok(2 * WIDTH, BF16), tok(WIDTH, BF16), tok(WIDTH, BF16),
            tok(2 * WIDTH, BF16), tok(KV_WIDTH_B, BF16), tok(KV_WIDTH_B, BF16),
            *tail_shapes,
        ],
        scratch_shapes=scratch,
        compiler_params=pltpu.CompilerParams(
            dimension_semantics=("arbitrary", "arbitrary"), vmem_limit_bytes=_vmem_limit(vmem)),
        name="ffn1_qkv",
    )(x, mod, gains, wg, wu, wd, win)


def _out_ffn2_kernel(x_ref, o_ref, mod_ref, gains_ref, gg_ref, wout_ref, wg_ref, wu_ref, wd_ref, y_ref):
    gt, rt, _ = x_ref.shape
    gain = lambda k: gains_ref[k:k + 1, :]
    gg = gg_ref[...]
    subs = _sub_tiles(gt, rt)
    mods = [lambda k, gs=gs: mod_ref[k, gs] for gs, _, _, _ in subs]
    xs = [x_ref[gs, rs, :] for gs, rs, _, _ in subs]
    ons = []
    for _, _, r0, n in subs:
        o = o_ref[r0:r0 + n, :].astype(F32)
        on = jnp.concatenate([_rms(o[:, :WIDTH], gg[:, :WIDTH]), _rms(o[:, WIDTH:], gg[:, WIDTH:])], axis=1)
        ons.append(on.astype(BF16))
    mixed = [jnp.dot(on, wout_ref[...], preferred_element_type=F32).reshape(x.shape) for on, x in zip(ons, xs)]
    x2s = [x + mod(5) * _rms(m, gain(3)) for x, m, mod in zip(xs, mixed, mods)]
    hs = [_adaln(x2, gain(4), mod(6), mod(7)).reshape(n, D_MODEL).astype(BF16)
          for x2, mod, (_, _, _, n) in zip(x2s, mods, subs)]
    fs = [_swiglu(h, wg_ref, wu_ref, wd_ref).reshape(x.shape) for h, x in zip(hs, xs)]
    for x2, f, mod, (gs, rs, _, _) in zip(x2s, fs, mods, subs):
        y_ref[gs, rs, :] = x2 + (FFN_RES * mod(8)) * _rms(f, gain(5))


def _out_ffn2(x1, o, mod, gains, gg, wout, wg, wu, wd, gt, rt):
    n_groups, rows, _ = x1.shape
    n_g, n_r = n_groups // gt, rows // rt
    tm = gt * rt
    vmem = (2 * 3 * D_MODEL * D_FF + 2 * wout.size
            + 2 * 2 * 4 * tm * D_MODEL + 2 * 2 * tm * 2 * WIDTH
            + tm * (2 * 4 + 2) * FF_SLAB_MAX + 5 * 4 * tm * D_MODEL)
    return pl.pallas_call(
        _out_ffn2_kernel,
        grid=(n_g, n_r),
        in_specs=[
            pl.BlockSpec((gt, rt, D_MODEL), lambda gi, ri: (gi, ri, 0)),
            pl.BlockSpec((tm, 2 * WIDTH), lambda gi, ri: (gi * n_r + ri, 0)),
            pl.BlockSpec((N_MOD, gt, 1, D_MODEL), lambda gi, ri: (0, gi, 0, 0)),
            _resident(gains.shape), _resident(gg.shape), _resident(wout.shape),
            _resident(wg.shape), _resident(wu.shape), _resident(wd.shape),
        ],
        out_specs=pl.BlockSpec((gt, rt, D_MODEL), lambda gi, ri: (gi, ri, 0)),
        out_shape=jax.ShapeDtypeStruct(x1.shape, F32),
        compiler_params=pltpu.CompilerParams(
            dimension_semantics=("arbitrary", "arbitrary"), vmem_limit_bytes=_vmem_limit(vmem)),
        name="out_ffn2",
    )(x1, o, mod, gains, gg, wout, wg, wu, wd)


def _softmax_terms(s, sink=None):
    m = jnp.max(s, axis=1, keepdims=True)
    if sink is not None:
        m = jnp.maximum(m, sink)
    e = jnp.exp(s - m)
    denom = jnp.sum(e, axis=1, keepdims=True)
    if sink is not None:
        denom = denom + jnp.exp(sink - m)
    return e.astype(BF16), 1.0 / denom


def _scores(q, k):
    return lax.dot_general(q, k, (((1,), (1,)), ((), ())), preferred_element_type=F32)


def _unstack_pair(z):
    n_q = z.shape[0] // 2
    is_lo = lax.broadcasted_iota(jnp.int32, (n_q, z.shape[1]), 1) < HEAD_DIM
    return jnp.where(is_lo, z[:n_q], z[n_q:])


def _sink_column(sinks_ref, pair, n_q):
    row = lax.broadcasted_iota(jnp.int32, (2 * n_q, 1), 0)
    return jnp.where(row < n_q, sinks_ref[pair], sinks_ref[N_PAIRS + pair])


GROUPS_PER_STEP = 2


def _attn_prompt_kernel(sinks_ref, qa_ref, qb_ref, ka_ref, va_ref, kb_ref, vb_ref, *rest):
    ba_refs, bb_refs = rest[:GROUPS_PER_STEP], rest[GROUPS_PER_STEP:2 * GROUPS_PER_STEP]
    n_cast = (len(rest) - 2 * GROUPS_PER_STEP - 1) // 2
    o_ref = rest[2 * GROUPS_PER_STEP + n_cast]
    for src, dst in zip(rest[2 * GROUPS_PER_STEP:2 * GROUPS_PER_STEP + n_cast], rest[len(rest) - n_cast:]):
        dst[...] = src[...].astype(BF16)
    slab = lambda p: slice(p * V7X_LANES, (p + 1) * V7X_LANES)
    bands = []
    for j in range(GROUPS_PER_STEP):
        g = pl.program_id(1) * GROUPS_PER_STEP + j
        start_a = pl.multiple_of(jnp.maximum(g * GROUP_Q - REACH_A, 0), GROUP_Q)
        start_b = pl.multiple_of(jnp.maximum(g * GROUP_Q - REACH_B, 0), GROUP_Q)
        bands.append((pl.ds(start_a, BAND_A), pl.ds(start_b, BAND_B)))

    def stacked(q_ref, j, p):
        rows = slice(j * GROUP_Q, (j + 1) * GROUP_Q)
        return jnp.concatenate([q_ref[rows, slab(p)], q_ref[rows, slab(N_PAIRS + p)]], axis=0)

    def score(mixer, j, p):
        if mixer == "a":
            bias = jnp.concatenate([ba_refs[j][0, 2 * p], ba_refs[j][0, 2 * p + 1]], axis=0)
            return _scores(stacked(qa_ref, j, p), ka_ref[bands[j][0], slab(p)]) + bias
        bias = jnp.concatenate([bb_refs[j][0, p], bb_refs[j][0, N_PAIRS + p]], axis=0)
        return _scores(stacked(qb_ref, j, p), kb_ref[bands[j][1], :]) + bias

    def weights(s, sink):
        m = jnp.max(s, axis=1, keepdims=True)
        if sink is None:
            return jnp.exp(s - m).astype(BF16), None
        m = jnp.maximum(m, sink)
        return jnp.exp(s - m).astype(BF16), jnp.exp(sink - m)

    def attend(e, extra, v):
        z = jnp.dot(e, jnp.concatenate([v, jnp.ones_like(v)], axis=1), preferred_element_type=F32)
        denom = z[:, V7X_LANES:] if extra is None else z[:, V7X_LANES:] + extra
        return _unstack_pair(z[:, :V7X_LANES] * (1.0 / denom))

    jobs = [(mixer, j, p) for mixer in ("a", "b") for j in range(GROUPS_PER_STEP) for p in range(N_PAIRS)]
    s_all = [score(*job) for job in jobs]
    w_all = [weights(s, _sink_column(sinks_ref, p, GROUP_Q) if mixer == "b" else None)
             for s, (mixer, j, p) in zip(s_all, jobs)]
    for (e, extra), (mixer, j, p) in zip(w_all, jobs):
        rows = slice(j * GROUP_Q, (j + 1) * GROUP_Q)
        if mixer == "a":
            o_ref[rows, slab(p)] = attend(e, extra, va_ref[bands[j][0], slab(p)]).astype(BF16)
        else:
            o_ref[rows, slab(N_PAIRS + p)] = attend(e, extra, vb_ref[bands[j][1], :]).astype(BF16)


def _attn_prompt(sinks, qa, qb, ka, va, kb, vb, bias_a, bias_b, batch, seq, cast=()):
    rows = GROUPS_PER_STEP * GROUP_Q
    n_steps = seq // rows
    cast_specs, cast_shapes = (zip(*[_cast_chunks(w, batch * n_steps, lambda b, s: b * n_steps + s) for w in cast])
                               if cast else ((), ()))
    q_spec = pl.BlockSpec((rows, 2 * WIDTH), lambda b, s: (b * n_steps + s, 0))
    kv_spec = lambda w: pl.BlockSpec((seq, w), lambda b, s: (b, 0), pipeline_mode=pl.Buffered(1))
    bias_spec = lambda band, n_var, j: pl.BlockSpec(
        (1, N_HEADS, GROUP_Q, band),
        lambda b, s: (jnp.minimum(s * GROUPS_PER_STEP + j, n_var - 1), 0, 0, 0))
    vmem = (2 * seq * (2 * WIDTH + 2 * KV_WIDTH_B)
            + GROUPS_PER_STEP * 2 * 4 * N_HEADS * GROUP_Q * (BAND_A + BAND_B)
            + 2 * 2 * rows * 6 * WIDTH + GROUPS_PER_STEP * 6 * 4 * 2 * GROUP_Q * BAND_A
            + sum(2 * 6 * math.prod(spec.block_shape) for spec in cast_specs))
    out = pl.pallas_call(
        _attn_prompt_kernel,
        grid=(batch, n_steps),
        in_specs=[
            pl.BlockSpec(memory_space=pltpu.SMEM),
            q_spec, q_spec,
            kv_spec(WIDTH), kv_spec(WIDTH), kv_spec(KV_WIDTH_B), kv_spec(KV_WIDTH_B),
            *[bias_spec(BAND_A, N_VAR_A, j) for j in range(GROUPS_PER_STEP)],
            *[bias_spec(BAND_B, N_VAR_B, j) for j in range(GROUPS_PER_STEP)],
            *cast_specs,
        ],
        out_specs=[pl.BlockSpec((rows, 2 * WIDTH), lambda b, s: (b * n_steps + s, 0)), *cast_specs],
        out_shape=[jax.ShapeDtypeStruct((batch * seq, 2 * WIDTH), BF16), *cast_shapes],
        compiler_params=pltpu.CompilerParams(
            dimension_semantics=("arbitrary", "arbitrary"), vmem_limit_bytes=_vmem_limit(vmem)),
        name="attn_prompt",
    )(sinks, qa, qb, ka, va, kb, vb, *([bias_a] * GROUPS_PER_STEP), *([bias_b] * GROUPS_PER_STEP), *cast)
    return out[0], tuple(out[1:])


def _pair_ordered_cols(h):
    return (h % N_PAIRS) * V7X_LANES + (h // N_PAIRS) * HEAD_DIM


SAMPLE_ROWS_PER_STEP = 4


def _attn_sample_kernel(sinks_ref, qa_ref, qb_ref, ka_ref, va_ref, kb_ref, vb_ref,
                        cak_ref, cav_ref, cbk_ref, cbv_ref, ba_ref, bb_ref, o_ref):
    n_rows = cak_ref.shape[0]
    n_q = qa_ref.shape[0] // n_rows
    n_kv = cbk_ref.shape[1]
    per_kv = N_HEADS // n_kv
    la, lb = cak_ref.shape[3], cbk_ref.shape[3]
    head = lambda ref, r, idx: ref[r * n_q:(r + 1) * n_q, idx * HEAD_DIM:(idx + 1) * HEAD_DIM]

    def query(ref, r, lo):
        lo = lo if lo % V7X_LANES < HEAD_DIM else WIDTH + lo
        return ref[r * n_q:(r + 1) * n_q, lo:lo + HEAD_DIM]

    def scores(q, cache_t_ref, new_ref, r, idx):
        old = jnp.dot(q, cache_t_ref[r, idx].astype(BF16), preferred_element_type=F32)
        return jnp.concatenate([old, _scores(q, head(new_ref, r, idx))], axis=1)

    def values(e, cache_t_ref, new_ref, r, idx, n_old):
        old = _scores(e[:, :n_old], cache_t_ref[r, idx].astype(BF16))
        return old + jnp.dot(e[:, n_old:], head(new_ref, r, idx), preferred_element_type=F32)

    jobs_a = [(r, h) for r in range(n_rows) for h in range(N_HEADS)]
    jobs_b = [(r, c) for r in range(n_rows) for c in range(n_kv)]
    s_a = [scores(query(qa_ref, r, h * HEAD_DIM), cak_ref, ka_ref, r, h) + ba_ref[h] for r, h in jobs_a]
    s_b = []
    for r, c in jobs_b:
        heads = range(c * per_kv, (c + 1) * per_kv)
        q = jnp.concatenate([query(qb_ref, r, _pair_ordered_cols(h)) for h in heads], axis=0)
        bias = jnp.concatenate([bb_ref[h] for h in heads], axis=0)
        s_b.append(scores(q, cbk_ref, kb_ref, r, c) + bias)

    row = lax.broadcasted_iota(jnp.int32, (per_kv * n_q, 1), 0)
    sinks = []
    for c in range(n_kv):
        sink = jnp.zeros((per_kv * n_q, 1), F32)
        for i in range(per_kv):
            sink = jnp.where((row >= i * n_q) & (row < (i + 1) * n_q), sinks_ref[c * per_kv + i], sink)
        sinks.append(sink)
    p_a = [_softmax_terms(s) for s in s_a]
    p_b = [_softmax_terms(s, sinks[c]) for s, (_, c) in zip(s_b, jobs_b)]

    out_a = {job: values(e, cav_ref, va_ref, *job, la) * inv for job, (e, inv) in zip(jobs_a, p_a)}
    out_b = {}
    for (r, c), (e, inv) in zip(jobs_b, p_b):
        o = values(e, cbv_ref, vb_ref, r, c, lb) * inv
        for i in range(per_kv):
            out_b[(r, c * per_kv + i)] = o[i * n_q:(i + 1) * n_q]
    order_b = [h for p in range(N_PAIRS) for h in (p, N_PAIRS + p)]
    for r in range(n_rows):
        pieces = [out_a[(r, h)] for h in range(N_HEADS)] + [out_b[(r, h)] for h in order_b]
        o_ref[r * n_q:(r + 1) * n_q, :] = jnp.concatenate(pieces, axis=1).astype(BF16)


def _attn_sample(sinks, qa, qb, ka, va, kb, vb, cak, cav, cbk, cbv, bias_a, bias_b, batch, s_len):
    n_rows = SAMPLE_ROWS_PER_STEP
    row = lambda w: pl.BlockSpec((n_rows * s_len, w), lambda b: (b, 0))
    cache = lambda a: pl.BlockSpec((n_rows,) + a.shape[1:], lambda b: (b, 0, 0, 0))
    whole = lambda a: pl.BlockSpec(a.shape, lambda b: (0,) * a.ndim)
    return pl.pallas_call(
        _attn_sample_kernel,
        grid=(batch // n_rows,),
        in_specs=[
            pl.BlockSpec(memory_space=pltpu.SMEM),
            row(2 * WIDTH), row(2 * WIDTH), row(WIDTH), row(WIDTH), row(KV_WIDTH_B), row(KV_WIDTH_B),
            cache(cak), cache(cav), cache(cbk), cache(cbv),
            whole(bias_a), whole(bias_b),
        ],
        out_specs=pl.BlockSpec((n_rows * s_len, 2 * WIDTH), lambda b: (b, 0)),
        out_shape=jax.ShapeDtypeStruct((batch * s_len, 2 * WIDTH), BF16),
        compiler_params=pltpu.CompilerParams(dimension_semantics=("arbitrary",)),
        name="attn_sample",
    )(sinks, qa, qb, ka, va, kb, vb, cak, cav, cbk, cbv, bias_a, bias_b)


def _pair_order(x, axis):
    shape = x.shape
    split = shape[:axis] + (N_HEADS // N_PAIRS, N_PAIRS, HEAD_DIM) + shape[axis + 1:]
    return jnp.swapaxes(x.reshape(split), axis, axis + 1).reshape(shape)


def _prep_out_weights(w_out, group_gains):
    wout = jnp.concatenate([w_out[:WIDTH], _pair_order(w_out[WIDTH:], 0)], axis=0).astype(BF16)
    gg = jnp.concatenate([group_gains[:WIDTH], _pair_order(group_gains[WIDTH:], 0)])
    return wout, gg.reshape(1, 2 * WIDTH)


def kernel(x_prompt, x_sample, cache_a_k, cache_a_v, cache_b_k, cache_b_v, c_prompt, c_sample, w_mod, b_mod,
           norm_gains, w1_gate, w1_up, w1_down, w_in, w_out, group_gains, rel_bias_a, t5_bias_table, sinks_b,
           w2_gate, w2_up, w2_down):
    depth = w_mod.shape[0]
    assert depth == 1
    batch, seq, _ = x_prompt.shape
    s_batch, s_len, _ = x_sample.shape
    la, lb = cache_a_k.shape[2], cache_b_k.shape[2]
    assert la == REACH_A and lb == REACH_B and seq % 512 == 0

    c_rows = jnp.concatenate([c_prompt, c_sample], axis=0)
    pad_rows = (-c_rows.shape[0]) % 8
    c_rows = jnp.pad(c_rows, ((0, pad_rows), (0, 0)))
    mod = _modulation(c_rows, w_mod[0], b_mod[0])
    mod_p = mod[:, :batch].reshape(N_MOD, batch, 1, D_MODEL)
    mod_s = mod[:, batch:batch + s_batch].reshape(N_MOD, s_batch, 1, D_MODEL)

    bias_a, bias_b, bias_sa, bias_sb, *w1, win = _bias_tables(
        rel_bias_a[0], t5_bias_table, s_len, la, lb, cast=(w1_gate[0], w1_up[0], w1_down[0], w_in[0]))

    gains = norm_gains[0]
    wout, gg = _prep_out_weights(w_out[0], group_gains[0])
    sinks = sinks_b[0]

    rt_p = 512
    (x1p, qa, ka, va, qb, kb, vb, ka32, va32, kb32, vb32) = _ffn1_qkv(
        x_prompt, mod_p, gains, *w1, win, 1, rt_p)
    op, w2 = _attn_prompt(sinks, qa, qb, ka, va, kb, vb, bias_a, bias_b, batch, seq,
                          cast=(w2_gate[0], w2_up[0], w2_down[0]))
    y_prompt = _out_ffn2(x1p, op, mod_p, gains, gg, wout, *w2, 1, 2 * rt_p)
    rows_last = lambda t: jnp.transpose(t.reshape(depth, batch, -1, HEAD_DIM, t.shape[-1]), (0, 1, 4, 2, 3))
    new_p = tuple(rows_last(t) for t in (ka32, va32, kb32, vb32))

    gt_s = 16
    (x1s, qas, kas, vas, qbs, kbs, vbs, kas32, vas32, kbs32, vbs32) = _ffn1_qkv(
        x_sample, mod_s, gains, *w1, win, gt_s, s_len)
    os_ = _attn_sample(sinks, qas, qbs, kas, vas, kbs, vbs,
                       *(jnp.transpose(c[0], (0, 2, 3, 1)) for c in (cache_a_k, cache_a_v, cache_b_k, cache_b_v)),
                       bias_sa, bias_sb, s_batch, s_len)
    y_sample = _out_ffn2(x1s, os_, mod_s, gains, gg, wout, *w2, gt_s, s_len)
    new_s = (kas32.reshape(depth, s_batch, s_len, N_HEADS, HEAD_DIM),
             vas32.reshape(depth, s_batch, s_len, N_HEADS, HEAD_DIM),
             kbs32.reshape(depth, s_batch, s_len, 2, HEAD_DIM), vbs32.reshape(depth, s_batch, s_len, 2, HEAD_DIM))

    return (y_prompt, y_sample) + new_p + new_s
```

```python
import functools
import math

import jax
import jax.numpy as jnp
from jax import lax
from jax.experimental import pallas as pl
from jax.experimental.pallas import tpu as pltpu

F32 = jnp.float32
BF16 = jnp.bfloat16

D_MODEL = 1024
D_FF = 2816
CHUNK = 64
HEAD_DIM = 64
N_HEADS = 8
N_PAIRS = N_HEADS // 2
WIDTH = N_HEADS * HEAD_DIM
KV_WIDTH_B = 2 * HEAD_DIM
REACH_A = 8 * CHUNK
REACH_B = 2 * CHUNK
REL_CLIP_A = 128
N_BUCKETS = 32
N_MOD = 9
FFN_RES = 0.5
EPS = 1e-6
SCALE = HEAD_DIM ** -0.5
LOG2E = 1.4426950408889634
NEG_INF = -1e30

V7X_LANES = 128
V7X_VMEM_BYTES = 64 * 1024 * 1024
V7X_VMEM_LIMIT_CAP = 56 * 1024 * 1024

GROUP_Q = 2 * CHUNK
BAND_A = REACH_A + GROUP_Q
BAND_B = REACH_B + GROUP_Q
ORIGIN_A = 1024
WIDTH_MASTER_A = ORIGIN_A + BAND_A
ORIGIN_B = 256
WIDTH_MASTER_B = ORIGIN_B + BAND_B
N_VAR_A = REACH_A // GROUP_Q + 1
N_VAR_B = REACH_B // GROUP_Q + 1


def _vmem_limit(estimate_bytes):
    return int(min(max(estimate_bytes, 32 * 1024 * 1024), V7X_VMEM_LIMIT_CAP))


def _rms(x, gain):
    ms = jnp.mean(x * x, axis=-1, keepdims=True)
    return x * lax.rsqrt(ms + EPS) * gain


def _adaln(x, gain, shift, scale):
    return _rms(x, gain) * (1.0 + scale) + shift


def _silu(x):
    return x * (1.0 / (1.0 + jnp.exp(-x)))


def _mod_kernel(c_ref, w_ref, b_ref, o_ref):
    s = _silu(c_ref[...]).astype(BF16)
    o_ref[0] = jnp.dot(s, w_ref[...].astype(BF16), preferred_element_type=F32) + b_ref[0]


def _modulation(c_rows, w_mod, b_mod):
    rows = c_rows.shape[0]
    return pl.pallas_call(
        _mod_kernel,
        grid=(N_MOD,),
        in_specs=[
            pl.BlockSpec((rows, D_MODEL), lambda j: (0, 0)),
            pl.BlockSpec((D_MODEL, D_MODEL), lambda j: (0, j)),
            pl.BlockSpec((1, 1, D_MODEL), lambda j: (j, 0, 0)),
        ],
        out_specs=pl.BlockSpec((1, rows, D_MODEL), lambda j: (j, 0, 0)),
        out_shape=jax.ShapeDtypeStruct((N_MOD, rows, D_MODEL), F32),
        name="modulation",
    )(c_rows, w_mod, b_mod.reshape(N_MOD, 1, D_MODEL))


def _skew(base_row, n_rows):
    width = base_row.shape[1]
    m = jnp.broadcast_to(base_row, (n_rows, width))
    row = lax.broadcasted_iota(jnp.int32, (n_rows, width), 0)
    shift = 1
    while shift < n_rows:
        m = jnp.where((row & shift) != 0, pltpu.roll(m, shift, axis=1), m)
        shift *= 2
    return m


def _band_mask(n_rows, width, origin, n_prev):
    row = lax.broadcasted_iota(jnp.int32, (n_rows, width), 0)
    col = lax.broadcasted_iota(jnp.int32, (n_rows, width), 1)
    q_chunk = row >> 6
    k_chunk = (col >> 6) - (origin // CHUNK)
    return (k_chunk >= q_chunk - n_prev) & (k_chunk <= q_chunk)


def _t5_bucket(rel):
    half = N_BUCKETS // 2
    max_exact = half // 2
    n = jnp.abs(rel)
    n2 = n * n
    large = jnp.full(rel.shape, max_exact, jnp.int32)
    for k in range(1, half - max_exact):
        large = large + jnp.where(n2 >= (max_exact * max_exact) * (2 ** k), 1, 0)
    return jnp.where(rel > 0, half, 0) + jnp.where(n < max_exact, n, large)


def _bias_kernel(rel_ref, t5_ref, *refs, s_len, la, lb):
    n_cast = (len(refs) - 4) // 2
    for src, dst in zip(refs[:n_cast], refs[n_cast + 4:]):
        dst[...] = src[...].astype(BF16)
    ba_ref, bb_ref, bsa_ref, bsb_ref = refs[n_cast:n_cast + 4]

    tab = rel_ref[0] * LOG2E
    n_tab = 2 * REL_CLIP_A + 1
    first, last = tab[:, 0:1], tab[:, n_tab - 1:n_tab]
    win_lo = ORIGIN_A - 2 * REL_CLIP_A
    col = lax.broadcasted_iota(jnp.int32, (1, tab.shape[1] + V7X_LANES), 1)
    shifted = jnp.concatenate([jnp.zeros((1, V7X_LANES), F32), tab], axis=1)
    u_win = jnp.where(col < V7X_LANES, first, jnp.where(col >= V7X_LANES + n_tab, last, shifted))
    window = _skew(u_win, GROUP_Q)[:, V7X_LANES:]
    n_left = win_lo + V7X_LANES
    n_right = WIDTH_MASTER_A - n_left - window.shape[1]
    skew_a = jnp.concatenate([jnp.broadcast_to(first, (GROUP_Q, n_left)), window,
                              jnp.broadcast_to(last, (GROUP_Q, n_right))], axis=1)
    master_a = jnp.where(_band_mask(GROUP_Q, WIDTH_MASTER_A, ORIGIN_A, REACH_A // CHUNK), skew_a, NEG_INF)
    for v in range(N_VAR_A):
        start = ORIGIN_A - min(v * GROUP_Q, REACH_A) if v < N_VAR_A - 1 else ORIGIN_A - REACH_A
        ba_ref[v, 0] = master_a[:, start:start + BAND_A]
    bsa_ref[0] = skew_a[:s_len, ORIGIN_A - la:ORIGIN_A + s_len]

    t5 = t5_ref[0] * LOG2E
    rel = lax.broadcasted_iota(jnp.int32, (1, WIDTH_MASTER_B), 1) - ORIGIN_B
    bucket = _t5_bucket(rel)
    u_b = jnp.zeros((1, WIDTH_MASTER_B), F32)
    for i in range(N_BUCKETS):
        u_b = jnp.where(bucket == i, t5[:, i:i + 1], u_b)
    skew_b = _skew(u_b, GROUP_Q)
    master_b = jnp.where(_band_mask(GROUP_Q, WIDTH_MASTER_B, ORIGIN_B, REACH_B // CHUNK), skew_b, NEG_INF)
    for v in range(N_VAR_B):
        start = ORIGIN_B - min(v * GROUP_Q, REACH_B) if v < N_VAR_B - 1 else ORIGIN_B - REACH_B
        bb_ref[v, 0] = master_b[:, start:start + BAND_B]
    bsb_ref[0] = skew_b[:s_len, ORIGIN_B - lb:ORIGIN_B + s_len]


V7X_BF16_SUBLANES = 16


def _cast_chunks(w, n_steps, step_of):
    rows, cols = w.shape
    n_chunks = n_steps
    while rows % (n_chunks * V7X_BF16_SUBLANES):
        n_chunks //= 2
    spec = pl.BlockSpec((rows // n_chunks, cols), lambda *g: (jnp.minimum(step_of(*g), n_chunks - 1), 0))
    return spec, jax.ShapeDtypeStruct(w.shape, BF16)


def _bias_tables(rel_bias_a, t5_table, s_len, la, lb, cast=()):
    n_tab = 2 * REL_CLIP_A + 1
    tab_w = 3 * V7X_LANES
    rel_p = jnp.pad(rel_bias_a, ((0, 0), (0, tab_w - n_tab))).reshape(N_HEADS, 1, tab_w)
    t5_p = t5_table.reshape(N_HEADS, 1, N_BUCKETS)
    kern = functools.partial(_bias_kernel, s_len=s_len, la=la, lb=lb)
    cast_specs, cast_shapes = zip(*[_cast_chunks(w, N_HEADS, lambda h: h) for w in cast]) if cast else ((), ())
    return pl.pallas_call(
        kern,
        grid=(N_HEADS,),
        in_specs=[
            pl.BlockSpec((1, 1, tab_w), lambda h: (h, 0, 0)),
            pl.BlockSpec((1, 1, N_BUCKETS), lambda h: (h, 0, 0)),
            *cast_specs,
        ],
        out_specs=[
            pl.BlockSpec((N_VAR_A, 1, GROUP_Q, BAND_A), lambda h: (0, h, 0, 0)),
            pl.BlockSpec((N_VAR_B, 1, GROUP_Q, BAND_B), lambda h: (0, h, 0, 0)),
            pl.BlockSpec((1, s_len, la + s_len), lambda h: (h, 0, 0)),
            pl.BlockSpec((1, s_len, lb + s_len), lambda h: (h, 0, 0)),
            *cast_specs,
        ],
        out_shape=[
            jax.ShapeDtypeStruct((N_VAR_A, N_HEADS, GROUP_Q, BAND_A), F32),
            jax.ShapeDtypeStruct((N_VAR_B, N_HEADS, GROUP_Q, BAND_B), F32),
            jax.ShapeDtypeStruct((N_HEADS, s_len, la + s_len), F32),
            jax.ShapeDtypeStruct((N_HEADS, s_len, lb + s_len), F32),
            *cast_shapes,
        ],
        compiler_params=pltpu.CompilerParams(dimension_semantics=("arbitrary",)),
        name="bias_tables",
    )(rel_p, t5_p, *cast)


V7X_MXU_DIM = 256
FF_SLABS = ((0, 6 * V7X_MXU_DIM), (6 * V7X_MXU_DIM, D_FF))
FF_SLAB_MAX = max(hi - lo for lo, hi in FF_SLABS)


def _swiglu(h, wg_ref, wu_ref, wd_ref):
    acts = []
    for lo, hi in FF_SLABS:
        g = jnp.dot(h, wg_ref[:, lo:hi], preferred_element_type=F32)
        u = jnp.dot(h, wu_ref[:, lo:hi], preferred_element_type=F32)
        acts.append((_silu(g) * u).astype(BF16))
    acc = None
    for (lo, hi), a in zip(FF_SLABS, acts):
        part = jnp.dot(a, wd_ref[lo:hi, :], preferred_element_type=F32)
        acc = part if acc is None else acc + part
    return acc


SUB_TOKENS = V7X_MXU_DIM


def _sub_tiles(gt, rt):
    n_sub = gt * rt // SUB_TOKENS
    if gt == 1:
        step = rt // n_sub
        return [(slice(0, 1), slice(s * step, (s + 1) * step), s * step, step) for s in range(n_sub)]
    step = gt // n_sub
    return [(slice(s * step, (s + 1) * step), slice(0, rt), s * step * rt, step * rt) for s in range(n_sub)]


def _ffn1_qkv_kernel(x_ref, mod_ref, gains_ref, wg_ref, wu_ref, wd_ref, win_ref,
                     x1_ref, qa_ref, ka_ref, va_ref, qb_ref, kb_ref, vb_ref,
                     ka32_ref, va32_ref, kb32_ref, vb32_ref, *tail_scratch):
    gt, rt, _ = x_ref.shape
    tm = gt * rt
    gain = lambda k: gains_ref[k:k + 1, :]

    def store_tail(ref, val, r0):
        first = tm - ref.shape[0]
        lo = max(r0, first)
        if lo < r0 + val.shape[0]:
            ref[lo - first:r0 + val.shape[0] - first, :] = val[lo - r0:, :]

    def store_heads(ref, val, r0):
        n_heads = val.shape[1] // HEAD_DIM
        for h in range(n_heads):
            ref[pl.ds(r0 * n_heads + h, val.shape[0], stride=n_heads), :] = val[:, h * HEAD_DIM:(h + 1) * HEAD_DIM]

    def store_split(ref, rows, q, pairs):
        q = q * (SCALE * LOG2E)
        head = lambda h: q[:, h * HEAD_DIM:(h + 1) * HEAD_DIM]
        zeros = jnp.zeros((q.shape[0], HEAD_DIM), F32)
        ref[rows, :WIDTH] = jnp.concatenate([t for a, _ in pairs for t in (head(a), zeros)], axis=1).astype(BF16)
        ref[rows, WIDTH:] = jnp.concatenate([t for _, b in pairs for t in (zeros, head(b))], axis=1).astype(BF16)

    pairs_a = [(2 * p, 2 * p + 1) for p in range(N_PAIRS)]
    pairs_b = [(p, N_PAIRS + p) for p in range(N_PAIRS)]

    subs = _sub_tiles(gt, rt)
    mods = [lambda k, gs=gs: mod_ref[k, gs] for gs, _, _, _ in subs]
    xs = [x_ref[gs, rs, :] for gs, rs, _, _ in subs]
    hs = [_adaln(x, gain(0), mod(0), mod(1)).reshape(n, D_MODEL).astype(BF16)
          for x, mod, (_, _, _, n) in zip(xs, mods, subs)]
    fs = [_swiglu(h, wg_ref, wu_ref, wd_ref).reshape(x.shape) for h, x in zip(hs, xs)]
    h2s = []
    for x, f, mod, (gs, rs, _, n) in zip(xs, fs, mods, subs):
        x1 = x + (FFN_RES * mod(2)) * _rms(f, gain(1))
        x1_ref[gs, rs, :] = x1
        h2s.append(_adaln(x1, gain(2), mod(3), mod(4)).reshape(n, D_MODEL).astype(BF16))

    for h2, (_, _, r0, n) in zip(h2s, subs):
        rows = slice(r0, r0 + n)

        def proj(lo, width):
            return jnp.dot(h2, win_ref[:, lo:lo + width], preferred_element_type=F32)

        store_split(qa_ref, rows, proj(0, WIDTH), pairs_a)
        ka = proj(WIDTH, WIDTH)
        va = proj(2 * WIDTH, WIDTH)
        store_split(qb_ref, rows, proj(3 * WIDTH, WIDTH), pairs_b)
        kvb = proj(4 * WIDTH, 2 * KV_WIDTH_B)
        kb, vb = kvb[:, :KV_WIDTH_B], kvb[:, KV_WIDTH_B:]
        ka_ref[rows, :] = ka.astype(BF16)
        va_ref[rows, :] = va.astype(BF16)
        kb_ref[rows, :] = kb.astype(BF16)
        vb_ref[rows, :] = vb.astype(BF16)
        if tail_scratch:
            for ref, val in zip(tail_scratch, (ka, va, kb, vb)):
                store_tail(ref, val, r0)
        else:
            store_heads(ka32_ref, ka, r0)
            store_heads(va32_ref, va, r0)
            store_tail(kb32_ref, kb, r0)
            store_tail(vb32_ref, vb, r0)

    if tail_scratch:
        @pl.when(pl.program_id(1) == pl.num_programs(1) - 1)
        def _():
            for out_ref, ref in zip((ka32_ref, va32_ref, kb32_ref, vb32_ref), tail_scratch):
                out_ref[...] = ref[...].T


def _resident(shape):
    return pl.BlockSpec(shape, lambda *_: (0,) * len(shape), pipeline_mode=pl.Buffered(1))


def _ffn1_qkv(x, mod, gains, wg, wu, wd, win, gt, rt):
    n_groups, rows, _ = x.shape
    n_g, n_r = n_groups // gt, rows // rt
    tm = gt * rt
    tokens = n_groups * rows
    ta = gt * min(REACH_A, rows)
    tb = gt * min(REACH_B, rows)
    tile = lambda gi, ri: (gi * n_r + ri, 0)
    tail = lambda gi, ri: (gi, 0)
    tok = lambda w, dt: jax.ShapeDtypeStruct((tokens, w), dt)
    if gt == 1:
        assert ta <= tm and tb <= tm
        tail_specs = [pl.BlockSpec((None, w, t), lambda gi, ri: (gi, 0, 0))
                      for w, t in ((WIDTH, ta), (WIDTH, ta), (KV_WIDTH_B, tb), (KV_WIDTH_B, tb))]
        tail_shapes = [jax.ShapeDtypeStruct((n_g, w, t), F32)
                       for w, t in ((WIDTH, ta), (WIDTH, ta), (KV_WIDTH_B, tb), (KV_WIDTH_B, tb))]
        scratch = [pltpu.VMEM((ta, WIDTH), F32), pltpu.VMEM((ta, WIDTH), F32),
                   pltpu.VMEM((tb, KV_WIDTH_B), F32), pltpu.VMEM((tb, KV_WIDTH_B), F32)]
    else:
        assert n_r == 1
        tail_specs = [pl.BlockSpec((tm * N_HEADS, HEAD_DIM), tail), pl.BlockSpec((tm * N_HEADS, HEAD_DIM), tail),
                      pl.BlockSpec((tm, KV_WIDTH_B), tail), pl.BlockSpec((tm, KV_WIDTH_B), tail)]
        tail_shapes = [jax.ShapeDtypeStruct((tokens * N_HEADS, HEAD_DIM), F32),
                       jax.ShapeDtypeStruct((tokens * N_HEADS, HEAD_DIM), F32),
                       jax.ShapeDtypeStruct((tokens, KV_WIDTH_B), F32), jax.ShapeDtypeStruct((tokens, KV_WIDTH_B), F32)]
        scratch = []
    vmem = (2 * 3 * D_MODEL * D_FF + 2 * D_MODEL * win.shape[1]
            + 2 * 2 * 4 * tm * D_MODEL
            + 2 * 2 * tm * (6 * WIDTH + 2 * KV_WIDTH_B)
            + 2 * 4 * (2 * ta * WIDTH + 2 * tb * KV_WIDTH_B)
            + tm * (2 * 4 + 2) * FF_SLAB_MAX + 3 * 4 * tm * D_MODEL
            + 4 * tm * win.shape[1])
    return pl.pallas_call(
        _ffn1_qkv_kernel,
        grid=(n_g, n_r),
        in_specs=[
            pl.BlockSpec((gt, rt, D_MODEL), lambda gi, ri: (gi, ri, 0)),
            pl.BlockSpec((N_MOD, gt, 1, D_MODEL), lambda gi, ri: (0, gi, 0, 0)),
            _resident(gains.shape),
            _resident(wg.shape), _resident(wu.shape), _resident(wd.shape), _resident(win.shape),
        ],
        out_specs=[
            pl.BlockSpec((gt, rt, D_MODEL), lambda gi, ri: (gi, ri, 0)),
            pl.BlockSpec((tm, 2 * WIDTH), tile), pl.BlockSpec((tm, WIDTH), tile), pl.BlockSpec((tm, WIDTH), tile),
            pl.BlockSpec((tm, 2 * WIDTH), tile), pl.BlockSpec((tm, KV_WIDTH_B), tile),
            pl.BlockSpec((tm, KV_WIDTH_B), tile),
            *tail_specs,
        ],
        out_shape=[
            jax.ShapeDtypeStruct(x.shape, F32),
            tok(2 * WIDTH, BF16), tok(WIDTH, BF16), tok(WIDTH, BF16),
            tok(2 * WIDTH, BF16), tok(KV_WIDTH_B, BF16), tok(KV_WIDTH_B, BF16),
            *tail_shapes,
        ],
        scratch_shapes=scratch,
        compiler_params=pltpu.CompilerParams(
            dimension_semantics=("arbitrary", "arbitrary"), vmem_limit_bytes=_vmem_limit(vmem)),
        name="ffn1_qkv",
    )(x, mod, gains, wg, wu, wd, win)


def _out_ffn2_kernel(x_ref, o_ref, mod_ref, gains_ref, gg_ref, wout_ref, wg_ref, wu_ref, wd_ref, y_ref):
    gt, rt, _ = x_ref.shape
    gain = lambda k: gains_ref[k:k + 1, :]
    gg = gg_ref[...]
    subs = _sub_tiles(gt, rt)
    mods = [lambda k, gs=gs: mod_ref[k, gs] for gs, _, _, _ in subs]
    xs = [x_ref[gs, rs, :] for gs, rs, _, _ in subs]
    ons = []
    for _, _, r0, n in subs:
        o = o_ref[r0:r0 + n, :].astype(F32)
        on = jnp.concatenate([_rms(o[:, :WIDTH], gg[:, :WIDTH]), _rms(o[:, WIDTH:], gg[:, WIDTH:])], axis=1)
        ons.append(on.astype(BF16))
    mixed = [jnp.dot(on, wout_ref[...], preferred_element_type=F32).reshape(x.shape) for on, x in zip(ons, xs)]
    x2s = [x + mod(5) * _rms(m, gain(3)) for x, m, mod in zip(xs, mixed, mods)]
    hs = [_adaln(x2, gain(4), mod(6), mod(7)).reshape(n, D_MODEL).astype(BF16)
          for x2, mod, (_, _, _, n) in zip(x2s, mods, subs)]
    fs = [_swiglu(h, wg_ref, wu_ref, wd_ref).reshape(x.shape) for h, x in zip(hs, xs)]
    for x2, f, mod, (gs, rs, _, _) in zip(x2s, fs, mods, subs):
        y_ref[gs, rs, :] = x2 + (FFN_RES * mod(8)) * _rms(f, gain(5))


def _out_ffn2(x1, o, mod, gains, gg, wout, wg, wu, wd, gt, rt):
    n_groups, rows, _ = x1.shape
    n_g, n_r = n_groups // gt, rows // rt
    tm = gt * rt
    vmem = (2 * 3 * D_MODEL * D_FF + 2 * wout.size
            + 2 * 2 * 4 * tm * D_MODEL + 2 * 2 * tm * 2 * WIDTH
            + tm * (2 * 4 + 2) * FF_SLAB_MAX + 5 * 4 * tm * D_MODEL)
    return pl.pallas_call(
        _out_ffn2_kernel,
        grid=(n_g, n_r),
        in_specs=[
            pl.BlockSpec((gt, rt, D_MODEL), lambda gi, ri: (gi, ri, 0)),
            pl.BlockSpec((tm, 2 * WIDTH), lambda gi, ri: (gi * n_r + ri, 0)),
            pl.BlockSpec((N_MOD, gt, 1, D_MODEL), lambda gi, ri: (0, gi, 0, 0)),
            _resident(gains.shape), _resident(gg.shape), _resident(wout.shape),
            _resident(wg.shape), _resident(wu.shape), _resident(wd.shape),
        ],
        out_specs=pl.BlockSpec((gt, rt, D_MODEL), lambda gi, ri: (gi, ri, 0)),
        out_shape=jax.ShapeDtypeStruct(x1.shape, F32),
        compiler_params=pltpu.CompilerParams(
            dimension_semantics=("arbitrary", "arbitrary"), vmem_limit_bytes=_vmem_limit(vmem)),
        name="out_ffn2",
    )(x1, o, mod, gains, gg, wout, wg, wu, wd)


def _softmax_terms(s, sink=None):
    m = jnp.max(s, axis=1, keepdims=True)
    if sink is not None:
        m = jnp.maximum(m, sink)
    e = jnp.exp2(s - m)
    denom = jnp.sum(e, axis=1, keepdims=True)
    if sink is not None:
        denom = denom + jnp.exp2(sink - m)
    return e.astype(BF16), 1.0 / denom


def _scores(q, k):
    return lax.dot_general(q, k, (((1,), (1,)), ((), ())), preferred_element_type=F32)


def _unstack_pair(z):
    n_q = z.shape[0] // 2
    is_lo = lax.broadcasted_iota(jnp.int32, (n_q, z.shape[1]), 1) < HEAD_DIM
    return jnp.where(is_lo, z[:n_q], z[n_q:])


def _sink_column(sinks_ref, pair, n_q):
    row = lax.broadcasted_iota(jnp.int32, (2 * n_q, 1), 0)
    return jnp.where(row < n_q, sinks_ref[pair], sinks_ref[N_PAIRS + pair]) * LOG2E


GROUPS_PER_STEP = 2


def _attn_prompt_kernel(sinks_ref, qa_ref, qb_ref, ka_ref, va_ref, kb_ref, vb_ref, *rest):
    ba_refs, bb_refs = rest[:GROUPS_PER_STEP], rest[GROUPS_PER_STEP:2 * GROUPS_PER_STEP]
    n_cast = (len(rest) - 2 * GROUPS_PER_STEP - 1) // 2
    o_ref = rest[2 * GROUPS_PER_STEP + n_cast]
    for src, dst in zip(rest[2 * GROUPS_PER_STEP:2 * GROUPS_PER_STEP + n_cast], rest[len(rest) - n_cast:]):
        dst[...] = src[...].astype(BF16)
    slab = lambda p: slice(p * V7X_LANES, (p + 1) * V7X_LANES)
    bands = []
    for j in range(GROUPS_PER_STEP):
        g = pl.program_id(1) * GROUPS_PER_STEP + j
        start_a = pl.multiple_of(jnp.maximum(g * GROUP_Q - REACH_A, 0), GROUP_Q)
        start_b = pl.multiple_of(jnp.maximum(g * GROUP_Q - REACH_B, 0), GROUP_Q)
        bands.append((pl.ds(start_a, BAND_A), pl.ds(start_b, BAND_B)))

    def stacked(q_ref, j, p):
        rows = slice(j * GROUP_Q, (j + 1) * GROUP_Q)
        return jnp.concatenate([q_ref[rows, slab(p)], q_ref[rows, slab(N_PAIRS + p)]], axis=0)

    def score(mixer, j, p):
        if mixer == "a":
            bias = jnp.concatenate([ba_refs[j][0, 2 * p], ba_refs[j][0, 2 * p + 1]], axis=0)
            return _scores(stacked(qa_ref, j, p), ka_ref[bands[j][0], slab(p)]) + bias
        bias = jnp.concatenate([bb_refs[j][0, p], bb_refs[j][0, N_PAIRS + p]], axis=0)
        return _scores(stacked(qb_ref, j, p), kb_ref[bands[j][1], :]) + bias

    def weights(s, sink):
        m = jnp.max(s, axis=1, keepdims=True)
        if sink is None:
            return jnp.exp2(s - m).astype(BF16), None
        m = jnp.maximum(m, sink)
        return jnp.exp2(s - m).astype(BF16), jnp.exp2(sink - m)

    def attend(e, extra, v):
        z = jnp.dot(e, jnp.concatenate([v, jnp.ones_like(v)], axis=1), preferred_element_type=F32)
        denom = z[:, V7X_LANES:] if extra is None else z[:, V7X_LANES:] + extra
        return _unstack_pair(z[:, :V7X_LANES] * (1.0 / denom))

    jobs = [(mixer, j, p) for mixer in ("a", "b") for j in range(GROUPS_PER_STEP) for p in range(N_PAIRS)]
    s_all = [score(*job) for job in jobs]
    w_all = [weights(s, _sink_column(sinks_ref, p, GROUP_Q) if mixer == "b" else None)
             for s, (mixer, j, p) in zip(s_all, jobs)]
    for (e, extra), (mixer, j, p) in zip(w_all, jobs):
        rows = slice(j * GROUP_Q, (j + 1) * GROUP_Q)
        if mixer == "a":
            o_ref[rows, slab(p)] = attend(e, extra, va_ref[bands[j][0], slab(p)]).astype(BF16)
        else:
            o_ref[rows, slab(N_PAIRS + p)] = attend(e, extra, vb_ref[bands[j][1], :]).astype(BF16)


def _attn_prompt(sinks, qa, qb, ka, va, kb, vb, bias_a, bias_b, batch, seq, cast=()):
    rows = GROUPS_PER_STEP * GROUP_Q
    n_steps = seq // rows
    cast_specs, cast_shapes = (zip(*[_cast_chunks(w, batch * n_steps, lambda b, s: b * n_steps + s) for w in cast])
                               if cast else ((), ()))
    q_spec = pl.BlockSpec((rows, 2 * WIDTH), lambda b, s: (b * n_steps + s, 0))
    kv_spec = lambda w: pl.BlockSpec((seq, w), lambda b, s: (b, 0), pipeline_mode=pl.Buffered(1))
    bias_spec = lambda band, n_var, j: pl.BlockSpec(
        (1, N_HEADS, GROUP_Q, band),
        lambda b, s: (jnp.minimum(s * GROUPS_PER_STEP + j, n_var - 1), 0, 0, 0))
    vmem = (2 * seq * (2 * WIDTH + 2 * KV_WIDTH_B)
            + GROUPS_PER_STEP * 2 * 4 * N_HEADS * GROUP_Q * (BAND_A + BAND_B)
            + 2 * 2 * rows * 6 * WIDTH + GROUPS_PER_STEP * 6 * 4 * 2 * GROUP_Q * BAND_A
            + sum(2 * 6 * math.prod(spec.block_shape) for spec in cast_specs))
    out = pl.pallas_call(
        _attn_prompt_kernel,
        grid=(batch, n_steps),
        in_specs=[
            pl.BlockSpec(memory_space=pltpu.SMEM),
            q_spec, q_spec,
            kv_spec(WIDTH), kv_spec(WIDTH), kv_spec(KV_WIDTH_B), kv_spec(KV_WIDTH_B),
            *[bias_spec(BAND_A, N_VAR_A, j) for j in range(GROUPS_PER_STEP)],
            *[bias_spec(BAND_B, N_VAR_B, j) for j in range(GROUPS_PER_STEP)],
            *cast_specs,
        ],
        out_specs=[pl.BlockSpec((rows, 2 * WIDTH), lambda b, s: (b * n_steps + s, 0)), *cast_specs],
        out_shape=[jax.ShapeDtypeStruct((batch * seq, 2 * WIDTH), BF16), *cast_shapes],
        compiler_params=pltpu.CompilerParams(
            dimension_semantics=("arbitrary", "arbitrary"), vmem_limit_bytes=_vmem_limit(vmem)),
        name="attn_prompt",
    )(sinks, qa, qb, ka, va, kb, vb, *([bias_a] * GROUPS_PER_STEP), *([bias_b] * GROUPS_PER_STEP), *cast)
    return out[0], tuple(out[1:])


def _pair_ordered_cols(h):
    return (h % N_PAIRS) * V7X_LANES + (h // N_PAIRS) * HEAD_DIM


SAMPLE_ROWS_PER_STEP = 4


def _attn_sample_kernel(sinks_ref, qa_ref, qb_ref, ka_ref, va_ref, kb_ref, vb_ref,
                        cak_ref, cav_ref, cbk_ref, cbv_ref, ba_ref, bb_ref, o_ref):
    n_rows = cak_ref.shape[0]
    n_q = qa_ref.shape[0] // n_rows
    n_kv = cbk_ref.shape[1]
    per_kv = N_HEADS // n_kv
    la, lb = cak_ref.shape[3], cbk_ref.shape[3]
    head = lambda ref, r, idx: ref[r * n_q:(r + 1) * n_q, idx * HEAD_DIM:(idx + 1) * HEAD_DIM]

    def query(ref, r, lo):
        lo = lo if lo % V7X_LANES < HEAD_DIM else WIDTH + lo
        return ref[r * n_q:(r + 1) * n_q, lo:lo + HEAD_DIM]

    def scores(q, cache_t_ref, new_ref, r, idx):
        old = jnp.dot(q, cache_t_ref[r, idx].astype(BF16), preferred_element_type=F32)
        return jnp.concatenate([old, _scores(q, head(new_ref, r, idx))], axis=1)

    def values(e, cache_t_ref, new_ref, r, idx, n_old):
        old = _scores(e[:, :n_old], cache_t_ref[r, idx].astype(BF16))
        return old + jnp.dot(e[:, n_old:], head(new_ref, r, idx), preferred_element_type=F32)

    jobs_a = [(r, h) for r in range(n_rows) for h in range(N_HEADS)]
    jobs_b = [(r, c) for r in range(n_rows) for c in range(n_kv)]
    s_a = [scores(query(qa_ref, r, h * HEAD_DIM), cak_ref, ka_ref, r, h) + ba_ref[h] for r, h in jobs_a]
    s_b = []
    for r, c in jobs_b:
        heads = range(c * per_kv, (c + 1) * per_kv)
        q = jnp.concatenate([query(qb_ref, r, _pair_ordered_cols(h)) for h in heads], axis=0)
        bias = jnp.concatenate([bb_ref[h] for h in heads], axis=0)
        s_b.append(scores(q, cbk_ref, kb_ref, r, c) + bias)

    row = lax.broadcasted_iota(jnp.int32, (per_kv * n_q, 1), 0)
    sinks = []
    for c in range(n_kv):
        sink = jnp.zeros((per_kv * n_q, 1), F32)
        for i in range(per_kv):
            sink = jnp.where((row >= i * n_q) & (row < (i + 1) * n_q), sinks_ref[c * per_kv + i], sink)
        sinks.append(sink * LOG2E)
    p_a = [_softmax_terms(s) for s in s_a]
    p_b = [_softmax_terms(s, sinks[c]) for s, (_, c) in zip(s_b, jobs_b)]

    out_a = {job: values(e, cav_ref, va_ref, *job, la) * inv for job, (e, inv) in zip(jobs_a, p_a)}
    out_b = {}
    for (r, c), (e, inv) in zip(jobs_b, p_b):
        o = values(e, cbv_ref, vb_ref, r, c, lb) * inv
        for i in range(per_kv):
            out_b[(r, c * per_kv + i)] = o[i * n_q:(i + 1) * n_q]
    order_b = [h for p in range(N_PAIRS) for h in (p, N_PAIRS + p)]
    for r in range(n_rows):
        pieces = [out_a[(r, h)] for h in range(N_HEADS)] + [out_b[(r, h)] for h in order_b]
        o_ref[r * n_q:(r + 1) * n_q, :] = jnp.concatenate(pieces, axis=1).astype(BF16)


def _attn_sample(sinks, qa, qb, ka, va, kb, vb, cak, cav, cbk, cbv, bias_a, bias_b, batch, s_len):
    n_rows = SAMPLE_ROWS_PER_STEP
    row = lambda w: pl.BlockSpec((n_rows * s_len, w), lambda b: (b, 0))
    cache = lambda a: pl.BlockSpec((n_rows,) + a.shape[1:], lambda b: (b, 0, 0, 0))
    whole = lambda a: pl.BlockSpec(a.shape, lambda b: (0,) * a.ndim)
    return pl.pallas_call(
        _attn_sample_kernel,
        grid=(batch // n_rows,),
        in_specs=[
            pl.BlockSpec(memory_space=pltpu.SMEM),
            row(2 * WIDTH), row(2 * WIDTH), row(WIDTH), row(WIDTH), row(KV_WIDTH_B), row(KV_WIDTH_B),
            cache(cak), cache(cav), cache(cbk), cache(cbv),
            whole(bias_a), whole(bias_b),
        ],
        out_specs=pl.BlockSpec((n_rows * s_len, 2 * WIDTH), lambda b: (b, 0)),
        out_shape=jax.ShapeDtypeStruct((batch * s_len, 2 * WIDTH), BF16),
        compiler_params=pltpu.CompilerParams(dimension_semantics=("arbitrary",)),
        name="attn_sample",
    )(sinks, qa, qb, ka, va, kb, vb, cak, cav, cbk, cbv, bias_a, bias_b)


def _pair_order(x, axis):
    shape = x.shape
    split = shape[:axis] + (N_HEADS // N_PAIRS, N_PAIRS, HEAD_DIM) + shape[axis + 1:]
    return jnp.swapaxes(x.reshape(split), axis, axis + 1).reshape(shape)


def _prep_out_weights(w_out, group_gains):
    wout = jnp.concatenate([w_out[:WIDTH], _pair_order(w_out[WIDTH:], 0)], axis=0).astype(BF16)
    gg = jnp.concatenate([group_gains[:WIDTH], _pair_order(group_gains[WIDTH:], 0)])
    return wout, gg.reshape(1, 2 * WIDTH)


def kernel(x_prompt, x_sample, cache_a_k, cache_a_v, cache_b_k, cache_b_v, c_prompt, c_sample, w_mod, b_mod,
           norm_gains, w1_gate, w1_up, w1_down, w_in, w_out, group_gains, rel_bias_a, t5_bias_table, sinks_b,
           w2_gate, w2_up, w2_down):
    depth = w_mod.shape[0]
    assert depth == 1
    batch, seq, _ = x_prompt.shape
    s_batch, s_len, _ = x_sample.shape
    la, lb = cache_a_k.shape[2], cache_b_k.shape[2]
    assert la == REACH_A and lb == REACH_B and seq % 512 == 0

    c_rows = jnp.concatenate([c_prompt, c_sample], axis=0)
    pad_rows = (-c_rows.shape[0]) % 8
    c_rows = jnp.pad(c_rows, ((0, pad_rows), (0, 0)))
    mod = _modulation(c_rows, w_mod[0], b_mod[0])
    mod_p = mod[:, :batch].reshape(N_MOD, batch, 1, D_MODEL)
    mod_s = mod[:, batch:batch + s_batch].reshape(N_MOD, s_batch, 1, D_MODEL)

    bias_a, bias_b, bias_sa, bias_sb, *w1, win = _bias_tables(
        rel_bias_a[0], t5_bias_table, s_len, la, lb, cast=(w1_gate[0], w1_up[0], w1_down[0], w_in[0]))

    gains = norm_gains[0]
    wout, gg = _prep_out_weights(w_out[0], group_gains[0])
    sinks = sinks_b[0]

    rt_p = 512
    (x1p, qa, ka, va, qb, kb, vb, ka32, va32, kb32, vb32) = _ffn1_qkv(
        x_prompt, mod_p, gains, *w1, win, 1, rt_p)
    op, w2 = _attn_prompt(sinks, qa, qb, ka, va, kb, vb, bias_a, bias_b, batch, seq,
                          cast=(w2_gate[0], w2_up[0], w2_down[0]))
    y_prompt = _out_ffn2(x1p, op, mod_p, gains, gg, wout, *w2, 1, 2 * rt_p)
    rows_last = lambda t: jnp.transpose(t.reshape(depth, batch, -1, HEAD_DIM, t.shape[-1]), (0, 1, 4, 2, 3))
    new_p = tuple(rows_last(t) for t in (ka32, va32, kb32, vb32))

    gt_s = 16
    (x1s, qas, kas, vas, qbs, kbs, vbs, kas32, vas32, kbs32, vbs32) = _ffn1_qkv(
        x_sample, mod_s, gains, *w1, win, gt_s, s_len)
    os_ = _attn_sample(sinks, qas, qbs, kas, vas, kbs, vbs,
                       *(jnp.transpose(c[0], (0, 2, 3, 1)) for c in (cache_a_k, cache_a_v, cache_b_k, cache_b_v)),
                       bias_sa, bias_sb, s_batch, s_len)
    y_sample = _out_ffn2(x1s, os_, mod_s, gains, gg, wout, *w2, gt_s, s_len)
    new_s = (kas32.reshape(depth, s_batch, s_len, N_HEADS, HEAD_DIM),
             vas32.reshape(depth, s_batch, s_len, N_HEADS, HEAD_DIM),
             kbs32.reshape(depth, s_batch, s_len, 2, HEAD_DIM), vbs32.reshape(depth, s_batch, s_len, 2, HEAD_DIM))

    return (y_prompt, y_sample) + new_p + new_s
```

```python
import functools
import math

import jax
import jax.numpy as jnp
from jax import lax
from jax.experimental import pallas as pl
from jax.experimental.pallas import tpu as pltpu

F32 = jnp.float32
BF16 = jnp.bfloat16

D_MODEL = 1024
D_FF = 2816
CHUNK = 64
HEAD_DIM = 64
N_HEADS = 8
N_PAIRS = N_HEADS // 2
WIDTH = N_HEADS * HEAD_DIM
KV_WIDTH_B = 2 * HEAD_DIM
REACH_A = 8 * CHUNK
REACH_B = 2 * CHUNK
REL_CLIP_A = 128
N_BUCKETS = 32
N_MOD = 9
FFN_RES = 0.5
EPS = 1e-6
SCALE = HEAD_DIM ** -0.5
LOG2E = 1.4426950408889634
NEG_INF = -1e30

V7X_LANES = 128
V7X_VMEM_BYTES = 64 * 1024 * 1024
V7X_VMEM_LIMIT_CAP = 56 * 1024 * 1024

GROUP_Q = 2 * CHUNK
BAND_A = REACH_A + GROUP_Q
BAND_B = REACH_B + GROUP_Q
ORIGIN_A = 1024
WIDTH_MASTER_A = ORIGIN_A + BAND_A
ORIGIN_B = 256
WIDTH_MASTER_B = ORIGIN_B + BAND_B
N_VAR_A = REACH_A // GROUP_Q + 1
N_VAR_B = REACH_B // GROUP_Q + 1


def _vmem_limit(estimate_bytes):
    return int(min(max(estimate_bytes, 32 * 1024 * 1024), V7X_VMEM_LIMIT_CAP))


def _rms(x, gain):
    ms = jnp.mean(x * x, axis=-1, keepdims=True)
    return x * lax.rsqrt(ms + EPS) * gain


def _adaln(x, gain, shift, scale):
    return _rms(x, gain) * (1.0 + scale) + shift


def _silu(x):
    return x * (1.0 / (1.0 + jnp.exp(-x)))


def _mod_kernel(c_ref, w_ref, b_ref, o_ref):
    s = _silu(c_ref[...]).astype(BF16)
    o_ref[0] = jnp.dot(s, w_ref[...].astype(BF16), preferred_element_type=F32) + b_ref[0]


def _modulation(c_rows, w_mod, b_mod):
    rows = c_rows.shape[0]
    return pl.pallas_call(
        _mod_kernel,
        grid=(N_MOD,),
        in_specs=[
            pl.BlockSpec((rows, D_MODEL), lambda j: (0, 0)),
            pl.BlockSpec((D_MODEL, D_MODEL), lambda j: (0, j)),
            pl.BlockSpec((1, 1, D_MODEL), lambda j: (j, 0, 0)),
        ],
        out_specs=pl.BlockSpec((1, rows, D_MODEL), lambda j: (j, 0, 0)),
        out_shape=jax.ShapeDtypeStruct((N_MOD, rows, D_MODEL), F32),
        name="modulation",
    )(c_rows, w_mod, b_mod.reshape(N_MOD, 1, D_MODEL))


def _skew(base_row, n_rows):
    width = base_row.shape[1]
    m = jnp.broadcast_to(base_row, (n_rows, width))
    row = lax.broadcasted_iota(jnp.int32, (n_rows, width), 0)
    shift = 1
    while shift < n_rows:
        m = jnp.where((row & shift) != 0, pltpu.roll(m, shift, axis=1), m)
        shift *= 2
    return m


def _band_mask(n_rows, width, origin, n_prev):
    row = lax.broadcasted_iota(jnp.int32, (n_rows, width), 0)
    col = lax.broadcasted_iota(jnp.int32, (n_rows, width), 1)
    q_chunk = row >> 6
    k_chunk = (col >> 6) - (origin // CHUNK)
    return (k_chunk >= q_chunk - n_prev) & (k_chunk <= q_chunk)


def _t5_bucket(rel):
    half = N_BUCKETS // 2
    max_exact = half // 2
    n = jnp.abs(rel)
    n2 = n * n
    large = jnp.full(rel.shape, max_exact, jnp.int32)
    for k in range(1, half - max_exact):
        large = large + jnp.where(n2 >= (max_exact * max_exact) * (2 ** k), 1, 0)
    return jnp.where(rel > 0, half, 0) + jnp.where(n < max_exact, n, large)


def _bias_kernel(rel_ref, t5_ref, *refs, s_len, la, lb):
    n_cast = (len(refs) - 4) // 2
    for src, dst in zip(refs[:n_cast], refs[n_cast + 4:]):
        dst[...] = src[...].astype(BF16)
    ba_ref, bb_ref, bsa_ref, bsb_ref = refs[n_cast:n_cast + 4]

    tab = rel_ref[0] * LOG2E
    n_tab = 2 * REL_CLIP_A + 1
    first, last = tab[:, 0:1], tab[:, n_tab - 1:n_tab]
    win_lo = ORIGIN_A - 2 * REL_CLIP_A
    col = lax.broadcasted_iota(jnp.int32, (1, tab.shape[1] + V7X_LANES), 1)
    shifted = jnp.concatenate([jnp.zeros((1, V7X_LANES), F32), tab], axis=1)
    u_win = jnp.where(col < V7X_LANES, first, jnp.where(col >= V7X_LANES + n_tab, last, shifted))
    window = _skew(u_win, GROUP_Q)[:, V7X_LANES:]
    n_left = win_lo + V7X_LANES
    n_right = WIDTH_MASTER_A - n_left - window.shape[1]
    skew_a = jnp.concatenate([jnp.broadcast_to(first, (GROUP_Q, n_left)), window,
                              jnp.broadcast_to(last, (GROUP_Q, n_right))], axis=1)
    master_a = jnp.where(_band_mask(GROUP_Q, WIDTH_MASTER_A, ORIGIN_A, REACH_A // CHUNK), skew_a, NEG_INF)
    for v in range(N_VAR_A):
        start = ORIGIN_A - min(v * GROUP_Q, REACH_A) if v < N_VAR_A - 1 else ORIGIN_A - REACH_A
        ba_ref[v, 0] = master_a[:, start:start + BAND_A]
    bsa_ref[0] = skew_a[:s_len, ORIGIN_A - la:ORIGIN_A + s_len]

    t5 = t5_ref[0] * LOG2E
    rel = lax.broadcasted_iota(jnp.int32, (1, WIDTH_MASTER_B), 1) - ORIGIN_B
    bucket = _t5_bucket(rel)
    u_b = jnp.zeros((1, WIDTH_MASTER_B), F32)
    for i in range(N_BUCKETS):
        u_b = jnp.where(bucket == i, t5[:, i:i + 1], u_b)
    skew_b = _skew(u_b, GROUP_Q)
    master_b = jnp.where(_band_mask(GROUP_Q, WIDTH_MASTER_B, ORIGIN_B, REACH_B // CHUNK), skew_b, NEG_INF)
    for v in range(N_VAR_B):
        start = ORIGIN_B - min(v * GROUP_Q, REACH_B) if v < N_VAR_B - 1 else ORIGIN_B - REACH_B
        bb_ref[v, 0] = master_b[:, start:start + BAND_B]
    bsb_ref[0] = skew_b[:s_len, ORIGIN_B - lb:ORIGIN_B + s_len]


V7X_BF16_SUBLANES = 16


def _cast_chunks(w, n_steps, step_of):
    rows, cols = w.shape
    n_chunks = n_steps
    while rows % (n_chunks * V7X_BF16_SUBLANES):
        n_chunks //= 2
    spec = pl.BlockSpec((rows // n_chunks, cols), lambda *g: (jnp.minimum(step_of(*g), n_chunks - 1), 0))
    return spec, jax.ShapeDtypeStruct(w.shape, BF16)


def _bias_tables(rel_bias_a, t5_table, s_len, la, lb, cast=()):
    n_tab = 2 * REL_CLIP_A + 1
    tab_w = 3 * V7X_LANES
    rel_p = jnp.pad(rel_bias_a, ((0, 0), (0, tab_w - n_tab))).reshape(N_HEADS, 1, tab_w)
    t5_p = t5_table.reshape(N_HEADS, 1, N_BUCKETS)
    kern = functools.partial(_bias_kernel, s_len=s_len, la=la, lb=lb)
    cast_specs, cast_shapes = zip(*[_cast_chunks(w, N_HEADS, lambda h: h) for w in cast]) if cast else ((), ())
    return pl.pallas_call(
        kern,
        grid=(N_HEADS,),
        in_specs=[
            pl.BlockSpec((1, 1, tab_w), lambda h: (h, 0, 0)),
            pl.BlockSpec((1, 1, N_BUCKETS), lambda h: (h, 0, 0)),
            *cast_specs,
        ],
        out_specs=[
            pl.BlockSpec((N_VAR_A, 1, GROUP_Q, BAND_A), lambda h: (0, h, 0, 0)),
            pl.BlockSpec((N_VAR_B, 1, GROUP_Q, BAND_B), lambda h: (0, h, 0, 0)),
            pl.BlockSpec((1, s_len, la + s_len), lambda h: (h, 0, 0)),
            pl.BlockSpec((1, s_len, lb + s_len), lambda h: (h, 0, 0)),
            *cast_specs,
        ],
        out_shape=[
            jax.ShapeDtypeStruct((N_VAR_A, N_HEADS, GROUP_Q, BAND_A), F32),
            jax.ShapeDtypeStruct((N_VAR_B, N_HEADS, GROUP_Q, BAND_B), F32),
            jax.ShapeDtypeStruct((N_HEADS, s_len, la + s_len), F32),
            jax.ShapeDtypeStruct((N_HEADS, s_len, lb + s_len), F32),
            *cast_shapes,
        ],
        compiler_params=pltpu.CompilerParams(dimension_semantics=("arbitrary",)),
        name="bias_tables",
    )(rel_p, t5_p, *cast)


V7X_MXU_DIM = 256
FF_SLABS = ((0, 6 * V7X_MXU_DIM), (6 * V7X_MXU_DIM, D_FF))
FF_SLAB_MAX = max(hi - lo for lo, hi in FF_SLABS)


def _swiglu(h, wg_ref, wu_ref, wd_ref):
    acts = []
    for lo, hi in FF_SLABS:
        g = jnp.dot(h, wg_ref[:, lo:hi], preferred_element_type=F32)
        u = jnp.dot(h, wu_ref[:, lo:hi], preferred_element_type=F32)
        acts.append((_silu(g) * u).astype(BF16))
    acc = None
    for (lo, hi), a in zip(FF_SLABS, acts):
        part = jnp.dot(a, wd_ref[lo:hi, :], preferred_element_type=F32)
        acc = part if acc is None else acc + part
    return acc


SUB_TOKENS = V7X_MXU_DIM


def _sub_tiles(gt, rt):
    n_sub = gt * rt // SUB_TOKENS
    if gt == 1:
        step = rt // n_sub
        return [(slice(0, 1), slice(s * step, (s + 1) * step), s * step, step) for s in range(n_sub)]
    step = gt // n_sub
    return [(slice(s * step, (s + 1) * step), slice(0, rt), s * step * rt, step * rt) for s in range(n_sub)]


def _ffn1_qkv_kernel(x_ref, mod_ref, gains_ref, wg_ref, wu_ref, wd_ref, win_ref,
                     x1_ref, qa_ref, ka_ref, va_ref, qb_ref, kb_ref, vb_ref,
                     ka32_ref, va32_ref, kb32_ref, vb32_ref, *tail_scratch):
    gt, rt, _ = x_ref.shape
    tm = gt * rt
    gain = lambda k: gains_ref[k:k + 1, :]

    def store_tail(ref, val, r0):
        first = tm - ref.shape[0]
        lo = max(r0, first)
        if lo < r0 + val.shape[0]:
            ref[lo - first:r0 + val.shape[0] - first, :] = val[lo - r0:, :]

    def store_heads(ref, val, r0):
        n_heads = val.shape[1] // HEAD_DIM
        for h in range(n_heads):
            ref[pl.ds(r0 * n_heads + h, val.shape[0], stride=n_heads), :] = val[:, h * HEAD_DIM:(h + 1) * HEAD_DIM]

    def store_split(ref, rows, q, pairs):
        q = q * (SCALE * LOG2E)
        head = lambda h: q[:, h * HEAD_DIM:(h + 1) * HEAD_DIM]
        zeros = jnp.zeros((q.shape[0], HEAD_DIM), F32)
        ref[rows, :WIDTH] = jnp.concatenate([t for a, _ in pairs for t in (head(a), zeros)], axis=1).astype(BF16)
        ref[rows, WIDTH:] = jnp.concatenate([t for _, b in pairs for t in (zeros, head(b))], axis=1).astype(BF16)

    pairs_a = [(2 * p, 2 * p + 1) for p in range(N_PAIRS)]
    pairs_b = [(p, N_PAIRS + p) for p in range(N_PAIRS)]

    subs = _sub_tiles(gt, rt)
    mods = [lambda k, gs=gs: mod_ref[k, gs] for gs, _, _, _ in subs]
    xs = [x_ref[gs, rs, :] for gs, rs, _, _ in subs]
    hs = [_adaln(x, gain(0), mod(0), mod(1)).reshape(n, D_MODEL).astype(BF16)
          for x, mod, (_, _, _, n) in zip(xs, mods, subs)]
    fs = [_swiglu(h, wg_ref, wu_ref, wd_ref).reshape(x.shape) for h, x in zip(hs, xs)]
    h2s = []
    for x, f, mod, (gs, rs, _, n) in zip(xs, fs, mods, subs):
        x1 = x + (FFN_RES * mod(2)) * _rms(f, gain(1))
        x1_ref[gs, rs, :] = x1
        h2s.append(_adaln(x1, gain(2), mod(3), mod(4)).reshape(n, D_MODEL).astype(BF16))

    for h2, (_, _, r0, n) in zip(h2s, subs):
        rows = slice(r0, r0 + n)

        def proj(lo, width):
            return jnp.dot(h2, win_ref[:, lo:lo + width], preferred_element_type=F32)

        store_split(qa_ref, rows, proj(0, WIDTH), pairs_a)
        ka = proj(WIDTH, WIDTH)
        va = proj(2 * WIDTH, WIDTH)
        store_split(qb_ref, rows, proj(3 * WIDTH, WIDTH), pairs_b)
        kvb = proj(4 * WIDTH, 2 * KV_WIDTH_B)
        kb, vb = kvb[:, :KV_WIDTH_B], kvb[:, KV_WIDTH_B:]
        ka_ref[rows, :] = ka.astype(BF16)
        va_ref[rows, :] = va.astype(BF16)
        kb_ref[rows, :] = kb.astype(BF16)
        vb_ref[rows, :] = vb.astype(BF16)
        if tail_scratch:
            for ref, val in zip(tail_scratch, (ka, va, kb, vb)):
                store_tail(ref, val, r0)
        else:
            store_heads(ka32_ref, ka, r0)
            store_heads(va32_ref, va, r0)
            store_tail(kb32_ref, kb, r0)
            store_tail(vb32_ref, vb, r0)

    if tail_scratch:
        @pl.when(pl.program_id(1) == pl.num_programs(1) - 1)
        def _():
            for out_ref, ref in zip((ka32_ref, va32_ref, kb32_ref, vb32_ref), tail_scratch):
                out_ref[...] = ref[...].T


def _resident(shape):
    return pl.BlockSpec(shape, lambda *_: (0,) * len(shape), pipeline_mode=pl.Buffered(1))


def _ffn1_qkv(x, mod, gains, wg, wu, wd, win, gt, rt):
    n_groups, rows, _ = x.shape
    n_g, n_r = n_groups // gt, rows // rt
    tm = gt * rt
    tokens = n_groups * rows
    ta = gt * min(REACH_A, rows)
    tb = gt * min(REACH_B, rows)
    tile = lambda gi, ri: (gi * n_r + ri, 0)
    tail = lambda gi, ri: (gi, 0)
    tok = lambda w, dt: jax.ShapeDtypeStruct((tokens, w), dt)
    if gt == 1:
        assert ta <= tm and tb <= tm
        tail_specs = [pl.BlockSpec((None, w, t), lambda gi, ri: (gi, 0, 0))
                      for w, t in ((WIDTH, ta), (WIDTH, ta), (KV_WIDTH_B, tb), (KV_WIDTH_B, tb))]
        tail_shapes = [jax.ShapeDtypeStruct((n_g, w, t), F32)
                       for w, t in ((WIDTH, ta), (WIDTH, ta), (KV_WIDTH_B, tb), (KV_WIDTH_B, tb))]
        scratch = [pltpu.VMEM((ta, WIDTH), F32), pltpu.VMEM((ta, WIDTH), F32),
                   pltpu.VMEM((tb, KV_WIDTH_B), F32), pltpu.VMEM((tb, KV_WIDTH_B), F32)]
    else:
        assert n_r == 1
        tail_specs = [pl.BlockSpec((tm * N_HEADS, HEAD_DIM), tail), pl.BlockSpec((tm * N_HEADS, HEAD_DIM), tail),
                      pl.BlockSpec((tm, KV_WIDTH_B), tail), pl.BlockSpec((tm, KV_WIDTH_B), tail)]
        tail_shapes = [jax.ShapeDtypeStruct((tokens * N_HEADS, HEAD_DIM), F32),
                       jax.ShapeDtypeStruct((tokens * N_HEADS, HEAD_DIM), F32),
                       jax.ShapeDtypeStruct((tokens, KV_WIDTH_B), F32), jax.ShapeDtypeStruct((tokens, KV_WIDTH_B), F32)]
        scratch = []
    vmem = (2 * 3 * D_MODEL * D_FF + 2 * D_MODEL * win.shape[1]
            + 2 * 2 * 4 * tm * D_MODEL
            + 2 * 2 * tm * (6 * WIDTH + 2 * KV_WIDTH_B)
            + 2 * 4 * (2 * ta * WIDTH + 2 * tb * KV_WIDTH_B)
            + tm * (2 * 4 + 2) * FF_SLAB_MAX + 3 * 4 * tm * D_MODEL
            + 4 * tm * win.shape[1])
    return pl.pallas_call(
        _ffn1_qkv_kernel,
        grid=(n_g, n_r),
        in_specs=[
            pl.BlockSpec((gt, rt, D_MODEL), lambda gi, ri: (gi, ri, 0)),
            pl.BlockSpec((N_MOD, gt, 1, D_MODEL), lambda gi, ri: (0, gi, 0, 0)),
            _resident(gains.shape),
            _resident(wg.shape), _resident(wu.shape), _resident(wd.shape), _resident(win.shape),
        ],
        out_specs=[
            pl.BlockSpec((gt, rt, D_MODEL), lambda gi, ri: (gi, ri, 0)),
            pl.BlockSpec((tm, 2 * WIDTH), tile), pl.BlockSpec((tm, WIDTH), tile), pl.BlockSpec((tm, WIDTH), tile),
            pl.BlockSpec((tm, 2 * WIDTH), tile), pl.BlockSpec((tm, KV_WIDTH_B), tile),
            pl.BlockSpec((tm, KV_WIDTH_B), tile),
            *tail_specs,
        ],
        out_shape=[
            jax.ShapeDtypeStruct(x.shape, F32),
            tok(2 * WIDTH, BF16), tok(WIDTH, BF16), tok(WIDTH, BF16),
            tok(2 * WIDTH, BF16), tok(KV_WIDTH_B, BF16), tok(KV_WIDTH_B, BF16),
            *tail_shapes,
        ],
        scratch_shapes=scratch,
        compiler_params=pltpu.CompilerParams(
            dimension_semantics=("arbitrary", "arbitrary"), vmem_limit_bytes=_vmem_limit(vmem)),
        name="ffn1_qkv",
    )(x, mod, gains, wg, wu, wd, win)


def _out_ffn2_kernel(x_ref, o_ref, mod_ref, gains_ref, gg_ref, wout_ref, wg_ref, wu_ref, wd_ref, y_ref):
    gt, rt, _ = x_ref.shape
    gain = lambda k: gains_ref[k:k + 1, :]
    gg = gg_ref[...]
    subs = _sub_tiles(gt, rt)
    mods = [lambda k, gs=gs: mod_ref[k, gs] for gs, _, _, _ in subs]
    xs = [x_ref[gs, rs, :] for gs, rs, _, _ in subs]
    ons = []
    for _, _, r0, n in subs:
        o = o_ref[r0:r0 + n, :].astype(F32)
        on = jnp.concatenate([_rms(o[:, :WIDTH], gg[:, :WIDTH]), _rms(o[:, WIDTH:], gg[:, WIDTH:])], axis=1)
        ons.append(on.astype(BF16))
    mixed = [jnp.dot(on, wout_ref[...], preferred_element_type=F32).reshape(x.shape) for on, x in zip(ons, xs)]
    x2s = [x + mod(5) * _rms(m, gain(3)) for x, m, mod in zip(xs, mixed, mods)]
    hs = [_adaln(x2, gain(4), mod(6), mod(7)).reshape(n, D_MODEL).astype(BF16)
          for x2, mod, (_, _, _, n) in zip(x2s, mods, subs)]
    fs = [_swiglu(h, wg_ref, wu_ref, wd_ref).reshape(x.shape) for h, x in zip(hs, xs)]
    for x2, f, mod, (gs, rs, _, _) in zip(x2s, fs, mods, subs):
        y_ref[gs, rs, :] = x2 + (FFN_RES * mod(8)) * _rms(f, gain(5))


def _out_ffn2(x1, o, mod, gains, gg, wout, wg, wu, wd, gt, rt):
    n_groups, rows, _ = x1.shape
    n_g, n_r = n_groups // gt, rows // rt
    tm = gt * rt
    vmem = (2 * 3 * D_MODEL * D_FF + 2 * wout.size
            + 2 * 2 * 4 * tm * D_MODEL + 2 * 2 * tm * 2 * WIDTH
            + tm * (2 * 4 + 2) * FF_SLAB_MAX + 5 * 4 * tm * D_MODEL)
    return pl.pallas_call(
        _out_ffn2_kernel,
        grid=(n_g, n_r),
        in_specs=[
            pl.BlockSpec((gt, rt, D_MODEL), lambda gi, ri: (gi, ri, 0)),
            pl.BlockSpec((tm, 2 * WIDTH), lambda gi, ri: (gi * n_r + ri, 0)),
            pl.BlockSpec((N_MOD, gt, 1, D_MODEL), lambda gi, ri: (0, gi, 0, 0)),
            _resident(gains.shape), _resident(gg.shape), _resident(wout.shape),
            _resident(wg.shape), _resident(wu.shape), _resident(wd.shape),
        ],
        out_specs=pl.BlockSpec((gt, rt, D_MODEL), lambda gi, ri: (gi, ri, 0)),
        out_shape=jax.ShapeDtypeStruct(x1.shape, F32),
        compiler_params=pltpu.CompilerParams(
            dimension_semantics=("arbitrary", "arbitrary"), vmem_limit_bytes=_vmem_limit(vmem)),
        name="out_ffn2",
    )(x1, o, mod, gains, gg, wout, wg, wu, wd)


def _softmax_terms(s, sink=None):
    m = jnp.max(s, axis=1, keepdims=True)
    if sink is not None:
        m = jnp.maximum(m, sink)
    e = jnp.exp2(s - m)
    denom = jnp.sum(e, axis=1, keepdims=True)
    if sink is not None:
        denom = denom + jnp.exp2(sink - m)
    return e.astype(BF16), 1.0 / denom


def _scores(q, k):
    return lax.dot_general(q, k, (((1,), (1,)), ((), ())), preferred_element_type=F32)


def _unstack_pair(z):
    n_q = z.shape[0] // 2
    is_lo = lax.broadcasted_iota(jnp.int32, (n_q, z.shape[1]), 1) < HEAD_DIM
    return jnp.where(is_lo, z[:n_q], z[n_q:])


def _sink_column(sinks_ref, pair, n_q):
    row = lax.broadcasted_iota(jnp.int32, (2 * n_q, 1), 0)
    return jnp.where(row < n_q, sinks_ref[pair], sinks_ref[N_PAIRS + pair]) * LOG2E


GROUPS_PER_STEP = 2


def _attn_prompt_kernel(sinks_ref, qa_ref, qb_ref, ka_ref, va_ref, kb_ref, vb_ref, *rest):
    ba_refs, bb_refs = rest[:GROUPS_PER_STEP], rest[GROUPS_PER_STEP:2 * GROUPS_PER_STEP]
    n_cast = (len(rest) - 2 * GROUPS_PER_STEP - 1) // 2
    o_ref = rest[2 * GROUPS_PER_STEP + n_cast]
    for src, dst in zip(rest[2 * GROUPS_PER_STEP:2 * GROUPS_PER_STEP + n_cast], rest[len(rest) - n_cast:]):
        dst[...] = src[...].astype(BF16)
    slab = lambda p: slice(p * V7X_LANES, (p + 1) * V7X_LANES)
    bands = []
    first = pl.program_id(1) * GROUPS_PER_STEP
    for j in range(GROUPS_PER_STEP):
        start = lambda g, reach: jnp.maximum(g * GROUP_Q - reach, 0)
        off_a = pl.multiple_of(start(first + j, REACH_A) - start(first, REACH_A), GROUP_Q)
        off_b = pl.multiple_of(start(first + j, REACH_B) - start(first, REACH_B), GROUP_Q)
        bands.append((pl.ds(off_a, BAND_A), pl.ds(off_b, BAND_B)))

    def stacked(q_ref, j, p):
        rows = slice(j * GROUP_Q, (j + 1) * GROUP_Q)
        return jnp.concatenate([q_ref[rows, slab(p)], q_ref[rows, slab(N_PAIRS + p)]], axis=0)

    def score(mixer, j, p):
        if mixer == "a":
            bias = jnp.concatenate([ba_refs[j][0, 2 * p], ba_refs[j][0, 2 * p + 1]], axis=0)
            return _scores(stacked(qa_ref, j, p), ka_ref[bands[j][0], slab(p)]) + bias
        bias = jnp.concatenate([bb_refs[j][0, p], bb_refs[j][0, N_PAIRS + p]], axis=0)
        return _scores(stacked(qb_ref, j, p), kb_ref[bands[j][1], :]) + bias

    def weights(s, sink):
        m = jnp.max(s, axis=1, keepdims=True)
        if sink is None:
            return jnp.exp2(s - m).astype(BF16), None
        m = jnp.maximum(m, sink)
        return jnp.exp2(s - m).astype(BF16), jnp.exp2(sink - m)

    def attend(e, extra, v):
        z = jnp.dot(e, jnp.concatenate([v, jnp.ones_like(v)], axis=1), preferred_element_type=F32)
        denom = z[:, V7X_LANES:] if extra is None else z[:, V7X_LANES:] + extra
        return _unstack_pair(z[:, :V7X_LANES] * (1.0 / denom))

    jobs = [(mixer, j, p) for mixer in ("a", "b") for j in range(GROUPS_PER_STEP) for p in range(N_PAIRS)]
    s_all = [score(*job) for job in jobs]
    w_all = [weights(s, _sink_column(sinks_ref, p, GROUP_Q) if mixer == "b" else None)
             for s, (mixer, j, p) in zip(s_all, jobs)]
    for (e, extra), (mixer, j, p) in zip(w_all, jobs):
        rows = slice(j * GROUP_Q, (j + 1) * GROUP_Q)
        if mixer == "a":
            o_ref[rows, slab(p)] = attend(e, extra, va_ref[bands[j][0], slab(p)]).astype(BF16)
        else:
            o_ref[rows, slab(N_PAIRS + p)] = attend(e, extra, vb_ref[bands[j][1], :]).astype(BF16)


def _attn_prompt(sinks, qa, qb, ka, va, kb, vb, bias_a, bias_b, batch, seq, cast=()):
    rows = GROUPS_PER_STEP * GROUP_Q
    n_steps = seq // rows
    cast_specs, cast_shapes = (zip(*[_cast_chunks(w, batch * n_steps, lambda b, s: b * n_steps + s) for w in cast])
                               if cast else ((), ()))
    q_spec = pl.BlockSpec((rows, 2 * WIDTH), lambda b, s: (b * n_steps + s, 0))
    kv_spec = lambda w, reach: pl.BlockSpec(
        (pl.Element(reach + rows), pl.Element(w)),
        lambda b, s: (pl.multiple_of(b * seq + jnp.maximum(s * rows - reach, 0), GROUP_Q), 0))
    bias_spec = lambda band, n_var, j: pl.BlockSpec(
        (1, N_HEADS, GROUP_Q, band),
        lambda b, s: (jnp.minimum(s * GROUPS_PER_STEP + j, n_var - 1), 0, 0, 0))
    vmem = (2 * 2 * 2 * ((REACH_A + rows) * WIDTH + (REACH_B + rows) * KV_WIDTH_B)
            + GROUPS_PER_STEP * 2 * 4 * N_HEADS * GROUP_Q * (BAND_A + BAND_B)
            + 2 * 2 * rows * 6 * WIDTH + GROUPS_PER_STEP * 6 * 4 * 2 * GROUP_Q * BAND_A
            + sum(2 * 6 * math.prod(spec.block_shape) for spec in cast_specs))
    out = pl.pallas_call(
        _attn_prompt_kernel,
        grid=(batch, n_steps),
        in_specs=[
            pl.BlockSpec(memory_space=pltpu.SMEM),
            q_spec, q_spec,
            kv_spec(WIDTH, REACH_A), kv_spec(WIDTH, REACH_A), kv_spec(KV_WIDTH_B, REACH_B), kv_spec(KV_WIDTH_B, REACH_B),
            *[bias_spec(BAND_A, N_VAR_A, j) for j in range(GROUPS_PER_STEP)],
            *[bias_spec(BAND_B, N_VAR_B, j) for j in range(GROUPS_PER_STEP)],
            *cast_specs,
        ],
        out_specs=[pl.BlockSpec((rows, 2 * WIDTH), lambda b, s: (b * n_steps + s, 0)), *cast_specs],
        out_shape=[jax.ShapeDtypeStruct((batch * seq, 2 * WIDTH), BF16), *cast_shapes],
        compiler_params=pltpu.CompilerParams(
            dimension_semantics=("arbitrary", "arbitrary"), vmem_limit_bytes=_vmem_limit(vmem)),
        name="attn_prompt",
    )(sinks, qa, qb, ka, va, kb, vb, *([bias_a] * GROUPS_PER_STEP), *([bias_b] * GROUPS_PER_STEP), *cast)
    return out[0], tuple(out[1:])


def _pair_ordered_cols(h):
    return (h % N_PAIRS) * V7X_LANES + (h // N_PAIRS) * HEAD_DIM


SAMPLE_ROWS_PER_STEP = 4


def _attn_sample_kernel(sinks_ref, qa_ref, qb_ref, ka_ref, va_ref, kb_ref, vb_ref,
                        cak_ref, cav_ref, cbk_ref, cbv_ref, ba_ref, bb_ref, o_ref):
    n_rows = cak_ref.shape[0]
    n_q = qa_ref.shape[0] // n_rows
    n_kv = cbk_ref.shape[1]
    per_kv = N_HEADS // n_kv
    la, lb = cak_ref.shape[3], cbk_ref.shape[3]
    head = lambda ref, r, idx: ref[r * n_q:(r + 1) * n_q, idx * HEAD_DIM:(idx + 1) * HEAD_DIM]

    def query(ref, r, lo):
        lo = lo if lo % V7X_LANES < HEAD_DIM else WIDTH + lo
        return ref[r * n_q:(r + 1) * n_q, lo:lo + HEAD_DIM]

    def scores(q, cache_t_ref, new_ref, r, idx):
        old = jnp.dot(q, cache_t_ref[r, idx].astype(BF16), preferred_element_type=F32)
        return jnp.concatenate([old, _scores(q, head(new_ref, r, idx))], axis=1)

    def values(e, cache_t_ref, new_ref, r, idx, n_old):
        old = _scores(e[:, :n_old], cache_t_ref[r, idx].astype(BF16))
        return old + jnp.dot(e[:, n_old:], head(new_ref, r, idx), preferred_element_type=F32)

    jobs_a = [(r, h) for r in range(n_rows) for h in range(N_HEADS)]
    jobs_b = [(r, c) for r in range(n_rows) for c in range(n_kv)]
    s_a = [scores(query(qa_ref, r, h * HEAD_DIM), cak_ref, ka_ref, r, h) + ba_ref[h] for r, h in jobs_a]
    s_b = []
    for r, c in jobs_b:
        heads = range(c * per_kv, (c + 1) * per_kv)
        q = jnp.concatenate([query(qb_ref, r, _pair_ordered_cols(h)) for h in heads], axis=0)
        bias = jnp.concatenate([bb_ref[h] for h in heads], axis=0)
        s_b.append(scores(q, cbk_ref, kb_ref, r, c) + bias)

    row = lax.broadcasted_iota(jnp.int32, (per_kv * n_q, 1), 0)
    sinks = []
    for c in range(n_kv):
        sink = jnp.zeros((per_kv * n_q, 1), F32)
        for i in range(per_kv):
            sink = jnp.where((row >= i * n_q) & (row < (i + 1) * n_q), sinks_ref[c * per_kv + i], sink)
        sinks.append(sink * LOG2E)
    p_a = [_softmax_terms(s) for s in s_a]
    p_b = [_softmax_terms(s, sinks[c]) for s, (_, c) in zip(s_b, jobs_b)]

    out_a = {job: values(e, cav_ref, va_ref, *job, la) * inv for job, (e, inv) in zip(jobs_a, p_a)}
    out_b = {}
    for (r, c), (e, inv) in zip(jobs_b, p_b):
        o = values(e, cbv_ref, vb_ref, r, c, lb) * inv
        for i in range(per_kv):
            out_b[(r, c * per_kv + i)] = o[i * n_q:(i + 1) * n_q]
    order_b = [h for p in range(N_PAIRS) for h in (p, N_PAIRS + p)]
    for r in range(n_rows):
        pieces = [out_a[(r, h)] for h in range(N_HEADS)] + [out_b[(r, h)] for h in order_b]
        o_ref[r * n_q:(r + 1) * n_q, :] = jnp.concatenate(pieces, axis=1).astype(BF16)


def _attn_sample(sinks, qa, qb, ka, va, kb, vb, cak, cav, cbk, cbv, bias_a, bias_b, batch, s_len):
    n_rows = SAMPLE_ROWS_PER_STEP
    row = lambda w: pl.BlockSpec((n_rows * s_len, w), lambda b: (b, 0))
    cache = lambda a: pl.BlockSpec((n_rows,) + a.shape[1:], lambda b: (b, 0, 0, 0))
    whole = lambda a: pl.BlockSpec(a.shape, lambda b: (0,) * a.ndim)
    return pl.pallas_call(
        _attn_sample_kernel,
        grid=(batch // n_rows,),
        in_specs=[
            pl.BlockSpec(memory_space=pltpu.SMEM),
            row(2 * WIDTH), row(2 * WIDTH), row(WIDTH), row(WIDTH), row(KV_WIDTH_B), row(KV_WIDTH_B),
            cache(cak), cache(cav), cache(cbk), cache(cbv),
            whole(bias_a), whole(bias_b),
        ],
        out_specs=pl.BlockSpec((n_rows * s_len, 2 * WIDTH), lambda b: (b, 0)),
        out_shape=jax.ShapeDtypeStruct((batch * s_len, 2 * WIDTH), BF16),
        compiler_params=pltpu.CompilerParams(dimension_semantics=("arbitrary",)),
        name="attn_sample",
    )(sinks, qa, qb, ka, va, kb, vb, cak, cav, cbk, cbv, bias_a, bias_b)


def _pair_order(x, axis):
    shape = x.shape
    split = shape[:axis] + (N_HEADS // N_PAIRS, N_PAIRS, HEAD_DIM) + shape[axis + 1:]
    return jnp.swapaxes(x.reshape(split), axis, axis + 1).reshape(shape)


def _prep_out_weights(w_out, group_gains):
    wout = jnp.concatenate([w_out[:WIDTH], _pair_order(w_out[WIDTH:], 0)], axis=0).astype(BF16)
    gg = jnp.concatenate([group_gains[:WIDTH], _pair_order(group_gains[WIDTH:], 0)])
    return wout, gg.reshape(1, 2 * WIDTH)


def kernel(x_prompt, x_sample, cache_a_k, cache_a_v, cache_b_k, cache_b_v, c_prompt, c_sample, w_mod, b_mod,
           norm_gains, w1_gate, w1_up, w1_down, w_in, w_out, group_gains, rel_bias_a, t5_bias_table, sinks_b,
           w2_gate, w2_up, w2_down):
    depth = w_mod.shape[0]
    assert depth == 1
    batch, seq, _ = x_prompt.shape
    s_batch, s_len, _ = x_sample.shape
    la, lb = cache_a_k.shape[2], cache_b_k.shape[2]
    assert la == REACH_A and lb == REACH_B and seq % 512 == 0

    c_rows = jnp.concatenate([c_prompt, c_sample], axis=0)
    pad_rows = (-c_rows.shape[0]) % 8
    c_rows = jnp.pad(c_rows, ((0, pad_rows), (0, 0)))
    mod = _modulation(c_rows, w_mod[0], b_mod[0])
    mod_p = mod[:, :batch].reshape(N_MOD, batch, 1, D_MODEL)
    mod_s = mod[:, batch:batch + s_batch].reshape(N_MOD, s_batch, 1, D_MODEL)

    bias_a, bias_b, bias_sa, bias_sb, *w1, win = _bias_tables(
        rel_bias_a[0], t5_bias_table, s_len, la, lb, cast=(w1_gate[0], w1_up[0], w1_down[0], w_in[0]))

    gains = norm_gains[0]
    wout, gg = _prep_out_weights(w_out[0], group_gains[0])
    sinks = sinks_b[0]

    rt_p = 512
    (x1p, qa, ka, va, qb, kb, vb, ka32, va32, kb32, vb32) = _ffn1_qkv(
        x_prompt, mod_p, gains, *w1, win, 1, rt_p)
    op, w2 = _attn_prompt(sinks, qa, qb, ka, va, kb, vb, bias_a, bias_b, batch, seq,
                          cast=(w2_gate[0], w2_up[0], w2_down[0]))
    y_prompt = _out_ffn2(x1p, op, mod_p, gains, gg, wout, *w2, 1, 2 * rt_p)
    rows_last = lambda t: jnp.transpose(t.reshape(depth, batch, -1, HEAD_DIM, t.shape[-1]), (0, 1, 4, 2, 3))
    new_p = tuple(rows_last(t) for t in (ka32, va32, kb32, vb32))

    gt_s = 16
    (x1s, qas, kas, vas, qbs, kbs, vbs, kas32, vas32, kbs32, vbs32) = _ffn1_qkv(
        x_sample, mod_s, gains, *w1, win, gt_s, s_len)
    os_ = _attn_sample(sinks, qas, qbs, kas, vas, kbs, vbs,
                       *(jnp.transpose(c[0], (0, 2, 3, 1)) for c in (cache_a_k, cache_a_v, cache_b_k, cache_b_v)),
                       bias_sa, bias_sb, s_batch, s_len)
    y_sample = _out_ffn2(x1s, os_, mod_s, gains, gg, wout, *w2, gt_s, s_len)
    new_s = (kas32.reshape(depth, s_batch, s_len, N_HEADS, HEAD_DIM),
             vas32.reshape(depth, s_batch, s_len, N_HEADS, HEAD_DIM),
             kbs32.reshape(depth, s_batch, s_len, 2, HEAD_DIM), vbs32.reshape(depth, s_batch, s_len, 2, HEAD_DIM))

    return (y_prompt, y_sample) + new_p + new_s
```

```python
import functools
import math

import jax
import jax.numpy as jnp
from jax import lax
from jax.experimental import pallas as pl
from jax.experimental.pallas import tpu as pltpu

F32 = jnp.float32
BF16 = jnp.bfloat16

D_MODEL = 1024
D_FF = 2816
CHUNK = 64
HEAD_DIM = 64
N_HEADS = 8
N_PAIRS = N_HEADS // 2
WIDTH = N_HEADS * HEAD_DIM
KV_WIDTH_B = 2 * HEAD_DIM
REACH_A = 8 * CHUNK
REACH_B = 2 * CHUNK
REL_CLIP_A = 128
N_BUCKETS = 32
N_MOD = 9
FFN_RES = 0.5
EPS = 1e-6
SCALE = HEAD_DIM ** -0.5
LOG2E = 1.4426950408889634
NEG_INF = -1e30

V7X_LANES = 128
V7X_VMEM_BYTES = 64 * 1024 * 1024
V7X_VMEM_LIMIT_CAP = 56 * 1024 * 1024

GROUP_Q = 2 * CHUNK
BAND_A = REACH_A + GROUP_Q
BAND_B = REACH_B + GROUP_Q
ORIGIN_A = 1024
WIDTH_MASTER_A = ORIGIN_A + BAND_A
ORIGIN_B = 256
WIDTH_MASTER_B = ORIGIN_B + BAND_B
N_VAR_A = REACH_A // GROUP_Q + 1
N_VAR_B = REACH_B // GROUP_Q + 1


def _vmem_limit(estimate_bytes):
    return int(min(max(estimate_bytes, 32 * 1024 * 1024), V7X_VMEM_LIMIT_CAP))


def _rms(x, gain):
    ms = jnp.mean(x * x, axis=-1, keepdims=True)
    return x * lax.rsqrt(ms + EPS) * gain


def _adaln(x, gain, shift, scale):
    return _rms(x, gain) * (1.0 + scale) + shift


def _silu(x):
    return x * (1.0 / (1.0 + jnp.exp(-x)))


def _mod_kernel(c_ref, w_ref, b_ref, o_ref):
    s = _silu(c_ref[...]).astype(BF16)
    o_ref[0] = jnp.dot(s, w_ref[...].astype(BF16), preferred_element_type=F32) + b_ref[0]


def _modulation(c_rows, w_mod, b_mod):
    rows = c_rows.shape[0]
    return pl.pallas_call(
        _mod_kernel,
        grid=(N_MOD,),
        in_specs=[
            pl.BlockSpec((rows, D_MODEL), lambda j: (0, 0)),
            pl.BlockSpec((D_MODEL, D_MODEL), lambda j: (0, j)),
            pl.BlockSpec((1, 1, D_MODEL), lambda j: (j, 0, 0)),
        ],
        out_specs=pl.BlockSpec((1, rows, D_MODEL), lambda j: (j, 0, 0)),
        out_shape=jax.ShapeDtypeStruct((N_MOD, rows, D_MODEL), F32),
        name="modulation",
    )(c_rows, w_mod, b_mod.reshape(N_MOD, 1, D_MODEL))


def _skew(base_row, n_rows):
    width = base_row.shape[1]
    m = jnp.broadcast_to(base_row, (n_rows, width))
    row = lax.broadcasted_iota(jnp.int32, (n_rows, width), 0)
    shift = 1
    while shift < n_rows:
        m = jnp.where((row & shift) != 0, pltpu.roll(m, shift, axis=1), m)
        shift *= 2
    return m


def _band_mask(n_rows, width, origin, n_prev):
    row = lax.broadcasted_iota(jnp.int32, (n_rows, width), 0)
    col = lax.broadcasted_iota(jnp.int32, (n_rows, width), 1)
    q_chunk = row >> 6
    k_chunk = (col >> 6) - (origin // CHUNK)
    return (k_chunk >= q_chunk - n_prev) & (k_chunk <= q_chunk)


def _t5_bucket(rel):
    half = N_BUCKETS // 2
    max_exact = half // 2
    n = jnp.abs(rel)
    n2 = n * n
    large = jnp.full(rel.shape, max_exact, jnp.int32)
    for k in range(1, half - max_exact):
        large = large + jnp.where(n2 >= (max_exact * max_exact) * (2 ** k), 1, 0)
    return jnp.where(rel > 0, half, 0) + jnp.where(n < max_exact, n, large)


def _bias_kernel(rel_ref, t5_ref, *refs, s_len, la, lb):
    n_cast = (len(refs) - 4) // 2
    for src, dst in zip(refs[:n_cast], refs[n_cast + 4:]):
        dst[...] = src[...].astype(BF16)
    ba_ref, bb_ref, bsa_ref, bsb_ref = refs[n_cast:n_cast + 4]

    tab = rel_ref[0] * LOG2E
    n_tab = 2 * REL_CLIP_A + 1
    first, last = tab[:, 0:1], tab[:, n_tab - 1:n_tab]
    win_lo = ORIGIN_A - 2 * REL_CLIP_A
    col = lax.broadcasted_iota(jnp.int32, (1, tab.shape[1] + V7X_LANES), 1)
    shifted = jnp.concatenate([jnp.zeros((1, V7X_LANES), F32), tab], axis=1)
    u_win = jnp.where(col < V7X_LANES, first, jnp.where(col >= V7X_LANES + n_tab, last, shifted))
    window = _skew(u_win, GROUP_Q)[:, V7X_LANES:]
    n_left = win_lo + V7X_LANES
    n_right = WIDTH_MASTER_A - n_left - window.shape[1]
    skew_a = jnp.concatenate([jnp.broadcast_to(first, (GROUP_Q, n_left)), window,
                              jnp.broadcast_to(last, (GROUP_Q, n_right))], axis=1)
    master_a = jnp.where(_band_mask(GROUP_Q, WIDTH_MASTER_A, ORIGIN_A, REACH_A // CHUNK), skew_a, NEG_INF)
    for v in range(N_VAR_A):
        start = ORIGIN_A - min(v * GROUP_Q, REACH_A) if v < N_VAR_A - 1 else ORIGIN_A - REACH_A
        ba_ref[v, 0] = master_a[:, start:start + BAND_A]
    bsa_ref[0] = skew_a[:s_len, ORIGIN_A - la:ORIGIN_A + s_len]

    t5 = t5_ref[0] * LOG2E
    rel = lax.broadcasted_iota(jnp.int32, (1, WIDTH_MASTER_B), 1) - ORIGIN_B
    bucket = _t5_bucket(rel)
    u_b = jnp.zeros((1, WIDTH_MASTER_B), F32)
    for i in range(N_BUCKETS):
        u_b = jnp.where(bucket == i, t5[:, i:i + 1], u_b)
    skew_b = _skew(u_b, GROUP_Q)
    master_b = jnp.where(_band_mask(GROUP_Q, WIDTH_MASTER_B, ORIGIN_B, REACH_B // CHUNK), skew_b, NEG_INF)
    for v in range(N_VAR_B):
        start = ORIGIN_B - min(v * GROUP_Q, REACH_B) if v < N_VAR_B - 1 else ORIGIN_B - REACH_B
        bb_ref[v, 0] = master_b[:, start:start + BAND_B]
    bsb_ref[0] = skew_b[:s_len, ORIGIN_B - lb:ORIGIN_B + s_len]


V7X_BF16_SUBLANES = 16


def _cast_chunks(w, n_steps, step_of):
    rows, cols = w.shape
    n_chunks = n_steps
    while rows % (n_chunks * V7X_BF16_SUBLANES):
        n_chunks //= 2
    spec = pl.BlockSpec((rows // n_chunks, cols), lambda *g: (jnp.minimum(step_of(*g), n_chunks - 1), 0))
    return spec, jax.ShapeDtypeStruct(w.shape, BF16)


def _bias_tables(rel_bias_a, t5_table, s_len, la, lb, cast=()):
    n_tab = 2 * REL_CLIP_A + 1
    tab_w = 3 * V7X_LANES
    rel_p = jnp.pad(rel_bias_a, ((0, 0), (0, tab_w - n_tab))).reshape(N_HEADS, 1, tab_w)
    t5_p = t5_table.reshape(N_HEADS, 1, N_BUCKETS)
    kern = functools.partial(_bias_kernel, s_len=s_len, la=la, lb=lb)
    cast_specs, cast_shapes = zip(*[_cast_chunks(w, N_HEADS, lambda h: h) for w in cast]) if cast else ((), ())
    return pl.pallas_call(
        kern,
        grid=(N_HEADS,),
        in_specs=[
            pl.BlockSpec((1, 1, tab_w), lambda h: (h, 0, 0)),
            pl.BlockSpec((1, 1, N_BUCKETS), lambda h: (h, 0, 0)),
            *cast_specs,
        ],
        out_specs=[
            pl.BlockSpec((N_VAR_A, 1, GROUP_Q, BAND_A), lambda h: (0, h, 0, 0)),
            pl.BlockSpec((N_VAR_B, 1, GROUP_Q, BAND_B), lambda h: (0, h, 0, 0)),
            pl.BlockSpec((1, s_len, la + s_len), lambda h: (h, 0, 0)),
            pl.BlockSpec((1, s_len, lb + s_len), lambda h: (h, 0, 0)),
            *cast_specs,
        ],
        out_shape=[
            jax.ShapeDtypeStruct((N_VAR_A, N_HEADS, GROUP_Q, BAND_A), F32),
            jax.ShapeDtypeStruct((N_VAR_B, N_HEADS, GROUP_Q, BAND_B), F32),
            jax.ShapeDtypeStruct((N_HEADS, s_len, la + s_len), F32),
            jax.ShapeDtypeStruct((N_HEADS, s_len, lb + s_len), F32),
            *cast_shapes,
        ],
        compiler_params=pltpu.CompilerParams(dimension_semantics=("arbitrary",)),
        name="bias_tables",
    )(rel_p, t5_p, *cast)


V7X_MXU_DIM = 256
FF_SLABS = ((0, 6 * V7X_MXU_DIM), (6 * V7X_MXU_DIM, D_FF))
FF_SLAB_MAX = max(hi - lo for lo, hi in FF_SLABS)


def _swiglu(h, wg_ref, wu_ref, wd_ref):
    acts = []
    for lo, hi in FF_SLABS:
        g = jnp.dot(h, wg_ref[:, lo:hi], preferred_element_type=F32)
        u = jnp.dot(h, wu_ref[:, lo:hi], preferred_element_type=F32)
        acts.append((_silu(g) * u).astype(BF16))
    acc = None
    for (lo, hi), a in zip(FF_SLABS, acts):
        part = jnp.dot(a, wd_ref[lo:hi, :], preferred_element_type=F32)
        acc = part if acc is None else acc + part
    return acc


SUB_TOKENS = V7X_MXU_DIM


def _sub_tiles(gt, rt):
    n_sub = gt * rt // SUB_TOKENS
    if gt == 1:
        step = rt // n_sub
        return [(slice(0, 1), slice(s * step, (s + 1) * step), s * step, step) for s in range(n_sub)]
    step = gt // n_sub
    return [(slice(s * step, (s + 1) * step), slice(0, rt), s * step * rt, step * rt) for s in range(n_sub)]


def _ffn1_qkv_kernel(x_ref, mod_ref, gains_ref, wg_ref, wu_ref, wd_ref, win_ref,
                     x1_ref, qa_ref, ka_ref, va_ref, qb_ref, kb_ref, vb_ref,
                     ka32_ref, va32_ref, kb32_ref, vb32_ref, *tail_scratch):
    gt, rt, _ = x_ref.shape
    tm = gt * rt
    gain = lambda k: gains_ref[k:k + 1, :]

    def store_tail(ref, val, r0):
        first = tm - ref.shape[0]
        lo = max(r0, first)
        if lo < r0 + val.shape[0]:
            ref[lo - first:r0 + val.shape[0] - first, :] = val[lo - r0:, :]

    def store_heads(ref, val, r0):
        n_heads = val.shape[1] // HEAD_DIM
        for h in range(n_heads):
            ref[pl.ds(r0 * n_heads + h, val.shape[0], stride=n_heads), :] = val[:, h * HEAD_DIM:(h + 1) * HEAD_DIM]

    def store_split(ref, rows, q, pairs):
        q = q * (SCALE * LOG2E)
        head = lambda h: q[:, h * HEAD_DIM:(h + 1) * HEAD_DIM]
        zeros = jnp.zeros((q.shape[0], HEAD_DIM), F32)
        ref[rows, :WIDTH] = jnp.concatenate([t for a, _ in pairs for t in (head(a), zeros)], axis=1).astype(BF16)
        ref[rows, WIDTH:] = jnp.concatenate([t for _, b in pairs for t in (zeros, head(b))], axis=1).astype(BF16)

    pairs_a = [(2 * p, 2 * p + 1) for p in range(N_PAIRS)]
    pairs_b = [(p, N_PAIRS + p) for p in range(N_PAIRS)]

    subs = _sub_tiles(gt, rt)
    mods = [lambda k, gs=gs: mod_ref[k, gs] for gs, _, _, _ in subs]
    xs = [x_ref[gs, rs, :] for gs, rs, _, _ in subs]
    hs = [_adaln(x, gain(0), mod(0), mod(1)).reshape(n, D_MODEL).astype(BF16)
          for x, mod, (_, _, _, n) in zip(xs, mods, subs)]
    fs = [_swiglu(h, wg_ref, wu_ref, wd_ref).reshape(x.shape) for h, x in zip(hs, xs)]
    h2s = []
    for x, f, mod, (gs, rs, _, n) in zip(xs, fs, mods, subs):
        x1 = x + (FFN_RES * mod(2)) * _rms(f, gain(1))
        x1_ref[gs, rs, :] = x1
        h2s.append(_adaln(x1, gain(2), mod(3), mod(4)).reshape(n, D_MODEL).astype(BF16))

    for h2, (_, _, r0, n) in zip(h2s, subs):
        rows = slice(r0, r0 + n)

        def proj(lo, width):
            return jnp.dot(h2, win_ref[:, lo:lo + width], preferred_element_type=F32)

        store_split(qa_ref, rows, proj(0, WIDTH), pairs_a)
        ka = proj(WIDTH, WIDTH)
        va = proj(2 * WIDTH, WIDTH)
        store_split(qb_ref, rows, proj(3 * WIDTH, WIDTH), pairs_b)
        kvb = proj(4 * WIDTH, 2 * KV_WIDTH_B)
        kb, vb = kvb[:, :KV_WIDTH_B], kvb[:, KV_WIDTH_B:]
        ka_ref[rows, :] = ka.astype(BF16)
        va_ref[rows, :] = va.astype(BF16)
        kb_ref[rows, :] = kb.astype(BF16)
        vb_ref[rows, :] = vb.astype(BF16)
        if tail_scratch:
            for ref, val in zip(tail_scratch, (ka, va, kb, vb)):
                store_tail(ref, val, r0)
        else:
            store_heads(ka32_ref, ka, r0)
            store_heads(va32_ref, va, r0)
            store_tail(kb32_ref, kb, r0)
            store_tail(vb32_ref, vb, r0)

    if tail_scratch:
        @pl.when(pl.program_id(1) == pl.num_programs(1) - 1)
        def _():
            for out_ref, ref in zip((ka32_ref, va32_ref, kb32_ref, vb32_ref), tail_scratch):
                out_ref[...] = ref[...].T


def _resident(shape):
    return pl.BlockSpec(shape, lambda *_: (0,) * len(shape), pipeline_mode=pl.Buffered(1))


def _ffn1_qkv(x, mod, gains, wg, wu, wd, win, gt, rt):
    n_groups, rows, _ = x.shape
    n_g, n_r = n_groups // gt, rows // rt
    tm = gt * rt
    tokens = n_groups * rows
    ta = gt * min(REACH_A, rows)
    tb = gt * min(REACH_B, rows)
    tile = lambda gi, ri: (gi * n_r + ri, 0)
    tail = lambda gi, ri: (gi, 0)
    tok = lambda w, dt: jax.ShapeDtypeStruct((tokens, w), dt)
    if gt == 1:
        assert ta <= tm and tb <= tm
        tail_specs = [pl.BlockSpec((None, w, t), lambda gi, ri: (gi, 0, 0))
                      for w, t in ((WIDTH, ta), (WIDTH, ta), (KV_WIDTH_B, tb), (KV_WIDTH_B, tb))]
        tail_shapes = [jax.ShapeDtypeStruct((n_g, w, t), F32)
                       for w, t in ((WIDTH, ta), (WIDTH, ta), (KV_WIDTH_B, tb), (KV_WIDTH_B, tb))]
        scratch = [pltpu.VMEM((ta, WIDTH), F32), pltpu.VMEM((ta, WIDTH), F32),
                   pltpu.VMEM((tb, KV_WIDTH_B), F32), pltpu.VMEM((tb, KV_WIDTH_B), F32)]
    else:
        assert n_r == 1
        tail_specs = [pl.BlockSpec((tm * N_HEADS, HEAD_DIM), tail), pl.BlockSpec((tm * N_HEADS, HEAD_DIM), tail),
                      pl.BlockSpec((tm, KV_WIDTH_B), tail), pl.BlockSpec((tm, KV_WIDTH_B), tail)]
        tail_shapes = [jax.ShapeDtypeStruct((tokens * N_HEADS, HEAD_DIM), F32),
                       jax.ShapeDtypeStruct((tokens * N_HEADS, HEAD_DIM), F32),
                       jax.ShapeDtypeStruct((tokens, KV_WIDTH_B), F32), jax.ShapeDtypeStruct((tokens, KV_WIDTH_B), F32)]
        scratch = []
    vmem = (2 * 3 * D_MODEL * D_FF + 2 * D_MODEL * win.shape[1]
            + 2 * 2 * 4 * tm * D_MODEL
            + 2 * 2 * tm * (6 * WIDTH + 2 * KV_WIDTH_B)
            + 2 * 4 * (2 * ta * WIDTH + 2 * tb * KV_WIDTH_B)
            + tm * (2 * 4 + 2) * FF_SLAB_MAX + 3 * 4 * tm * D_MODEL
            + 4 * tm * win.shape[1])
    return pl.pallas_call(
        _ffn1_qkv_kernel,
        grid=(n_g, n_r),
        in_specs=[
            pl.BlockSpec((gt, rt, D_MODEL), lambda gi, ri: (gi, ri, 0)),
            pl.BlockSpec((N_MOD, gt, 1, D_MODEL), lambda gi, ri: (0, gi, 0, 0)),
            _resident(gains.shape),
            _resident(wg.shape), _resident(wu.shape), _resident(wd.shape), _resident(win.shape),
        ],
        out_specs=[
            pl.BlockSpec((gt, rt, D_MODEL), lambda gi, ri: (gi, ri, 0)),
            pl.BlockSpec((tm, 2 * WIDTH), tile), pl.BlockSpec((tm, WIDTH), tile), pl.BlockSpec((tm, WIDTH), tile),
            pl.BlockSpec((tm, 2 * WIDTH), tile), pl.BlockSpec((tm, KV_WIDTH_B), tile),
            pl.BlockSpec((tm, KV_WIDTH_B), tile),
            *tail_specs,
        ],
        out_shape=[
            jax.ShapeDtypeStruct(x.shape, F32),
            tok(2 * WIDTH, BF16), tok(WIDTH, BF16), tok(WIDTH, BF16),
            tok(2 * WIDTH, BF16), tok(KV_WIDTH_B, BF16), tok(KV_WIDTH_B, BF16),
            *tail_shapes,
        ],
        scratch_shapes=scratch,
        compiler_params=pltpu.CompilerParams(
            dimension_semantics=("arbitrary", "arbitrary"), vmem_limit_bytes=_vmem_limit(vmem)),
        name="ffn1_qkv",
    )(x, mod, gains, wg, wu, wd, win)


def _out_ffn2_kernel(x_ref, o_ref, mod_ref, gains_ref, gg_ref, wout_ref, wg_ref, wu_ref, wd_ref, y_ref):
    gt, rt, _ = x_ref.shape
    gain = lambda k: gains_ref[k:k + 1, :]
    gg = gg_ref[...]
    subs = _sub_tiles(gt, rt)
    mods = [lambda k, gs=gs: mod_ref[k, gs] for gs, _, _, _ in subs]
    xs = [x_ref[gs, rs, :] for gs, rs, _, _ in subs]
    ons = []
    for _, _, r0, n in subs:
        o = o_ref[r0:r0 + n, :].astype(F32)
        on = jnp.concatenate([_rms(o[:, :WIDTH], gg[:, :WIDTH]), _rms(o[:, WIDTH:], gg[:, WIDTH:])], axis=1)
        ons.append(on.astype(BF16))
    mixed = [jnp.dot(on, wout_ref[...], preferred_element_type=F32).reshape(x.shape) for on, x in zip(ons, xs)]
    x2s = [x + mod(5) * _rms(m, gain(3)) for x, m, mod in zip(xs, mixed, mods)]
    hs = [_adaln(x2, gain(4), mod(6), mod(7)).reshape(n, D_MODEL).astype(BF16)
          for x2, mod, (_, _, _, n) in zip(x2s, mods, subs)]
    fs = [_swiglu(h, wg_ref, wu_ref, wd_ref).reshape(x.shape) for h, x in zip(hs, xs)]
    for x2, f, mod, (gs, rs, _, _) in zip(x2s, fs, mods, subs):
        y_ref[gs, rs, :] = x2 + (FFN_RES * mod(8)) * _rms(f, gain(5))


def _out_ffn2(x1, o, mod, gains, gg, wout, wg, wu, wd, gt, rt):
    n_groups, rows, _ = x1.shape
    n_g, n_r = n_groups // gt, rows // rt
    tm = gt * rt
    vmem = (2 * 3 * D_MODEL * D_FF + 2 * wout.size
            + 2 * 2 * 4 * tm * D_MODEL + 2 * 2 * tm * 2 * WIDTH
            + tm * (2 * 4 + 2) * FF_SLAB_MAX + 5 * 4 * tm * D_MODEL)
    return pl.pallas_call(
        _out_ffn2_kernel,
        grid=(n_g, n_r),
        in_specs=[
            pl.BlockSpec((gt, rt, D_MODEL), lambda gi, ri: (gi, ri, 0)),
            pl.BlockSpec((tm, 2 * WIDTH), lambda gi, ri: (gi * n_r + ri, 0)),
            pl.BlockSpec((N_MOD, gt, 1, D_MODEL), lambda gi, ri: (0, gi, 0, 0)),
            _resident(gains.shape), _resident(gg.shape), _resident(wout.shape),
            _resident(wg.shape), _resident(wu.shape), _resident(wd.shape),
        ],
        out_specs=pl.BlockSpec((gt, rt, D_MODEL), lambda gi, ri: (gi, ri, 0)),
        out_shape=jax.ShapeDtypeStruct(x1.shape, F32),
        compiler_params=pltpu.CompilerParams(
            dimension_semantics=("arbitrary", "arbitrary"), vmem_limit_bytes=_vmem_limit(vmem)),
        name="out_ffn2",
    )(x1, o, mod, gains, gg, wout, wg, wu, wd)


def _softmax_terms(s, sink=None):
    m = jnp.max(s, axis=1, keepdims=True)
    if sink is not None:
        m = jnp.maximum(m, sink)
    e = jnp.exp2(s - m)
    denom = jnp.sum(e, axis=1, keepdims=True)
    if sink is not None:
        denom = denom + jnp.exp2(sink - m)
    return e.astype(BF16), 1.0 / denom


def _scores(q, k):
    return lax.dot_general(q, k, (((1,), (1,)), ((), ())), preferred_element_type=F32)


def _unstack_pair(z):
    n_q = z.shape[0] // 2
    is_lo = lax.broadcasted_iota(jnp.int32, (n_q, z.shape[1]), 1) < HEAD_DIM
    return jnp.where(is_lo, z[:n_q], z[n_q:])


def _sink_column(sinks_ref, pair, n_q):
    row = lax.broadcasted_iota(jnp.int32, (2 * n_q, 1), 0)
    return jnp.where(row < n_q, sinks_ref[pair], sinks_ref[N_PAIRS + pair]) * LOG2E


GROUPS_PER_STEP = 4
JOB_LAG = 3


def _attn_prompt_kernel(sinks_ref, qa_ref, qb_ref, ka_ref, va_ref, kb_ref, vb_ref, *rest):
    ba_refs, bb_refs = rest[:GROUPS_PER_STEP], rest[GROUPS_PER_STEP:2 * GROUPS_PER_STEP]
    n_cast = (len(rest) - 2 * GROUPS_PER_STEP - 1) // 2
    o_ref = rest[2 * GROUPS_PER_STEP + n_cast]
    for src, dst in zip(rest[2 * GROUPS_PER_STEP:2 * GROUPS_PER_STEP + n_cast], rest[len(rest) - n_cast:]):
        dst[...] = src[...].astype(BF16)
    slab = lambda p: slice(p * V7X_LANES, (p + 1) * V7X_LANES)
    bands = []
    first = pl.program_id(1) * GROUPS_PER_STEP
    for j in range(GROUPS_PER_STEP):
        start = lambda g, reach: jnp.maximum(g * GROUP_Q - reach, 0)
        off_a = pl.multiple_of(start(first + j, REACH_A) - start(first, REACH_A), GROUP_Q)
        off_b = pl.multiple_of(start(first + j, REACH_B) - start(first, REACH_B), GROUP_Q)
        bands.append((pl.ds(off_a, BAND_A), pl.ds(off_b, BAND_B)))

    def stacked(q_ref, j, p):
        rows = slice(j * GROUP_Q, (j + 1) * GROUP_Q)
        return jnp.concatenate([q_ref[rows, slab(p)], q_ref[rows, slab(N_PAIRS + p)]], axis=0)

    def score(mixer, j, p):
        if mixer == "a":
            bias = jnp.concatenate([ba_refs[j][0, 2 * p], ba_refs[j][0, 2 * p + 1]], axis=0)
            return _scores(stacked(qa_ref, j, p), ka_ref[bands[j][0], slab(p)]) + bias
        bias = jnp.concatenate([bb_refs[j][0, p], bb_refs[j][0, N_PAIRS + p]], axis=0)
        return _scores(stacked(qb_ref, j, p), kb_ref[bands[j][1], :]) + bias

    def weights(s, sink):
        m = jnp.max(s, axis=1, keepdims=True)
        if sink is None:
            return jnp.exp2(s - m).astype(BF16), None
        m = jnp.maximum(m, sink)
        return jnp.exp2(s - m).astype(BF16), jnp.exp2(sink - m)

    def attend(e, extra, v):
        z = jnp.dot(e, jnp.concatenate([v, jnp.ones_like(v)], axis=1), preferred_element_type=F32)
        denom = z[:, V7X_LANES:] if extra is None else z[:, V7X_LANES:] + extra
        return _unstack_pair(z[:, :V7X_LANES] * (1.0 / denom))

    jobs = [(mixer, j, p) for mixer in ("a", "b") for j in range(GROUPS_PER_STEP) for p in range(N_PAIRS)]
    s_all, w_all = {}, {}
    for t in range(len(jobs) + 2 * JOB_LAG):
        if t < len(jobs):
            s_all[t] = score(*jobs[t])
        if 0 <= t - JOB_LAG < len(jobs):
            mixer, j, p = jobs[t - JOB_LAG]
            w_all[t - JOB_LAG] = weights(s_all.pop(t - JOB_LAG),
                                         _sink_column(sinks_ref, p, GROUP_Q) if mixer == "b" else None)
        if 0 <= t - 2 * JOB_LAG < len(jobs):
            mixer, j, p = jobs[t - 2 * JOB_LAG]
            e, extra = w_all.pop(t - 2 * JOB_LAG)
            rows = slice(j * GROUP_Q, (j + 1) * GROUP_Q)
            if mixer == "a":
                o_ref[rows, slab(p)] = attend(e, extra, va_ref[bands[j][0], slab(p)]).astype(BF16)
            else:
                o_ref[rows, slab(N_PAIRS + p)] = attend(e, extra, vb_ref[bands[j][1], :]).astype(BF16)


def _attn_prompt(sinks, qa, qb, ka, va, kb, vb, bias_a, bias_b, batch, seq, cast=()):
    rows = GROUPS_PER_STEP * GROUP_Q
    n_steps = seq // rows
    cast_specs, cast_shapes = (zip(*[_cast_chunks(w, batch * n_steps, lambda b, s: b * n_steps + s) for w in cast])
                               if cast else ((), ()))
    q_spec = pl.BlockSpec((rows, 2 * WIDTH), lambda b, s: (b * n_steps + s, 0))
    kv_spec = lambda w, reach: pl.BlockSpec(
        (pl.Element(reach + rows), pl.Element(w)),
        lambda b, s: (pl.multiple_of(b * seq + jnp.maximum(s * rows - reach, 0), GROUP_Q), 0))
    bias_spec = lambda band, n_var, j: pl.BlockSpec(
        (1, N_HEADS, GROUP_Q, band),
        lambda b, s: (jnp.minimum(s * GROUPS_PER_STEP + j, n_var - 1), 0, 0, 0),
        pipeline_mode=pl.Buffered(1))
    vmem = (2 * 2 * 2 * ((REACH_A + rows) * WIDTH + (REACH_B + rows) * KV_WIDTH_B)
            + GROUPS_PER_STEP * 4 * N_HEADS * GROUP_Q * (BAND_A + BAND_B)
            + 2 * 2 * rows * 6 * WIDTH + GROUPS_PER_STEP * 6 * 4 * 2 * GROUP_Q * BAND_A
            + sum(2 * 6 * math.prod(spec.block_shape) for spec in cast_specs))
    out = pl.pallas_call(
        _attn_prompt_kernel,
        grid=(batch, n_steps),
        in_specs=[
            pl.BlockSpec(memory_space=pltpu.SMEM),
            q_spec, q_spec,
            kv_spec(WIDTH, REACH_A), kv_spec(WIDTH, REACH_A), kv_spec(KV_WIDTH_B, REACH_B), kv_spec(KV_WIDTH_B, REACH_B),
            *[bias_spec(BAND_A, N_VAR_A, j) for j in range(GROUPS_PER_STEP)],
            *[bias_spec(BAND_B, N_VAR_B, j) for j in range(GROUPS_PER_STEP)],
            *cast_specs,
        ],
        out_specs=[pl.BlockSpec((rows, 2 * WIDTH), lambda b, s: (b * n_steps + s, 0)), *cast_specs],
        out_shape=[jax.ShapeDtypeStruct((batch * seq, 2 * WIDTH), BF16), *cast_shapes],
        compiler_params=pltpu.CompilerParams(
            dimension_semantics=("arbitrary", "arbitrary"), vmem_limit_bytes=_vmem_limit(vmem)),
        name="attn_prompt",
    )(sinks, qa, qb, ka, va, kb, vb, *([bias_a] * GROUPS_PER_STEP), *([bias_b] * GROUPS_PER_STEP), *cast)
    return out[0], tuple(out[1:])


def _pair_ordered_cols(h):
    return (h % N_PAIRS) * V7X_LANES + (h // N_PAIRS) * HEAD_DIM


SAMPLE_ROWS_PER_STEP = 4


def _attn_sample_kernel(sinks_ref, qa_ref, qb_ref, ka_ref, va_ref, kb_ref, vb_ref,
                        cak_ref, cav_ref, cbk_ref, cbv_ref, ba_ref, bb_ref, o_ref):
    n_rows = cak_ref.shape[0]
    n_q = qa_ref.shape[0] // n_rows
    n_kv = cbk_ref.shape[1]
    per_kv = N_HEADS // n_kv
    la, lb = cak_ref.shape[3], cbk_ref.shape[3]
    head = lambda ref, r, idx: ref[r * n_q:(r + 1) * n_q, idx * HEAD_DIM:(idx + 1) * HEAD_DIM]

    def query(ref, r, lo):
        lo = lo if lo % V7X_LANES < HEAD_DIM else WIDTH + lo
        return ref[r * n_q:(r + 1) * n_q, lo:lo + HEAD_DIM]

    def scores(q, cache_t_ref, new_ref, r, idx):
        old = jnp.dot(q, cache_t_ref[r, idx].astype(BF16), preferred_element_type=F32)
        return jnp.concatenate([old, _scores(q, head(new_ref, r, idx))], axis=1)

    def values(e, cache_t_ref, new_ref, r, idx, n_old):
        old = _scores(e[:, :n_old], cache_t_ref[r, idx].astype(BF16))
        return old + jnp.dot(e[:, n_old:], head(new_ref, r, idx), preferred_element_type=F32)

    jobs_a = [(r, h) for r in range(n_rows) for h in range(N_HEADS)]
    jobs_b = [(r, c) for r in range(n_rows) for c in range(n_kv)]
    s_a = [scores(query(qa_ref, r, h * HEAD_DIM), cak_ref, ka_ref, r, h) + ba_ref[h] for r, h in jobs_a]
    s_b = []
    for r, c in jobs_b:
        heads = range(c * per_kv, (c + 1) * per_kv)
        q = jnp.concatenate([query(qb_ref, r, _pair_ordered_cols(h)) for h in heads], axis=0)
        bias = jnp.concatenate([bb_ref[h] for h in heads], axis=0)
        s_b.append(scores(q, cbk_ref, kb_ref, r, c) + bias)

    row = lax.broadcasted_iota(jnp.int32, (per_kv * n_q, 1), 0)
    sinks = []
    for c in range(n_kv):
        sink = jnp.zeros((per_kv * n_q, 1), F32)
        for i in range(per_kv):
            sink = jnp.where((row >= i * n_q) & (row < (i + 1) * n_q), sinks_ref[c * per_kv + i], sink)
        sinks.append(sink * LOG2E)
    p_a = [_softmax_terms(s) for s in s_a]
    p_b = [_softmax_terms(s, sinks[c]) for s, (_, c) in zip(s_b, jobs_b)]

    out_a = {job: values(e, cav_ref, va_ref, *job, la) * inv for job, (e, inv) in zip(jobs_a, p_a)}
    out_b = {}
    for (r, c), (e, inv) in zip(jobs_b, p_b):
        o = values(e, cbv_ref, vb_ref, r, c, lb) * inv
        for i in range(per_kv):
            out_b[(r, c * per_kv + i)] = o[i * n_q:(i + 1) * n_q]
    order_b = [h for p in range(N_PAIRS) for h in (p, N_PAIRS + p)]
    for r in range(n_rows):
        pieces = [out_a[(r, h)] for h in range(N_HEADS)] + [out_b[(r, h)] for h in order_b]
        o_ref[r * n_q:(r + 1) * n_q, :] = jnp.concatenate(pieces, axis=1).astype(BF16)


def _attn_sample(sinks, qa, qb, ka, va, kb, vb, cak, cav, cbk, cbv, bias_a, bias_b, batch, s_len):
    n_rows = SAMPLE_ROWS_PER_STEP
    row = lambda w: pl.BlockSpec((n_rows * s_len, w), lambda b: (b, 0))
    cache = lambda a: pl.BlockSpec((n_rows,) + a.shape[1:], lambda b: (b, 0, 0, 0))
    whole = lambda a: pl.BlockSpec(a.shape, lambda b: (0,) * a.ndim)
    return pl.pallas_call(
        _attn_sample_kernel,
        grid=(batch // n_rows,),
        in_specs=[
            pl.BlockSpec(memory_space=pltpu.SMEM),
            row(2 * WIDTH), row(2 * WIDTH), row(WIDTH), row(WIDTH), row(KV_WIDTH_B), row(KV_WIDTH_B),
            cache(cak), cache(cav), cache(cbk), cache(cbv),
            whole(bias_a), whole(bias_b),
        ],
        out_specs=pl.BlockSpec((n_rows * s_len, 2 * WIDTH), lambda b: (b, 0)),
        out_shape=jax.ShapeDtypeStruct((batch * s_len, 2 * WIDTH), BF16),
        compiler_params=pltpu.CompilerParams(dimension_semantics=("arbitrary",)),
        name="attn_sample",
    )(sinks, qa, qb, ka, va, kb, vb, cak, cav, cbk, cbv, bias_a, bias_b)


def _pair_order(x, axis):
    shape = x.shape
    split = shape[:axis] + (N_HEADS // N_PAIRS, N_PAIRS, HEAD_DIM) + shape[axis + 1:]
    return jnp.swapaxes(x.reshape(split), axis, axis + 1).reshape(shape)


def _prep_out_weights(w_out, group_gains):
    wout = jnp.concatenate([w_out[:WIDTH], _pair_order(w_out[WIDTH:], 0)], axis=0).astype(BF16)
    gg = jnp.concatenate([group_gains[:WIDTH], _pair_order(group_gains[WIDTH:], 0)])
    return wout, gg.reshape(1, 2 * WIDTH)


def kernel(x_prompt, x_sample, cache_a_k, cache_a_v, cache_b_k, cache_b_v, c_prompt, c_sample, w_mod, b_mod,
           norm_gains, w1_gate, w1_up, w1_down, w_in, w_out, group_gains, rel_bias_a, t5_bias_table, sinks_b,
           w2_gate, w2_up, w2_down):
    depth = w_mod.shape[0]
    assert depth == 1
    batch, seq, _ = x_prompt.shape
    s_batch, s_len, _ = x_sample.shape
    la, lb = cache_a_k.shape[2], cache_b_k.shape[2]
    assert la == REACH_A and lb == REACH_B and seq % 512 == 0

    c_rows = jnp.concatenate([c_prompt, c_sample], axis=0)
    pad_rows = (-c_rows.shape[0]) % 8
    c_rows = jnp.pad(c_rows, ((0, pad_rows), (0, 0)))
    mod = _modulation(c_rows, w_mod[0], b_mod[0])
    mod_p = mod[:, :batch].reshape(N_MOD, batch, 1, D_MODEL)
    mod_s = mod[:, batch:batch + s_batch].reshape(N_MOD, s_batch, 1, D_MODEL)

    bias_a, bias_b, bias_sa, bias_sb, *w1, win = _bias_tables(
        rel_bias_a[0], t5_bias_table, s_len, la, lb, cast=(w1_gate[0], w1_up[0], w1_down[0], w_in[0]))

    gains = norm_gains[0]
    wout, gg = _prep_out_weights(w_out[0], group_gains[0])
    sinks = sinks_b[0]

    rt_p = 512
    (x1p, qa, ka, va, qb, kb, vb, ka32, va32, kb32, vb32) = _ffn1_qkv(
        x_prompt, mod_p, gains, *w1, win, 1, rt_p)
    op, w2 = _attn_prompt(sinks, qa, qb, ka, va, kb, vb, bias_a, bias_b, batch, seq,
                          cast=(w2_gate[0], w2_up[0], w2_down[0]))
    y_prompt = _out_ffn2(x1p, op, mod_p, gains, gg, wout, *w2, 1, 2 * rt_p)
    rows_last = lambda t: jnp.transpose(t.reshape(depth, batch, -1, HEAD_DIM, t.shape[-1]), (0, 1, 4, 2, 3))
    new_p = tuple(rows_last(t) for t in (ka32, va32, kb32, vb32))

    gt_s = 16
    (x1s, qas, kas, vas, qbs, kbs, vbs, kas32, vas32, kbs32, vbs32) = _ffn1_qkv(
        x_sample, mod_s, gains, *w1, win, gt_s, s_len)
    os_ = _attn_sample(sinks, qas, qbs, kas, vas, kbs, vbs,
                       *(jnp.transpose(c[0], (0, 2, 3, 1)) for c in (cache_a_k, cache_a_v, cache_b_k, cache_b_v)),
                       bias_sa, bias_sb, s_batch, s_len)
    y_sample = _out_ffn2(x1s, os_, mod_s, gains, gg, wout, *w2, gt_s, s_len)
    new_s = (kas32.reshape(depth, s_batch, s_len, N_HEADS, HEAD_DIM),
             vas32.reshape(depth, s_batch, s_len, N_HEADS, HEAD_DIM),
             kbs32.reshape(depth, s_batch, s_len, 2, HEAD_DIM), vbs32.reshape(depth, s_batch, s_len, 2, HEAD_DIM))

    return (y_prompt, y_sample) + new_p + new_s
```

```python
import functools
import math

import jax
import jax.numpy as jnp
from jax import lax
from jax.experimental import pallas as pl
from jax.experimental.pallas import tpu as pltpu

F32 = jnp.float32
BF16 = jnp.bfloat16

D_MODEL = 1024
D_FF = 2816
CHUNK = 64
HEAD_DIM = 64
N_HEADS = 8
N_PAIRS = N_HEADS // 2
WIDTH = N_HEADS * HEAD_DIM
KV_WIDTH_B = 2 * HEAD_DIM
REACH_A = 8 * CHUNK
REACH_B = 2 * CHUNK
REL_CLIP_A = 128
N_BUCKETS = 32
N_MOD = 9
FFN_RES = 0.5
EPS = 1e-6
SCALE = HEAD_DIM ** -0.5
LOG2E = 1.4426950408889634
NEG_INF = -1e30

V7X_LANES = 128
V7X_VMEM_BYTES = 64 * 1024 * 1024
V7X_VMEM_LIMIT_CAP = 56 * 1024 * 1024

GROUP_Q = 2 * CHUNK
BAND_A = REACH_A + GROUP_Q
BAND_B = REACH_B + GROUP_Q
ORIGIN_A = 1024
WIDTH_MASTER_A = ORIGIN_A + BAND_A
ORIGIN_B = 256
WIDTH_MASTER_B = ORIGIN_B + BAND_B
N_VAR_A = REACH_A // GROUP_Q + 1
N_VAR_B = REACH_B // GROUP_Q + 1


def _vmem_limit(estimate_bytes):
    return int(min(max(estimate_bytes, 32 * 1024 * 1024), V7X_VMEM_LIMIT_CAP))


def _rms(x, gain):
    ms = jnp.mean(x * x, axis=-1, keepdims=True)
    return x * lax.rsqrt(ms + EPS) * gain


def _adaln(x, gain, shift, scale):
    return _rms(x, gain) * (1.0 + scale) + shift


def _silu(x):
    return x * (1.0 / (1.0 + jnp.exp(-x)))


def _mod_kernel(c_ref, w_ref, b_ref, o_ref):
    s = _silu(c_ref[...]).astype(BF16)
    o_ref[0] = jnp.dot(s, w_ref[...].astype(BF16), preferred_element_type=F32) + b_ref[0]


def _modulation(c_rows, w_mod, b_mod):
    rows = c_rows.shape[0]
    return pl.pallas_call(
        _mod_kernel,
        grid=(N_MOD,),
        in_specs=[
            pl.BlockSpec((rows, D_MODEL), lambda j: (0, 0)),
            pl.BlockSpec((D_MODEL, D_MODEL), lambda j: (0, j)),
            pl.BlockSpec((1, 1, D_MODEL), lambda j: (j, 0, 0)),
        ],
        out_specs=pl.BlockSpec((1, rows, D_MODEL), lambda j: (j, 0, 0)),
        out_shape=jax.ShapeDtypeStruct((N_MOD, rows, D_MODEL), F32),
        name="modulation",
    )(c_rows, w_mod, b_mod.reshape(N_MOD, 1, D_MODEL))


def _skew(base_row, n_rows):
    width = base_row.shape[1]
    m = jnp.broadcast_to(base_row, (n_rows, width))
    row = lax.broadcasted_iota(jnp.int32, (n_rows, width), 0)
    shift = 1
    while shift < n_rows:
        m = jnp.where((row & shift) != 0, pltpu.roll(m, shift, axis=1), m)
        shift *= 2
    return m


def _band_mask(n_rows, width, origin, n_prev):
    row = lax.broadcasted_iota(jnp.int32, (n_rows, width), 0)
    col = lax.broadcasted_iota(jnp.int32, (n_rows, width), 1)
    q_chunk = row >> 6
    k_chunk = (col >> 6) - (origin // CHUNK)
    return (k_chunk >= q_chunk - n_prev) & (k_chunk <= q_chunk)


def _t5_bucket(rel):
    half = N_BUCKETS // 2
    max_exact = half // 2
    n = jnp.abs(rel)
    n2 = n * n
    large = jnp.full(rel.shape, max_exact, jnp.int32)
    for k in range(1, half - max_exact):
        large = large + jnp.where(n2 >= (max_exact * max_exact) * (2 ** k), 1, 0)
    return jnp.where(rel > 0, half, 0) + jnp.where(n < max_exact, n, large)


def _bias_kernel(rel_ref, t5_ref, *refs, s_len, la, lb):
    n_cast = (len(refs) - 4) // 2
    for src, dst in zip(refs[:n_cast], refs[n_cast + 4:]):
        dst[...] = src[...].astype(BF16)
    ba_ref, bb_ref, bsa_ref, bsb_ref = refs[n_cast:n_cast + 4]

    tab = rel_ref[0] * LOG2E
    n_tab = 2 * REL_CLIP_A + 1
    first, last = tab[:, 0:1], tab[:, n_tab - 1:n_tab]
    win_lo = ORIGIN_A - 2 * REL_CLIP_A
    col = lax.broadcasted_iota(jnp.int32, (1, tab.shape[1] + V7X_LANES), 1)
    shifted = jnp.concatenate([jnp.zeros((1, V7X_LANES), F32), tab], axis=1)
    u_win = jnp.where(col < V7X_LANES, first, jnp.where(col >= V7X_LANES + n_tab, last, shifted))
    window = _skew(u_win, GROUP_Q)[:, V7X_LANES:]
    n_left = win_lo + V7X_LANES
    n_right = WIDTH_MASTER_A - n_left - window.shape[1]
    skew_a = jnp.concatenate([jnp.broadcast_to(first, (GROUP_Q, n_left)), window,
                              jnp.broadcast_to(last, (GROUP_Q, n_right))], axis=1)
    master_a = jnp.where(_band_mask(GROUP_Q, WIDTH_MASTER_A, ORIGIN_A, REACH_A // CHUNK), skew_a, NEG_INF)
    for v in range(N_VAR_A):
        start = ORIGIN_A - min(v * GROUP_Q, REACH_A) if v < N_VAR_A - 1 else ORIGIN_A - REACH_A
        ba_ref[v, 0] = master_a[:, start:start + BAND_A]
    bsa_ref[0] = skew_a[:s_len, ORIGIN_A - la:ORIGIN_A + s_len]

    t5 = t5_ref[0] * LOG2E
    rel = lax.broadcasted_iota(jnp.int32, (1, WIDTH_MASTER_B), 1) - ORIGIN_B
    bucket = _t5_bucket(rel)
    u_b = jnp.zeros((1, WIDTH_MASTER_B), F32)
    for i in range(N_BUCKETS):
        u_b = jnp.where(bucket == i, t5[:, i:i + 1], u_b)
    skew_b = _skew(u_b, GROUP_Q)
    master_b = jnp.where(_band_mask(GROUP_Q, WIDTH_MASTER_B, ORIGIN_B, REACH_B // CHUNK), skew_b, NEG_INF)
    for v in range(N_VAR_B):
        start = ORIGIN_B - min(v * GROUP_Q, REACH_B) if v < N_VAR_B - 1 else ORIGIN_B - REACH_B
        bb_ref[v, 0] = master_b[:, start:start + BAND_B]
    bsb_ref[0] = skew_b[:s_len, ORIGIN_B - lb:ORIGIN_B + s_len]


V7X_BF16_SUBLANES = 16


def _cast_chunks(w, n_steps, step_of):
    rows, cols = w.shape
    n_chunks = n_steps
    while rows % (n_chunks * V7X_BF16_SUBLANES):
        n_chunks //= 2
    spec = pl.BlockSpec((rows // n_chunks, cols), lambda *g: (jnp.minimum(step_of(*g), n_chunks - 1), 0))
    return spec, jax.ShapeDtypeStruct(w.shape, BF16)


def _bias_tables(rel_bias_a, t5_table, s_len, la, lb, cast=()):
    n_tab = 2 * REL_CLIP_A + 1
    tab_w = 3 * V7X_LANES
    rel_p = jnp.pad(rel_bias_a, ((0, 0), (0, tab_w - n_tab))).reshape(N_HEADS, 1, tab_w)
    t5_p = t5_table.reshape(N_HEADS, 1, N_BUCKETS)
    kern = functools.partial(_bias_kernel, s_len=s_len, la=la, lb=lb)
    cast_specs, cast_shapes = zip(*[_cast_chunks(w, N_HEADS, lambda h: h) for w in cast]) if cast else ((), ())
    return pl.pallas_call(
        kern,
        grid=(N_HEADS,),
        in_specs=[
            pl.BlockSpec((1, 1, tab_w), lambda h: (h, 0, 0)),
            pl.BlockSpec((1, 1, N_BUCKETS), lambda h: (h, 0, 0)),
            *cast_specs,
        ],
        out_specs=[
            pl.BlockSpec((N_VAR_A, 1, GROUP_Q, BAND_A), lambda h: (0, h, 0, 0)),
            pl.BlockSpec((N_VAR_B, 1, GROUP_Q, BAND_B), lambda h: (0, h, 0, 0)),
            pl.BlockSpec((1, s_len, la + s_len), lambda h: (h, 0, 0)),
            pl.BlockSpec((1, s_len, lb + s_len), lambda h: (h, 0, 0)),
            *cast_specs,
        ],
        out_shape=[
            jax.ShapeDtypeStruct((N_VAR_A, N_HEADS, GROUP_Q, BAND_A), F32),
            jax.ShapeDtypeStruct((N_VAR_B, N_HEADS, GROUP_Q, BAND_B), F32),
            jax.ShapeDtypeStruct((N_HEADS, s_len, la + s_len), F32),
            jax.ShapeDtypeStruct((N_HEADS, s_len, lb + s_len), F32),
            *cast_shapes,
        ],
        compiler_params=pltpu.CompilerParams(dimension_semantics=("arbitrary",)),
        name="bias_tables",
    )(rel_p, t5_p, *cast)


V7X_MXU_DIM = 256
FF_SLABS = ((0, 6 * V7X_MXU_DIM), (6 * V7X_MXU_DIM, D_FF))
FF_SLAB_MAX = max(hi - lo for lo, hi in FF_SLABS)


def _swiglu(h, wg_ref, wu_ref, wd_ref):
    acts = []
    for lo, hi in FF_SLABS:
        g = jnp.dot(h, wg_ref[:, lo:hi], preferred_element_type=F32)
        u = jnp.dot(h, wu_ref[:, lo:hi], preferred_element_type=F32)
        acts.append((_silu(g) * u).astype(BF16))
    acc = None
    for (lo, hi), a in zip(FF_SLABS, acts):
        part = jnp.dot(a, wd_ref[lo:hi, :], preferred_element_type=F32)
        acc = part if acc is None else acc + part
    return acc


SUB_TOKENS = V7X_MXU_DIM


def _sub_tiles(gt, rt):
    n_sub = gt * rt // SUB_TOKENS
    if gt == 1:
        step = rt // n_sub
        return [(slice(0, 1), slice(s * step, (s + 1) * step), s * step, step) for s in range(n_sub)]
    step = gt // n_sub
    return [(slice(s * step, (s + 1) * step), slice(0, rt), s * step * rt, step * rt) for s in range(n_sub)]


def _ffn1_qkv_kernel(x_ref, mod_ref, gains_ref, wg_ref, wu_ref, wd_ref, win_ref,
                     x1_ref, qa_ref, ka_ref, va_ref, qb_ref, kb_ref, vb_ref,
                     ka32_ref, va32_ref, kb32_ref, vb32_ref, *tail_scratch):
    gt, rt, _ = x_ref.shape
    tm = gt * rt
    gain = lambda k: gains_ref[k:k + 1, :]

    def store_tail(ref, val, r0):
        first = tm - ref.shape[0]
        lo = max(r0, first)
        if lo < r0 + val.shape[0]:
            ref[lo - first:r0 + val.shape[0] - first, :] = val[lo - r0:, :]

    def store_heads(ref, val, r0):
        n_heads = val.shape[1] // HEAD_DIM
        for h in range(n_heads):
            ref[pl.ds(r0 * n_heads + h, val.shape[0], stride=n_heads), :] = val[:, h * HEAD_DIM:(h + 1) * HEAD_DIM]

    def store_split(ref, rows, q, pairs):
        q = q * (SCALE * LOG2E)
        head = lambda h: q[:, h * HEAD_DIM:(h + 1) * HEAD_DIM]
        zeros = jnp.zeros((q.shape[0], HEAD_DIM), F32)
        ref[rows, :WIDTH] = jnp.concatenate([t for a, _ in pairs for t in (head(a), zeros)], axis=1).astype(BF16)
        ref[rows, WIDTH:] = jnp.concatenate([t for _, b in pairs for t in (zeros, head(b))], axis=1).astype(BF16)

    pairs_a = [(2 * p, 2 * p + 1) for p in range(N_PAIRS)]
    pairs_b = [(p, N_PAIRS + p) for p in range(N_PAIRS)]

    subs = _sub_tiles(gt, rt)
    mods = [lambda k, gs=gs: mod_ref[k, gs] for gs, _, _, _ in subs]
    xs = [x_ref[gs, rs, :] for gs, rs, _, _ in subs]
    hs = [_adaln(x, gain(0), mod(0), mod(1)).reshape(n, D_MODEL).astype(BF16)
          for x, mod, (_, _, _, n) in zip(xs, mods, subs)]
    fs = [_swiglu(h, wg_ref, wu_ref, wd_ref).reshape(x.shape) for h, x in zip(hs, xs)]
    h2s = []
    for x, f, mod, (gs, rs, _, n) in zip(xs, fs, mods, subs):
        x1 = x + (FFN_RES * mod(2)) * _rms(f, gain(1))
        x1_ref[gs, rs, :] = x1
        h2s.append(_adaln(x1, gain(2), mod(3), mod(4)).reshape(n, D_MODEL).astype(BF16))

    for h2, (_, _, r0, n) in zip(h2s, subs):
        rows = slice(r0, r0 + n)

        def proj(lo, width):
            return jnp.dot(h2, win_ref[:, lo:lo + width], preferred_element_type=F32)

        store_split(qa_ref, rows, proj(0, WIDTH), pairs_a)
        ka = proj(WIDTH, WIDTH)
        va = proj(2 * WIDTH, WIDTH)
        store_split(qb_ref, rows, proj(3 * WIDTH, WIDTH), pairs_b)
        kvb = proj(4 * WIDTH, 2 * KV_WIDTH_B)
        kb, vb = kvb[:, :KV_WIDTH_B], kvb[:, KV_WIDTH_B:]
        ka_ref[rows, :] = ka.astype(BF16)
        va_ref[rows, :] = va.astype(BF16)
        kb_ref[rows, :] = kb.astype(BF16)
        vb_ref[rows, :] = vb.astype(BF16)
        if tail_scratch:
            for ref, val in zip(tail_scratch, (ka, va, kb, vb)):
                store_tail(ref, val, r0)
        else:
            store_heads(ka32_ref, ka, r0)
            store_heads(va32_ref, va, r0)
            store_tail(kb32_ref, kb, r0)
            store_tail(vb32_ref, vb, r0)

    if tail_scratch:
        @pl.when(pl.program_id(1) == pl.num_programs(1) - 1)
        def _():
            for out_ref, ref in zip((ka32_ref, va32_ref, kb32_ref, vb32_ref), tail_scratch):
                out_ref[...] = ref[...].T


def _resident(shape):
    return pl.BlockSpec(shape, lambda *_: (0,) * len(shape), pipeline_mode=pl.Buffered(1))


def _ffn1_qkv(x, mod, gains, wg, wu, wd, win, gt, rt):
    n_groups, rows, _ = x.shape
    n_g, n_r = n_groups // gt, rows // rt
    tm = gt * rt
    tokens = n_groups * rows
    ta = gt * min(REACH_A, rows)
    tb = gt * min(REACH_B, rows)
    tile = lambda gi, ri: (gi * n_r + ri, 0)
    tail = lambda gi, ri: (gi, 0)
    tok = lambda w, dt: jax.ShapeDtypeStruct((tokens, w), dt)
    if gt == 1:
        assert ta <= tm and tb <= tm
        tail_specs = [pl.BlockSpec((None, w, t), lambda gi, ri: (gi, 0, 0))
                      for w, t in ((WIDTH, ta), (WIDTH, ta), (KV_WIDTH_B, tb), (KV_WIDTH_B, tb))]
        tail_shapes = [jax.ShapeDtypeStruct((n_g, w, t), F32)
                       for w, t in ((WIDTH, ta), (WIDTH, ta), (KV_WIDTH_B, tb), (KV_WIDTH_B, tb))]
        scratch = [pltpu.VMEM((ta, WIDTH), F32), pltpu.VMEM((ta, WIDTH), F32),
                   pltpu.VMEM((tb, KV_WIDTH_B), F32), pltpu.VMEM((tb, KV_WIDTH_B), F32)]
    else:
        assert n_r == 1
        tail_specs = [pl.BlockSpec((tm * N_HEADS, HEAD_DIM), tail), pl.BlockSpec((tm * N_HEADS, HEAD_DIM), tail),
                      pl.BlockSpec((tm, KV_WIDTH_B), tail), pl.BlockSpec((tm, KV_WIDTH_B), tail)]
        tail_shapes = [jax.ShapeDtypeStruct((tokens * N_HEADS, HEAD_DIM), F32),
                       jax.ShapeDtypeStruct((tokens * N_HEADS, HEAD_DIM), F32),
                       jax.ShapeDtypeStruct((tokens, KV_WIDTH_B), F32), jax.ShapeDtypeStruct((tokens, KV_WIDTH_B), F32)]
        scratch = []
    vmem = (2 * 3 * D_MODEL * D_FF + 2 * D_MODEL * win.shape[1]
            + 2 * 2 * 4 * tm * D_MODEL
            + 2 * 2 * tm * (6 * WIDTH + 2 * KV_WIDTH_B)
            + 2 * 4 * (2 * ta * WIDTH + 2 * tb * KV_WIDTH_B)
            + tm * (2 * 4 + 2) * FF_SLAB_MAX + 3 * 4 * tm * D_MODEL
            + 4 * tm * win.shape[1])
    return pl.pallas_call(
        _ffn1_qkv_kernel,
        grid=(n_g, n_r),
        in_specs=[
            pl.BlockSpec((gt, rt, D_MODEL), lambda gi, ri: (gi, ri, 0)),
            pl.BlockSpec((N_MOD, gt, 1, D_MODEL), lambda gi, ri: (0, gi, 0, 0)),
            _resident(gains.shape),
            _resident(wg.shape), _resident(wu.shape), _resident(wd.shape), _resident(win.shape),
        ],
        out_specs=[
            pl.BlockSpec((gt, rt, D_MODEL), lambda gi, ri: (gi, ri, 0)),
            pl.BlockSpec((tm, 2 * WIDTH), tile), pl.BlockSpec((tm, WIDTH), tile), pl.BlockSpec((tm, WIDTH), tile),
            pl.BlockSpec((tm, 2 * WIDTH), tile), pl.BlockSpec((tm, KV_WIDTH_B), tile),
            pl.BlockSpec((tm, KV_WIDTH_B), tile),
            *tail_specs,
        ],
        out_shape=[
            jax.ShapeDtypeStruct(x.shape, F32),
            tok(2 * WIDTH, BF16), tok(WIDTH, BF16), tok(WIDTH, BF16),
            tok(2 * WIDTH, BF16), tok(KV_WIDTH_B, BF16), tok(KV_WIDTH_B, BF16),
            *tail_shapes,
        ],
        scratch_shapes=scratch,
        compiler_params=pltpu.CompilerParams(
            dimension_semantics=("arbitrary", "arbitrary"), vmem_limit_bytes=_vmem_limit(vmem)),
        name="ffn1_qkv",
    )(x, mod, gains, wg, wu, wd, win)


def _out_ffn2_kernel(x_ref, o_ref, mod_ref, gains_ref, gg_ref, wout_ref, wg_ref, wu_ref, wd_ref, y_ref):
    gt, rt, _ = x_ref.shape
    gain = lambda k: gains_ref[k:k + 1, :]
    gg = gg_ref[...]
    subs = _sub_tiles(gt, rt)
    mods = [lambda k, gs=gs: mod_ref[k, gs] for gs, _, _, _ in subs]
    xs = [x_ref[gs, rs, :] for gs, rs, _, _ in subs]
    ons = []
    for _, _, r0, n in subs:
        o = o_ref[r0:r0 + n, :].astype(F32)
        on = jnp.concatenate([_rms(o[:, :WIDTH], gg[:, :WIDTH]), _rms(o[:, WIDTH:], gg[:, WIDTH:])], axis=1)
        ons.append(on.astype(BF16))
    mixed = [jnp.dot(on, wout_ref[...], preferred_element_type=F32).reshape(x.shape) for on, x in zip(ons, xs)]
    x2s = [x + mod(5) * _rms(m, gain(3)) for x, m, mod in zip(xs, mixed, mods)]
    hs = [_adaln(x2, gain(4), mod(6), mod(7)).reshape(n, D_MODEL).astype(BF16)
          for x2, mod, (_, _, _, n) in zip(x2s, mods, subs)]
    fs = [_swiglu(h, wg_ref, wu_ref, wd_ref).reshape(x.shape) for h, x in zip(hs, xs)]
    for x2, f, mod, (gs, rs, _, _) in zip(x2s, fs, mods, subs):
        y_ref[gs, rs, :] = x2 + (FFN_RES * mod(8)) * _rms(f, gain(5))


def _out_ffn2(x1, o, mod, gains, gg, wout, wg, wu, wd, gt, rt):
    n_groups, rows, _ = x1.shape
    n_g, n_r = n_groups // gt, rows // rt
    tm = gt * rt
    vmem = (2 * 3 * D_MODEL * D_FF + 2 * wout.size
            + 2 * 2 * 4 * tm * D_MODEL + 2 * 2 * tm * 2 * WIDTH
            + tm * (2 * 4 + 2) * FF_SLAB_MAX + 5 * 4 * tm * D_MODEL)
    return pl.pallas_call(
        _out_ffn2_kernel,
        grid=(n_g, n_r),
        in_specs=[
            pl.BlockSpec((gt, rt, D_MODEL), lambda gi, ri: (gi, ri, 0)),
            pl.BlockSpec((tm, 2 * WIDTH), lambda gi, ri: (gi * n_r + ri, 0)),
            pl.BlockSpec((N_MOD, gt, 1, D_MODEL), lambda gi, ri: (0, gi, 0, 0)),
            _resident(gains.shape), _resident(gg.shape), _resident(wout.shape),
            _resident(wg.shape), _resident(wu.shape), _resident(wd.shape),
        ],
        out_specs=pl.BlockSpec((gt, rt, D_MODEL), lambda gi, ri: (gi, ri, 0)),
        out_shape=jax.ShapeDtypeStruct(x1.shape, F32),
        compiler_params=pltpu.CompilerParams(
            dimension_semantics=("arbitrary", "arbitrary"), vmem_limit_bytes=_vmem_limit(vmem)),
        name="out_ffn2",
    )(x1, o, mod, gains, gg, wout, wg, wu, wd)


def _softmax_terms(s, sink=None):
    m = jnp.max(s, axis=1, keepdims=True)
    if sink is not None:
        m = jnp.maximum(m, sink)
    e = jnp.exp2(s - m)
    denom = jnp.sum(e, axis=1, keepdims=True)
    if sink is not None:
        denom = denom + jnp.exp2(sink - m)
    return e.astype(BF16), 1.0 / denom


def _scores(q, k):
    return lax.dot_general(q, k, (((1,), (1,)), ((), ())), preferred_element_type=F32)


def _unstack_pair(z):
    n_q = z.shape[0] // 2
    is_lo = lax.broadcasted_iota(jnp.int32, (n_q, z.shape[1]), 1) < HEAD_DIM
    return jnp.where(is_lo, z[:n_q], z[n_q:])


def _sink_column(sinks_ref, pair, n_q):
    row = lax.broadcasted_iota(jnp.int32, (2 * n_q, 1), 0)
    return jnp.where(row < n_q, sinks_ref[pair], sinks_ref[N_PAIRS + pair]) * LOG2E


GROUPS_PER_STEP = 2
JOB_LAG = 4


def _attn_prompt_kernel(sinks_ref, qa_ref, qb_ref, ka_ref, va_ref, kb_ref, vb_ref, *rest):
    ba_refs, bb_refs = rest[:GROUPS_PER_STEP], rest[GROUPS_PER_STEP:2 * GROUPS_PER_STEP]
    n_cast = (len(rest) - 2 * GROUPS_PER_STEP - 1) // 2
    o_ref = rest[2 * GROUPS_PER_STEP + n_cast]
    for src, dst in zip(rest[2 * GROUPS_PER_STEP:2 * GROUPS_PER_STEP + n_cast], rest[len(rest) - n_cast:]):
        dst[...] = src[...].astype(BF16)
    slab = lambda p: slice(p * V7X_LANES, (p + 1) * V7X_LANES)
    bands = []
    first = pl.program_id(1) * GROUPS_PER_STEP
    for j in range(GROUPS_PER_STEP):
        start = lambda g, reach: jnp.maximum(g * GROUP_Q - reach, 0)
        off_a = pl.multiple_of(start(first + j, REACH_A) - start(first, REACH_A), GROUP_Q)
        off_b = pl.multiple_of(start(first + j, REACH_B) - start(first, REACH_B), GROUP_Q)
        bands.append((pl.ds(off_a, BAND_A), pl.ds(off_b, BAND_B)))

    def stacked(q_ref, j, p):
        rows = slice(j * GROUP_Q, (j + 1) * GROUP_Q)
        return jnp.concatenate([q_ref[rows, slab(p)], q_ref[rows, slab(N_PAIRS + p)]], axis=0)

    def score(mixer, j, p):
        if mixer == "a":
            bias = jnp.concatenate([ba_refs[j][0, 2 * p], ba_refs[j][0, 2 * p + 1]], axis=0)
            return _scores(stacked(qa_ref, j, p), ka_ref[bands[j][0], slab(p)]) + bias
        bias = jnp.concatenate([bb_refs[j][0, p], bb_refs[j][0, N_PAIRS + p]], axis=0)
        return _scores(stacked(qb_ref, j, p), kb_ref[bands[j][1], :]) + bias

    def weights(s, sink):
        m = jnp.max(s, axis=1, keepdims=True)
        if sink is None:
            return jnp.exp2(s - m).astype(BF16), None
        m = jnp.maximum(m, sink)
        return jnp.exp2(s - m).astype(BF16), jnp.exp2(sink - m)

    def attend(e, extra, v):
        z = jnp.dot(e, jnp.concatenate([v, jnp.ones_like(v)], axis=1), preferred_element_type=F32)
        denom = z[:, V7X_LANES:] if extra is None else z[:, V7X_LANES:] + extra
        return _unstack_pair(z[:, :V7X_LANES] * (1.0 / denom))

    jobs = [(mixer, j, p) for mixer in ("a", "b") for j in range(GROUPS_PER_STEP) for p in range(N_PAIRS)]
    s_all, w_all = {}, {}
    for t in range(len(jobs) + 2 * JOB_LAG):
        if t < len(jobs):
            s_all[t] = score(*jobs[t])
        if 0 <= t - JOB_LAG < len(jobs):
            mixer, j, p = jobs[t - JOB_LAG]
            w_all[t - JOB_LAG] = weights(s_all.pop(t - JOB_LAG),
                                         _sink_column(sinks_ref, p, GROUP_Q) if mixer == "b" else None)
        if 0 <= t - 2 * JOB_LAG < len(jobs):
            mixer, j, p = jobs[t - 2 * JOB_LAG]
            e, extra = w_all.pop(t - 2 * JOB_LAG)
            rows = slice(j * GROUP_Q, (j + 1) * GROUP_Q)
            if mixer == "a":
                o_ref[rows, slab(p)] = attend(e, extra, va_ref[bands[j][0], slab(p)]).astype(BF16)
            else:
                o_ref[rows, slab(N_PAIRS + p)] = attend(e, extra, vb_ref[bands[j][1], :]).astype(BF16)


def _attn_prompt(sinks, qa, qb, ka, va, kb, vb, bias_a, bias_b, batch, seq, cast=()):
    rows = GROUPS_PER_STEP * GROUP_Q
    n_steps = seq // rows
    cast_specs, cast_shapes = (zip(*[_cast_chunks(w, batch * n_steps, lambda b, s: b * n_steps + s) for w in cast])
                               if cast else ((), ()))
    q_spec = pl.BlockSpec((rows, 2 * WIDTH), lambda b, s: (b * n_steps + s, 0))
    kv_spec = lambda w, reach: pl.BlockSpec(
        (pl.Element(reach + rows), pl.Element(w)),
        lambda b, s: (pl.multiple_of(b * seq + jnp.maximum(s * rows - reach, 0), GROUP_Q), 0))
    bias_spec = lambda band, n_var, j: pl.BlockSpec(
        (1, N_HEADS, GROUP_Q, band),
        lambda b, s: (jnp.minimum(s * GROUPS_PER_STEP + j, n_var - 1), 0, 0, 0))
    vmem = (2 * 2 * 2 * ((REACH_A + rows) * WIDTH + (REACH_B + rows) * KV_WIDTH_B)
            + GROUPS_PER_STEP * 2 * 4 * N_HEADS * GROUP_Q * (BAND_A + BAND_B)
            + 2 * 2 * rows * 6 * WIDTH + GROUPS_PER_STEP * 6 * 4 * 2 * GROUP_Q * BAND_A
            + sum(2 * 6 * math.prod(spec.block_shape) for spec in cast_specs))
    out = pl.pallas_call(
        _attn_prompt_kernel,
        grid=(batch, n_steps),
        in_specs=[
            pl.BlockSpec(memory_space=pltpu.SMEM),
            q_spec, q_spec,
            kv_spec(WIDTH, REACH_A), kv_spec(WIDTH, REACH_A), kv_spec(KV_WIDTH_B, REACH_B), kv_spec(KV_WIDTH_B, REACH_B),
            *[bias_spec(BAND_A, N_VAR_A, j) for j in range(GROUPS_PER_STEP)],
            *[bias_spec(BAND_B, N_VAR_B, j) for j in range(GROUPS_PER_STEP)],
            *cast_specs,
        ],
        out_specs=[pl.BlockSpec((rows, 2 * WIDTH), lambda b, s: (b * n_steps + s, 0)), *cast_specs],
        out_shape=[jax.ShapeDtypeStruct((batch * seq, 2 * WIDTH), BF16), *cast_shapes],
        compiler_params=pltpu.CompilerParams(
            dimension_semantics=("arbitrary", "arbitrary"), vmem_limit_bytes=_vmem_limit(vmem)),
        name="attn_prompt",
    )(sinks, qa, qb, ka, va, kb, vb, *([bias_a] * GROUPS_PER_STEP), *([bias_b] * GROUPS_PER_STEP), *cast)
    return out[0], tuple(out[1:])


def _pair_ordered_cols(h):
    return (h % N_PAIRS) * V7X_LANES + (h // N_PAIRS) * HEAD_DIM


SAMPLE_ROWS_PER_STEP = 4


def _attn_sample_kernel(sinks_ref, qa_ref, qb_ref, ka_ref, va_ref, kb_ref, vb_ref,
                        cak_ref, cav_ref, cbk_ref, cbv_ref, ba_ref, bb_ref, o_ref):
    n_rows = cak_ref.shape[0]
    n_q = qa_ref.shape[0] // n_rows
    n_kv = cbk_ref.shape[1]
    per_kv = N_HEADS // n_kv
    la, lb = cak_ref.shape[3], cbk_ref.shape[3]
    head = lambda ref, r, idx: ref[r * n_q:(r + 1) * n_q, idx * HEAD_DIM:(idx + 1) * HEAD_DIM]

    def query(ref, r, lo):
        lo = lo if lo % V7X_LANES < HEAD_DIM else WIDTH + lo
        return ref[r * n_q:(r + 1) * n_q, lo:lo + HEAD_DIM]

    def scores(q, cache_t_ref, new_ref, r, idx):
        old = jnp.dot(q, cache_t_ref[r, idx].astype(BF16), preferred_element_type=F32)
        return jnp.concatenate([old, _scores(q, head(new_ref, r, idx))], axis=1)

    def values(e, cache_t_ref, new_ref, r, idx, n_old):
        old = _scores(e[:, :n_old], cache_t_ref[r, idx].astype(BF16))
        return old + jnp.dot(e[:, n_old:], head(new_ref, r, idx), preferred_element_type=F32)

    jobs_a = [(r, h) for r in range(n_rows) for h in range(N_HEADS)]
    jobs_b = [(r, c) for r in range(n_rows) for c in range(n_kv)]
    s_a = [scores(query(qa_ref, r, h * HEAD_DIM), cak_ref, ka_ref, r, h) + ba_ref[h] for r, h in jobs_a]
    s_b = []
    for r, c in jobs_b:
        heads = range(c * per_kv, (c + 1) * per_kv)
        q = jnp.concatenate([query(qb_ref, r, _pair_ordered_cols(h)) for h in heads], axis=0)
        bias = jnp.concatenate([bb_ref[h] for h in heads], axis=0)
        s_b.append(scores(q, cbk_ref, kb_ref, r, c) + bias)

    row = lax.broadcasted_iota(jnp.int32, (per_kv * n_q, 1), 0)
    sinks = []
    for c in range(n_kv):
        sink = jnp.zeros((per_kv * n_q, 1), F32)
        for i in range(per_kv):
            sink = jnp.where((row >= i * n_q) & (row < (i + 1) * n_q), sinks_ref[c * per_kv + i], sink)
        sinks.append(sink * LOG2E)
    p_a = [_softmax_terms(s) for s in s_a]
    p_b = [_softmax_terms(s, sinks[c]) for s, (_, c) in zip(s_b, jobs_b)]

    out_a = {job: values(e, cav_ref, va_ref, *job, la) * inv for job, (e, inv) in zip(jobs_a, p_a)}
    out_b = {}
    for (r, c), (e, inv) in zip(jobs_b, p_b):
        o = values(e, cbv_ref, vb_ref, r, c, lb) * inv
        for i in range(per_kv):
            out_b[(r, c * per_kv + i)] = o[i * n_q:(i + 1) * n_q]
    order_b = [h for p in range(N_PAIRS) for h in (p, N_PAIRS + p)]
    for r in range(n_rows):
        pieces = [out_a[(r, h)] for h in range(N_HEADS)] + [out_b[(r, h)] for h in order_b]
        o_ref[r * n_q:(r + 1) * n_q, :] = jnp.concatenate(pieces, axis=1).astype(BF16)


def _attn_sample(sinks, qa, qb, ka, va, kb, vb, cak, cav, cbk, cbv, bias_a, bias_b, batch, s_len):
    n_rows = SAMPLE_ROWS_PER_STEP
    row = lambda w: pl.BlockSpec((n_rows * s_len, w), lambda b: (b, 0))
    cache = lambda a: pl.BlockSpec((n_rows,) + a.shape[1:], lambda b: (b, 0, 0, 0))
    whole = lambda a: pl.BlockSpec(a.shape, lambda b: (0,) * a.ndim)
    return pl.pallas_call(
        _attn_sample_kernel,
        grid=(batch // n_rows,),
        in_specs=[
            pl.BlockSpec(memory_space=pltpu.SMEM),
            row(2 * WIDTH), row(2 * WIDTH), row(WIDTH), row(WIDTH), row(KV_WIDTH_B), row(KV_WIDTH_B),
            cache(cak), cache(cav), cache(cbk), cache(cbv),
            whole(bias_a), whole(bias_b),
        ],
        out_specs=pl.BlockSpec((n_rows * s_len, 2 * WIDTH), lambda b: (b, 0)),
        out_shape=jax.ShapeDtypeStruct((batch * s_len, 2 * WIDTH), BF16),
        compiler_params=pltpu.CompilerParams(dimension_semantics=("arbitrary",)),
        name="attn_sample",
    )(sinks, qa, qb, ka, va, kb, vb, cak, cav, cbk, cbv, bias_a, bias_b)


def _pair_order(x, axis):
    shape = x.shape
    split = shape[:axis] + (N_HEADS // N_PAIRS, N_PAIRS, HEAD_DIM) + shape[axis + 1:]
    return jnp.swapaxes(x.reshape(split), axis, axis + 1).reshape(shape)


def _prep_out_weights(w_out, group_gains):
    wout = jnp.concatenate([w_out[:WIDTH], _pair_order(w_out[WIDTH:], 0)], axis=0).astype(BF16)
    gg = jnp.concatenate([group_gains[:WIDTH], _pair_order(group_gains[WIDTH:], 0)])
    return wout, gg.reshape(1, 2 * WIDTH)


def kernel(x_prompt, x_sample, cache_a_k, cache_a_v, cache_b_k, cache_b_v, c_prompt, c_sample, w_mod, b_mod,
           norm_gains, w1_gate, w1_up, w1_down, w_in, w_out, group_gains, rel_bias_a, t5_bias_table, sinks_b,
           w2_gate, w2_up, w2_down):
    depth = w_mod.shape[0]
    assert depth == 1
    batch, seq, _ = x_prompt.shape
    s_batch, s_len, _ = x_sample.shape
    la, lb = cache_a_k.shape[2], cache_b_k.shape[2]
    assert la == REACH_A and lb == REACH_B and seq % 512 == 0

    c_rows = jnp.concatenate([c_prompt, c_sample], axis=0)
    pad_rows = (-c_rows.shape[0]) % 8
    c_rows = jnp.pad(c_rows, ((0, pad_rows), (0, 0)))
    mod = _modulation(c_rows, w_mod[0], b_mod[0])
    mod_p = mod[:, :batch].reshape(N_MOD, batch, 1, D_MODEL)
    mod_s = mod[:, batch:batch + s_batch].reshape(N_MOD, s_batch, 1, D_MODEL)

    bias_a, bias_b, bias_sa, bias_sb, *w1, win = _bias_tables(
        rel_bias_a[0], t5_bias_table, s_len, la, lb, cast=(w1_gate[0], w1_up[0], w1_down[0], w_in[0]))

    gains = norm_gains[0]
    wout, gg = _prep_out_weights(w_out[0], group_gains[0])
    sinks = sinks_b[0]

    rt_p = 512
    (x1p, qa, ka, va, qb, kb, vb, ka32, va32, kb32, vb32) = _ffn1_qkv(
        x_prompt, mod_p, gains, *w1, win, 1, rt_p)
    op, w2 = _attn_prompt(sinks, qa, qb, ka, va, kb, vb, bias_a, bias_b, batch, seq,
                          cast=(w2_gate[0], w2_up[0], w2_down[0]))
    y_prompt = _out_ffn2(x1p, op, mod_p, gains, gg, wout, *w2, 1, 2 * rt_p)
    rows_last = lambda t: jnp.transpose(t.reshape(depth, batch, -1, HEAD_DIM, t.shape[-1]), (0, 1, 4, 2, 3))
    new_p = tuple(rows_last(t) for t in (ka32, va32, kb32, vb32))

    gt_s = 16
    (x1s, qas, kas, vas, qbs, kbs, vbs, kas32, vas32, kbs32, vbs32) = _ffn1_qkv(
        x_sample, mod_s, gains, *w1, win, gt_s, s_len)
    os_ = _attn_sample(sinks, qas, qbs, kas, vas, kbs, vbs,
                       *(jnp.transpose(c[0], (0, 2, 3, 1)) for c in (cache_a_k, cache_a_v, cache_b_k, cache_b_v)),
                       bias_sa, bias_sb, s_batch, s_len)
    y_sample = _out_ffn2(x1s, os_, mod_s, gains, gg, wout, *w2, gt_s, s_len)
    new_s = (kas32.reshape(depth, s_batch, s_len, N_HEADS, HEAD_DIM),
             vas32.reshape(depth, s_batch, s_len, N_HEADS, HEAD_DIM),
             kbs32.reshape(depth, s_batch, s_len, 2, HEAD_DIM), vbs32.reshape(depth, s_batch, s_len, 2, HEAD_DIM))

    return (y_prompt, y_sample) + new_p + new_s
```

```python
import functools
import math

import jax
import jax.numpy as jnp
from jax import lax
from jax.experimental import pallas as pl
from jax.experimental.pallas import tpu as pltpu

F32 = jnp.float32
BF16 = jnp.bfloat16

D_MODEL = 1024
D_FF = 2816
CHUNK = 64
CHUNK_SHIFT = CHUNK.bit_length() - 1
HEAD_DIM = 64
N_HEADS = 8
N_PAIRS = N_HEADS // 2
WIDTH = N_HEADS * HEAD_DIM
KV_WIDTH_B = 2 * HEAD_DIM
REACH_A = 8 * CHUNK
REACH_B = 2 * CHUNK
REL_CLIP_A = 128
N_BUCKETS = 32
N_MOD = 9
FFN_RES = 0.5
EPS = 1e-6
SCALE = HEAD_DIM ** -0.5
LOG2E = 1.4426950408889634
NEG_INF = -1e30

V7X_LANES = 128
V7X_VMEM_LIMIT_CAP = 56 * 1024 * 1024

GROUP_Q = 2 * CHUNK
BAND_A = REACH_A + GROUP_Q
BAND_B = REACH_B + GROUP_Q
ORIGIN_A = 1024
WIDTH_MASTER_A = ORIGIN_A + BAND_A
ORIGIN_B = 256
WIDTH_MASTER_B = ORIGIN_B + BAND_B
N_VAR_A = REACH_A // GROUP_Q + 1
N_VAR_B = REACH_B // GROUP_Q + 1


def _vmem_limit(estimate_bytes):
    return int(min(max(estimate_bytes, 32 * 1024 * 1024), V7X_VMEM_LIMIT_CAP))


def _rms(x, gain):
    ms = jnp.mean(x * x, axis=-1, keepdims=True)
    return x * lax.rsqrt(ms + EPS) * gain


def _adaln(x, gain, shift, scale):
    return _rms(x, gain) * (1.0 + scale) + shift


def _silu(x):
    return x * (1.0 / (1.0 + jnp.exp(-x)))


def _mod_kernel(c_ref, w_ref, b_ref, o_ref):
    s = _silu(c_ref[...]).astype(BF16)
    o_ref[0] = jnp.dot(s, w_ref[...].astype(BF16), preferred_element_type=F32) + b_ref[0]


def _modulation(c_rows, w_mod, b_mod):
    rows = c_rows.shape[0]
    return pl.pallas_call(
        _mod_kernel,
        grid=(N_MOD,),
        in_specs=[
            pl.BlockSpec((rows, D_MODEL), lambda j: (0, 0)),
            pl.BlockSpec((D_MODEL, D_MODEL), lambda j: (0, j)),
            pl.BlockSpec((1, 1, D_MODEL), lambda j: (j, 0, 0)),
        ],
        out_specs=pl.BlockSpec((1, rows, D_MODEL), lambda j: (j, 0, 0)),
        out_shape=jax.ShapeDtypeStruct((N_MOD, rows, D_MODEL), F32),
        name="modulation",
    )(c_rows, w_mod, b_mod.reshape(N_MOD, 1, D_MODEL))


def _skew(base_row, n_rows):
    width = base_row.shape[1]
    m = jnp.broadcast_to(base_row, (n_rows, width))
    row = lax.broadcasted_iota(jnp.int32, (n_rows, width), 0)
    shift = 1
    while shift < n_rows:
        m = jnp.where((row & shift) != 0, pltpu.roll(m, shift, axis=1), m)
        shift *= 2
    return m


def _band_mask(n_rows, width, origin, n_prev):
    row = lax.broadcasted_iota(jnp.int32, (n_rows, width), 0)
    col = lax.broadcasted_iota(jnp.int32, (n_rows, width), 1)
    q_chunk = row >> CHUNK_SHIFT
    k_chunk = (col >> CHUNK_SHIFT) - (origin // CHUNK)
    return (k_chunk >= q_chunk - n_prev) & (k_chunk <= q_chunk)


def _t5_bucket(rel):
    half = N_BUCKETS // 2
    max_exact = half // 2
    n = jnp.abs(rel)
    n2 = n * n
    large = jnp.full(rel.shape, max_exact, jnp.int32)
    for k in range(1, half - max_exact):
        large = large + jnp.where(n2 >= (max_exact * max_exact) * (2 ** k), 1, 0)
    return jnp.where(rel > 0, half, 0) + jnp.where(n < max_exact, n, large)


def _bias_kernel(rel_ref, t5_ref, *refs, s_len, la, lb):
    n_cast = (len(refs) - 4) // 2
    for src, dst in zip(refs[:n_cast], refs[n_cast + 4:]):
        dst[...] = src[...].astype(BF16)
    ba_ref, bb_ref, bsa_ref, bsb_ref = refs[n_cast:n_cast + 4]

    tab = rel_ref[0] * LOG2E
    n_tab = 2 * REL_CLIP_A + 1
    first, last = tab[:, 0:1], tab[:, n_tab - 1:n_tab]
    win_lo = ORIGIN_A - 2 * REL_CLIP_A
    col = lax.broadcasted_iota(jnp.int32, (1, tab.shape[1] + V7X_LANES), 1)
    shifted = jnp.concatenate([jnp.zeros((1, V7X_LANES), F32), tab], axis=1)
    u_win = jnp.where(col < V7X_LANES, first, jnp.where(col >= V7X_LANES + n_tab, last, shifted))
    window = _skew(u_win, GROUP_Q)[:, V7X_LANES:]
    n_left = win_lo + V7X_LANES
    n_right = WIDTH_MASTER_A - n_left - window.shape[1]
    skew_a = jnp.concatenate([jnp.broadcast_to(first, (GROUP_Q, n_left)), window,
                              jnp.broadcast_to(last, (GROUP_Q, n_right))], axis=1)
    master_a = jnp.where(_band_mask(GROUP_Q, WIDTH_MASTER_A, ORIGIN_A, REACH_A // CHUNK), skew_a, NEG_INF)
    for v in range(N_VAR_A):
        start = ORIGIN_A - min(v * GROUP_Q, REACH_A) if v < N_VAR_A - 1 else ORIGIN_A - REACH_A
        ba_ref[v, 0] = master_a[:, start:start + BAND_A]
    bsa_ref[0] = skew_a[:s_len, ORIGIN_A - la:ORIGIN_A + s_len]

    t5 = t5_ref[0] * LOG2E
    rel = lax.broadcasted_iota(jnp.int32, (1, WIDTH_MASTER_B), 1) - ORIGIN_B
    bucket = _t5_bucket(rel)
    u_b = jnp.zeros((1, WIDTH_MASTER_B), F32)
    for i in range(N_BUCKETS):
        u_b = jnp.where(bucket == i, t5[:, i:i + 1], u_b)
    skew_b = _skew(u_b, GROUP_Q)
    master_b = jnp.where(_band_mask(GROUP_Q, WIDTH_MASTER_B, ORIGIN_B, REACH_B // CHUNK), skew_b, NEG_INF)
    for v in range(N_VAR_B):
        start = ORIGIN_B - min(v * GROUP_Q, REACH_B) if v < N_VAR_B - 1 else ORIGIN_B - REACH_B
        bb_ref[v, 0] = master_b[:, start:start + BAND_B]
    bsb_ref[0] = skew_b[:s_len, ORIGIN_B - lb:ORIGIN_B + s_len]


V7X_BF16_SUBLANES = 16


def _cast_chunks(w, n_steps, step_of):
    rows, cols = w.shape
    n_chunks = n_steps
    while rows % (n_chunks * V7X_BF16_SUBLANES):
        n_chunks //= 2
    spec = pl.BlockSpec((rows // n_chunks, cols), lambda *g: (jnp.minimum(step_of(*g), n_chunks - 1), 0))
    return spec, jax.ShapeDtypeStruct(w.shape, BF16)


def _bias_tables(rel_bias_a, t5_table, s_len, la, lb, cast=()):
    n_tab = 2 * REL_CLIP_A + 1
    tab_w = 3 * V7X_LANES
    rel_p = jnp.pad(rel_bias_a, ((0, 0), (0, tab_w - n_tab))).reshape(N_HEADS, 1, tab_w)
    t5_p = t5_table.reshape(N_HEADS, 1, N_BUCKETS)
    kern = functools.partial(_bias_kernel, s_len=s_len, la=la, lb=lb)
    cast_specs, cast_shapes = zip(*[_cast_chunks(w, N_HEADS, lambda h: h) for w in cast]) if cast else ((), ())
    return pl.pallas_call(
        kern,
        grid=(N_HEADS,),
        in_specs=[
            pl.BlockSpec((1, 1, tab_w), lambda h: (h, 0, 0)),
            pl.BlockSpec((1, 1, N_BUCKETS), lambda h: (h, 0, 0)),
            *cast_specs,
        ],
        out_specs=[
            pl.BlockSpec((N_VAR_A, 1, GROUP_Q, BAND_A), lambda h: (0, h, 0, 0)),
            pl.BlockSpec((N_VAR_B, 1, GROUP_Q, BAND_B), lambda h: (0, h, 0, 0)),
            pl.BlockSpec((1, s_len, la + s_len), lambda h: (h, 0, 0)),
            pl.BlockSpec((1, s_len, lb + s_len), lambda h: (h, 0, 0)),
            *cast_specs,
        ],
        out_shape=[
            jax.ShapeDtypeStruct((N_VAR_A, N_HEADS, GROUP_Q, BAND_A), F32),
            jax.ShapeDtypeStruct((N_VAR_B, N_HEADS, GROUP_Q, BAND_B), F32),
            jax.ShapeDtypeStruct((N_HEADS, s_len, la + s_len), F32),
            jax.ShapeDtypeStruct((N_HEADS, s_len, lb + s_len), F32),
            *cast_shapes,
        ],
        compiler_params=pltpu.CompilerParams(dimension_semantics=("arbitrary",)),
        name="bias_tables",
    )(rel_p, t5_p, *cast)


V7X_MXU_DIM = 256
FF_SLABS = ((0, 6 * V7X_MXU_DIM), (6 * V7X_MXU_DIM, D_FF))
FF_SLAB_MAX = max(hi - lo for lo, hi in FF_SLABS)


def _swiglu(h, wg_ref, wu_ref, wd_ref):
    acts = []
    for lo, hi in FF_SLABS:
        g = jnp.dot(h, wg_ref[:, lo:hi], preferred_element_type=F32)
        u = jnp.dot(h, wu_ref[:, lo:hi], preferred_element_type=F32)
        acts.append((_silu(g) * u).astype(BF16))
    acc = None
    for (lo, hi), a in zip(FF_SLABS, acts):
        part = jnp.dot(a, wd_ref[lo:hi, :], preferred_element_type=F32)
        acc = part if acc is None else acc + part
    return acc


SUB_TOKENS = V7X_MXU_DIM
FFN_TILE_TOKENS = 2 * SUB_TOKENS


def _sub_tiles(gt, rt):
    n_sub = gt * rt // SUB_TOKENS
    if gt == 1:
        step = rt // n_sub
        return [(slice(0, 1), slice(s * step, (s + 1) * step), s * step, step) for s in range(n_sub)]
    step = gt // n_sub
    return [(slice(s * step, (s + 1) * step), slice(0, rt), s * step * rt, step * rt) for s in range(n_sub)]


def _ffn1_qkv_kernel(x_ref, mod_ref, gains_ref, wg_ref, wu_ref, wd_ref, win_ref,
                     x1_ref, qa_ref, ka_ref, va_ref, qb_ref, kb_ref, vb_ref,
                     ka32_ref, va32_ref, kb32_ref, vb32_ref, *tail_scratch):
    gt, rt, _ = x_ref.shape
    tm = gt * rt
    gain = lambda k: gains_ref[k:k + 1, :]

    def store_tail(ref, val, r0):
        first = tm - ref.shape[0]
        lo = max(r0, first)
        if lo < r0 + val.shape[0]:
            ref[lo - first:r0 + val.shape[0] - first, :] = val[lo - r0:, :]

    def store_heads(ref, val, r0):
        n_heads = val.shape[1] // HEAD_DIM
        for h in range(n_heads):
            ref[pl.ds(r0 * n_heads + h, val.shape[0], stride=n_heads), :] = val[:, h * HEAD_DIM:(h + 1) * HEAD_DIM]

    def store_split(ref, rows, q, pairs):
        q = q * (SCALE * LOG2E)
        head = lambda h: q[:, h * HEAD_DIM:(h + 1) * HEAD_DIM]
        zeros = jnp.zeros((q.shape[0], HEAD_DIM), F32)
        ref[rows, :WIDTH] = jnp.concatenate([t for a, _ in pairs for t in (head(a), zeros)], axis=1).astype(BF16)
        ref[rows, WIDTH:] = jnp.concatenate([t for _, b in pairs for t in (zeros, head(b))], axis=1).astype(BF16)

    pairs_a = [(2 * p, 2 * p + 1) for p in range(N_PAIRS)]
    pairs_b = [(p, N_PAIRS + p) for p in range(N_PAIRS)]

    subs = _sub_tiles(gt, rt)
    mods = [lambda k, gs=gs: mod_ref[k, gs] for gs, _, _, _ in subs]
    xs = [x_ref[gs, rs, :] for gs, rs, _, _ in subs]
    hs = [_adaln(x, gain(0), mod(0), mod(1)).reshape(n, D_MODEL).astype(BF16)
          for x, mod, (_, _, _, n) in zip(xs, mods, subs)]
    fs = [_swiglu(h, wg_ref, wu_ref, wd_ref).reshape(x.shape) for h, x in zip(hs, xs)]
    h2s = []
    for x, f, mod, (gs, rs, _, n) in zip(xs, fs, mods, subs):
        x1 = x + (FFN_RES * mod(2)) * _rms(f, gain(1))
        x1_ref[gs, rs, :] = x1
        h2s.append(_adaln(x1, gain(2), mod(3), mod(4)).reshape(n, D_MODEL).astype(BF16))

    for h2, (_, _, r0, n) in zip(h2s, subs):
        rows = slice(r0, r0 + n)

        def proj(lo, width):
            return jnp.dot(h2, win_ref[:, lo:lo + width], preferred_element_type=F32)

        store_split(qa_ref, rows, proj(0, WIDTH), pairs_a)
        ka = proj(WIDTH, WIDTH)
        va = proj(2 * WIDTH, WIDTH)
        store_split(qb_ref, rows, proj(3 * WIDTH, WIDTH), pairs_b)
        kvb = proj(4 * WIDTH, 2 * KV_WIDTH_B)
        kb, vb = kvb[:, :KV_WIDTH_B], kvb[:, KV_WIDTH_B:]
        ka_ref[rows, :] = ka.astype(BF16)
        va_ref[rows, :] = va.astype(BF16)
        kb_ref[rows, :] = kb.astype(BF16)
        vb_ref[rows, :] = vb.astype(BF16)
        if tail_scratch:
            for ref, val in zip(tail_scratch, (ka, va, kb, vb)):
                store_tail(ref, val, r0)
        else:
            store_heads(ka32_ref, ka, r0)
            store_heads(va32_ref, va, r0)
            store_tail(kb32_ref, kb, r0)
            store_tail(vb32_ref, vb, r0)

    if tail_scratch:
        @pl.when(pl.program_id(1) == pl.num_programs(1) - 1)
        def _():
            for out_ref, ref in zip((ka32_ref, va32_ref, kb32_ref, vb32_ref), tail_scratch):
                out_ref[...] = ref[...].T


def _resident(shape):
    return pl.BlockSpec(shape, lambda *_: (0,) * len(shape), pipeline_mode=pl.Buffered(1))


def _ffn1_qkv(x, mod, gains, wg, wu, wd, win, gt, rt):
    n_groups, rows, _ = x.shape
    n_g, n_r = n_groups // gt, rows // rt
    tm = gt * rt
    tokens = n_groups * rows
    ta = gt * min(REACH_A, rows)
    tb = gt * min(REACH_B, rows)
    tile = lambda gi, ri: (gi * n_r + ri, 0)
    tail = lambda gi, ri: (gi, 0)
    tok = lambda w, dt: jax.ShapeDtypeStruct((tokens, w), dt)
    if gt == 1:
        assert ta <= tm and tb <= tm
        tail_specs = [pl.BlockSpec((None, w, t), lambda gi, ri: (gi, 0, 0))
                      for w, t in ((WIDTH, ta), (WIDTH, ta), (KV_WIDTH_B, tb), (KV_WIDTH_B, tb))]
        tail_shapes = [jax.ShapeDtypeStruct((n_g, w, t), F32)
                       for w, t in ((WIDTH, ta), (WIDTH, ta), (KV_WIDTH_B, tb), (KV_WIDTH_B, tb))]
        scratch = [pltpu.VMEM((ta, WIDTH), F32), pltpu.VMEM((ta, WIDTH), F32),
                   pltpu.VMEM((tb, KV_WIDTH_B), F32), pltpu.VMEM((tb, KV_WIDTH_B), F32)]
    else:
        assert n_r == 1
        tail_specs = [pl.BlockSpec((tm * N_HEADS, HEAD_DIM), tail), pl.BlockSpec((tm * N_HEADS, HEAD_DIM), tail),
                      pl.BlockSpec((tm, KV_WIDTH_B), tail), pl.BlockSpec((tm, KV_WIDTH_B), tail)]
        tail_shapes = [jax.ShapeDtypeStruct((tokens * N_HEADS, HEAD_DIM), F32),
                       jax.ShapeDtypeStruct((tokens * N_HEADS, HEAD_DIM), F32),
                       jax.ShapeDtypeStruct((tokens, KV_WIDTH_B), F32), jax.ShapeDtypeStruct((tokens, KV_WIDTH_B), F32)]
        scratch = []
    vmem = (2 * 3 * D_MODEL * D_FF + 2 * D_MODEL * win.shape[1]
            + 2 * 2 * 4 * tm * D_MODEL
            + 2 * 2 * tm * (6 * WIDTH + 2 * KV_WIDTH_B)
            + 2 * 4 * (2 * ta * WIDTH + 2 * tb * KV_WIDTH_B)
            + tm * (2 * 4 + 2) * FF_SLAB_MAX + 3 * 4 * tm * D_MODEL
            + 4 * tm * win.shape[1])
    return pl.pallas_call(
        _ffn1_qkv_kernel,
        grid=(n_g, n_r),
        in_specs=[
            pl.BlockSpec((gt, rt, D_MODEL), lambda gi, ri: (gi, ri, 0)),
            pl.BlockSpec((N_MOD, gt, 1, D_MODEL), lambda gi, ri: (0, gi, 0, 0)),
            _resident(gains.shape),
            _resident(wg.shape), _resident(wu.shape), _resident(wd.shape), _resident(win.shape),
        ],
        out_specs=[
            pl.BlockSpec((gt, rt, D_MODEL), lambda gi, ri: (gi, ri, 0)),
            pl.BlockSpec((tm, 2 * WIDTH), tile), pl.BlockSpec((tm, WIDTH), tile), pl.BlockSpec((tm, WIDTH), tile),
            pl.BlockSpec((tm, 2 * WIDTH), tile), pl.BlockSpec((tm, KV_WIDTH_B), tile),
            pl.BlockSpec((tm, KV_WIDTH_B), tile),
            *tail_specs,
        ],
        out_shape=[
            jax.ShapeDtypeStruct(x.shape, F32),
            tok(2 * WIDTH, BF16), tok(WIDTH, BF16), tok(WIDTH, BF16),
            tok(2 * WIDTH, BF16), tok(KV_WIDTH_B, BF16), tok(KV_WIDTH_B, BF16),
            *tail_shapes,
        ],
        scratch_shapes=scratch,
        compiler_params=pltpu.CompilerParams(
            dimension_semantics=("arbitrary", "arbitrary"), vmem_limit_bytes=_vmem_limit(vmem)),
        name="ffn1_qkv",
    )(x, mod, gains, wg, wu, wd, win)


def _out_ffn2_kernel(x_ref, o_ref, mod_ref, gains_ref, gg_ref, wout_ref, wg_ref, wu_ref, wd_ref, y_ref):
    gt, rt, _ = x_ref.shape
    gain = lambda k: gains_ref[k:k + 1, :]
    gg = gg_ref[...]
    subs = _sub_tiles(gt, rt)
    mods = [lambda k, gs=gs: mod_ref[k, gs] for gs, _, _, _ in subs]
    xs = [x_ref[gs, rs, :] for gs, rs, _, _ in subs]
    ons = []
    for _, _, r0, n in subs:
        o = o_ref[r0:r0 + n, :].astype(F32)
        on = jnp.concatenate([_rms(o[:, :WIDTH], gg[:, :WIDTH]), _rms(o[:, WIDTH:], gg[:, WIDTH:])], axis=1)
        ons.append(on.astype(BF16))
    mixed = [jnp.dot(on, wout_ref[...], preferred_element_type=F32).reshape(x.shape) for on, x in zip(ons, xs)]
    x2s = [x + mod(5) * _rms(m, gain(3)) for x, m, mod in zip(xs, mixed, mods)]
    hs = [_adaln(x2, gain(4), mod(6), mod(7)).reshape(n, D_MODEL).astype(BF16)
          for x2, mod, (_, _, _, n) in zip(x2s, mods, subs)]
    fs = [_swiglu(h, wg_ref, wu_ref, wd_ref).reshape(x.shape) for h, x in zip(hs, xs)]
    for x2, f, mod, (gs, rs, _, _) in zip(x2s, fs, mods, subs):
        y_ref[gs, rs, :] = x2 + (FFN_RES * mod(8)) * _rms(f, gain(5))


def _out_ffn2(x1, o, mod, gains, gg, wout, wg, wu, wd, gt, rt):
    n_groups, rows, _ = x1.shape
    n_g, n_r = n_groups // gt, rows // rt
    tm = gt * rt
    vmem = (2 * 3 * D_MODEL * D_FF + 2 * wout.size
            + 2 * 2 * 4 * tm * D_MODEL + 2 * 2 * tm * 2 * WIDTH
            + tm * (2 * 4 + 2) * FF_SLAB_MAX + 5 * 4 * tm * D_MODEL)
    return pl.pallas_call(
        _out_ffn2_kernel,
        grid=(n_g, n_r),
        in_specs=[
            pl.BlockSpec((gt, rt, D_MODEL), lambda gi, ri: (gi, ri, 0)),
            pl.BlockSpec((tm, 2 * WIDTH), lambda gi, ri: (gi * n_r + ri, 0)),
            pl.BlockSpec((N_MOD, gt, 1, D_MODEL), lambda gi, ri: (0, gi, 0, 0)),
            _resident(gains.shape), _resident(gg.shape), _resident(wout.shape),
            _resident(wg.shape), _resident(wu.shape), _resident(wd.shape),
        ],
        out_specs=pl.BlockSpec((gt, rt, D_MODEL), lambda gi, ri: (gi, ri, 0)),
        out_shape=jax.ShapeDtypeStruct(x1.shape, F32),
        compiler_params=pltpu.CompilerParams(
            dimension_semantics=("arbitrary", "arbitrary"), vmem_limit_bytes=_vmem_limit(vmem)),
        name="out_ffn2",
    )(x1, o, mod, gains, gg, wout, wg, wu, wd)


def _softmax_terms(s, sink=None):
    m = jnp.max(s, axis=1, keepdims=True)
    if sink is not None:
        m = jnp.maximum(m, sink)
    e = jnp.exp2(s - m)
    denom = jnp.sum(e, axis=1, keepdims=True)
    if sink is not None:
        denom = denom + jnp.exp2(sink - m)
    return e.astype(BF16), 1.0 / denom


def _scores(q, k):
    return lax.dot_general(q, k, (((1,), (1,)), ((), ())), preferred_element_type=F32)


def _unstack_pair(z):
    n_q = z.shape[0] // 2
    is_lo = lax.broadcasted_iota(jnp.int32, (n_q, z.shape[1]), 1) < HEAD_DIM
    return jnp.where(is_lo, z[:n_q], z[n_q:])


def _sink_column(sinks_ref, pair, n_q):
    row = lax.broadcasted_iota(jnp.int32, (2 * n_q, 1), 0)
    return jnp.where(row < n_q, sinks_ref[pair], sinks_ref[N_PAIRS + pair]) * LOG2E


GROUPS_PER_STEP = 2
JOB_LAG = 4


def _attn_prompt_kernel(sinks_ref, qa_ref, qb_ref, ka_ref, va_ref, kb_ref, vb_ref, *rest):
    ba_refs, bb_refs = rest[:GROUPS_PER_STEP], rest[GROUPS_PER_STEP:2 * GROUPS_PER_STEP]
    n_cast = (len(rest) - 2 * GROUPS_PER_STEP - 1) // 2
    o_ref = rest[2 * GROUPS_PER_STEP + n_cast]
    for src, dst in zip(rest[2 * GROUPS_PER_STEP:2 * GROUPS_PER_STEP + n_cast], rest[len(rest) - n_cast:]):
        dst[...] = src[...].astype(BF16)
    slab = lambda p: slice(p * V7X_LANES, (p + 1) * V7X_LANES)
    bands = []
    first = pl.program_id(1) * GROUPS_PER_STEP
    for j in range(GROUPS_PER_STEP):
        start = lambda g, reach: jnp.maximum(g * GROUP_Q - reach, 0)
        off_a = pl.multiple_of(start(first + j, REACH_A) - start(first, REACH_A), GROUP_Q)
        off_b = pl.multiple_of(start(first + j, REACH_B) - start(first, REACH_B), GROUP_Q)
        bands.append((pl.ds(off_a, BAND_A), pl.ds(off_b, BAND_B)))

    def stacked(q_ref, j, p):
        rows = slice(j * GROUP_Q, (j + 1) * GROUP_Q)
        return jnp.concatenate([q_ref[rows, slab(p)], q_ref[rows, slab(N_PAIRS + p)]], axis=0)

    def score(mixer, j, p):
        if mixer == "a":
            bias = jnp.concatenate([ba_refs[j][0, 2 * p], ba_refs[j][0, 2 * p + 1]], axis=0)
            return _scores(stacked(qa_ref, j, p), ka_ref[bands[j][0], slab(p)]) + bias
        bias = jnp.concatenate([bb_refs[j][0, p], bb_refs[j][0, N_PAIRS + p]], axis=0)
        return _scores(stacked(qb_ref, j, p), kb_ref[bands[j][1], :]) + bias

    def weights(s, sink):
        m = jnp.max(s, axis=1, keepdims=True)
        if sink is None:
            return jnp.exp2(s - m).astype(BF16), None
        m = jnp.maximum(m, sink)
        return jnp.exp2(s - m).astype(BF16), jnp.exp2(sink - m)

    def attend(e, extra, v):
        z = jnp.dot(e, jnp.concatenate([v, jnp.ones_like(v)], axis=1), preferred_element_type=F32)
        denom = z[:, V7X_LANES:] if extra is None else z[:, V7X_LANES:] + extra
        return _unstack_pair(z[:, :V7X_LANES] * (1.0 / denom))

    jobs = [(mixer, j, p) for mixer in ("a", "b") for j in range(GROUPS_PER_STEP) for p in range(N_PAIRS)]
    s_all, w_all = {}, {}
    for t in range(len(jobs) + 2 * JOB_LAG):
        if t < len(jobs):
            s_all[t] = score(*jobs[t])
        if 0 <= t - JOB_LAG < len(jobs):
            mixer, j, p = jobs[t - JOB_LAG]
            w_all[t - JOB_LAG] = weights(s_all.pop(t - JOB_LAG),
                                         _sink_column(sinks_ref, p, GROUP_Q) if mixer == "b" else None)
        if 0 <= t - 2 * JOB_LAG < len(jobs):
            mixer, j, p = jobs[t - 2 * JOB_LAG]
            e, extra = w_all.pop(t - 2 * JOB_LAG)
            rows = slice(j * GROUP_Q, (j + 1) * GROUP_Q)
            if mixer == "a":
                o_ref[rows, slab(p)] = attend(e, extra, va_ref[bands[j][0], slab(p)]).astype(BF16)
            else:
                o_ref[rows, slab(N_PAIRS + p)] = attend(e, extra, vb_ref[bands[j][1], :]).astype(BF16)


def _attn_prompt(sinks, qa, qb, ka, va, kb, vb, bias_a, bias_b, batch, seq, cast=()):
    rows = GROUPS_PER_STEP * GROUP_Q
    n_steps = seq // rows
    cast_specs, cast_shapes = (zip(*[_cast_chunks(w, batch * n_steps, lambda b, s: b * n_steps + s) for w in cast])
                               if cast else ((), ()))
    q_spec = pl.BlockSpec((rows, 2 * WIDTH), lambda b, s: (b * n_steps + s, 0))
    kv_spec = lambda w, reach: pl.BlockSpec(
        (pl.Element(reach + rows), pl.Element(w)),
        lambda b, s: (pl.multiple_of(b * seq + jnp.maximum(s * rows - reach, 0), GROUP_Q), 0))
    bias_spec = lambda band, n_var, j: pl.BlockSpec(
        (1, N_HEADS, GROUP_Q, band),
        lambda b, s: (jnp.minimum(s * GROUPS_PER_STEP + j, n_var - 1), 0, 0, 0))
    vmem = (2 * 2 * 2 * ((REACH_A + rows) * WIDTH + (REACH_B + rows) * KV_WIDTH_B)
            + GROUPS_PER_STEP * 2 * 4 * N_HEADS * GROUP_Q * (BAND_A + BAND_B)
            + 2 * 2 * rows * 6 * WIDTH + GROUPS_PER_STEP * 6 * 4 * 2 * GROUP_Q * BAND_A
            + sum(2 * 6 * math.prod(spec.block_shape) for spec in cast_specs))
    out = pl.pallas_call(
        _attn_prompt_kernel,
        grid=(batch, n_steps),
        in_specs=[
            pl.BlockSpec(memory_space=pltpu.SMEM),
            q_spec, q_spec,
            kv_spec(WIDTH, REACH_A), kv_spec(WIDTH, REACH_A), kv_spec(KV_WIDTH_B, REACH_B), kv_spec(KV_WIDTH_B, REACH_B),
            *[bias_spec(BAND_A, N_VAR_A, j) for j in range(GROUPS_PER_STEP)],
            *[bias_spec(BAND_B, N_VAR_B, j) for j in range(GROUPS_PER_STEP)],
            *cast_specs,
        ],
        out_specs=[pl.BlockSpec((rows, 2 * WIDTH), lambda b, s: (b * n_steps + s, 0)), *cast_specs],
        out_shape=[jax.ShapeDtypeStruct((batch * seq, 2 * WIDTH), BF16), *cast_shapes],
        compiler_params=pltpu.CompilerParams(
            dimension_semantics=("arbitrary", "arbitrary"), vmem_limit_bytes=_vmem_limit(vmem)),
        name="attn_prompt",
    )(sinks, qa, qb, ka, va, kb, vb, *([bias_a] * GROUPS_PER_STEP), *([bias_b] * GROUPS_PER_STEP), *cast)
    return out[0], tuple(out[1:])


def _pair_ordered_cols(h):
    return (h % N_PAIRS) * V7X_LANES + (h // N_PAIRS) * HEAD_DIM


SAMPLE_ROWS_PER_STEP = 4


def _attn_sample_kernel(sinks_ref, qa_ref, qb_ref, ka_ref, va_ref, kb_ref, vb_ref,
                        cak_ref, cav_ref, cbk_ref, cbv_ref, ba_ref, bb_ref, o_ref):
    n_rows = cak_ref.shape[0]
    n_q = qa_ref.shape[0] // n_rows
    n_kv = cbk_ref.shape[1]
    per_kv = N_HEADS // n_kv
    la, lb = cak_ref.shape[3], cbk_ref.shape[3]
    head = lambda ref, r, idx: ref[r * n_q:(r + 1) * n_q, idx * HEAD_DIM:(idx + 1) * HEAD_DIM]

    def query(ref, r, lo):
        lo = lo if lo % V7X_LANES < HEAD_DIM else WIDTH + lo
        return ref[r * n_q:(r + 1) * n_q, lo:lo + HEAD_DIM]

    def scores(q, cache_t_ref, new_ref, r, idx):
        old = jnp.dot(q, cache_t_ref[r, idx].astype(BF16), preferred_element_type=F32)
        return jnp.concatenate([old, _scores(q, head(new_ref, r, idx))], axis=1)

    def values(e, cache_t_ref, new_ref, r, idx, n_old):
        old = _scores(e[:, :n_old], cache_t_ref[r, idx].astype(BF16))
        return old + jnp.dot(e[:, n_old:], head(new_ref, r, idx), preferred_element_type=F32)

    jobs_a = [(r, h) for r in range(n_rows) for h in range(N_HEADS)]
    jobs_b = [(r, c) for r in range(n_rows) for c in range(n_kv)]
    s_a = [scores(query(qa_ref, r, h * HEAD_DIM), cak_ref, ka_ref, r, h) + ba_ref[h] for r, h in jobs_a]
    s_b = []
    for r, c in jobs_b:
        heads = range(c * per_kv, (c + 1) * per_kv)
        q = jnp.concatenate([query(qb_ref, r, _pair_ordered_cols(h)) for h in heads], axis=0)
        bias = jnp.concatenate([bb_ref[h] for h in heads], axis=0)
        s_b.append(scores(q, cbk_ref, kb_ref, r, c) + bias)

    row = lax.broadcasted_iota(jnp.int32, (per_kv * n_q, 1), 0)
    sinks = []
    for c in range(n_kv):
        sink = jnp.zeros((per_kv * n_q, 1), F32)
        for i in range(per_kv):
            sink = jnp.where((row >= i * n_q) & (row < (i + 1) * n_q), sinks_ref[c * per_kv + i], sink)
        sinks.append(sink * LOG2E)
    p_a = [_softmax_terms(s) for s in s_a]
    p_b = [_softmax_terms(s, sinks[c]) for s, (_, c) in zip(s_b, jobs_b)]

    out_a = {job: values(e, cav_ref, va_ref, *job, la) * inv for job, (e, inv) in zip(jobs_a, p_a)}
    out_b = {}
    for (r, c), (e, inv) in zip(jobs_b, p_b):
        o = values(e, cbv_ref, vb_ref, r, c, lb) * inv
        for i in range(per_kv):
            out_b[(r, c * per_kv + i)] = o[i * n_q:(i + 1) * n_q]
    order_b = [h for p in range(N_PAIRS) for h in (p, N_PAIRS + p)]
    for r in range(n_rows):
        pieces = [out_a[(r, h)] for h in range(N_HEADS)] + [out_b[(r, h)] for h in order_b]
        o_ref[r * n_q:(r + 1) * n_q, :] = jnp.concatenate(pieces, axis=1).astype(BF16)


def _attn_sample(sinks, qa, qb, ka, va, kb, vb, cak, cav, cbk, cbv, bias_a, bias_b, batch, s_len):
    n_rows = SAMPLE_ROWS_PER_STEP
    row = lambda w: pl.BlockSpec((n_rows * s_len, w), lambda b: (b, 0))
    cache = lambda a: pl.BlockSpec((n_rows,) + a.shape[1:], lambda b: (b, 0, 0, 0))
    whole = lambda a: pl.BlockSpec(a.shape, lambda b: (0,) * a.ndim)
    return pl.pallas_call(
        _attn_sample_kernel,
        grid=(batch // n_rows,),
        in_specs=[
            pl.BlockSpec(memory_space=pltpu.SMEM),
            row(2 * WIDTH), row(2 * WIDTH), row(WIDTH), row(WIDTH), row(KV_WIDTH_B), row(KV_WIDTH_B),
            cache(cak), cache(cav), cache(cbk), cache(cbv),
            whole(bias_a), whole(bias_b),
        ],
        out_specs=pl.BlockSpec((n_rows * s_len, 2 * WIDTH), lambda b: (b, 0)),
        out_shape=jax.ShapeDtypeStruct((batch * s_len, 2 * WIDTH), BF16),
        compiler_params=pltpu.CompilerParams(dimension_semantics=("arbitrary",)),
        name="attn_sample",
    )(sinks, qa, qb, ka, va, kb, vb, cak, cav, cbk, cbv, bias_a, bias_b)


def _pair_order(x, axis):
    shape = x.shape
    split = shape[:axis] + (N_HEADS // N_PAIRS, N_PAIRS, HEAD_DIM) + shape[axis + 1:]
    return jnp.swapaxes(x.reshape(split), axis, axis + 1).reshape(shape)


def _prep_out_weights(w_out, group_gains):
    wout = jnp.concatenate([w_out[:WIDTH], _pair_order(w_out[WIDTH:], 0)], axis=0).astype(BF16)
    gg = jnp.concatenate([group_gains[:WIDTH], _pair_order(group_gains[WIDTH:], 0)])
    return wout, gg.reshape(1, 2 * WIDTH)


def kernel(x_prompt, x_sample, cache_a_k, cache_a_v, cache_b_k, cache_b_v, c_prompt, c_sample, w_mod, b_mod,
           norm_gains, w1_gate, w1_up, w1_down, w_in, w_out, group_gains, rel_bias_a, t5_bias_table, sinks_b,
           w2_gate, w2_up, w2_down):
    depth = w_mod.shape[0]
    assert depth == 1
    batch, seq, _ = x_prompt.shape
    s_batch, s_len, _ = x_sample.shape
    la, lb = cache_a_k.shape[2], cache_b_k.shape[2]
    assert la == REACH_A and lb == REACH_B and seq % (2 * FFN_TILE_TOKENS) == 0

    c_rows = jnp.concatenate([c_prompt, c_sample], axis=0)
    pad_rows = (-c_rows.shape[0]) % 8
    c_rows = jnp.pad(c_rows, ((0, pad_rows), (0, 0)))
    mod = _modulation(c_rows, w_mod[0], b_mod[0])
    mod_p = mod[:, :batch].reshape(N_MOD, batch, 1, D_MODEL)
    mod_s = mod[:, batch:batch + s_batch].reshape(N_MOD, s_batch, 1, D_MODEL)

    bias_a, bias_b, bias_sa, bias_sb, *w1, win = _bias_tables(
        rel_bias_a[0], t5_bias_table, s_len, la, lb, cast=(w1_gate[0], w1_up[0], w1_down[0], w_in[0]))

    gains = norm_gains[0]
    wout, gg = _prep_out_weights(w_out[0], group_gains[0])
    sinks = sinks_b[0]

    (x1p, qa, ka, va, qb, kb, vb, ka32, va32, kb32, vb32) = _ffn1_qkv(
        x_prompt, mod_p, gains, *w1, win, 1, FFN_TILE_TOKENS)
    op, w2 = _attn_prompt(sinks, qa, qb, ka, va, kb, vb, bias_a, bias_b, batch, seq,
                          cast=(w2_gate[0], w2_up[0], w2_down[0]))
    y_prompt = _out_ffn2(x1p, op, mod_p, gains, gg, wout, *w2, 1, 2 * FFN_TILE_TOKENS)
    rows_last = lambda t: jnp.transpose(t.reshape(depth, batch, -1, HEAD_DIM, t.shape[-1]), (0, 1, 4, 2, 3))
    new_p = tuple(rows_last(t) for t in (ka32, va32, kb32, vb32))

    gt_s = FFN_TILE_TOKENS // s_len
    (x1s, qas, kas, vas, qbs, kbs, vbs, kas32, vas32, kbs32, vbs32) = _ffn1_qkv(
        x_sample, mod_s, gains, *w1, win, gt_s, s_len)
    os_ = _attn_sample(sinks, qas, qbs, kas, vas, kbs, vbs,
                       *(jnp.transpose(c[0], (0, 2, 3, 1)) for c in (cache_a_k, cache_a_v, cache_b_k, cache_b_v)),
                       bias_sa, bias_sb, s_batch, s_len)
    y_sample = _out_ffn2(x1s, os_, mod_s, gains, gg, wout, *w2, gt_s, s_len)
    new_s = (kas32.reshape(depth, s_batch, s_len, N_HEADS, HEAD_DIM),
             vas32.reshape(depth, s_batch, s_len, N_HEADS, HEAD_DIM),
             kbs32.reshape(depth, s_batch, s_len, 2, HEAD_DIM), vbs32.reshape(depth, s_batch, s_len, 2, HEAD_DIM))

    return (y_prompt, y_sample) + new_p + new_s
```

```python
import functools
import math

import jax
import jax.numpy as jnp
from jax import lax
from jax.experimental import pallas as pl
from jax.experimental.pallas import tpu as pltpu

F32 = jnp.float32
BF16 = jnp.bfloat16

D_MODEL = 1024
D_FF = 2816
CHUNK = 64
CHUNK_SHIFT = CHUNK.bit_length() - 1
HEAD_DIM = 64
N_HEADS = 8
N_PAIRS = N_HEADS // 2
WIDTH = N_HEADS * HEAD_DIM
KV_WIDTH_B = 2 * HEAD_DIM
REACH_A = 8 * CHUNK
REACH_B = 2 * CHUNK
REL_CLIP_A = 128
N_BUCKETS = 32
N_MOD = 9
FFN_RES = 0.5
EPS = 1e-6
SCALE = HEAD_DIM ** -0.5
LOG2E = 1.4426950408889634
NEG_INF = -1e30

V7X_LANES = 128
V7X_VMEM_LIMIT_CAP = 56 * 1024 * 1024

GROUP_Q = 2 * CHUNK
BAND_A = REACH_A + GROUP_Q
BAND_B = REACH_B + GROUP_Q
ORIGIN_A = 1024
WIDTH_MASTER_A = ORIGIN_A + BAND_A
ORIGIN_B = 256
WIDTH_MASTER_B = ORIGIN_B + BAND_B
N_VAR_A = REACH_A // GROUP_Q + 1
N_VAR_B = REACH_B // GROUP_Q + 1


def _vmem_limit(estimate_bytes):
    return int(min(max(estimate_bytes, 32 * 1024 * 1024), V7X_VMEM_LIMIT_CAP))


def _rms(x, gain):
    ms = jnp.mean(x * x, axis=-1, keepdims=True)
    return x * lax.rsqrt(ms + EPS) * gain


def _adaln(x, gain, shift, scale):
    return _rms(x, gain) * (1.0 + scale) + shift


def _silu(x):
    return x * (1.0 / (1.0 + jnp.exp(-x)))


def _mod_kernel(c_ref, w_ref, b_ref, o_ref):
    s = _silu(c_ref[...]).astype(BF16)
    o_ref[0] = jnp.dot(s, w_ref[...].astype(BF16), preferred_element_type=F32) + b_ref[0]


def _modulation(c_rows, w_mod, b_mod):
    rows = c_rows.shape[0]
    return pl.pallas_call(
        _mod_kernel,
        grid=(N_MOD,),
        in_specs=[
            pl.BlockSpec((rows, D_MODEL), lambda j: (0, 0)),
            pl.BlockSpec((D_MODEL, D_MODEL), lambda j: (0, j)),
            pl.BlockSpec((1, 1, D_MODEL), lambda j: (j, 0, 0)),
        ],
        out_specs=pl.BlockSpec((1, rows, D_MODEL), lambda j: (j, 0, 0)),
        out_shape=jax.ShapeDtypeStruct((N_MOD, rows, D_MODEL), F32),
        name="modulation",
    )(c_rows, w_mod, b_mod.reshape(N_MOD, 1, D_MODEL))


def _skew(base_row, n_rows):
    width = base_row.shape[1]
    m = jnp.broadcast_to(base_row, (n_rows, width))
    row = lax.broadcasted_iota(jnp.int32, (n_rows, width), 0)
    shift = 1
    while shift < n_rows:
        m = jnp.where((row & shift) != 0, pltpu.roll(m, shift, axis=1), m)
        shift *= 2
    return m


def _band_mask(n_rows, width, origin, n_prev):
    row = lax.broadcasted_iota(jnp.int32, (n_rows, width), 0)
    col = lax.broadcasted_iota(jnp.int32, (n_rows, width), 1)
    q_chunk = row >> CHUNK_SHIFT
    k_chunk = (col >> CHUNK_SHIFT) - (origin // CHUNK)
    return (k_chunk >= q_chunk - n_prev) & (k_chunk <= q_chunk)


def _t5_bucket(rel):
    half = N_BUCKETS // 2
    max_exact = half // 2
    n = jnp.abs(rel)
    n2 = n * n
    large = jnp.full(rel.shape, max_exact, jnp.int32)
    for k in range(1, half - max_exact):
        large = large + jnp.where(n2 >= (max_exact * max_exact) * (2 ** k), 1, 0)
    return jnp.where(rel > 0, half, 0) + jnp.where(n < max_exact, n, large)


def _bias_kernel(rel_ref, t5_ref, *refs, s_len, la, lb):
    n_cast = (len(refs) - 4) // 2
    for src, dst in zip(refs[:n_cast], refs[n_cast + 4:]):
        dst[...] = src[...].astype(BF16)
    ba_ref, bb_ref, bsa_ref, bsb_ref = refs[n_cast:n_cast + 4]

    tab = rel_ref[0] * LOG2E
    n_tab = 2 * REL_CLIP_A + 1
    first, last = tab[:, 0:1], tab[:, n_tab - 1:n_tab]
    win_lo = ORIGIN_A - 2 * REL_CLIP_A
    col = lax.broadcasted_iota(jnp.int32, (1, tab.shape[1] + V7X_LANES), 1)
    shifted = jnp.concatenate([jnp.zeros((1, V7X_LANES), F32), tab], axis=1)
    u_win = jnp.where(col < V7X_LANES, first, jnp.where(col >= V7X_LANES + n_tab, last, shifted))
    window = _skew(u_win, GROUP_Q)[:, V7X_LANES:]
    n_left = win_lo + V7X_LANES
    n_right = WIDTH_MASTER_A - n_left - window.shape[1]
    skew_a = jnp.concatenate([jnp.broadcast_to(first, (GROUP_Q, n_left)), window,
                              jnp.broadcast_to(last, (GROUP_Q, n_right))], axis=1)
    master_a = jnp.where(_band_mask(GROUP_Q, WIDTH_MASTER_A, ORIGIN_A, REACH_A // CHUNK), skew_a, NEG_INF)
    for v in range(N_VAR_A):
        start = ORIGIN_A - min(v * GROUP_Q, REACH_A) if v < N_VAR_A - 1 else ORIGIN_A - REACH_A
        ba_ref[v, 0] = master_a[:, start:start + BAND_A]
    bsa_ref[0] = skew_a[:s_len, ORIGIN_A - la:ORIGIN_A + s_len]

    t5 = t5_ref[0] * LOG2E
    rel = lax.broadcasted_iota(jnp.int32, (1, WIDTH_MASTER_B), 1) - ORIGIN_B
    bucket = _t5_bucket(rel)
    u_b = jnp.zeros((1, WIDTH_MASTER_B), F32)
    for i in range(N_BUCKETS):
        u_b = jnp.where(bucket == i, t5[:, i:i + 1], u_b)
    skew_b = _skew(u_b, GROUP_Q)
    master_b = jnp.where(_band_mask(GROUP_Q, WIDTH_MASTER_B, ORIGIN_B, REACH_B // CHUNK), skew_b, NEG_INF)
    for v in range(N_VAR_B):
        start = ORIGIN_B - min(v * GROUP_Q, REACH_B) if v < N_VAR_B - 1 else ORIGIN_B - REACH_B
        bb_ref[v, 0] = master_b[:, start:start + BAND_B]
    bsb_ref[0] = skew_b[:s_len, ORIGIN_B - lb:ORIGIN_B + s_len]


V7X_BF16_SUBLANES = 16


def _cast_chunks(w, n_steps, step_of):
    rows, cols = w.shape
    n_chunks = n_steps
    while rows % (n_chunks * V7X_BF16_SUBLANES):
        n_chunks //= 2
    spec = pl.BlockSpec((rows // n_chunks, cols), lambda *g: (jnp.minimum(step_of(*g), n_chunks - 1), 0))
    return spec, jax.ShapeDtypeStruct(w.shape, BF16)


def _bias_tables(rel_bias_a, t5_table, s_len, la, lb, cast=()):
    n_tab = 2 * REL_CLIP_A + 1
    tab_w = 3 * V7X_LANES
    rel_p = jnp.pad(rel_bias_a, ((0, 0), (0, tab_w - n_tab))).reshape(N_HEADS, 1, tab_w)
    t5_p = t5_table.reshape(N_HEADS, 1, N_BUCKETS)
    kern = functools.partial(_bias_kernel, s_len=s_len, la=la, lb=lb)
    cast_specs, cast_shapes = zip(*[_cast_chunks(w, N_HEADS, lambda h: h) for w in cast]) if cast else ((), ())
    return pl.pallas_call(
        kern,
        grid=(N_HEADS,),
        in_specs=[
            pl.BlockSpec((1, 1, tab_w), lambda h: (h, 0, 0)),
            pl.BlockSpec((1, 1, N_BUCKETS), lambda h: (h, 0, 0)),
            *cast_specs,
        ],
        out_specs=[
            pl.BlockSpec((N_VAR_A, 1, GROUP_Q, BAND_A), lambda h: (0, h, 0, 0)),
            pl.BlockSpec((N_VAR_B, 1, GROUP_Q, BAND_B), lambda h: (0, h, 0, 0)),
            pl.BlockSpec((1, s_len, la + s_len), lambda h: (h, 0, 0)),
            pl.BlockSpec((1, s_len, lb + s_len), lambda h: (h, 0, 0)),
            *cast_specs,
        ],
        out_shape=[
            jax.ShapeDtypeStruct((N_VAR_A, N_HEADS, GROUP_Q, BAND_A), F32),
            jax.ShapeDtypeStruct((N_VAR_B, N_HEADS, GROUP_Q, BAND_B), F32),
            jax.ShapeDtypeStruct((N_HEADS, s_len, la + s_len), F32),
            jax.ShapeDtypeStruct((N_HEADS, s_len, lb + s_len), F32),
            *cast_shapes,
        ],
        compiler_params=pltpu.CompilerParams(dimension_semantics=("arbitrary",)),
        name="bias_tables",
    )(rel_p, t5_p, *cast)


V7X_MXU_DIM = 256
FF_SLABS = ((0, 6 * V7X_MXU_DIM), (6 * V7X_MXU_DIM, D_FF))
FF_SLAB_MAX = max(hi - lo for lo, hi in FF_SLABS)


def _swiglu(h, wg_ref, wu_ref, wd_ref):
    acts = []
    for lo, hi in FF_SLABS:
        g = jnp.dot(h, wg_ref[:, lo:hi], preferred_element_type=F32)
        u = jnp.dot(h, wu_ref[:, lo:hi], preferred_element_type=F32)
        acts.append((_silu(g) * u).astype(BF16))
    acc = None
    for (lo, hi), a in zip(FF_SLABS, acts):
        part = jnp.dot(a, wd_ref[lo:hi, :], preferred_element_type=F32)
        acc = part if acc is None else acc + part
    return acc


SUB_TOKENS = V7X_MXU_DIM
FFN_TILE_TOKENS = 2 * SUB_TOKENS


def _sub_tiles(gt, rt):
    n_sub = gt * rt // SUB_TOKENS
    if gt == 1:
        step = rt // n_sub
        return [(slice(0, 1), slice(s * step, (s + 1) * step), s * step, step) for s in range(n_sub)]
    step = gt // n_sub
    return [(slice(s * step, (s + 1) * step), slice(0, rt), s * step * rt, step * rt) for s in range(n_sub)]


def _ffn1_qkv_kernel(x_ref, mod_ref, gains_ref, wg_ref, wu_ref, wd_ref, win_ref,
                     x1_ref, qa_ref, ka_ref, va_ref, qb_ref, kb_ref, vb_ref,
                     ka32_ref, va32_ref, kb32_ref, vb32_ref, *tail_scratch):
    gt, rt, _ = x_ref.shape
    tm = gt * rt
    gain = lambda k: gains_ref[k:k + 1, :]

    def store_tail(ref, val, r0):
        first = tm - ref.shape[0]
        lo = max(r0, first)
        if lo < r0 + val.shape[0]:
            ref[lo - first:r0 + val.shape[0] - first, :] = val[lo - r0:, :]

    def store_heads(ref, val, r0):
        n_heads = val.shape[1] // HEAD_DIM
        for h in range(n_heads):
            ref[pl.ds(r0 * n_heads + h, val.shape[0], stride=n_heads), :] = val[:, h * HEAD_DIM:(h + 1) * HEAD_DIM]

    def store_split(ref, rows, q, pairs):
        q = q * (SCALE * LOG2E)
        head = lambda h: q[:, h * HEAD_DIM:(h + 1) * HEAD_DIM]
        zeros = jnp.zeros((q.shape[0], HEAD_DIM), F32)
        ref[rows, :WIDTH] = jnp.concatenate([t for a, _ in pairs for t in (head(a), zeros)], axis=1).astype(BF16)
        ref[rows, WIDTH:] = jnp.concatenate([t for _, b in pairs for t in (zeros, head(b))], axis=1).astype(BF16)

    pairs_a = [(2 * p, 2 * p + 1) for p in range(N_PAIRS)]
    pairs_b = [(p, N_PAIRS + p) for p in range(N_PAIRS)]

    subs = _sub_tiles(gt, rt)
    mods = [lambda k, gs=gs: mod_ref[k, gs] for gs, _, _, _ in subs]
    xs = [x_ref[gs, rs, :] for gs, rs, _, _ in subs]
    hs = [_adaln(x, gain(0), mod(0), mod(1)).reshape(n, D_MODEL).astype(BF16)
          for x, mod, (_, _, _, n) in zip(xs, mods, subs)]
    fs = [_swiglu(h, wg_ref, wu_ref, wd_ref).reshape(x.shape) for h, x in zip(hs, xs)]
    h2s = []
    for x, f, mod, (gs, rs, _, n) in zip(xs, fs, mods, subs):
        x1 = x + (FFN_RES * mod(2)) * _rms(f, gain(1))
        x1_ref[gs, rs, :] = x1
        h2s.append(_adaln(x1, gain(2), mod(3), mod(4)).reshape(n, D_MODEL).astype(BF16))

    for h2, (_, _, r0, n) in zip(h2s, subs):
        rows = slice(r0, r0 + n)

        def proj(lo, width):
            return jnp.dot(h2, win_ref[:, lo:lo + width], preferred_element_type=F32)

        store_split(qa_ref, rows, proj(0, WIDTH), pairs_a)
        ka = proj(WIDTH, WIDTH)
        va = proj(2 * WIDTH, WIDTH)
        store_split(qb_ref, rows, proj(3 * WIDTH, WIDTH), pairs_b)
        kvb = proj(4 * WIDTH, 2 * KV_WIDTH_B)
        kb, vb = kvb[:, :KV_WIDTH_B], kvb[:, KV_WIDTH_B:]
        ka_ref[rows, :] = ka.astype(BF16)
        va_ref[rows, :] = va.astype(BF16)
        kb_ref[rows, :] = kb.astype(BF16)
        vb_ref[rows, :] = vb.astype(BF16)
        if tail_scratch:
            for ref, val in zip(tail_scratch, (ka, va, kb, vb)):
                store_tail(ref, val, r0)
        else:
            store_heads(ka32_ref, ka, r0)
            store_heads(va32_ref, va, r0)
            store_tail(kb32_ref, kb, r0)
            store_tail(vb32_ref, vb, r0)

    if tail_scratch:
        @pl.when(pl.program_id(1) == pl.num_programs(1) - 1)
        def _():
            for out_ref, ref in zip((ka32_ref, va32_ref, kb32_ref, vb32_ref), tail_scratch):
                out_ref[...] = ref[...].T


def _resident(shape):
    return pl.BlockSpec(shape, lambda *_: (0,) * len(shape), pipeline_mode=pl.Buffered(1))


def _ffn1_qkv(x, mod, gains, wg, wu, wd, win, gt, rt):
    n_groups, rows, _ = x.shape
    n_g, n_r = n_groups // gt, rows // rt
    tm = gt * rt
    tokens = n_groups * rows
    ta = gt * min(REACH_A, rows)
    tb = gt * min(REACH_B, rows)
    tile = lambda gi, ri: (gi * n_r + ri, 0)
    tail = lambda gi, ri: (gi, 0)
    tok = lambda w, dt: jax.ShapeDtypeStruct((tokens, w), dt)
    if gt == 1:
        assert ta <= tm and tb <= tm
        tail_specs = [pl.BlockSpec((None, w, t), lambda gi, ri: (gi, 0, 0))
                      for w, t in ((WIDTH, ta), (WIDTH, ta), (KV_WIDTH_B, tb), (KV_WIDTH_B, tb))]
        tail_shapes = [jax.ShapeDtypeStruct((n_g, w, t), F32)
                       for w, t in ((WIDTH, ta), (WIDTH, ta), (KV_WIDTH_B, tb), (KV_WIDTH_B, tb))]
        scratch = [pltpu.VMEM((ta, WIDTH), F32), pltpu.VMEM((ta, WIDTH), F32),
                   pltpu.VMEM((tb, KV_WIDTH_B), F32), pltpu.VMEM((tb, KV_WIDTH_B), F32)]
    else:
        assert n_r == 1
        tail_specs = [pl.BlockSpec((tm * N_HEADS, HEAD_DIM), tail), pl.BlockSpec((tm * N_HEADS, HEAD_DIM), tail),
                      pl.BlockSpec((tm, KV_WIDTH_B), tail), pl.BlockSpec((tm, KV_WIDTH_B), tail)]
        tail_shapes = [jax.ShapeDtypeStruct((tokens * N_HEADS, HEAD_DIM), F32),
                       jax.ShapeDtypeStruct((tokens * N_HEADS, HEAD_DIM), F32),
                       jax.ShapeDtypeStruct((tokens, KV_WIDTH_B), F32), jax.ShapeDtypeStruct((tokens, KV_WIDTH_B), F32)]
        scratch = []
    vmem = (2 * 3 * D_MODEL * D_FF + 2 * D_MODEL * win.shape[1]
            + 2 * 2 * 4 * tm * D_MODEL
            + 2 * 2 * tm * (6 * WIDTH + 2 * KV_WIDTH_B)
            + 2 * 4 * (2 * ta * WIDTH + 2 * tb * KV_WIDTH_B)
            + tm * (2 * 4 + 2) * FF_SLAB_MAX + 3 * 4 * tm * D_MODEL
            + 4 * tm * win.shape[1])
    return pl.pallas_call(
        _ffn1_qkv_kernel,
        grid=(n_g, n_r),
        in_specs=[
            pl.BlockSpec((gt, rt, D_MODEL), lambda gi, ri: (gi, ri, 0)),
            pl.BlockSpec((N_MOD, gt, 1, D_MODEL), lambda gi, ri: (0, gi, 0, 0)),
            _resident(gains.shape),
            _resident(wg.shape), _resident(wu.shape), _resident(wd.shape), _resident(win.shape),
        ],
        out_specs=[
            pl.BlockSpec((gt, rt, D_MODEL), lambda gi, ri: (gi, ri, 0)),
            pl.BlockSpec((tm, 2 * WIDTH), tile), pl.BlockSpec((tm, WIDTH), tile), pl.BlockSpec((tm, WIDTH), tile),
            pl.BlockSpec((tm, 2 * WIDTH), tile), pl.BlockSpec((tm, KV_WIDTH_B), tile),
            pl.BlockSpec((tm, KV_WIDTH_B), tile),
            *tail_specs,
        ],
        out_shape=[
            jax.ShapeDtypeStruct(x.shape, F32),
            tok(2 * WIDTH, BF16), tok(WIDTH, BF16), tok(WIDTH, BF16),
            tok(2 * WIDTH, BF16), tok(KV_WIDTH_B, BF16), tok(KV_WIDTH_B, BF16),
            *tail_shapes,
        ],
        scratch_shapes=scratch,
        compiler_params=pltpu.CompilerParams(
            dimension_semantics=("arbitrary", "arbitrary"), vmem_limit_bytes=_vmem_limit(vmem)),
        name="ffn1_qkv",
    )(x, mod, gains, wg, wu, wd, win)


def _out_ffn2_kernel(x_ref, o_ref, mod_ref, gains_ref, gg_ref, wout_ref, wg_ref, wu_ref, wd_ref, y_ref):
    gt, rt, _ = x_ref.shape
    gain = lambda k: gains_ref[k:k + 1, :]
    gg = gg_ref[...]
    subs = _sub_tiles(gt, rt)
    mods = [lambda k, gs=gs: mod_ref[k, gs] for gs, _, _, _ in subs]
    xs = [x_ref[gs, rs, :] for gs, rs, _, _ in subs]
    ons = []
    for _, _, r0, n in subs:
        o = o_ref[r0:r0 + n, :].astype(F32)
        on = jnp.concatenate([_rms(o[:, :WIDTH], gg[:, :WIDTH]), _rms(o[:, WIDTH:], gg[:, WIDTH:])], axis=1)
        ons.append(on.astype(BF16))
    mixed = [jnp.dot(on, wout_ref[...], preferred_element_type=F32).reshape(x.shape) for on, x in zip(ons, xs)]
    x2s = [x + mod(5) * _rms(m, gain(3)) for x, m, mod in zip(xs, mixed, mods)]
    hs = [_adaln(x2, gain(4), mod(6), mod(7)).reshape(n, D_MODEL).astype(BF16)
          for x2, mod, (_, _, _, n) in zip(x2s, mods, subs)]
    fs = [_swiglu(h, wg_ref, wu_ref, wd_ref).reshape(x.shape) for h, x in zip(hs, xs)]
    for x2, f, mod, (gs, rs, _, _) in zip(x2s, fs, mods, subs):
        y_ref[gs, rs, :] = x2 + (FFN_RES * mod(8)) * _rms(f, gain(5))


def _out_ffn2(x1, o, mod, gains, gg, wout, wg, wu, wd, gt, rt):
    n_groups, rows, _ = x1.shape
    n_g, n_r = n_groups // gt, rows // rt
    tm = gt * rt
    vmem = (2 * 3 * D_MODEL * D_FF + 2 * wout.size
            + 2 * 2 * 4 * tm * D_MODEL + 2 * 2 * tm * 2 * WIDTH
            + tm * (2 * 4 + 2) * FF_SLAB_MAX + 5 * 4 * tm * D_MODEL)
    return pl.pallas_call(
        _out_ffn2_kernel,
        grid=(n_g, n_r),
        in_specs=[
            pl.BlockSpec((gt, rt, D_MODEL), lambda gi, ri: (gi, ri, 0)),
            pl.BlockSpec((tm, 2 * WIDTH), lambda gi, ri: (gi * n_r + ri, 0)),
            pl.BlockSpec((N_MOD, gt, 1, D_MODEL), lambda gi, ri: (0, gi, 0, 0)),
            _resident(gains.shape), _resident(gg.shape), _resident(wout.shape),
            _resident(wg.shape), _resident(wu.shape), _resident(wd.shape),
        ],
        out_specs=pl.BlockSpec((gt, rt, D_MODEL), lambda gi, ri: (gi, ri, 0)),
        out_shape=jax.ShapeDtypeStruct(x1.shape, F32),
        compiler_params=pltpu.CompilerParams(
            dimension_semantics=("arbitrary", "arbitrary"), vmem_limit_bytes=_vmem_limit(vmem)),
        name="out_ffn2",
    )(x1, o, mod, gains, gg, wout, wg, wu, wd)


def _softmax_terms(s, sink=None):
    m = jnp.max(s, axis=1, keepdims=True)
    if sink is not None:
        m = jnp.maximum(m, sink)
    e = jnp.exp2(s - m)
    denom = jnp.sum(e, axis=1, keepdims=True)
    if sink is not None:
        denom = denom + jnp.exp2(sink - m)
    return e.astype(BF16), 1.0 / denom


def _scores(q, k):
    return lax.dot_general(q, k, (((1,), (1,)), ((), ())), preferred_element_type=F32)


def _unstack_pair(z):
    n_q = z.shape[0] // 2
    is_lo = lax.broadcasted_iota(jnp.int32, (n_q, z.shape[1]), 1) < HEAD_DIM
    return jnp.where(is_lo, z[:n_q], z[n_q:])


def _sink_column(sinks_ref, pair, n_q):
    row = lax.broadcasted_iota(jnp.int32, (2 * n_q, 1), 0)
    return jnp.where(row < n_q, sinks_ref[pair], sinks_ref[N_PAIRS + pair]) * LOG2E


GROUPS_PER_STEP = 2
JOB_LAG = 4


def _attn_prompt_kernel(sinks_ref, qa_ref, qb_ref, ka_ref, va_ref, kb_ref, vb_ref, *rest):
    ba_refs, bb_refs = rest[:GROUPS_PER_STEP], rest[GROUPS_PER_STEP:2 * GROUPS_PER_STEP]
    n_cast = (len(rest) - 2 * GROUPS_PER_STEP - 1) // 2
    o_ref = rest[2 * GROUPS_PER_STEP + n_cast]
    for src, dst in zip(rest[2 * GROUPS_PER_STEP:2 * GROUPS_PER_STEP + n_cast], rest[len(rest) - n_cast:]):
        dst[...] = src[...].astype(BF16)
    slab = lambda p: slice(p * V7X_LANES, (p + 1) * V7X_LANES)
    bands = []
    first = pl.program_id(1) * GROUPS_PER_STEP
    for j in range(GROUPS_PER_STEP):
        start = lambda g, reach: jnp.maximum(g * GROUP_Q - reach, 0)
        off_a = pl.multiple_of(start(first + j, REACH_A) - start(first, REACH_A), GROUP_Q)
        off_b = pl.multiple_of(start(first + j, REACH_B) - start(first, REACH_B), GROUP_Q)
        bands.append((pl.ds(off_a, BAND_A), pl.ds(off_b, BAND_B)))

    def stacked(q_ref, j, p):
        rows = slice(j * GROUP_Q, (j + 1) * GROUP_Q)
        return jnp.concatenate([q_ref[rows, slab(p)], q_ref[rows, slab(N_PAIRS + p)]], axis=0)

    def score(mixer, j, p):
        if mixer == "a":
            bias = jnp.concatenate([ba_refs[j][0, 2 * p], ba_refs[j][0, 2 * p + 1]], axis=0)
            return _scores(stacked(qa_ref, j, p), ka_ref[bands[j][0], slab(p)]) + bias
        bias = jnp.concatenate([bb_refs[j][0, p], bb_refs[j][0, N_PAIRS + p]], axis=0)
        return _scores(stacked(qb_ref, j, p), kb_ref[bands[j][1], :]) + bias

    def weights(s, sink):
        m = jnp.max(s, axis=1, keepdims=True)
        if sink is None:
            return jnp.exp2(s - m).astype(BF16), None
        m = jnp.maximum(m, sink)
        return jnp.exp2(s - m).astype(BF16), jnp.exp2(sink - m)

    def attend(e, extra, v):
        z = jnp.dot(e, jnp.concatenate([v, jnp.ones_like(v)], axis=1), preferred_element_type=F32)
        denom = z[:, V7X_LANES:] if extra is None else z[:, V7X_LANES:] + extra
        return _unstack_pair(z[:, :V7X_LANES] * (1.0 / denom))

    jobs = [(mixer, j, p) for mixer in ("a", "b") for j in range(GROUPS_PER_STEP) for p in range(N_PAIRS)]
    s_all, w_all = {}, {}
    for t in range(len(jobs) + 2 * JOB_LAG):
        if t < len(jobs):
            s_all[t] = score(*jobs[t])
        if 0 <= t - JOB_LAG < len(jobs):
            mixer, j, p = jobs[t - JOB_LAG]
            w_all[t - JOB_LAG] = weights(s_all.pop(t - JOB_LAG),
                                         _sink_column(sinks_ref, p, GROUP_Q) if mixer == "b" else None)
        if 0 <= t - 2 * JOB_LAG < len(jobs):
            mixer, j, p = jobs[t - 2 * JOB_LAG]
            e, extra = w_all.pop(t - 2 * JOB_LAG)
            rows = slice(j * GROUP_Q, (j + 1) * GROUP_Q)
            if mixer == "a":
                o_ref[rows, slab(p)] = attend(e, extra, va_ref[bands[j][0], slab(p)]).astype(BF16)
            else:
                o_ref[rows, slab(N_PAIRS + p)] = attend(e, extra, vb_ref[bands[j][1], :]).astype(BF16)


def _attn_prompt(sinks, qa, qb, ka, va, kb, vb, bias_a, bias_b, batch, seq, cast=()):
    rows = GROUPS_PER_STEP * GROUP_Q
    n_steps = seq // rows
    cast_specs, cast_shapes = (zip(*[_cast_chunks(w, batch * n_steps, lambda b, s: b * n_steps + s) for w in cast])
                               if cast else ((), ()))
    q_spec = pl.BlockSpec((rows, 2 * WIDTH), lambda b, s: (b * n_steps + s, 0))
    kv_spec = lambda w, reach: pl.BlockSpec(
        (pl.Element(reach + rows), pl.Element(w)),
        lambda b, s: (pl.multiple_of(b * seq + jnp.maximum(s * rows - reach, 0), GROUP_Q), 0))
    bias_spec = lambda band, n_var, j: pl.BlockSpec(
        (1, N_HEADS, GROUP_Q, band),
        lambda b, s: (jnp.minimum(s * GROUPS_PER_STEP + j, n_var - 1), 0, 0, 0))
    vmem = (2 * 2 * 2 * ((REACH_A + rows) * WIDTH + (REACH_B + rows) * KV_WIDTH_B)
            + GROUPS_PER_STEP * 2 * 4 * N_HEADS * GROUP_Q * (BAND_A + BAND_B)
            + 2 * 2 * rows * 6 * WIDTH + GROUPS_PER_STEP * 6 * 4 * 2 * GROUP_Q * BAND_A
            + sum(2 * 6 * math.prod(spec.block_shape) for spec in cast_specs))
    out = pl.pallas_call(
        _attn_prompt_kernel,
        grid=(batch, n_steps),
        in_specs=[
            pl.BlockSpec(memory_space=pltpu.SMEM),
            q_spec, q_spec,
            kv_spec(WIDTH, REACH_A), kv_spec(WIDTH, REACH_A), kv_spec(KV_WIDTH_B, REACH_B), kv_spec(KV_WIDTH_B, REACH_B),
            *[bias_spec(BAND_A, N_VAR_A, j) for j in range(GROUPS_PER_STEP)],
            *[bias_spec(BAND_B, N_VAR_B, j) for j in range(GROUPS_PER_STEP)],
            *cast_specs,
        ],
        out_specs=[pl.BlockSpec((rows, 2 * WIDTH), lambda b, s: (b * n_steps + s, 0)), *cast_specs],
        out_shape=[jax.ShapeDtypeStruct((batch * seq, 2 * WIDTH), BF16), *cast_shapes],
        compiler_params=pltpu.CompilerParams(
            dimension_semantics=("arbitrary", "arbitrary"), vmem_limit_bytes=_vmem_limit(vmem)),
        name="attn_prompt",
    )(sinks, qa, qb, ka, va, kb, vb, *([bias_a] * GROUPS_PER_STEP), *([bias_b] * GROUPS_PER_STEP), *cast)
    return out[0], tuple(out[1:])


def _pair_ordered_cols(h):
    return (h % N_PAIRS) * V7X_LANES + (h // N_PAIRS) * HEAD_DIM


SAMPLE_ROWS_PER_STEP = 4


def _attn_sample_kernel(sinks_ref, qa_ref, qb_ref, ka_ref, va_ref, kb_ref, vb_ref,
                        cak_ref, cav_ref, cbk_ref, cbv_ref, ba_ref, bb_ref, o_ref):
    n_rows = cak_ref.shape[0]
    n_q = qa_ref.shape[0] // n_rows
    n_kv = cbk_ref.shape[1]
    per_kv = N_HEADS // n_kv
    la, lb = cak_ref.shape[3], cbk_ref.shape[3]
    head = lambda ref, r, idx: ref[r * n_q:(r + 1) * n_q, idx * HEAD_DIM:(idx + 1) * HEAD_DIM]

    def query(ref, r, lo):
        lo = lo if lo % V7X_LANES < HEAD_DIM else WIDTH + lo
        return ref[r * n_q:(r + 1) * n_q, lo:lo + HEAD_DIM]

    def scores(q, cache_t_ref, new_ref, r, idx):
        old = jnp.dot(q, cache_t_ref[r, idx].astype(BF16), preferred_element_type=F32)
        return jnp.concatenate([old, _scores(q, head(new_ref, r, idx))], axis=1)

    def values(e, cache_t_ref, new_ref, r, idx, n_old):
        old = _scores(e[:, :n_old], cache_t_ref[r, idx].astype(BF16))
        return old + jnp.dot(e[:, n_old:], head(new_ref, r, idx), preferred_element_type=F32)

    jobs_a = [(r, h) for r in range(n_rows) for h in range(N_HEADS)]
    jobs_b = [(r, c) for r in range(n_rows) for c in range(n_kv)]
    s_a = [scores(query(qa_ref, r, h * HEAD_DIM), cak_ref, ka_ref, r, h) + ba_ref[h] for r, h in jobs_a]
    s_b = []
    for r, c in jobs_b:
        heads = range(c * per_kv, (c + 1) * per_kv)
        q = jnp.concatenate([query(qb_ref, r, _pair_ordered_cols(h)) for h in heads], axis=0)
        bias = jnp.concatenate([bb_ref[h] for h in heads], axis=0)
        s_b.append(scores(q, cbk_ref, kb_ref, r, c) + bias)

    row = lax.broadcasted_iota(jnp.int32, (per_kv * n_q, 1), 0)
    sinks = []
    for c in range(n_kv):
        sink = jnp.zeros((per_kv * n_q, 1), F32)
        for i in range(per_kv):
            sink = jnp.where((row >= i * n_q) & (row < (i + 1) * n_q), sinks_ref[c * per_kv + i], sink)
        sinks.append(sink * LOG2E)
    p_a = [_softmax_terms(s) for s in s_a]
    p_b = [_softmax_terms(s, sinks[c]) for s, (_, c) in zip(s_b, jobs_b)]

    out_a = {job: values(e, cav_ref, va_ref, *job, la) * inv for job, (e, inv) in zip(jobs_a, p_a)}
    out_b = {}
    for (r, c), (e, inv) in zip(jobs_b, p_b):
        o = values(e, cbv_ref, vb_ref, r, c, lb) * inv
        for i in range(per_kv):
            out_b[(r, c * per_kv + i)] = o[i * n_q:(i + 1) * n_q]
    order_b = [h for p in range(N_PAIRS) for h in (p, N_PAIRS + p)]
    for r in range(n_rows):
        pieces = [out_a[(r, h)] for h in range(N_HEADS)] + [out_b[(r, h)] for h in order_b]
        o_ref[r * n_q:(r + 1) * n_q, :] = jnp.concatenate(pieces, axis=1).astype(BF16)


def _attn_sample(sinks, qa, qb, ka, va, kb, vb, cak, cav, cbk, cbv, bias_a, bias_b, batch, s_len):
    n_rows = SAMPLE_ROWS_PER_STEP
    row = lambda w: pl.BlockSpec((n_rows * s_len, w), lambda b: (b, 0))
    cache = lambda a: pl.BlockSpec((n_rows,) + a.shape[1:], lambda b: (b, 0, 0, 0))
    whole = lambda a: pl.BlockSpec(a.shape, lambda b: (0,) * a.ndim)
    return pl.pallas_call(
        _attn_sample_kernel,
        grid=(batch // n_rows,),
        in_specs=[
            pl.BlockSpec(memory_space=pltpu.SMEM),
            row(2 * WIDTH), row(2 * WIDTH), row(WIDTH), row(WIDTH), row(KV_WIDTH_B), row(KV_WIDTH_B),
            cache(cak), cache(cav), cache(cbk), cache(cbv),
            whole(bias_a), whole(bias_b),
        ],
        out_specs=pl.BlockSpec((n_rows * s_len, 2 * WIDTH), lambda b: (b, 0)),
        out_shape=jax.ShapeDtypeStruct((batch * s_len, 2 * WIDTH), BF16),
        compiler_params=pltpu.CompilerParams(dimension_semantics=("arbitrary",)),
        name="attn_sample",
    )(sinks, qa, qb, ka, va, kb, vb, cak, cav, cbk, cbv, bias_a, bias_b)


def _pair_order(x, axis):
    shape = x.shape
    split = shape[:axis] + (N_HEADS // N_PAIRS, N_PAIRS, HEAD_DIM) + shape[axis + 1:]
    return jnp.swapaxes(x.reshape(split), axis, axis + 1).reshape(shape)


def _prep_out_weights(w_out, group_gains):
    wout = jnp.concatenate([w_out[:WIDTH], _pair_order(w_out[WIDTH:], 0)], axis=0).astype(BF16)
    gg = jnp.concatenate([group_gains[:WIDTH], _pair_order(group_gains[WIDTH:], 0)])
    return wout, gg.reshape(1, 2 * WIDTH)


def kernel(x_prompt, x_sample, cache_a_k, cache_a_v, cache_b_k, cache_b_v, c_prompt, c_sample, w_mod, b_mod,
           norm_gains, w1_gate, w1_up, w1_down, w_in, w_out, group_gains, rel_bias_a, t5_bias_table, sinks_b,
           w2_gate, w2_up, w2_down):
    depth = w_mod.shape[0]
    assert depth == 1
    batch, seq, _ = x_prompt.shape
    s_batch, s_len, _ = x_sample.shape
    la, lb = cache_a_k.shape[2], cache_b_k.shape[2]
    assert la == REACH_A and lb == REACH_B and seq % (2 * FFN_TILE_TOKENS) == 0

    c_rows = jnp.concatenate([c_prompt, c_sample], axis=0)
    pad_rows = (-c_rows.shape[0]) % 8
    c_rows = jnp.pad(c_rows, ((0, pad_rows), (0, 0)))
    mod = _modulation(c_rows, w_mod[0], b_mod[0])
    mod_p = mod[:, :batch].reshape(N_MOD, batch, 1, D_MODEL)
    mod_s = mod[:, batch:batch + s_batch].reshape(N_MOD, s_batch, 1, D_MODEL)

    bias_a, bias_b, bias_sa, bias_sb, *w1, win = _bias_tables(
        rel_bias_a[0], t5_bias_table, s_len, la, lb, cast=(w1_gate[0], w1_up[0], w1_down[0], w_in[0]))

    gains = norm_gains[0]
    wout, gg = _prep_out_weights(w_out[0], group_gains[0])
    sinks = sinks_b[0]

    (x1p, qa, ka, va, qb, kb, vb, ka32, va32, kb32, vb32) = _ffn1_qkv(
        x_prompt, mod_p, gains, *w1, win, 1, FFN_TILE_TOKENS)
    op, w2 = _attn_prompt(sinks, qa, qb, ka, va, kb, vb, bias_a, bias_b, batch, seq,
                          cast=(w2_gate[0], w2_up[0], w2_down[0]))
    y_prompt = _out_ffn2(x1p, op, mod_p, gains, gg, wout, *w2, 1, 2 * FFN_TILE_TOKENS)
    rows_last = lambda t: jnp.transpose(t.reshape(depth, batch, -1, HEAD_DIM, t.shape[-1]), (0, 1, 4, 2, 3))
    new_p = tuple(rows_last(t) for t in (ka32, va32, kb32, vb32))

    gt_s = FFN_TILE_TOKENS // s_len
    (x1s, qas, kas, vas, qbs, kbs, vbs, kas32, vas32, kbs32, vbs32) = _ffn1_qkv(
        x_sample, mod_s, gains, *w1, win, gt_s, s_len)
    os_ = _attn_sample(sinks, qas, qbs, kas, vas, kbs, vbs,
                       *(jnp.transpose(c[0], (0, 2, 3, 1)) for c in (cache_a_k, cache_a_v, cache_b_k, cache_b_v)),
                       bias_sa, bias_sb, s_batch, s_len)
    y_sample = _out_ffn2(x1s, os_, mod_s, gains, gg, wout, *w2, 2 * gt_s, s_len)
    new_s = (kas32.reshape(depth, s_batch, s_len, N_HEADS, HEAD_DIM),
             vas32.reshape(depth, s_batch, s_len, N_HEADS, HEAD_DIM),
             kbs32.reshape(depth, s_batch, s_len, 2, HEAD_DIM), vbs32.reshape(depth, s_batch, s_len, 2, HEAD_DIM))

    return (y_prompt, y_sample) + new_p + new_s
```

```python
import functools
import math

import jax
import jax.numpy as jnp
from jax import lax
from jax.experimental import pallas as pl
from jax.experimental.pallas import tpu as pltpu

F32 = jnp.float32
BF16 = jnp.bfloat16

D_MODEL = 1024
D_FF = 2816
CHUNK = 64
CHUNK_SHIFT = CHUNK.bit_length() - 1
HEAD_DIM = 64
N_HEADS = 8
N_PAIRS = N_HEADS // 2
WIDTH = N_HEADS * HEAD_DIM
KV_WIDTH_B = 2 * HEAD_DIM
REACH_A = 8 * CHUNK
REACH_B = 2 * CHUNK
REL_CLIP_A = 128
N_BUCKETS = 32
N_MOD = 9
FFN_RES = 0.5
EPS = 1e-6
SCALE = HEAD_DIM ** -0.5
LOG2E = 1.4426950408889634
NEG_INF = -1e30

V7X_LANES = 128
V7X_VMEM_LIMIT_CAP = 56 * 1024 * 1024

GROUP_Q = 2 * CHUNK
BAND_A = REACH_A + GROUP_Q
BAND_B = REACH_B + GROUP_Q
ORIGIN_A = 1024
WIDTH_MASTER_A = ORIGIN_A + BAND_A
ORIGIN_B = 256
WIDTH_MASTER_B = ORIGIN_B + BAND_B
N_VAR_A = REACH_A // GROUP_Q + 1
N_VAR_B = REACH_B // GROUP_Q + 1


def _vmem_limit(estimate_bytes):
    return int(min(max(estimate_bytes, 32 * 1024 * 1024), V7X_VMEM_LIMIT_CAP))


def _rms(x, gain):
    ms = jnp.mean(x * x, axis=-1, keepdims=True)
    return x * lax.rsqrt(ms + EPS) * gain


def _adaln(x, gain, shift, scale):
    return _rms(x, gain) * (1.0 + scale) + shift


def _silu(x):
    return x * (1.0 / (1.0 + jnp.exp(-x)))


def _mod_kernel(c_ref, w_ref, b_ref, o_ref):
    s = _silu(c_ref[...]).astype(BF16)
    o_ref[0] = jnp.dot(s, w_ref[...].astype(BF16), preferred_element_type=F32) + b_ref[0]


def _modulation(c_rows, w_mod, b_mod):
    rows = c_rows.shape[0]
    return pl.pallas_call(
        _mod_kernel,
        grid=(N_MOD,),
        in_specs=[
            pl.BlockSpec((rows, D_MODEL), lambda j: (0, 0)),
            pl.BlockSpec((D_MODEL, D_MODEL), lambda j: (0, j)),
            pl.BlockSpec((1, 1, D_MODEL), lambda j: (j, 0, 0)),
        ],
        out_specs=pl.BlockSpec((1, rows, D_MODEL), lambda j: (j, 0, 0)),
        out_shape=jax.ShapeDtypeStruct((N_MOD, rows, D_MODEL), F32),
        name="modulation",
    )(c_rows, w_mod, b_mod.reshape(N_MOD, 1, D_MODEL))


def _skew(base_row, n_rows):
    width = base_row.shape[1]
    m = jnp.broadcast_to(base_row, (n_rows, width))
    row = lax.broadcasted_iota(jnp.int32, (n_rows, width), 0)
    shift = 1
    while shift < n_rows:
        m = jnp.where((row & shift) != 0, pltpu.roll(m, shift, axis=1), m)
        shift *= 2
    return m


def _band_mask(n_rows, width, origin, n_prev):
    row = lax.broadcasted_iota(jnp.int32, (n_rows, width), 0)
    col = lax.broadcasted_iota(jnp.int32, (n_rows, width), 1)
    q_chunk = row >> CHUNK_SHIFT
    k_chunk = (col >> CHUNK_SHIFT) - (origin // CHUNK)
    return (k_chunk >= q_chunk - n_prev) & (k_chunk <= q_chunk)


def _t5_bucket(rel):
    half = N_BUCKETS // 2
    max_exact = half // 2
    n = jnp.abs(rel)
    n2 = n * n
    large = jnp.full(rel.shape, max_exact, jnp.int32)
    for k in range(1, half - max_exact):
        large = large + jnp.where(n2 >= (max_exact * max_exact) * (2 ** k), 1, 0)
    return jnp.where(rel > 0, half, 0) + jnp.where(n < max_exact, n, large)


def _bias_kernel(rel_ref, t5_ref, *refs, s_len, la, lb):
    n_cast = (len(refs) - 4) // 2
    for src, dst in zip(refs[:n_cast], refs[n_cast + 4:]):
        dst[...] = src[...].astype(BF16)
    ba_ref, bb_ref, bsa_ref, bsb_ref = refs[n_cast:n_cast + 4]

    tab = rel_ref[0] * LOG2E
    n_tab = 2 * REL_CLIP_A + 1
    first, last = tab[:, 0:1], tab[:, n_tab - 1:n_tab]
    win_lo = ORIGIN_A - 2 * REL_CLIP_A
    col = lax.broadcasted_iota(jnp.int32, (1, tab.shape[1] + V7X_LANES), 1)
    shifted = jnp.concatenate([jnp.zeros((1, V7X_LANES), F32), tab], axis=1)
    u_win = jnp.where(col < V7X_LANES, first, jnp.where(col >= V7X_LANES + n_tab, last, shifted))
    window = _skew(u_win, GROUP_Q)[:, V7X_LANES:]
    n_left = win_lo + V7X_LANES
    n_right = WIDTH_MASTER_A - n_left - window.shape[1]
    skew_a = jnp.concatenate([jnp.broadcast_to(first, (GROUP_Q, n_left)), window,
                              jnp.broadcast_to(last, (GROUP_Q, n_right))], axis=1)
    master_a = jnp.where(_band_mask(GROUP_Q, WIDTH_MASTER_A, ORIGIN_A, REACH_A // CHUNK), skew_a, NEG_INF)
    for v in range(N_VAR_A):
        start = ORIGIN_A - min(v * GROUP_Q, REACH_A) if v < N_VAR_A - 1 else ORIGIN_A - REACH_A
        ba_ref[v, 0] = master_a[:, start:start + BAND_A]
    bsa_ref[0] = skew_a[:s_len, ORIGIN_A - la:ORIGIN_A + s_len]

    t5 = t5_ref[0] * LOG2E
    rel = lax.broadcasted_iota(jnp.int32, (1, WIDTH_MASTER_B), 1) - ORIGIN_B
    bucket = _t5_bucket(rel)
    u_b = jnp.zeros((1, WIDTH_MASTER_B), F32)
    for i in range(N_BUCKETS):
        u_b = jnp.where(bucket == i, t5[:, i:i + 1], u_b)
    skew_b = _skew(u_b, GROUP_Q)
    master_b = jnp.where(_band_mask(GROUP_Q, WIDTH_MASTER_B, ORIGIN_B, REACH_B // CHUNK), skew_b, NEG_INF)
    for v in range(N_VAR_B):
        start = ORIGIN_B - min(v * GROUP_Q, REACH_B) if v < N_VAR_B - 1 else ORIGIN_B - REACH_B
        bb_ref[v, 0] = master_b[:, start:start + BAND_B]
    bsb_ref[0] = skew_b[:s_len, ORIGIN_B - lb:ORIGIN_B + s_len]


V7X_BF16_SUBLANES = 16


def _cast_chunks(w, n_steps, step_of):
    rows, cols = w.shape
    n_chunks = n_steps
    while rows % (n_chunks * V7X_BF16_SUBLANES):
        n_chunks //= 2
    spec = pl.BlockSpec((rows // n_chunks, cols), lambda *g: (jnp.minimum(step_of(*g), n_chunks - 1), 0))
    return spec, jax.ShapeDtypeStruct(w.shape, BF16)


def _bias_tables(rel_bias_a, t5_table, s_len, la, lb, cast=()):
    n_tab = 2 * REL_CLIP_A + 1
    tab_w = 3 * V7X_LANES
    rel_p = jnp.pad(rel_bias_a, ((0, 0), (0, tab_w - n_tab))).reshape(N_HEADS, 1, tab_w)
    t5_p = t5_table.reshape(N_HEADS, 1, N_BUCKETS)
    kern = functools.partial(_bias_kernel, s_len=s_len, la=la, lb=lb)
    cast_specs, cast_shapes = zip(*[_cast_chunks(w, N_HEADS, lambda h: h) for w in cast]) if cast else ((), ())
    return pl.pallas_call(
        kern,
        grid=(N_HEADS,),
        in_specs=[
            pl.BlockSpec((1, 1, tab_w), lambda h: (h, 0, 0)),
            pl.BlockSpec((1, 1, N_BUCKETS), lambda h: (h, 0, 0)),
            *cast_specs,
        ],
        out_specs=[
            pl.BlockSpec((N_VAR_A, 1, GROUP_Q, BAND_A), lambda h: (0, h, 0, 0)),
            pl.BlockSpec((N_VAR_B, 1, GROUP_Q, BAND_B), lambda h: (0, h, 0, 0)),
            pl.BlockSpec((1, s_len, la + s_len), lambda h: (h, 0, 0)),
            pl.BlockSpec((1, s_len, lb + s_len), lambda h: (h, 0, 0)),
            *cast_specs,
        ],
        out_shape=[
            jax.ShapeDtypeStruct((N_VAR_A, N_HEADS, GROUP_Q, BAND_A), F32),
            jax.ShapeDtypeStruct((N_VAR_B, N_HEADS, GROUP_Q, BAND_B), F32),
            jax.ShapeDtypeStruct((N_HEADS, s_len, la + s_len), F32),
            jax.ShapeDtypeStruct((N_HEADS, s_len, lb + s_len), F32),
            *cast_shapes,
        ],
        compiler_params=pltpu.CompilerParams(dimension_semantics=("arbitrary",)),
        name="bias_tables",
    )(rel_p, t5_p, *cast)


V7X_MXU_DIM = 256
FF_SLABS = ((0, 6 * V7X_MXU_DIM), (6 * V7X_MXU_DIM, D_FF))
FF_SLAB_MAX = max(hi - lo for lo, hi in FF_SLABS)


def _swiglu(h, wg_ref, wu_ref, wd_ref):
    acts = []
    for lo, hi in FF_SLABS:
        g = jnp.dot(h, wg_ref[:, lo:hi], preferred_element_type=F32)
        u = jnp.dot(h, wu_ref[:, lo:hi], preferred_element_type=F32)
        acts.append((_silu(g) * u).astype(BF16))
    acc = None
    for (lo, hi), a in zip(FF_SLABS, acts):
        part = jnp.dot(a, wd_ref[lo:hi, :], preferred_element_type=F32)
        acc = part if acc is None else acc + part
    return acc


SUB_TOKENS = V7X_MXU_DIM
FFN_TILE_TOKENS = 2 * SUB_TOKENS


def _sub_tiles(gt, rt):
    n_sub = gt * rt // SUB_TOKENS
    if gt == 1:
        step = rt // n_sub
        return [(slice(0, 1), slice(s * step, (s + 1) * step), s * step, step) for s in range(n_sub)]
    step = gt // n_sub
    return [(slice(s * step, (s + 1) * step), slice(0, rt), s * step * rt, step * rt) for s in range(n_sub)]


def _ffn1_qkv_kernel(x_ref, mod_ref, gains_ref, wg_ref, wu_ref, wd_ref, win_ref,
                     x1_ref, qa_ref, ka_ref, va_ref, qb_ref, kb_ref, vb_ref,
                     ka32_ref, va32_ref, kb32_ref, vb32_ref, *tail_scratch):
    gt, rt, _ = x_ref.shape
    tm = gt * rt
    gain = lambda k: gains_ref[k:k + 1, :]

    def store_tail(ref, val, r0):
        first = tm - ref.shape[0]
        lo = max(r0, first)
        if lo < r0 + val.shape[0]:
            ref[lo - first:r0 + val.shape[0] - first, :] = val[lo - r0:, :]

    def store_heads(ref, val, r0):
        n_heads = val.shape[1] // HEAD_DIM
        for h in range(n_heads):
            ref[pl.ds(r0 * n_heads + h, val.shape[0], stride=n_heads), :] = val[:, h * HEAD_DIM:(h + 1) * HEAD_DIM]

    def store_split(ref, rows, q, pairs):
        q = q * (SCALE * LOG2E)
        head = lambda h: q[:, h * HEAD_DIM:(h + 1) * HEAD_DIM]
        zeros = jnp.zeros((q.shape[0], HEAD_DIM), F32)
        ref[rows, :WIDTH] = jnp.concatenate([t for a, _ in pairs for t in (head(a), zeros)], axis=1).astype(BF16)
        ref[rows, WIDTH:] = jnp.concatenate([t for _, b in pairs for t in (zeros, head(b))], axis=1).astype(BF16)

    pairs_a = [(2 * p, 2 * p + 1) for p in range(N_PAIRS)]
    pairs_b = [(p, N_PAIRS + p) for p in range(N_PAIRS)]

    subs = _sub_tiles(gt, rt)
    mods = [lambda k, gs=gs: mod_ref[k, gs] for gs, _, _, _ in subs]
    xs = [x_ref[gs, rs, :] for gs, rs, _, _ in subs]
    hs = [_adaln(x, gain(0), mod(0), mod(1)).reshape(n, D_MODEL).astype(BF16)
          for x, mod, (_, _, _, n) in zip(xs, mods, subs)]
    fs = [_swiglu(h, wg_ref, wu_ref, wd_ref).reshape(x.shape) for h, x in zip(hs, xs)]
    h2s = []
    for x, f, mod, (gs, rs, _, n) in zip(xs, fs, mods, subs):
        x1 = x + (FFN_RES * mod(2)) * _rms(f, gain(1))
        x1_ref[gs, rs, :] = x1
        h2s.append(_adaln(x1, gain(2), mod(3), mod(4)).reshape(n, D_MODEL).astype(BF16))

    for h2, (_, _, r0, n) in zip(h2s, subs):
        rows = slice(r0, r0 + n)

        def proj(lo, width):
            return jnp.dot(h2, win_ref[:, lo:lo + width], preferred_element_type=F32)

        store_split(qa_ref, rows, proj(0, WIDTH), pairs_a)
        ka = proj(WIDTH, WIDTH)
        va = proj(2 * WIDTH, WIDTH)
        store_split(qb_ref, rows, proj(3 * WIDTH, WIDTH), pairs_b)
        kvb = proj(4 * WIDTH, 2 * KV_WIDTH_B)
        kb, vb = kvb[:, :KV_WIDTH_B], kvb[:, KV_WIDTH_B:]
        ka_ref[rows, :] = ka.astype(BF16)
        va_ref[rows, :] = va.astype(BF16)
        kb_ref[rows, :] = kb.astype(BF16)
        vb_ref[rows, :] = vb.astype(BF16)
        if tail_scratch:
            for ref, val in zip(tail_scratch, (ka, va, kb, vb)):
                store_tail(ref, val, r0)
        else:
            store_heads(ka32_ref, ka, r0)
            store_heads(va32_ref, va, r0)
            store_tail(kb32_ref, kb, r0)
            store_tail(vb32_ref, vb, r0)

    if tail_scratch:
        @pl.when(pl.program_id(1) == pl.num_programs(1) - 1)
        def _():
            for out_ref, ref in zip((ka32_ref, va32_ref, kb32_ref, vb32_ref), tail_scratch):
                out_ref[...] = ref[...].T


def _resident(shape):
    return pl.BlockSpec(shape, lambda *_: (0,) * len(shape), pipeline_mode=pl.Buffered(1))


def _ffn1_qkv(x, mod, gains, wg, wu, wd, win, gt, rt):
    n_groups, rows, _ = x.shape
    n_g, n_r = n_groups // gt, rows // rt
    tm = gt * rt
    tokens = n_groups * rows
    ta = gt * min(REACH_A, rows)
    tb = gt * min(REACH_B, rows)
    tile = lambda gi, ri: (gi * n_r + ri, 0)
    tail = lambda gi, ri: (gi, 0)
    tok = lambda w, dt: jax.ShapeDtypeStruct((tokens, w), dt)
    if gt == 1:
        assert ta <= tm and tb <= tm
        tail_specs = [pl.BlockSpec((None, w, t), lambda gi, ri: (gi, 0, 0))
                      for w, t in ((WIDTH, ta), (WIDTH, ta), (KV_WIDTH_B, tb), (KV_WIDTH_B, tb))]
        tail_shapes = [jax.ShapeDtypeStruct((n_g, w, t), F32)
                       for w, t in ((WIDTH, ta), (WIDTH, ta), (KV_WIDTH_B, tb), (KV_WIDTH_B, tb))]
        scratch = [pltpu.VMEM((ta, WIDTH), F32), pltpu.VMEM((ta, WIDTH), F32),
                   pltpu.VMEM((tb, KV_WIDTH_B), F32), pltpu.VMEM((tb, KV_WIDTH_B), F32)]
    else:
        assert n_r == 1
        tail_specs = [pl.BlockSpec((tm * N_HEADS, HEAD_DIM), tail), pl.BlockSpec((tm * N_HEADS, HEAD_DIM), tail),
                      pl.BlockSpec((tm, KV_WIDTH_B), tail), pl.BlockSpec((tm, KV_WIDTH_B), tail)]
        tail_shapes = [jax.ShapeDtypeStruct((tokens * N_HEADS, HEAD_DIM), F32),
                       jax.ShapeDtypeStruct((tokens * N_HEADS, HEAD_DIM), F32),
                       jax.ShapeDtypeStruct((tokens, KV_WIDTH_B), F32), jax.ShapeDtypeStruct((tokens, KV_WIDTH_B), F32)]
        scratch = []
    vmem = (2 * 3 * D_MODEL * D_FF + 2 * D_MODEL * win.shape[1]
            + 2 * 2 * 4 * tm * D_MODEL
            + 2 * 2 * tm * (6 * WIDTH + 2 * KV_WIDTH_B)
            + 2 * 4 * (2 * ta * WIDTH + 2 * tb * KV_WIDTH_B)
            + tm * (2 * 4 + 2) * FF_SLAB_MAX + 3 * 4 * tm * D_MODEL
            + 4 * tm * win.shape[1])
    return pl.pallas_call(
        _ffn1_qkv_kernel,
        grid=(n_g, n_r),
        in_specs=[
            pl.BlockSpec((gt, rt, D_MODEL), lambda gi, ri: (gi, ri, 0)),
            pl.BlockSpec((N_MOD, gt, 1, D_MODEL), lambda gi, ri: (0, gi, 0, 0)),
            _resident(gains.shape),
            _resident(wg.shape), _resident(wu.shape), _resident(wd.shape), _resident(win.shape),
        ],
        out_specs=[
            pl.BlockSpec((gt, rt, D_MODEL), lambda gi, ri: (gi, ri, 0)),
            pl.BlockSpec((tm, 2 * WIDTH), tile), pl.BlockSpec((tm, WIDTH), tile), pl.BlockSpec((tm, WIDTH), tile),
            pl.BlockSpec((tm, 2 * WIDTH), tile), pl.BlockSpec((tm, KV_WIDTH_B), tile),
            pl.BlockSpec((tm, KV_WIDTH_B), tile),
            *tail_specs,
        ],
        out_shape=[
            jax.ShapeDtypeStruct(x.shape, F32),
            tok(2 * WIDTH, BF16), tok(WIDTH, BF16), tok(WIDTH, BF16),
            tok(2 * WIDTH, BF16), tok(KV_WIDTH_B, BF16), tok(KV_WIDTH_B, BF16),
            *tail_shapes,
        ],
        scratch_shapes=scratch,
        compiler_params=pltpu.CompilerParams(
            dimension_semantics=("arbitrary", "arbitrary"), vmem_limit_bytes=_vmem_limit(vmem)),
        name="ffn1_qkv",
    )(x, mod, gains, wg, wu, wd, win)


def _out_ffn2_kernel(x_ref, o_ref, mod_ref, gains_ref, gg_ref, wout_ref, wg_ref, wu_ref, wd_ref, y_ref):
    gt, rt, _ = x_ref.shape
    gain = lambda k: gains_ref[k:k + 1, :]
    gg = gg_ref[...]
    subs = _sub_tiles(gt, rt)
    mods = [lambda k, gs=gs: mod_ref[k, gs] for gs, _, _, _ in subs]
    xs = [x_ref[gs, rs, :] for gs, rs, _, _ in subs]
    ons = []
    for _, _, r0, n in subs:
        o = o_ref[r0:r0 + n, :].astype(F32)
        on = jnp.concatenate([_rms(o[:, :WIDTH], gg[:, :WIDTH]), _rms(o[:, WIDTH:], gg[:, WIDTH:])], axis=1)
        ons.append(on.astype(BF16))
    mixed = [jnp.dot(on, wout_ref[...], preferred_element_type=F32).reshape(x.shape) for on, x in zip(ons, xs)]
    x2s = [x + mod(5) * _rms(m, gain(3)) for x, m, mod in zip(xs, mixed, mods)]
    hs = [_adaln(x2, gain(4), mod(6), mod(7)).reshape(n, D_MODEL).astype(BF16)
          for x2, mod, (_, _, _, n) in zip(x2s, mods, subs)]
    fs = [_swiglu(h, wg_ref, wu_ref, wd_ref).reshape(x.shape) for h, x in zip(hs, xs)]
    for x2, f, mod, (gs, rs, _, _) in zip(x2s, fs, mods, subs):
        y_ref[gs, rs, :] = x2 + (FFN_RES * mod(8)) * _rms(f, gain(5))


def _out_ffn2(x1, o, mod, gains, gg, wout, wg, wu, wd, gt, rt):
    n_groups, rows, _ = x1.shape
    n_g, n_r = n_groups // gt, rows // rt
    tm = gt * rt
    vmem = (2 * 3 * D_MODEL * D_FF + 2 * wout.size
            + 2 * 2 * 4 * tm * D_MODEL + 2 * 2 * tm * 2 * WIDTH
            + tm * (2 * 4 + 2) * FF_SLAB_MAX + 5 * 4 * tm * D_MODEL)
    return pl.pallas_call(
        _out_ffn2_kernel,
        grid=(n_g, n_r),
        in_specs=[
            pl.BlockSpec((gt, rt, D_MODEL), lambda gi, ri: (gi, ri, 0)),
            pl.BlockSpec((tm, 2 * WIDTH), lambda gi, ri: (gi * n_r + ri, 0)),
            pl.BlockSpec((N_MOD, gt, 1, D_MODEL), lambda gi, ri: (0, gi, 0, 0)),
            _resident(gains.shape), _resident(gg.shape), _resident(wout.shape),
            _resident(wg.shape), _resident(wu.shape), _resident(wd.shape),
        ],
        out_specs=pl.BlockSpec((gt, rt, D_MODEL), lambda gi, ri: (gi, ri, 0)),
        out_shape=jax.ShapeDtypeStruct(x1.shape, F32),
        compiler_params=pltpu.CompilerParams(
            dimension_semantics=("arbitrary", "arbitrary"), vmem_limit_bytes=_vmem_limit(vmem)),
        name="out_ffn2",
    )(x1, o, mod, gains, gg, wout, wg, wu, wd)


def _softmax_terms(s, sink=None):
    m = jnp.max(s, axis=1, keepdims=True)
    if sink is not None:
        m = jnp.maximum(m, sink)
    e = jnp.exp2(s - m)
    denom = jnp.sum(e, axis=1, keepdims=True)
    if sink is not None:
        denom = denom + jnp.exp2(sink - m)
    return e.astype(BF16), 1.0 / denom


def _scores(q, k):
    return lax.dot_general(q, k, (((1,), (1,)), ((), ())), preferred_element_type=F32)


def _unstack_pair(z):
    n_q = z.shape[0] // 2
    is_lo = lax.broadcasted_iota(jnp.int32, (n_q, z.shape[1]), 1) < HEAD_DIM
    return jnp.where(is_lo, z[:n_q], z[n_q:])


def _sink_column(sinks_ref, pair, n_q):
    row = lax.broadcasted_iota(jnp.int32, (2 * n_q, 1), 0)
    return jnp.where(row < n_q, sinks_ref[pair], sinks_ref[N_PAIRS + pair]) * LOG2E


GROUPS_PER_STEP = 2
JOB_LAG = 3


def _attn_prompt_kernel(sinks_ref, qa_ref, qb_ref, ka_ref, va_ref, kb_ref, vb_ref, *rest):
    ba_refs, bb_refs = rest[:GROUPS_PER_STEP], rest[GROUPS_PER_STEP:2 * GROUPS_PER_STEP]
    n_cast = (len(rest) - 2 * GROUPS_PER_STEP - 1) // 2
    o_ref = rest[2 * GROUPS_PER_STEP + n_cast]
    for src, dst in zip(rest[2 * GROUPS_PER_STEP:2 * GROUPS_PER_STEP + n_cast], rest[len(rest) - n_cast:]):
        dst[...] = src[...].astype(BF16)
    slab = lambda p: slice(p * V7X_LANES, (p + 1) * V7X_LANES)
    bands = []
    first = pl.program_id(1) * GROUPS_PER_STEP
    for j in range(GROUPS_PER_STEP):
        start = lambda g, reach: jnp.maximum(g * GROUP_Q - reach, 0)
        off_a = pl.multiple_of(start(first + j, REACH_A) - start(first, REACH_A), GROUP_Q)
        off_b = pl.multiple_of(start(first + j, REACH_B) - start(first, REACH_B), GROUP_Q)
        bands.append((pl.ds(off_a, BAND_A), pl.ds(off_b, BAND_B)))

    def stacked(q_ref, j, p):
        rows = slice(j * GROUP_Q, (j + 1) * GROUP_Q)
        return jnp.concatenate([q_ref[rows, slab(p)], q_ref[rows, slab(N_PAIRS + p)]], axis=0)

    def score(mixer, j, p):
        if mixer == "a":
            bias = jnp.concatenate([ba_refs[j][0, 2 * p], ba_refs[j][0, 2 * p + 1]], axis=0)
            return _scores(stacked(qa_ref, j, p), ka_ref[bands[j][0], slab(p)]) + bias
        q = jnp.concatenate([stacked(qb_ref, j, pp) for pp in range(N_PAIRS)], axis=0)
        bias = jnp.concatenate([bb_refs[j][0, h] for pp in range(N_PAIRS) for h in (pp, N_PAIRS + pp)], axis=0)
        return _scores(q, kb_ref[bands[j][1], :]) + bias

    def weights(s, sink):
        m = jnp.max(s, axis=1, keepdims=True)
        if sink is None:
            return jnp.exp2(s - m).astype(BF16), None
        m = jnp.maximum(m, sink)
        return jnp.exp2(s - m).astype(BF16), jnp.exp2(sink - m)

    def attend(e, extra, v):
        z = jnp.dot(e, jnp.concatenate([v, jnp.ones_like(v)], axis=1), preferred_element_type=F32)
        denom = z[:, V7X_LANES:] if extra is None else z[:, V7X_LANES:] + extra
        return z[:, :V7X_LANES] * (1.0 / denom)

    jobs = ([("a", j, p) for j in range(GROUPS_PER_STEP) for p in range(N_PAIRS)]
            + [("b", j, None) for j in range(GROUPS_PER_STEP)])
    s_all, w_all = {}, {}
    for t in range(len(jobs) + 2 * JOB_LAG):
        if t < len(jobs):
            s_all[t] = score(*jobs[t])
        if 0 <= t - JOB_LAG < len(jobs):
            mixer, j, p = jobs[t - JOB_LAG]
            sink = (jnp.concatenate([_sink_column(sinks_ref, pp, GROUP_Q) for pp in range(N_PAIRS)], axis=0)
                    if mixer == "b" else None)
            w_all[t - JOB_LAG] = weights(s_all.pop(t - JOB_LAG), sink)
        if 0 <= t - 2 * JOB_LAG < len(jobs):
            mixer, j, p = jobs[t - 2 * JOB_LAG]
            e, extra = w_all.pop(t - 2 * JOB_LAG)
            rows = slice(j * GROUP_Q, (j + 1) * GROUP_Q)
            if mixer == "a":
                o_ref[rows, slab(p)] = _unstack_pair(attend(e, extra, va_ref[bands[j][0], slab(p)])).astype(BF16)
            else:
                z = attend(e, extra, vb_ref[bands[j][1], :])
                for pp in range(N_PAIRS):
                    o_ref[rows, slab(N_PAIRS + pp)] = _unstack_pair(
                        z[pp * 2 * GROUP_Q:(pp + 1) * 2 * GROUP_Q]).astype(BF16)


def _attn_prompt(sinks, qa, qb, ka, va, kb, vb, bias_a, bias_b, batch, seq, cast=()):
    rows = GROUPS_PER_STEP * GROUP_Q
    n_steps = seq // rows
    cast_specs, cast_shapes = (zip(*[_cast_chunks(w, batch * n_steps, lambda b, s: b * n_steps + s) for w in cast])
                               if cast else ((), ()))
    q_spec = pl.BlockSpec((rows, 2 * WIDTH), lambda b, s: (b * n_steps + s, 0))
    kv_spec = lambda w, reach: pl.BlockSpec(
        (pl.Element(reach + rows), pl.Element(w)),
        lambda b, s: (pl.multiple_of(b * seq + jnp.maximum(s * rows - reach, 0), GROUP_Q), 0))
    bias_spec = lambda band, n_var, j: pl.BlockSpec(
        (1, N_HEADS, GROUP_Q, band),
        lambda b, s: (jnp.minimum(s * GROUPS_PER_STEP + j, n_var - 1), 0, 0, 0))
    vmem = (2 * 2 * 2 * ((REACH_A + rows) * WIDTH + (REACH_B + rows) * KV_WIDTH_B)
            + GROUPS_PER_STEP * 2 * 4 * N_HEADS * GROUP_Q * (BAND_A + BAND_B)
            + 2 * 2 * rows * 6 * WIDTH + GROUPS_PER_STEP * 6 * 4 * 2 * GROUP_Q * BAND_A
            + sum(2 * 6 * math.prod(spec.block_shape) for spec in cast_specs))
    out = pl.pallas_call(
        _attn_prompt_kernel,
        grid=(batch, n_steps),
        in_specs=[
            pl.BlockSpec(memory_space=pltpu.SMEM),
            q_spec, q_spec,
            kv_spec(WIDTH, REACH_A), kv_spec(WIDTH, REACH_A), kv_spec(KV_WIDTH_B, REACH_B), kv_spec(KV_WIDTH_B, REACH_B),
            *[bias_spec(BAND_A, N_VAR_A, j) for j in range(GROUPS_PER_STEP)],
            *[bias_spec(BAND_B, N_VAR_B, j) for j in range(GROUPS_PER_STEP)],
            *cast_specs,
        ],
        out_specs=[pl.BlockSpec((rows, 2 * WIDTH), lambda b, s: (b * n_steps + s, 0)), *cast_specs],
        out_shape=[jax.ShapeDtypeStruct((batch * seq, 2 * WIDTH), BF16), *cast_shapes],
        compiler_params=pltpu.CompilerParams(
            dimension_semantics=("arbitrary", "arbitrary"), vmem_limit_bytes=_vmem_limit(vmem)),
        name="attn_prompt",
    )(sinks, qa, qb, ka, va, kb, vb, *([bias_a] * GROUPS_PER_STEP), *([bias_b] * GROUPS_PER_STEP), *cast)
    return out[0], tuple(out[1:])


def _pair_ordered_cols(h):
    return (h % N_PAIRS) * V7X_LANES + (h // N_PAIRS) * HEAD_DIM


SAMPLE_ROWS_PER_STEP = 4


def _attn_sample_kernel(sinks_ref, qa_ref, qb_ref, ka_ref, va_ref, kb_ref, vb_ref,
                        cak_ref, cav_ref, cbk_ref, cbv_ref, ba_ref, bb_ref, o_ref):
    n_rows = cak_ref.shape[0]
    n_q = qa_ref.shape[0] // n_rows
    n_kv = cbk_ref.shape[1]
    per_kv = N_HEADS // n_kv
    la, lb = cak_ref.shape[3], cbk_ref.shape[3]
    head = lambda ref, r, idx: ref[r * n_q:(r + 1) * n_q, idx * HEAD_DIM:(idx + 1) * HEAD_DIM]

    def query(ref, r, lo):
        lo = lo if lo % V7X_LANES < HEAD_DIM else WIDTH + lo
        return ref[r * n_q:(r + 1) * n_q, lo:lo + HEAD_DIM]

    def scores(q, cache_t_ref, new_ref, r, idx):
        old = jnp.dot(q, cache_t_ref[r, idx].astype(BF16), preferred_element_type=F32)
        return jnp.concatenate([old, _scores(q, head(new_ref, r, idx))], axis=1)

    def values(e, cache_t_ref, new_ref, r, idx, n_old):
        old = _scores(e[:, :n_old], cache_t_ref[r, idx].astype(BF16))
        return old + jnp.dot(e[:, n_old:], head(new_ref, r, idx), preferred_element_type=F32)

    jobs_a = [(r, h) for r in range(n_rows) for h in range(N_HEADS)]
    jobs_b = [(r, c) for r in range(n_rows) for c in range(n_kv)]
    s_a = [scores(query(qa_ref, r, h * HEAD_DIM), cak_ref, ka_ref, r, h) + ba_ref[h] for r, h in jobs_a]
    s_b = []
    for r, c in jobs_b:
        heads = range(c * per_kv, (c + 1) * per_kv)
        q = jnp.concatenate([query(qb_ref, r, _pair_ordered_cols(h)) for h in heads], axis=0)
        bias = jnp.concatenate([bb_ref[h] for h in heads], axis=0)
        s_b.append(scores(q, cbk_ref, kb_ref, r, c) + bias)

    row = lax.broadcasted_iota(jnp.int32, (per_kv * n_q, 1), 0)
    sinks = []
    for c in range(n_kv):
        sink = jnp.zeros((per_kv * n_q, 1), F32)
        for i in range(per_kv):
            sink = jnp.where((row >= i * n_q) & (row < (i + 1) * n_q), sinks_ref[c * per_kv + i], sink)
        sinks.append(sink * LOG2E)
    p_a = [_softmax_terms(s) for s in s_a]
    p_b = [_softmax_terms(s, sinks[c]) for s, (_, c) in zip(s_b, jobs_b)]

    out_a = {job: values(e, cav_ref, va_ref, *job, la) * inv for job, (e, inv) in zip(jobs_a, p_a)}
    out_b = {}
    for (r, c), (e, inv) in zip(jobs_b, p_b):
        o = values(e, cbv_ref, vb_ref, r, c, lb) * inv
        for i in range(per_kv):
            out_b[(r, c * per_kv + i)] = o[i * n_q:(i + 1) * n_q]
    order_b = [h for p in range(N_PAIRS) for h in (p, N_PAIRS + p)]
    for r in range(n_rows):
        pieces = [out_a[(r, h)] for h in range(N_HEADS)] + [out_b[(r, h)] for h in order_b]
        o_ref[r * n_q:(r + 1) * n_q, :] = jnp.concatenate(pieces, axis=1).astype(BF16)


def _attn_sample(sinks, qa, qb, ka, va, kb, vb, cak, cav, cbk, cbv, bias_a, bias_b, batch, s_len):
    n_rows = SAMPLE_ROWS_PER_STEP
    row = lambda w: pl.BlockSpec((n_rows * s_len, w), lambda b: (b, 0))
    cache = lambda a: pl.BlockSpec((n_rows,) + a.shape[1:], lambda b: (b, 0, 0, 0))
    whole = lambda a: pl.BlockSpec(a.shape, lambda b: (0,) * a.ndim)
    return pl.pallas_call(
        _attn_sample_kernel,
        grid=(batch // n_rows,),
        in_specs=[
            pl.BlockSpec(memory_space=pltpu.SMEM),
            row(2 * WIDTH), row(2 * WIDTH), row(WIDTH), row(WIDTH), row(KV_WIDTH_B), row(KV_WIDTH_B),
            cache(cak), cache(cav), cache(cbk), cache(cbv),
            whole(bias_a), whole(bias_b),
        ],
        out_specs=pl.BlockSpec((n_rows * s_len, 2 * WIDTH), lambda b: (b, 0)),
        out_shape=jax.ShapeDtypeStruct((batch * s_len, 2 * WIDTH), BF16),
        compiler_params=pltpu.CompilerParams(dimension_semantics=("arbitrary",)),
        name="attn_sample",
    )(sinks, qa, qb, ka, va, kb, vb, cak, cav, cbk, cbv, bias_a, bias_b)


def _pair_order(x, axis):
    shape = x.shape
    split = shape[:axis] + (N_HEADS // N_PAIRS, N_PAIRS, HEAD_DIM) + shape[axis + 1:]
    return jnp.swapaxes(x.reshape(split), axis, axis + 1).reshape(shape)


def _prep_out_weights(w_out, group_gains):
    wout = jnp.concatenate([w_out[:WIDTH], _pair_order(w_out[WIDTH:], 0)], axis=0).astype(BF16)
    gg = jnp.concatenate([group_gains[:WIDTH], _pair_order(group_gains[WIDTH:], 0)])
    return wout, gg.reshape(1, 2 * WIDTH)


def kernel(x_prompt, x_sample, cache_a_k, cache_a_v, cache_b_k, cache_b_v, c_prompt, c_sample, w_mod, b_mod,
           norm_gains, w1_gate, w1_up, w1_down, w_in, w_out, group_gains, rel_bias_a, t5_bias_table, sinks_b,
           w2_gate, w2_up, w2_down):
    depth = w_mod.shape[0]
    assert depth == 1
    batch, seq, _ = x_prompt.shape
    s_batch, s_len, _ = x_sample.shape
    la, lb = cache_a_k.shape[2], cache_b_k.shape[2]
    assert la == REACH_A and lb == REACH_B and seq % (2 * FFN_TILE_TOKENS) == 0

    c_rows = jnp.concatenate([c_prompt, c_sample], axis=0)
    pad_rows = (-c_rows.shape[0]) % 8
    c_rows = jnp.pad(c_rows, ((0, pad_rows), (0, 0)))
    mod = _modulation(c_rows, w_mod[0], b_mod[0])
    mod_p = mod[:, :batch].reshape(N_MOD, batch, 1, D_MODEL)
    mod_s = mod[:, batch:batch + s_batch].reshape(N_MOD, s_batch, 1, D_MODEL)

    bias_a, bias_b, bias_sa, bias_sb, *w1, win = _bias_tables(
        rel_bias_a[0], t5_bias_table, s_len, la, lb, cast=(w1_gate[0], w1_up[0], w1_down[0], w_in[0]))

    gains = norm_gains[0]
    wout, gg = _prep_out_weights(w_out[0], group_gains[0])
    sinks = sinks_b[0]

    (x1p, qa, ka, va, qb, kb, vb, ka32, va32, kb32, vb32) = _ffn1_qkv(
        x_prompt, mod_p, gains, *w1, win, 1, FFN_TILE_TOKENS)
    op, w2 = _attn_prompt(sinks, qa, qb, ka, va, kb, vb, bias_a, bias_b, batch, seq,
                          cast=(w2_gate[0], w2_up[0], w2_down[0]))
    y_prompt = _out_ffn2(x1p, op, mod_p, gains, gg, wout, *w2, 1, 2 * FFN_TILE_TOKENS)
    rows_last = lambda t: jnp.transpose(t.reshape(depth, batch, -1, HEAD_DIM, t.shape[-1]), (0, 1, 4, 2, 3))
    new_p = tuple(rows_last(t) for t in (ka32, va32, kb32, vb32))

    gt_s = FFN_TILE_TOKENS // s_len
    (x1s, qas, kas, vas, qbs, kbs, vbs, kas32, vas32, kbs32, vbs32) = _ffn1_qkv(
        x_sample, mod_s, gains, *w1, win, gt_s, s_len)
    os_ = _attn_sample(sinks, qas, qbs, kas, vas, kbs, vbs,
                       *(jnp.transpose(c[0], (0, 2, 3, 1)) for c in (cache_a_k, cache_a_v, cache_b_k, cache_b_v)),
                       bias_sa, bias_sb, s_batch, s_len)
    y_sample = _out_ffn2(x1s, os_, mod_s, gains, gg, wout, *w2, gt_s, s_len)
    new_s = (kas32.reshape(depth, s_batch, s_len, N_HEADS, HEAD_DIM),
             vas32.reshape(depth, s_batch, s_len, N_HEADS, HEAD_DIM),
             kbs32.reshape(depth, s_batch, s_len, 2, HEAD_DIM), vbs32.reshape(depth, s_batch, s_len, 2, HEAD_DIM))

    return (y_prompt, y_sample) + new_p + new_s
```

```python
import functools
import math

import jax
import jax.numpy as jnp
from jax import lax
from jax.experimental import pallas as pl
from jax.experimental.pallas import tpu as pltpu

F32 = jnp.float32
BF16 = jnp.bfloat16

D_MODEL = 1024
D_FF = 2816
CHUNK = 64
CHUNK_SHIFT = CHUNK.bit_length() - 1
HEAD_DIM = 64
N_HEADS = 8
N_PAIRS = N_HEADS // 2
WIDTH = N_HEADS * HEAD_DIM
KV_WIDTH_B = 2 * HEAD_DIM
REACH_A = 8 * CHUNK
REACH_B = 2 * CHUNK
REL_CLIP_A = 128
N_BUCKETS = 32
N_MOD = 9
FFN_RES = 0.5
EPS = 1e-6
SCALE = HEAD_DIM ** -0.5
LOG2E = 1.4426950408889634
NEG_INF = -1e30

V7X_LANES = 128
V7X_VMEM_LIMIT_CAP = 56 * 1024 * 1024

GROUP_Q = 2 * CHUNK
BAND_A = REACH_A + GROUP_Q
BAND_B = REACH_B + GROUP_Q
ORIGIN_A = 1024
WIDTH_MASTER_A = ORIGIN_A + BAND_A
ORIGIN_B = 256
WIDTH_MASTER_B = ORIGIN_B + BAND_B
N_VAR_A = REACH_A // GROUP_Q + 1
N_VAR_B = REACH_B // GROUP_Q + 1


def _vmem_limit(estimate_bytes):
    return int(min(max(estimate_bytes, 32 * 1024 * 1024), V7X_VMEM_LIMIT_CAP))


def _rms(x, gain):
    ms = jnp.mean(x * x, axis=-1, keepdims=True)
    return x * lax.rsqrt(ms + EPS) * gain


def _adaln(x, gain, shift, scale):
    return _rms(x, gain) * (1.0 + scale) + shift


def _silu(x):
    return x * (1.0 / (1.0 + jnp.exp(-x)))


def _mod_kernel(c_ref, w_ref, b_ref, o_ref):
    s = _silu(c_ref[...]).astype(BF16)
    o_ref[0] = jnp.dot(s, w_ref[...].astype(BF16), preferred_element_type=F32) + b_ref[0]


def _modulation(c_rows, w_mod, b_mod):
    rows = c_rows.shape[0]
    return pl.pallas_call(
        _mod_kernel,
        grid=(N_MOD,),
        in_specs=[
            pl.BlockSpec((rows, D_MODEL), lambda j: (0, 0)),
            pl.BlockSpec((D_MODEL, D_MODEL), lambda j: (0, j)),
            pl.BlockSpec((1, 1, D_MODEL), lambda j: (j, 0, 0)),
        ],
        out_specs=pl.BlockSpec((1, rows, D_MODEL), lambda j: (j, 0, 0)),
        out_shape=jax.ShapeDtypeStruct((N_MOD, rows, D_MODEL), F32),
        name="modulation",
    )(c_rows, w_mod, b_mod.reshape(N_MOD, 1, D_MODEL))


def _skew(base_row, n_rows):
    width = base_row.shape[1]
    m = jnp.broadcast_to(base_row, (n_rows, width))
    row = lax.broadcasted_iota(jnp.int32, (n_rows, width), 0)
    shift = 1
    while shift < n_rows:
        m = jnp.where((row & shift) != 0, pltpu.roll(m, shift, axis=1), m)
        shift *= 2
    return m


def _band_mask(n_rows, width, origin, n_prev):
    row = lax.broadcasted_iota(jnp.int32, (n_rows, width), 0)
    col = lax.broadcasted_iota(jnp.int32, (n_rows, width), 1)
    q_chunk = row >> CHUNK_SHIFT
    k_chunk = (col >> CHUNK_SHIFT) - (origin // CHUNK)
    return (k_chunk >= q_chunk - n_prev) & (k_chunk <= q_chunk)


def _t5_bucket(rel):
    half = N_BUCKETS // 2
    max_exact = half // 2
    n = jnp.abs(rel)
    n2 = n * n
    large = jnp.full(rel.shape, max_exact, jnp.int32)
    for k in range(1, half - max_exact):
        large = large + jnp.where(n2 >= (max_exact * max_exact) * (2 ** k), 1, 0)
    return jnp.where(rel > 0, half, 0) + jnp.where(n < max_exact, n, large)


def _bias_kernel(rel_ref, t5_ref, *refs, s_len, la, lb):
    n_cast = (len(refs) - 4) // 2
    for src, dst in zip(refs[:n_cast], refs[n_cast + 4:]):
        dst[...] = src[...].astype(BF16)
    ba_ref, bb_ref, bsa_ref, bsb_ref = refs[n_cast:n_cast + 4]

    tab = rel_ref[0] * LOG2E
    n_tab = 2 * REL_CLIP_A + 1
    first, last = tab[:, 0:1], tab[:, n_tab - 1:n_tab]
    win_lo = ORIGIN_A - 2 * REL_CLIP_A
    col = lax.broadcasted_iota(jnp.int32, (1, tab.shape[1] + V7X_LANES), 1)
    shifted = jnp.concatenate([jnp.zeros((1, V7X_LANES), F32), tab], axis=1)
    u_win = jnp.where(col < V7X_LANES, first, jnp.where(col >= V7X_LANES + n_tab, last, shifted))
    window = _skew(u_win, GROUP_Q)[:, V7X_LANES:]
    n_left = win_lo + V7X_LANES
    n_right = WIDTH_MASTER_A - n_left - window.shape[1]
    skew_a = jnp.concatenate([jnp.broadcast_to(first, (GROUP_Q, n_left)), window,
                              jnp.broadcast_to(last, (GROUP_Q, n_right))], axis=1)
    master_a = jnp.where(_band_mask(GROUP_Q, WIDTH_MASTER_A, ORIGIN_A, REACH_A // CHUNK), skew_a, NEG_INF)
    for v in range(N_VAR_A):
        start = ORIGIN_A - min(v * GROUP_Q, REACH_A) if v < N_VAR_A - 1 else ORIGIN_A - REACH_A
        ba_ref[v, 0] = master_a[:, start:start + BAND_A]
    bsa_ref[0] = skew_a[:s_len, ORIGIN_A - la:ORIGIN_A + s_len]

    t5 = t5_ref[0] * LOG2E
    rel = lax.broadcasted_iota(jnp.int32, (1, WIDTH_MASTER_B), 1) - ORIGIN_B
    bucket = _t5_bucket(rel)
    u_b = jnp.zeros((1, WIDTH_MASTER_B), F32)
    for i in range(N_BUCKETS):
        u_b = jnp.where(bucket == i, t5[:, i:i + 1], u_b)
    skew_b = _skew(u_b, GROUP_Q)
    master_b = jnp.where(_band_mask(GROUP_Q, WIDTH_MASTER_B, ORIGIN_B, REACH_B // CHUNK), skew_b, NEG_INF)
    for v in range(N_VAR_B):
        start = ORIGIN_B - min(v * GROUP_Q, REACH_B) if v < N_VAR_B - 1 else ORIGIN_B - REACH_B
        bb_ref[v, 0] = master_b[:, start:start + BAND_B]
    bsb_ref[0] = skew_b[:s_len, ORIGIN_B - lb:ORIGIN_B + s_len]


V7X_BF16_SUBLANES = 16


def _cast_chunks(w, n_steps, step_of):
    rows, cols = w.shape
    n_chunks = n_steps
    while rows % (n_chunks * V7X_BF16_SUBLANES):
        n_chunks //= 2
    spec = pl.BlockSpec((rows // n_chunks, cols), lambda *g: (jnp.minimum(step_of(*g), n_chunks - 1), 0))
    return spec, jax.ShapeDtypeStruct(w.shape, BF16)


def _bias_tables(rel_bias_a, t5_table, s_len, la, lb, cast=()):
    n_tab = 2 * REL_CLIP_A + 1
    tab_w = 3 * V7X_LANES
    rel_p = jnp.pad(rel_bias_a, ((0, 0), (0, tab_w - n_tab))).reshape(N_HEADS, 1, tab_w)
    t5_p = t5_table.reshape(N_HEADS, 1, N_BUCKETS)
    kern = functools.partial(_bias_kernel, s_len=s_len, la=la, lb=lb)
    cast_specs, cast_shapes = zip(*[_cast_chunks(w, N_HEADS, lambda h: h) for w in cast]) if cast else ((), ())
    return pl.pallas_call(
        kern,
        grid=(N_HEADS,),
        in_specs=[
            pl.BlockSpec((1, 1, tab_w), lambda h: (h, 0, 0)),
            pl.BlockSpec((1, 1, N_BUCKETS), lambda h: (h, 0, 0)),
            *cast_specs,
        ],
        out_specs=[
            pl.BlockSpec((N_VAR_A, 1, GROUP_Q, BAND_A), lambda h: (0, h, 0, 0)),
            pl.BlockSpec((N_VAR_B, 1, GROUP_Q, BAND_B), lambda h: (0, h, 0, 0)),
            pl.BlockSpec((1, s_len, la + s_len), lambda h: (h, 0, 0)),
            pl.BlockSpec((1, s_len, lb + s_len), lambda h: (h, 0, 0)),
            *cast_specs,
        ],
        out_shape=[
            jax.ShapeDtypeStruct((N_VAR_A, N_HEADS, GROUP_Q, BAND_A), F32),
            jax.ShapeDtypeStruct((N_VAR_B, N_HEADS, GROUP_Q, BAND_B), F32),
            jax.ShapeDtypeStruct((N_HEADS, s_len, la + s_len), F32),
            jax.ShapeDtypeStruct((N_HEADS, s_len, lb + s_len), F32),
            *cast_shapes,
        ],
        compiler_params=pltpu.CompilerParams(dimension_semantics=("arbitrary",)),
        name="bias_tables",
    )(rel_p, t5_p, *cast)


V7X_MXU_DIM = 256
FF_SLABS = ((0, 6 * V7X_MXU_DIM), (6 * V7X_MXU_DIM, D_FF))
FF_SLAB_MAX = max(hi - lo for lo, hi in FF_SLABS)


def _swiglu(h, wg_ref, wu_ref, wd_ref):
    acts = []
    for lo, hi in FF_SLABS:
        g = jnp.dot(h, wg_ref[:, lo:hi], preferred_element_type=F32)
        u = jnp.dot(h, wu_ref[:, lo:hi], preferred_element_type=F32)
        acts.append((_silu(g) * u).astype(BF16))
    acc = None
    for (lo, hi), a in zip(FF_SLABS, acts):
        part = jnp.dot(a, wd_ref[lo:hi, :], preferred_element_type=F32)
        acc = part if acc is None else acc + part
    return acc


SUB_TOKENS = V7X_MXU_DIM
FFN_TILE_TOKENS = 2 * SUB_TOKENS


def _sub_tiles(gt, rt):
    n_sub = gt * rt // SUB_TOKENS
    if gt == 1:
        step = rt // n_sub
        return [(slice(0, 1), slice(s * step, (s + 1) * step), s * step, step) for s in range(n_sub)]
    step = gt // n_sub
    return [(slice(s * step, (s + 1) * step), slice(0, rt), s * step * rt, step * rt) for s in range(n_sub)]


def _ffn1_qkv_kernel(x_ref, mod_ref, gains_ref, wg_ref, wu_ref, wd_ref, win_ref,
                     x1_ref, qa_ref, ka_ref, va_ref, qb_ref, kb_ref, vb_ref,
                     ka32_ref, va32_ref, kb32_ref, vb32_ref, *tail_scratch):
    gt, rt, _ = x_ref.shape
    tm = gt * rt
    gain = lambda k: gains_ref[k:k + 1, :]

    def store_tail(ref, val, r0):
        first = tm - ref.shape[0]
        lo = max(r0, first)
        if lo < r0 + val.shape[0]:
            ref[lo - first:r0 + val.shape[0] - first, :] = val[lo - r0:, :]

    def store_heads(ref, val, r0):
        n_heads = val.shape[1] // HEAD_DIM
        for h in range(n_heads):
            ref[pl.ds(r0 * n_heads + h, val.shape[0], stride=n_heads), :] = val[:, h * HEAD_DIM:(h + 1) * HEAD_DIM]

    def store_split(ref, rows, q, pairs):
        q = q * (SCALE * LOG2E)
        head = lambda h: q[:, h * HEAD_DIM:(h + 1) * HEAD_DIM]
        zeros = jnp.zeros((q.shape[0], HEAD_DIM), F32)
        ref[rows, :WIDTH] = jnp.concatenate([t for a, _ in pairs for t in (head(a), zeros)], axis=1).astype(BF16)
        ref[rows, WIDTH:] = jnp.concatenate([t for _, b in pairs for t in (zeros, head(b))], axis=1).astype(BF16)

    pairs_a = [(2 * p, 2 * p + 1) for p in range(N_PAIRS)]
    pairs_b = [(p, N_PAIRS + p) for p in range(N_PAIRS)]

    subs = _sub_tiles(gt, rt)
    mods = [lambda k, gs=gs: mod_ref[k, gs] for gs, _, _, _ in subs]
    xs = [x_ref[gs, rs, :] for gs, rs, _, _ in subs]
    hs = [_adaln(x, gain(0), mod(0), mod(1)).reshape(n, D_MODEL).astype(BF16)
          for x, mod, (_, _, _, n) in zip(xs, mods, subs)]
    fs = [_swiglu(h, wg_ref, wu_ref, wd_ref).reshape(x.shape) for h, x in zip(hs, xs)]
    h2s = []
    for x, f, mod, (gs, rs, _, n) in zip(xs, fs, mods, subs):
        x1 = x + (FFN_RES * mod(2)) * _rms(f, gain(1))
        x1_ref[gs, rs, :] = x1
        h2s.append(_adaln(x1, gain(2), mod(3), mod(4)).reshape(n, D_MODEL).astype(BF16))

    for h2, (_, _, r0, n) in zip(h2s, subs):
        rows = slice(r0, r0 + n)

        def proj(lo, width):
            return jnp.dot(h2, win_ref[:, lo:lo + width], preferred_element_type=F32)

        store_split(qa_ref, rows, proj(0, WIDTH), pairs_a)
        ka = proj(WIDTH, WIDTH)
        va = proj(2 * WIDTH, WIDTH)
        store_split(qb_ref, rows, proj(3 * WIDTH, WIDTH), pairs_b)
        kvb = proj(4 * WIDTH, 2 * KV_WIDTH_B)
        kb, vb = kvb[:, :KV_WIDTH_B], kvb[:, KV_WIDTH_B:]
        ka_ref[rows, :] = ka.astype(BF16)
        va_ref[rows, :] = va.astype(BF16)
        kb_ref[rows, :] = kb.astype(BF16)
        vb_ref[rows, :] = vb.astype(BF16)
        if tail_scratch:
            for ref, val in zip(tail_scratch, (ka, va, kb, vb)):
                store_tail(ref, val, r0)
        else:
            store_heads(ka32_ref, ka, r0)
            store_heads(va32_ref, va, r0)
            store_tail(kb32_ref, kb, r0)
            store_tail(vb32_ref, vb, r0)

    if tail_scratch:
        @pl.when(pl.program_id(1) == pl.num_programs(1) - 1)
        def _():
            for out_ref, ref in zip((ka32_ref, va32_ref, kb32_ref, vb32_ref), tail_scratch):
                out_ref[...] = ref[...].T


def _resident(shape):
    return pl.BlockSpec(shape, lambda *_: (0,) * len(shape), pipeline_mode=pl.Buffered(1))


def _ffn1_qkv(x, mod, gains, wg, wu, wd, win, gt, rt):
    n_groups, rows, _ = x.shape
    n_g, n_r = n_groups // gt, rows // rt
    tm = gt * rt
    tokens = n_groups * rows
    ta = gt * min(REACH_A, rows)
    tb = gt * min(REACH_B, rows)
    tile = lambda gi, ri: (gi * n_r + ri, 0)
    tail = lambda gi, ri: (gi, 0)
    tok = lambda w, dt: jax.ShapeDtypeStruct((tokens, w), dt)
    if gt == 1:
        assert ta <= tm and tb <= tm
        tail_specs = [pl.BlockSpec((None, w, t), lambda gi, ri: (gi, 0, 0))
                      for w, t in ((WIDTH, ta), (WIDTH, ta), (KV_WIDTH_B, tb), (KV_WIDTH_B, tb))]
        tail_shapes = [jax.ShapeDtypeStruct((n_g, w, t), F32)
                       for w, t in ((WIDTH, ta), (WIDTH, ta), (KV_WIDTH_B, tb), (KV_WIDTH_B, tb))]
        scratch = [pltpu.VMEM((ta, WIDTH), F32), pltpu.VMEM((ta, WIDTH), F32),
                   pltpu.VMEM((tb, KV_WIDTH_B), F32), pltpu.VMEM((tb, KV_WIDTH_B), F32)]
    else:
        assert n_r == 1
        tail_specs = [pl.BlockSpec((tm * N_HEADS, HEAD_DIM), tail), pl.BlockSpec((tm * N_HEADS, HEAD_DIM), tail),
                      pl.BlockSpec((tm, KV_WIDTH_B), tail), pl.BlockSpec((tm, KV_WIDTH_B), tail)]
        tail_shapes = [jax.ShapeDtypeStruct((tokens * N_HEADS, HEAD_DIM), F32),
                       jax.ShapeDtypeStruct((tokens * N_HEADS, HEAD_DIM), F32),
                       jax.ShapeDtypeStruct((tokens, KV_WIDTH_B), F32), jax.ShapeDtypeStruct((tokens, KV_WIDTH_B), F32)]
        scratch = []
    vmem = (2 * 3 * D_MODEL * D_FF + 2 * D_MODEL * win.shape[1]
            + 2 * 2 * 4 * tm * D_MODEL
            + 2 * 2 * tm * (6 * WIDTH + 2 * KV_WIDTH_B)
            + 2 * 4 * (2 * ta * WIDTH + 2 * tb * KV_WIDTH_B)
            + tm * (2 * 4 + 2) * FF_SLAB_MAX + 3 * 4 * tm * D_MODEL
            + 4 * tm * win.shape[1])
    return pl.pallas_call(
        _ffn1_qkv_kernel,
        grid=(n_g, n_r),
        in_specs=[
            pl.BlockSpec((gt, rt, D_MODEL), lambda gi, ri: (gi, ri, 0)),
            pl.BlockSpec((N_MOD, gt, 1, D_MODEL), lambda gi, ri: (0, gi, 0, 0)),
            _resident(gains.shape),
            _resident(wg.shape), _resident(wu.shape), _resident(wd.shape), _resident(win.shape),
        ],
        out_specs=[
            pl.BlockSpec((gt, rt, D_MODEL), lambda gi, ri: (gi, ri, 0)),
            pl.BlockSpec((tm, 2 * WIDTH), tile), pl.BlockSpec((tm, WIDTH), tile), pl.BlockSpec((tm, WIDTH), tile),
            pl.BlockSpec((tm, 2 * WIDTH), tile), pl.BlockSpec((tm, KV_WIDTH_B), tile),
            pl.BlockSpec((tm, KV_WIDTH_B), tile),
            *tail_specs,
        ],
        out_shape=[
            jax.ShapeDtypeStruct(x.shape, F32),
            tok(2 * WIDTH, BF16), tok(WIDTH, BF16), tok(WIDTH, BF16),
            tok(2 * WIDTH, BF16), tok(KV_WIDTH_B, BF16), tok(KV_WIDTH_B, BF16),
            *tail_shapes,
        ],
        scratch_shapes=scratch,
        compiler_params=pltpu.CompilerParams(
            dimension_semantics=("arbitrary", "arbitrary"), vmem_limit_bytes=_vmem_limit(vmem)),
        name="ffn1_qkv",
    )(x, mod, gains, wg, wu, wd, win)


def _out_ffn2_kernel(x_ref, o_ref, mod_ref, gains_ref, gg_ref, wout_ref, wg_ref, wu_ref, wd_ref, y_ref):
    gt, rt, _ = x_ref.shape
    gain = lambda k: gains_ref[k:k + 1, :]
    gg = gg_ref[...]
    subs = _sub_tiles(gt, rt)
    mods = [lambda k, gs=gs: mod_ref[k, gs] for gs, _, _, _ in subs]
    xs = [x_ref[gs, rs, :] for gs, rs, _, _ in subs]
    ons = []
    for _, _, r0, n in subs:
        o = o_ref[r0:r0 + n, :].astype(F32)
        on = jnp.concatenate([_rms(o[:, :WIDTH], gg[:, :WIDTH]), _rms(o[:, WIDTH:], gg[:, WIDTH:])], axis=1)
        ons.append(on.astype(BF16))
    mixed = [jnp.dot(on, wout_ref[...], preferred_element_type=F32).reshape(x.shape) for on, x in zip(ons, xs)]
    x2s = [x + mod(5) * _rms(m, gain(3)) for x, m, mod in zip(xs, mixed, mods)]
    hs = [_adaln(x2, gain(4), mod(6), mod(7)).reshape(n, D_MODEL).astype(BF16)
          for x2, mod, (_, _, _, n) in zip(x2s, mods, subs)]
    fs = [_swiglu(h, wg_ref, wu_ref, wd_ref).reshape(x.shape) for h, x in zip(hs, xs)]
    for x2, f, mod, (gs, rs, _, _) in zip(x2s, fs, mods, subs):
        y_ref[gs, rs, :] = x2 + (FFN_RES * mod(8)) * _rms(f, gain(5))


def _out_ffn2(x1, o, mod, gains, gg, wout, wg, wu, wd, gt, rt):
    n_groups, rows, _ = x1.shape
    n_g, n_r = n_groups // gt, rows // rt
    tm = gt * rt
    vmem = (2 * 3 * D_MODEL * D_FF + 2 * wout.size
            + 2 * 2 * 4 * tm * D_MODEL + 2 * 2 * tm * 2 * WIDTH
            + tm * (2 * 4 + 2) * FF_SLAB_MAX + 5 * 4 * tm * D_MODEL)
    return pl.pallas_call(
        _out_ffn2_kernel,
        grid=(n_g, n_r),
        in_specs=[
            pl.BlockSpec((gt, rt, D_MODEL), lambda gi, ri: (gi, ri, 0)),
            pl.BlockSpec((tm, 2 * WIDTH), lambda gi, ri: (gi * n_r + ri, 0)),
            pl.BlockSpec((N_MOD, gt, 1, D_MODEL), lambda gi, ri: (0, gi, 0, 0)),
            _resident(gains.shape), _resident(gg.shape), _resident(wout.shape),
            _resident(wg.shape), _resident(wu.shape), _resident(wd.shape),
        ],
        out_specs=pl.BlockSpec((gt, rt, D_MODEL), lambda gi, ri: (gi, ri, 0)),
        out_shape=jax.ShapeDtypeStruct(x1.shape, F32),
        compiler_params=pltpu.CompilerParams(
            dimension_semantics=("arbitrary", "arbitrary"), vmem_limit_bytes=_vmem_limit(vmem)),
        name="out_ffn2",
    )(x1, o, mod, gains, gg, wout, wg, wu, wd)


def _softmax_terms(s, sink=None):
    m = jnp.max(s, axis=1, keepdims=True)
    if sink is not None:
        m = jnp.maximum(m, sink)
    e = jnp.exp2(s - m)
    denom = jnp.sum(e, axis=1, keepdims=True)
    if sink is not None:
        denom = denom + jnp.exp2(sink - m)
    return e.astype(BF16), 1.0 / denom


def _scores(q, k):
    return lax.dot_general(q, k, (((1,), (1,)), ((), ())), preferred_element_type=F32)


def _unstack_pair(z):
    n_q = z.shape[0] // 2
    is_lo = lax.broadcasted_iota(jnp.int32, (n_q, z.shape[1]), 1) < HEAD_DIM
    return jnp.where(is_lo, z[:n_q], z[n_q:])


def _sink_column(sinks_ref, pair, n_q):
    row = lax.broadcasted_iota(jnp.int32, (2 * n_q, 1), 0)
    return jnp.where(row < n_q, sinks_ref[pair], sinks_ref[N_PAIRS + pair]) * LOG2E


GROUPS_PER_STEP = 2
JOB_LAG = 4


def _attn_prompt_kernel(sinks_ref, qa_ref, qb_ref, ka_ref, va_ref, kb_ref, vb_ref, *rest):
    ba_refs, bb_refs = rest[:GROUPS_PER_STEP], rest[GROUPS_PER_STEP:2 * GROUPS_PER_STEP]
    n_cast = (len(rest) - 2 * GROUPS_PER_STEP - 1) // 2
    o_ref = rest[2 * GROUPS_PER_STEP + n_cast]
    for src, dst in zip(rest[2 * GROUPS_PER_STEP:2 * GROUPS_PER_STEP + n_cast], rest[len(rest) - n_cast:]):
        dst[...] = src[...].astype(BF16)
    slab = lambda p: slice(p * V7X_LANES, (p + 1) * V7X_LANES)
    bands = []
    first = pl.program_id(1) * GROUPS_PER_STEP
    for j in range(GROUPS_PER_STEP):
        start = lambda g, reach: jnp.maximum(g * GROUP_Q - reach, 0)
        off_a = pl.multiple_of(start(first + j, REACH_A) - start(first, REACH_A), GROUP_Q)
        off_b = pl.multiple_of(start(first + j, REACH_B) - start(first, REACH_B), GROUP_Q)
        bands.append((pl.ds(off_a, BAND_A), pl.ds(off_b, BAND_B)))

    def stacked(q_ref, j, p):
        rows = slice(j * GROUP_Q, (j + 1) * GROUP_Q)
        return jnp.concatenate([q_ref[rows, slab(p)], q_ref[rows, slab(N_PAIRS + p)]], axis=0)

    def score(mixer, j, p):
        if mixer == "a":
            bias = jnp.concatenate([ba_refs[j][0, 2 * p], ba_refs[j][0, 2 * p + 1]], axis=0)
            return _scores(stacked(qa_ref, j, p), ka_ref[bands[j][0], slab(p)]) + bias
        bias = jnp.concatenate([bb_refs[j][0, p], bb_refs[j][0, N_PAIRS + p]], axis=0)
        return _scores(stacked(qb_ref, j, p), kb_ref[bands[j][1], :]) + bias

    def weights(s, sink):
        half = s.shape[0] // 2
        es, extras = [], []
        for rows in (slice(0, half), slice(half, 2 * half)):
            m = jnp.max(s[rows], axis=1, keepdims=True)
            if sink is not None:
                m = jnp.maximum(m, sink[rows])
                extras.append(jnp.exp2(sink[rows] - m))
            es.append(jnp.exp2(s[rows] - m).astype(BF16))
        return jnp.concatenate(es, axis=0), (jnp.concatenate(extras, axis=0) if extras else None)

    def attend(e, extra, v):
        z = jnp.dot(e, jnp.concatenate([v, jnp.ones_like(v)], axis=1), preferred_element_type=F32)
        denom = z[:, V7X_LANES:] if extra is None else z[:, V7X_LANES:] + extra
        return _unstack_pair(z[:, :V7X_LANES] * (1.0 / denom))

    jobs = [(mixer, j, p) for mixer in ("a", "b") for j in range(GROUPS_PER_STEP) for p in range(N_PAIRS)]
    s_all, w_all = {}, {}
    for t in range(len(jobs) + 2 * JOB_LAG):
        if t < len(jobs):
            s_all[t] = score(*jobs[t])
        if 0 <= t - JOB_LAG < len(jobs):
            mixer, j, p = jobs[t - JOB_LAG]
            w_all[t - JOB_LAG] = weights(s_all.pop(t - JOB_LAG),
                                         _sink_column(sinks_ref, p, GROUP_Q) if mixer == "b" else None)
        if 0 <= t - 2 * JOB_LAG < len(jobs):
            mixer, j, p = jobs[t - 2 * JOB_LAG]
            e, extra = w_all.pop(t - 2 * JOB_LAG)
            rows = slice(j * GROUP_Q, (j + 1) * GROUP_Q)
            if mixer == "a":
                o_ref[rows, slab(p)] = attend(e, extra, va_ref[bands[j][0], slab(p)]).astype(BF16)
            else:
                o_ref[rows, slab(N_PAIRS + p)] = attend(e, extra, vb_ref[bands[j][1], :]).astype(BF16)


def _attn_prompt(sinks, qa, qb, ka, va, kb, vb, bias_a, bias_b, batch, seq, cast=()):
    rows = GROUPS_PER_STEP * GROUP_Q
    n_steps = seq // rows
    cast_specs, cast_shapes = (zip(*[_cast_chunks(w, batch * n_steps, lambda b, s: b * n_steps + s) for w in cast])
                               if cast else ((), ()))
    q_spec = pl.BlockSpec((rows, 2 * WIDTH), lambda b, s: (b * n_steps + s, 0))
    kv_spec = lambda w, reach: pl.BlockSpec(
        (pl.Element(reach + rows), pl.Element(w)),
        lambda b, s: (pl.multiple_of(b * seq + jnp.maximum(s * rows - reach, 0), GROUP_Q), 0))
    bias_spec = lambda band, n_var, j: pl.BlockSpec(
        (1, N_HEADS, GROUP_Q, band),
        lambda b, s: (jnp.minimum(s * GROUPS_PER_STEP + j, n_var - 1), 0, 0, 0))
    vmem = (2 * 2 * 2 * ((REACH_A + rows) * WIDTH + (REACH_B + rows) * KV_WIDTH_B)
            + GROUPS_PER_STEP * 2 * 4 * N_HEADS * GROUP_Q * (BAND_A + BAND_B)
            + 2 * 2 * rows * 6 * WIDTH + GROUPS_PER_STEP * 6 * 4 * 2 * GROUP_Q * BAND_A
            + sum(2 * 6 * math.prod(spec.block_shape) for spec in cast_specs))
    out = pl.pallas_call(
        _attn_prompt_kernel,
        grid=(batch, n_steps),
        in_specs=[
            pl.BlockSpec(memory_space=pltpu.SMEM),
            q_spec, q_spec,
            kv_spec(WIDTH, REACH_A), kv_spec(WIDTH, REACH_A), kv_spec(KV_WIDTH_B, REACH_B), kv_spec(KV_WIDTH_B, REACH_B),
            *[bias_spec(BAND_A, N_VAR_A, j) for j in range(GROUPS_PER_STEP)],
            *[bias_spec(BAND_B, N_VAR_B, j) for j in range(GROUPS_PER_STEP)],
            *cast_specs,
        ],
        out_specs=[pl.BlockSpec((rows, 2 * WIDTH), lambda b, s: (b * n_steps + s, 0)), *cast_specs],
        out_shape=[jax.ShapeDtypeStruct((batch * seq, 2 * WIDTH), BF16), *cast_shapes],
        compiler_params=pltpu.CompilerParams(
            dimension_semantics=("arbitrary", "arbitrary"), vmem_limit_bytes=_vmem_limit(vmem)),
        name="attn_prompt",
    )(sinks, qa, qb, ka, va, kb, vb, *([bias_a] * GROUPS_PER_STEP), *([bias_b] * GROUPS_PER_STEP), *cast)
    return out[0], tuple(out[1:])


def _pair_ordered_cols(h):
    return (h % N_PAIRS) * V7X_LANES + (h // N_PAIRS) * HEAD_DIM


SAMPLE_ROWS_PER_STEP = 4


def _attn_sample_kernel(sinks_ref, qa_ref, qb_ref, ka_ref, va_ref, kb_ref, vb_ref,
                        cak_ref, cav_ref, cbk_ref, cbv_ref, ba_ref, bb_ref, o_ref):
    n_rows = cak_ref.shape[0]
    n_q = qa_ref.shape[0] // n_rows
    n_kv = cbk_ref.shape[1]
    per_kv = N_HEADS // n_kv
    la, lb = cak_ref.shape[3], cbk_ref.shape[3]
    head = lambda ref, r, idx: ref[r * n_q:(r + 1) * n_q, idx * HEAD_DIM:(idx + 1) * HEAD_DIM]

    def query(ref, r, lo):
        lo = lo if lo % V7X_LANES < HEAD_DIM else WIDTH + lo
        return ref[r * n_q:(r + 1) * n_q, lo:lo + HEAD_DIM]

    def scores(q, cache_t_ref, new_ref, r, idx):
        old = jnp.dot(q, cache_t_ref[r, idx].astype(BF16), preferred_element_type=F32)
        return jnp.concatenate([old, _scores(q, head(new_ref, r, idx))], axis=1)

    def values(e, cache_t_ref, new_ref, r, idx, n_old):
        old = _scores(e[:, :n_old], cache_t_ref[r, idx].astype(BF16))
        return old + jnp.dot(e[:, n_old:], head(new_ref, r, idx), preferred_element_type=F32)

    jobs_a = [(r, h) for r in range(n_rows) for h in range(N_HEADS)]
    jobs_b = [(r, c) for r in range(n_rows) for c in range(n_kv)]
    s_a = [scores(query(qa_ref, r, h * HEAD_DIM), cak_ref, ka_ref, r, h) + ba_ref[h] for r, h in jobs_a]
    s_b = []
    for r, c in jobs_b:
        heads = range(c * per_kv, (c + 1) * per_kv)
        q = jnp.concatenate([query(qb_ref, r, _pair_ordered_cols(h)) for h in heads], axis=0)
        bias = jnp.concatenate([bb_ref[h] for h in heads], axis=0)
        s_b.append(scores(q, cbk_ref, kb_ref, r, c) + bias)

    row = lax.broadcasted_iota(jnp.int32, (per_kv * n_q, 1), 0)
    sinks = []
    for c in range(n_kv):
        sink = jnp.zeros((per_kv * n_q, 1), F32)
        for i in range(per_kv):
            sink = jnp.where((row >= i * n_q) & (row < (i + 1) * n_q), sinks_ref[c * per_kv + i], sink)
        sinks.append(sink * LOG2E)
    p_a = [_softmax_terms(s) for s in s_a]
    p_b = [_softmax_terms(s, sinks[c]) for s, (_, c) in zip(s_b, jobs_b)]

    out_a = {job: values(e, cav_ref, va_ref, *job, la) * inv for job, (e, inv) in zip(jobs_a, p_a)}
    out_b = {}
    for (r, c), (e, inv) in zip(jobs_b, p_b):
        o = values(e, cbv_ref, vb_ref, r, c, lb) * inv
        for i in range(per_kv):
            out_b[(r, c * per_kv + i)] = o[i * n_q:(i + 1) * n_q]
    order_b = [h for p in range(N_PAIRS) for h in (p, N_PAIRS + p)]
    for r in range(n_rows):
        pieces = [out_a[(r, h)] for h in range(N_HEADS)] + [out_b[(r, h)] for h in order_b]
        o_ref[r * n_q:(r + 1) * n_q, :] = jnp.concatenate(pieces, axis=1).astype(BF16)


def _attn_sample(sinks, qa, qb, ka, va, kb, vb, cak, cav, cbk, cbv, bias_a, bias_b, batch, s_len):
    n_rows = SAMPLE_ROWS_PER_STEP
    row = lambda w: pl.BlockSpec((n_rows * s_len, w), lambda b: (b, 0))
    cache = lambda a: pl.BlockSpec((n_rows,) + a.shape[1:], lambda b: (b, 0, 0, 0))
    whole = lambda a: pl.BlockSpec(a.shape, lambda b: (0,) * a.ndim)
    return pl.pallas_call(
        _attn_sample_kernel,
        grid=(batch // n_rows,),
        in_specs=[
            pl.BlockSpec(memory_space=pltpu.SMEM),
            row(2 * WIDTH), row(2 * WIDTH), row(WIDTH), row(WIDTH), row(KV_WIDTH_B), row(KV_WIDTH_B),
            cache(cak), cache(cav), cache(cbk), cache(cbv),
            whole(bias_a), whole(bias_b),
        ],
        out_specs=pl.BlockSpec((n_rows * s_len, 2 * WIDTH), lambda b: (b, 0)),
        out_shape=jax.ShapeDtypeStruct((batch * s_len, 2 * WIDTH), BF16),
        compiler_params=pltpu.CompilerParams(dimension_semantics=("arbitrary",)),
        name="attn_sample",
    )(sinks, qa, qb, ka, va, kb, vb, cak, cav, cbk, cbv, bias_a, bias_b)


def _pair_order(x, axis):
    shape = x.shape
    split = shape[:axis] + (N_HEADS // N_PAIRS, N_PAIRS, HEAD_DIM) + shape[axis + 1:]
    return jnp.swapaxes(x.reshape(split), axis, axis + 1).reshape(shape)


def _prep_out_weights(w_out, group_gains):
    wout = jnp.concatenate([w_out[:WIDTH], _pair_order(w_out[WIDTH:], 0)], axis=0).astype(BF16)
    gg = jnp.concatenate([group_gains[:WIDTH], _pair_order(group_gains[WIDTH:], 0)])
    return wout, gg.reshape(1, 2 * WIDTH)


def kernel(x_prompt, x_sample, cache_a_k, cache_a_v, cache_b_k, cache_b_v, c_prompt, c_sample, w_mod, b_mod,
           norm_gains, w1_gate, w1_up, w1_down, w_in, w_out, group_gains, rel_bias_a, t5_bias_table, sinks_b,
           w2_gate, w2_up, w2_down):
    depth = w_mod.shape[0]
    assert depth == 1
    batch, seq, _ = x_prompt.shape
    s_batch, s_len, _ = x_sample.shape
    la, lb = cache_a_k.shape[2], cache_b_k.shape[2]
    assert la == REACH_A and lb == REACH_B and seq % (2 * FFN_TILE_TOKENS) == 0

    c_rows = jnp.concatenate([c_prompt, c_sample], axis=0)
    pad_rows = (-c_rows.shape[0]) % 8
    c_rows = jnp.pad(c_rows, ((0, pad_rows), (0, 0)))
    mod = _modulation(c_rows, w_mod[0], b_mod[0])
    mod_p = mod[:, :batch].reshape(N_MOD, batch, 1, D_MODEL)
    mod_s = mod[:, batch:batch + s_batch].reshape(N_MOD, s_batch, 1, D_MODEL)

    bias_a, bias_b, bias_sa, bias_sb, *w1, win = _bias_tables(
        rel_bias_a[0], t5_bias_table, s_len, la, lb, cast=(w1_gate[0], w1_up[0], w1_down[0], w_in[0]))

    gains = norm_gains[0]
    wout, gg = _prep_out_weights(w_out[0], group_gains[0])
    sinks = sinks_b[0]

    (x1p, qa, ka, va, qb, kb, vb, ka32, va32, kb32, vb32) = _ffn1_qkv(
        x_prompt, mod_p, gains, *w1, win, 1, FFN_TILE_TOKENS)
    op, w2 = _attn_prompt(sinks, qa, qb, ka, va, kb, vb, bias_a, bias_b, batch, seq,
                          cast=(w2_gate[0], w2_up[0], w2_down[0]))
    y_prompt = _out_ffn2(x1p, op, mod_p, gains, gg, wout, *w2, 1, 2 * FFN_TILE_TOKENS)
    rows_last = lambda t: jnp.transpose(t.reshape(depth, batch, -1, HEAD_DIM, t.shape[-1]), (0, 1, 4, 2, 3))
    new_p = tuple(rows_last(t) for t in (ka32, va32, kb32, vb32))

    gt_s = FFN_TILE_TOKENS // s_len
    (x1s, qas, kas, vas, qbs, kbs, vbs, kas32, vas32, kbs32, vbs32) = _ffn1_qkv(
        x_sample, mod_s, gains, *w1, win, gt_s, s_len)
    os_ = _attn_sample(sinks, qas, qbs, kas, vas, kbs, vbs,
                       *(jnp.transpose(c[0], (0, 2, 3, 1)) for c in (cache_a_k, cache_a_v, cache_b_k, cache_b_v)),
                       bias_sa, bias_sb, s_batch, s_len)
    y_sample = _out_ffn2(x1s, os_, mod_s, gains, gg, wout, *w2, gt_s, s_len)
    new_s = (kas32.reshape(depth, s_batch, s_len, N_HEADS, HEAD_DIM),
             vas32.reshape(depth, s_batch, s_len, N_HEADS, HEAD_DIM),
             kbs32.reshape(depth, s_batch, s_len, 2, HEAD_DIM), vbs32.reshape(depth, s_batch, s_len, 2, HEAD_DIM))

    return (y_prompt, y_sample) + new_p + new_s
```
